```python
import jax
import jax.numpy as jnp
from jax import lax
import numpy as np

D_MODEL = 2048
BATCH = 8
SEQ = 4096
DEPTH = 1
DEC_BATCH = 8
DEC_SEQ = 2048
PAST_LEN = 128

RW_WIDTH = D_MODEL // 2
RW_HEAD_DIM = 64
RW_HEADS = RW_WIDTH // RW_HEAD_DIM
W_LORA = 64
A_LORA = 64
G_LORA = 128
RW_COLS = 3 * RW_WIDTH + 2 * W_LORA + 2 * A_LORA + G_LORA
RW_GN_EPS = 64e-5
HG_WIDTH = D_MODEL - RW_WIDTH
HG_HEADS = 8
HG_HEAD_DIM = HG_WIDTH // HG_HEADS
HG_COLS = 5 * HG_WIDTH
HG_CHUNK = 64
IN_COLS = RW_COLS + HG_COLS
N_EXPERTS = 64
TOP_K = 6
N_GROUPS = 8
TOPK_GROUPS = 4
D_EXPERT = 512
D_SHARED = 512
ROUTED_SCALE = 2.5
EXPERT_BLOCK = 256
NORM_EPS = 1e-6

kernel_name = 'hybrid_rwkv7_hgrn2_moe_encoder'


def _rms(x, g):
    xf = x.astype(jnp.float32)
    y = xf * lax.rsqrt(jnp.mean(xf * xf, axis=-1, keepdims=True) + NORM_EPS)
    return (y * g.astype(jnp.float32)).astype(x.dtype)


def _centred_shift(u):
    p = jnp.pad(u, ((0, 0), (1, 1), (0, 0)))
    return 0.5 * (p[:, :-2] + p[:, 2:])


def _flip(t):
    return jnp.flip(t, axis=1)


def _rwkv7_scan(r, w, k, v, kk, a):
    B, S, H, N = r.shape

    def step(state, inp):
        r_t, w_t, k_t, v_t, kk_t, a_t = inp
        sa = jnp.einsum('bhvk,bhk->bhv', state, -kk_t)
        state = (state * w_t[:, :, None, :] + sa[..., None] * (kk_t * a_t)[:, :, None, :]
                 + v_t[..., None] * k_t[:, :, None, :])
        return state, jnp.einsum('bhvk,bhk->bhv', state, r_t)

    xs = tuple(jnp.swapaxes(t, 0, 1) for t in (r, w, k, v, kk, a))
    _, o = lax.scan(step, jnp.zeros((B, H, N, N), jnp.float32), xs)
    return jnp.swapaxes(o, 0, 1)


def _rwkv7_group(zr, w0, w_up, a0, a_up, g_up, k_k, k_a, r_k, ln_w, ln_b):
    B, S, _ = zr.shape
    W = RW_WIDTH
    zr = zr.astype(jnp.float32)
    r = zr[..., 0:W]
    k = zr[..., W:2 * W]
    v = zr[..., 2 * W:3 * W]
    o0 = 3 * W
    w_lat = zr[..., o0:o0 + 2 * W_LORA].reshape(B, S, 2, W_LORA)
    o0 += 2 * W_LORA
    a_lat = zr[..., o0:o0 + 2 * A_LORA].reshape(B, S, 2, A_LORA)
    o0 += 2 * A_LORA
    g_lat = zr[..., o0:o0 + G_LORA]
    w_raw = w0 + jnp.einsum('bsdr,drw->bsdw', jnp.tanh(w_lat), w_up)
    decay = jnp.exp(-jnp.exp(-jax.nn.softplus(-w_raw) - 0.5))
    a = jax.nn.sigmoid(a0 + jnp.einsum('bsdr,drw->bsdw', a_lat, a_up))
    g = jax.nn.sigmoid(g_lat) @ g_up

    def heads(t):
        return t.reshape(*t.shape[:-1], RW_HEADS, RW_HEAD_DIM)

    kk = heads(k * k_k)
    kk = kk / jnp.maximum(jnp.sqrt(jnp.sum(kk * kk, axis=-1, keepdims=True)), 1e-12)
    k_dir = k[:, :, None, :] * (1.0 + (a - 1.0) * k_a)
    rh, vh = heads(r), heads(v)
    o_f = _rwkv7_scan(rh, heads(decay[:, :, 0]), heads(k_dir[:, :, 0]), vh, kk, heads(a[:, :, 0]))
    o_b = _flip(_rwkv7_scan(_flip(rh), _flip(heads(decay[:, :, 1])), _flip(heads(k_dir[:, :, 1])),
                            _flip(vh), _flip(kk), _flip(heads(a[:, :, 1]))))
    o = o_f + o_b
    mu = jnp.mean(o, axis=-1, keepdims=True)
    var = jnp.mean(jnp.square(o - mu), axis=-1, keepdims=True)
    o = ((o - mu) * lax.rsqrt(var + RW_GN_EPS)).reshape(B, S, W) * ln_w + ln_b
    k_bonus = heads(0.5 * (k_dir[:, :, 0] + k_dir[:, :, 1]))
    o = o + (jnp.sum(rh * k_bonus * r_k, axis=-1, keepdims=True) * vh).reshape(B, S, W)
    return o * g


def _hgrn2_chunk_scan(q, k, v, log_f):
    B, S, H, DK = q.shape
    DV = v.shape[-1]
    NC = S // HG_CHUNK

    def chunks(t):
        return t.reshape(B, NC, HG_CHUNK, H, t.shape[-1]).transpose(1, 0, 3, 2, 4)

    incl = jnp.tril(jnp.ones((HG_CHUNK, HG_CHUNK), bool))[:, :, None]

    def step(state, inp):
        qc, kc, vc, fc = inp
        b = jnp.cumsum(fc, axis=2)
        diff = b[:, :, :, None, :] - b[:, :, None, :, :]
        dec = jnp.exp(jnp.where(incl, diff, -jnp.inf))
        scores = jnp.einsum('bhtk,bhsk,bhtsk->bhts', qc, kc, dec)
        o = (jnp.einsum('bhts,bhsv->bhtv', scores, vc)
             + jnp.einsum('bhtk,bhkv->bhtv', qc * jnp.exp(b), state))
        b_last = b[:, :, -1:, :]
        state = (jnp.exp(b_last[:, :, 0, :])[..., None] * state
                 + jnp.einsum('bhsk,bhsv->bhkv', kc * jnp.exp(b_last - b), vc))
        return state, o

    xs = tuple(chunks(t) for t in (q, k, v, log_f))
    _, o = lax.scan(step, jnp.zeros((B, H, DK, DV), jnp.float32), xs)
    return o.transpose(1, 0, 3, 2, 4).reshape(B, S, H, DV)


def _hgrn2_group(zh, lb, norm_w):
    B, S, _ = zh.shape
    zh = zh.astype(jnp.float32)
    q, ff, fb, i, g = jnp.split(zh, 5, axis=-1)

    def heads(t):
        return t.reshape(B, S, HG_HEADS, HG_HEAD_DIM)

    q = heads(jax.nn.silu(q))
    i = heads(i)
    f_f = heads(lb[0] + (1.0 - lb[0]) * jax.nn.sigmoid(ff))
    f_b = heads(lb[1] + (1.0 - lb[1]) * jax.nn.sigmoid(fb))
    o_f = _hgrn2_chunk_scan(q, 1.0 - f_f, i, jnp.log(f_f))
    o_b = _flip(_hgrn2_chunk_scan(_flip(q), _flip(1.0 - f_b), _flip(i), _flip(jnp.log(f_b))))
    o = o_f + o_b
    o = o * lax.rsqrt(jnp.mean(o * o, axis=-1, keepdims=True) + NORM_EPS)
    return o.reshape(B, S, HG_WIDTH) * norm_w * jax.nn.silu(g)


def _token_mixer(h, w_in, rw_mu, rw_w0, rw_w_up, rw_a0, rw_a_up, rw_g_up, rw_k_k, rw_k_a,
                 rw_r_k, rw_ln_w, rw_ln_b, hg_lb, hg_norm_w, w_out):
    z = h @ w_in
    zr = z[..., :RW_COLS]
    zh = z[..., RW_COLS:]
    zr = zr + rw_mu * (_centred_shift(zr) - zr)
    o_rw = _rwkv7_group(zr, rw_w0, rw_w_up, rw_a0, rw_a_up, rw_g_up, rw_k_k, rw_k_a, rw_r_k,
                        rw_ln_w, rw_ln_b)
    o_hg = _hgrn2_group(zh, hg_lb, hg_norm_w)
    return jnp.concatenate([o_rw, o_hg], axis=-1).astype(h.dtype) @ w_out


def _swiglu(x, wg, wu, wd):
    return (jax.nn.silu(x @ wg) * (x @ wu)) @ wd


def _routed_experts(x, eidx, wsel, w_gate, w_up, w_down):
    T = x.shape[0]
    A = T * TOP_K
    flat_e = eidx.reshape(-1)
    order = jnp.argsort(flat_e)
    sorted_e = flat_e[order]
    counts = jnp.bincount(flat_e, length=N_EXPERTS)
    padded = (counts + EXPERT_BLOCK - 1) // EXPERT_BLOCK * EXPERT_BLOCK
    pad_end = jnp.cumsum(padded)
    pad_start = pad_end - padded
    grp_start = jnp.cumsum(counts) - counts
    dest = pad_start[sorted_e] + jnp.arange(A) - grp_start[sorted_e]
    n_blocks = (A + EXPERT_BLOCK - 1) // EXPERT_BLOCK + N_EXPERTS
    P = n_blocks * EXPERT_BLOCK
    row_tok = jnp.zeros((P,), jnp.int32).at[dest].set((order // TOP_K).astype(jnp.int32))
    row_w = jnp.zeros((P,), x.dtype).at[dest].set(wsel.reshape(-1)[order])
    block_e = jnp.minimum(jnp.searchsorted(pad_end, jnp.arange(n_blocks) * EXPERT_BLOCK, side='right'),
                          N_EXPERTS - 1)

    def step(acc, blk):
        b, e = blk
        rows = lax.dynamic_slice_in_dim(row_tok, b * EXPERT_BLOCK, EXPERT_BLOCK)
        wr = lax.dynamic_slice_in_dim(row_w, b * EXPERT_BLOCK, EXPERT_BLOCK)
        yb = _swiglu(x[rows], w_gate[e], w_up[e], w_down[e])
        return acc.at[rows].add(yb * wr[:, None]), None

    acc, _ = lax.scan(step, jnp.zeros_like(x), (jnp.arange(n_blocks), block_e))
    return acc


def _moe(h, w_router, e_bias, w_exp_gate, w_exp_up, w_exp_down, w_sh_gate, w_sh_up, w_sh_down):
    B, S, D = h.shape
    x = h.reshape(-1, D)
    T = x.shape[0]
    scores = jax.nn.sigmoid(x.astype(jnp.float32) @ w_router.astype(jnp.float32))
    biased = scores + e_bias.astype(jnp.float32)
    grp = biased.reshape(T, N_GROUPS, N_EXPERTS // N_GROUPS)
    grp_score = jnp.sum(lax.top_k(grp, 2)[0], axis=-1)
    _, gidx = lax.top_k(grp_score, TOPK_GROUPS)
    gmask = jnp.any(gidx[..., None] == jnp.arange(N_GROUPS), axis=-2)
    emask = jnp.repeat(gmask, N_EXPERTS // N_GROUPS, axis=-1)
    _, eidx = lax.top_k(jnp.where(emask, biased, -jnp.inf), TOP_K)
    wsel = jnp.take_along_axis(scores, eidx, axis=-1)
    wsel = wsel / jnp.sum(wsel, axis=-1, keepdims=True) * ROUTED_SCALE
    routed = _routed_experts(x, eidx, wsel.astype(x.dtype), w_exp_gate, w_exp_up, w_exp_down)
    shared = _swiglu(x, w_sh_gate, w_sh_up, w_sh_down)
    return (routed + shared).reshape(B, S, D)


def _trunk(x, c, params):
    (w_ada, b_ada, norm_pre_mix, norm_post_mix, norm_pre_ffn, norm_post_ffn, w_in, rw_mu,
     rw_w0, rw_w_up, rw_a0, rw_a_up, rw_g_up, rw_k_k, rw_k_a, rw_r_k, rw_ln_w, rw_ln_b,
     hg_lb_gamma, hg_norm_w, w_out, w_router, e_bias, w_exp_gate, w_exp_up, w_exp_down,
     w_sh_gate, w_sh_up, w_sh_down) = params
    lower_bounds = jnp.cumsum(jax.nn.softmax(hg_lb_gamma.astype(jnp.float32), axis=0), axis=0)
    for l in range(DEPTH):
        mod = (jax.nn.silu(c) @ w_ada[l] + b_ada[l])[:, None, :]
        sh1, sc1, gt1, sh2, sc2, gt2 = jnp.split(mod, 6, axis=-1)
        h = _rms(x, norm_pre_mix[l]) * (1 + sc1) + sh1
        m = _token_mixer(h, w_in[l], rw_mu[l], rw_w0[l], rw_w_up[l], rw_a0[l], rw_a_up[l],
                         rw_g_up[l], rw_k_k[l], rw_k_a[l], rw_r_k[l], rw_ln_w[l], rw_ln_b[l],
                         lower_bounds[l], hg_norm_w[l], w_out[l])
        x = x + gt1 * _rms(m, norm_post_mix[l])
        h = _rms(x, norm_pre_ffn[l]) * (1 + sc2) + sh2
        f = _moe(h, w_router[l], e_bias[l], w_exp_gate[l], w_exp_up[l], w_exp_down[l],
                 w_sh_gate[l], w_sh_up[l], w_sh_down[l])
        x = x + gt2 * _rms(f, norm_post_ffn[l])
    return x


def setup_inputs(seed: int = 0) -> dict:
    key = jax.random.key(seed)
    ks = iter(jax.random.split(key, 40))
    L, D = DEPTH, D_MODEL
    f32 = jnp.float32

    def nrm(shape, s):
        return jax.random.normal(next(ks), shape, f32) * s

    def uni(shape, lo, hi):
        return jax.random.uniform(next(ks), shape, f32, minval=lo, maxval=hi)

    return {
        'x_prompt': nrm((BATCH, SEQ, D), 1.0),
        'x_sample': nrm((DEC_BATCH, DEC_SEQ, D), 1.0),
        'c_prompt': nrm((BATCH, D), 1.0),
        'c_sample': nrm((DEC_BATCH, D), 1.0),
        'w_ada': nrm((L, D, 6 * D), 0.5 * D ** -0.5),
        'b_ada': nrm((L, 6 * D), 0.02),
        'norm_pre_mix': 1.0 + nrm((L, D), 0.05),
        'norm_post_mix': 1.0 + nrm((L, D), 0.05),
        'norm_pre_ffn': 1.0 + nrm((L, D), 0.05),
        'norm_post_ffn': 1.0 + nrm((L, D), 0.05),
        'w_in': nrm((L, D, IN_COLS), D ** -0.5),
        'rw_mu': uni((L, RW_COLS), 0.0, 1.0),
        'rw_w0': uni((L, 2, RW_WIDTH), -5.0, 0.0),
        'rw_w_up': nrm((L, 2, W_LORA, RW_WIDTH), 0.1),
        'rw_a0': nrm((L, 2, RW_WIDTH), 0.5),
        'rw_a_up': nrm((L, 2, A_LORA, RW_WIDTH), 0.5 * A_LORA ** -0.5),
        'rw_g_up': nrm((L, G_LORA, RW_WIDTH), G_LORA ** -0.5),
        'rw_k_k': 0.85 + nrm((L, RW_WIDTH), 0.05),
        'rw_k_a': 1.0 + nrm((L, RW_WIDTH), 0.05),
        'rw_r_k': nrm((L, RW_HEADS, RW_HEAD_DIM), 0.1),
        'rw_ln_w': 1.0 + nrm((L, RW_WIDTH), 0.05),
        'rw_ln_b': nrm((L, RW_WIDTH), 0.02),
        'hg_lb_gamma': nrm((L + 1, 2, HG_WIDTH), 0.5),
        'hg_norm_w': 1.0 + nrm((L, HG_WIDTH), 0.05),
        'w_out': nrm((L, D, D), D ** -0.5),
        'w_router': nrm((L, D, N_EXPERTS), D ** -0.5),
        'e_bias': nrm((L, N_EXPERTS), 0.01),
        'w_exp_gate': nrm((L, N_EXPERTS, D, D_EXPERT), D ** -0.5),
        'w_exp_up': nrm((L, N_EXPERTS, D, D_EXPERT), D ** -0.5),
        'w_exp_down': nrm((L, N_EXPERTS, D_EXPERT, D), D_EXPERT ** -0.5),
        'w_sh_gate': nrm((L, D, D_SHARED), D ** -0.5),
        'w_sh_up': nrm((L, D, D_SHARED), D ** -0.5),
        'w_sh_down': nrm((L, D_SHARED, D), D_SHARED ** -0.5),
    }


def reference(x_prompt, x_sample, c_prompt, c_sample, w_ada, b_ada, norm_pre_mix, norm_post_mix,
              norm_pre_ffn, norm_post_ffn, w_in, rw_mu, rw_w0, rw_w_up, rw_a0, rw_a_up, rw_g_up,
              rw_k_k, rw_k_a, rw_r_k, rw_ln_w, rw_ln_b, hg_lb_gamma, hg_norm_w, w_out, w_router,
              e_bias, w_exp_gate, w_exp_up, w_exp_down, w_sh_gate, w_sh_up, w_sh_down):
    params = (w_ada, b_ada, norm_pre_mix, norm_post_mix, norm_pre_ffn, norm_post_ffn, w_in, rw_mu,
              rw_w0, rw_w_up, rw_a0, rw_a_up, rw_g_up, rw_k_k, rw_k_a, rw_r_k, rw_ln_w, rw_ln_b,
              hg_lb_gamma, hg_norm_w, w_out, w_router, e_bias, w_exp_gate, w_exp_up, w_exp_down,
              w_sh_gate, w_sh_up, w_sh_down)
    y_prompt = _trunk(x_prompt, c_prompt, params)
    y_sample = _trunk(x_sample, c_sample, params)
    return (y_prompt, y_sample)
```

```python
import functools
import math

import jax
import jax.numpy as jnp
from jax import lax
from jax.experimental import pallas as pl
from jax.experimental.pallas import tpu as pltpu

F32 = jnp.float32
BF16 = jnp.bfloat16
I32 = jnp.int32

RW_HEAD_DIM = 64
W_LORA = 64
A_LORA = 64
G_LORA = 128
RW_GN_EPS = 64e-5
HG_HEADS = 8
N_EXPERTS = 64
TOP_K = 6
N_GROUPS = 8
TOPK_GROUPS = 4
ROUTED_SCALE = 2.5
EXPERT_BLOCK = 256
NORM_EPS = 1e-6

LANES = 128
SUBLANES = 8
VMEM_LIMIT = 56 * 1024 * 1024

CHUNK = 64
NEG_BIG = -1e30


def _cparams(sem, vmem=VMEM_LIMIT):
    return pltpu.CompilerParams(dimension_semantics=sem, vmem_limit_bytes=vmem)


def _sigmoid(x):
    return 1.0 / (1.0 + jnp.exp(-x))


def _silu(x):
    return x * _sigmoid(x)


def _dot(a, b):
    return jnp.dot(a.astype(BF16), b.astype(BF16), preferred_element_type=F32)


def _dot_nt(a, b):
    return lax.dot_general(a.astype(BF16), b.astype(BF16), (((1,), (1,)), ((), ())),
                           preferred_element_type=F32)


def _dot_tn(a, b):
    return lax.dot_general(a.astype(BF16), b.astype(BF16), (((0,), (0,)), ((), ())),
                           preferred_element_type=F32)


def _split2(x):
    hi = x.astype(BF16)
    lo = (x - hi.astype(F32)).astype(BF16)
    return hi, lo


def _split3(x):
    hi = x.astype(BF16)
    r1 = x - hi.astype(F32)
    mid = r1.astype(BF16)
    lo = (r1 - mid.astype(F32)).astype(BF16)
    return hi, mid, lo


def _seg_sum(x, e, et):
    hi, lo = _split2(x)
    s = jnp.dot(hi, e, preferred_element_type=F32) + jnp.dot(lo, e, preferred_element_type=F32)
    shi, slo = _split2(s)
    return jnp.dot(shi, et, preferred_element_type=F32) + jnp.dot(slo, et, preferred_element_type=F32)


def _rms_rows(x, g):
    return x * lax.rsqrt(jnp.mean(x * x, axis=-1, keepdims=True) + NORM_EPS) * g


def _ada_kernel(c_ref, w_ref, b_ref, o_ref):
    c = c_ref[...]
    o_ref[...] = _dot(_silu(c), w_ref[...]) + b_ref[...]


def _ada(c, w_ada, b_ada):
    nb, d = c.shape
    n = w_ada.shape[1]
    tn = 512
    return pl.pallas_call(
        _ada_kernel,
        grid=(n // tn,),
        in_specs=[pl.BlockSpec((nb, d), lambda j: (0, 0)),
                  pl.BlockSpec((d, tn), lambda j: (0, j)),
                  pl.BlockSpec((1, tn), lambda j: (0, j))],
        out_specs=pl.BlockSpec((nb, tn), lambda j: (0, j)),
        out_shape=jax.ShapeDtypeStruct((nb, n), F32),
        compiler_params=_cparams(("parallel",)),
        name="ada",
    )(c, w_ada, b_ada.reshape(1, n))


def _inproj_kernel(x_ref, mod_ref, g_ref, w_ref, o_ref, h_scr):
    @pl.when(pl.program_id(1) == 0)
    def _():
        m = mod_ref[0]
        h = _rms_rows(x_ref[...], g_ref[...]) * (1.0 + m[1:2]) + m[0:1]
        h_scr[...] = h.astype(BF16)

    o_ref[...] = jnp.dot(h_scr[...], w_ref[...], preferred_element_type=F32)


def _inproj(x2, mod, g, w_bf, seq):
    t, d = x2.shape
    n = w_bf.shape[1]
    tm, tn = 512, 512
    return pl.pallas_call(
        _inproj_kernel,
        grid=(t // tm, n // tn),
        in_specs=[pl.BlockSpec((tm, d), lambda i, j: (i, 0)),
                  pl.BlockSpec((1, 6, d), lambda i, j: ((i * tm) // seq, 0, 0)),
                  pl.BlockSpec((1, d), lambda i, j: (0, 0)),
                  pl.BlockSpec((d, tn), lambda i, j: (0, j))],
        out_specs=pl.BlockSpec((tm, tn), lambda i, j: (i, j)),
        out_shape=jax.ShapeDtypeStruct((t, n), F32),
        scratch_shapes=[pltpu.VMEM((tm, d), BF16)],
        compiler_params=_cparams(("parallel", "arbitrary")),
        name="inproj",
    )(x2, mod, g.reshape(1, d), w_bf)


def _rwprep_kernel(seq, tm, rw,
                   z_ref, zp_ref, zn_ref, l_ref, lp_ref, ln_ref,
                   mu_ref, mul_ref, wup_ref, aup_ref, gup_ref, w0_ref, a0_ref,
                   kk_ref, ka_ref, rk_ref, e_ref, et_ref,
                   r_o, v_o, kk_o, g_o, bon_o, lw0_o, lw1_o, b0_o, b1_o, kd0_o, kd1_o):
    i = pl.program_id(0)
    first = (i * tm) % seq == 0
    last = ((i + 1) * tm) % seq == 0

    def shifted(cur, prev_blk, next_blk, mu):
        rows = lax.broadcasted_iota(I32, cur.shape, 0)
        prow = jnp.where(first, 0.0, prev_blk[SUBLANES - 1:SUBLANES, :])
        nrow = jnp.where(last, 0.0, next_blk[0:1, :])
        prev = jnp.where(rows == 0, prow, pltpu.roll(cur, 1, axis=0))
        nxt = jnp.where(rows == tm - 1, nrow, pltpu.roll(cur, tm - 1, axis=0))
        return cur + mu * (0.5 * (prev + nxt) - cur)

    lat = shifted(l_ref[...], lp_ref[...], ln_ref[...], mul_ref[...])
    w_lat = lat[:, 0:2 * W_LORA]
    a_lat = lat[:, 2 * W_LORA:2 * W_LORA + 2 * A_LORA]
    g_lat = lat[:, 2 * W_LORA + 2 * A_LORA:2 * W_LORA + 2 * A_LORA + G_LORA]
    w_raw = _dot(jnp.tanh(w_lat), wup_ref[...]) + w0_ref[...]
    a_all = _sigmoid(_dot(a_lat, aup_ref[...]) + a0_ref[...])
    g_o[...] = _dot(_sigmoid(g_lat), gup_ref[...])
    lw = (-math.exp(-0.5)) * _sigmoid(w_raw)
    lw0_o[...] = lw[:, :rw]
    lw1_o[...] = lw[:, rw:]

    r = shifted(z_ref[:, 0:rw], zp_ref[:, 0:rw], zn_ref[:, 0:rw], mu_ref[:, 0:rw])
    k = shifted(z_ref[:, rw:2 * rw], zp_ref[:, rw:2 * rw], zn_ref[:, rw:2 * rw], mu_ref[:, rw:2 * rw])
    v = shifted(z_ref[:, 2 * rw:3 * rw], zp_ref[:, 2 * rw:3 * rw], zn_ref[:, 2 * rw:3 * rw],
                mu_ref[:, 2 * rw:3 * rw])
    r_o[...] = r
    v_o[...] = v
    kk = k * kk_ref[...]
    ss = _seg_sum(kk * kk, e_ref[...], et_ref[...])
    kk = kk / jnp.maximum(jnp.sqrt(ss), 1e-12)
    kk_o[...] = kk
    a0 = a_all[:, :rw]
    a1 = a_all[:, rw:]
    b0_o[...] = kk * a0
    b1_o[...] = kk * a1
    kd0 = k * (1.0 + (a0 - 1.0) * ka_ref[...])
    kd1 = k * (1.0 + (a1 - 1.0) * ka_ref[...])
    kd0_o[...] = kd0
    kd1_o[...] = kd1
    kb = 0.5 * (kd0 + kd1)
    bon_o[...] = _seg_sum(r * kb * rk_ref[...], e_ref[...], et_ref[...]) * v


def _rwprep(z, seq, rw, lat_off, mu_rkv, mu_lat, wup, aup, gup, w0, a0, k_k, k_a, r_k, e64, e64t):
    t = z.shape[0]
    tm = 256
    nlat = 512
    nrkv = 3 * rw
    tb = tm // SUBLANES
    nblk8 = t // SUBLANES
    lat_blk = lat_off // nlat

    def cur(i):
        return (i, 0)

    def prv(i):
        return (jnp.maximum(i * tb - 1, 0), 0)

    def nxt(i):
        return (jnp.minimum((i + 1) * tb, nblk8 - 1), 0)

    def full(shape):
        return pl.BlockSpec(shape, lambda i: (0,) * len(shape))

    out = jax.ShapeDtypeStruct((t, rw), F32)
    ospec = pl.BlockSpec((tm, rw), lambda i: (i, 0))
    return pl.pallas_call(
        functools.partial(_rwprep_kernel, seq, tm, rw),
        grid=(t // tm,),
        in_specs=[pl.BlockSpec((tm, nrkv), cur),
                  pl.BlockSpec((SUBLANES, nrkv), prv),
                  pl.BlockSpec((SUBLANES, nrkv), nxt),
                  pl.BlockSpec((tm, nlat), lambda i: (i, lat_blk)),
                  pl.BlockSpec((SUBLANES, nlat), lambda i: (jnp.maximum(i * tb - 1, 0), lat_blk)),
                  pl.BlockSpec((SUBLANES, nlat), lambda i: (jnp.minimum((i + 1) * tb, nblk8 - 1), lat_blk)),
                  full((1, nrkv)), full((1, nlat)),
                  full(wup.shape), full(aup.shape), full(gup.shape),
                  full((1, 2 * rw)), full((1, 2 * rw)),
                  full((1, rw)), full((1, rw)), full((1, rw)),
                  full(e64.shape), full(e64t.shape)],
        out_specs=[ospec] * 11,
        out_shape=[out] * 11,
        compiler_params=_cparams(("parallel",)),
        name="rwprep",
    )(z, z, z, z, z, z, mu_rkv, mu_lat, wup, aup, gup, w0, a0, k_k, k_a, r_k, e64, e64t)


def _tri(n, rev):
    i = lax.broadcasted_iota(I32, (n, n), 0)
    j = lax.broadcasted_iota(I32, (n, n), 1)
    m = (j >= i) if rev else (j <= i)
    return jnp.where(m, 1.0, 0.0).astype(BF16)


def _rw_dir(r, v, kk, lw, b, kd, s_in, rev):
    c = r.shape[0]
    hd = RW_HEAD_DIM
    n2 = 2 * c
    hi, lo = _split2(lw)
    cs = jnp.dot(_tri(c, rev), jnp.concatenate([hi, lo], axis=1), preferred_element_type=F32)
    cum = cs[:, :LANES] + cs[:, LANES:]
    tot = cum[0:1, :] if rev else cum[c - 1:c, :]
    g_incl = jnp.exp(cum)
    g_excl = jnp.exp(cum - lw)
    g_inv = jnp.exp(-cum)
    g_tail = jnp.exp(tot - cum)
    g_tot = jnp.exp(tot)

    lane = lax.broadcasted_iota(I32, (c, LANES), 1)
    head_a = lane < hd

    def pair(x):
        return jnp.concatenate([jnp.where(head_a, x, 0.0), jnp.where(head_a, 0.0, x)], axis=0)

    p2 = pair(-kk * g_excl)
    r2 = pair(r * g_incl)
    bi2 = pair(b * g_inv)
    ki2 = pair(kd * g_inv)
    bt2 = pair(b * g_tail)
    kt2 = pair(kd * g_tail)
    v2 = pair(v)

    gm = _dot_nt(jnp.concatenate([p2, r2], axis=0), jnp.concatenate([bi2, ki2], axis=0))
    ri = lax.broadcasted_iota(I32, (n2, n2), 0)
    ci = lax.broadcasted_iota(I32, (n2, n2), 1)
    ti = ri % c
    si = ci % c
    if rev:
        strict = si > ti
        incl = si >= ti
    else:
        strict = si < ti
        incl = si <= ti
    a2 = jnp.where(strict, gm[:n2, :n2], 0.0)
    b2 = jnp.where(strict, gm[:n2, n2:], 0.0)
    ap2 = jnp.where(incl, gm[n2:, :n2], 0.0)
    bp2 = jnp.where(incl, gm[n2:, n2:], 0.0)

    same16 = (ri // 16) == (ci // 16)
    same32 = (ri // 32) == (ci // 32)
    eye = ri == ci
    x = jnp.where(same16, a2, 0.0)
    tinv = jnp.where(eye, 1.0, 0.0) + x
    for _ in range(3):
        x = _dot(x, x)
        tinv = tinv + _dot(tinv, x)
    l1 = jnp.where(jnp.logical_and(same32, jnp.logical_not(same16)), a2, 0.0)
    tinv = tinv + _dot(_dot(tinv, l1), tinv)
    l2 = jnp.where(same32, 0.0, a2)
    tinv = tinv + _dot(_dot(tinv, l2), tinv)

    bv = _dot(b2, v2)
    wu = _dot(tinv, jnp.concatenate([p2, bv], axis=1))
    qo = _dot(ap2, wu)
    q2 = r2 + qo[:, :LANES]
    o2 = qo[:, LANES:] + _dot(bp2, v2)

    m2 = _dot_tn(wu[:, :LANES], bt2)
    ei = lax.broadcasted_iota(I32, (LANES, LANES), 0)
    ej = lax.broadcasted_iota(I32, (LANES, LANES), 1)
    m2 = m2 + jnp.where(ei == ej, jnp.broadcast_to(g_tot, (LANES, LANES)), 0.0)
    nn2 = _dot_tn(wu[:, LANES:], bt2) + _dot_tn(v2, kt2)

    s_hi, s_lo = _split2(s_in)
    q2b = q2.astype(BF16)
    oo = (lax.dot_general(q2b, s_hi, (((1,), (1,)), ((), ())), preferred_element_type=F32)
          + lax.dot_general(q2b, s_lo, (((1,), (1,)), ((), ())), preferred_element_type=F32) + o2)
    o = oo[:c] + oo[c:]
    m2b = m2.astype(BF16)
    s_out = (jnp.dot(s_hi, m2b, preferred_element_type=F32)
             + jnp.dot(s_lo, m2b, preferred_element_type=F32) + nn2)
    return o, s_out


def _rwscan_kernel(rf, vf, kkf, lwf, bf, kdf, rb, vb, kkb, lwb, bb, kdb, of_ref, ob_ref, sf, sb):
    @pl.when(pl.program_id(2) == 0)
    def _():
        sf[...] = jnp.zeros_like(sf)
        sb[...] = jnp.zeros_like(sb)

    o, s = _rw_dir(rf[...], vf[...], kkf[...], lwf[...], bf[...], kdf[...], sf[...], False)
    of_ref[...] = o
    sf[...] = s
    o, s = _rw_dir(rb[...], vb[...], kkb[...], lwb[...], bb[...], kdb[...], sb[...], True)
    ob_ref[...] = o
    sb[...] = s


def _rwscan(nb, seq, r, v, kk, lw0, lw1, b0, b1, kd0, kd1):
    t, rw = r.shape
    nc = seq // CHUNK
    npair = rw // LANES
    fw = pl.BlockSpec((CHUNK, LANES), lambda bi, hp, c: (bi * nc + c, hp))
    bw = pl.BlockSpec((CHUNK, LANES), lambda bi, hp, c: (bi * nc + nc - 1 - c, hp))
    out = jax.ShapeDtypeStruct((t, rw), F32)
    return pl.pallas_call(
        _rwscan_kernel,
        grid=(nb, npair, nc),
        in_specs=[fw] * 6 + [bw] * 6,
        out_specs=[fw, bw],
        out_shape=[out, out],
        scratch_shapes=[pltpu.VMEM((LANES, LANES), F32), pltpu.VMEM((LANES, LANES), F32)],
        compiler_params=_cparams(("parallel", "parallel", "arbitrary")),
        name="rwscan",
    )(r, v, kk, lw0, b0, kd0, r, v, kk, lw1, b1, kd1)


def _hg_dir(q, k, v, lf, st, rev):
    c = q.shape[0]
    hi, mid, lo = _split3(lf)
    cs = jnp.dot(_tri(c, rev), jnp.concatenate([hi, mid, lo], axis=1), preferred_element_type=F32)
    dk = lf.shape[1]
    cum = cs[:, :dk] + cs[:, dk:2 * dk] + cs[:, 2 * dk:]
    tot = cum[0:1, :] if rev else cum[c - 1:c, :]

    row = lax.broadcasted_iota(I32, (c, dk), 0)
    ri = lax.broadcasted_iota(I32, (c, c), 0)
    ci = lax.broadcasted_iota(I32, (c, c), 1)
    scores = jnp.where(ri == ci, _dot_nt(q, k), 0.0)
    rolled = {}

    def rolled_cum(d):
        if d not in rolled:
            rolled[d] = pltpu.roll(cum, d % c, axis=0)
        return rolled[d]

    h = c // 2
    while h >= 1:
        blk = 2 * h
        pos = row % blk
        if h >= SUBLANES:
            pieces = []
            for m0 in range(0, c, blk):
                idx = m0 + h if rev else m0 + h - 1
                pieces.append(jnp.broadcast_to(cum[idx:idx + 1, :], (blk, dk)))
            ref = jnp.concatenate(pieces, axis=0) if len(pieces) > 1 else pieces[0]
        else:
            off = h if rev else h - 1
            ref = jnp.zeros_like(cum)
            for p in range(blk):
                ref = jnp.where(pos == p, rolled_cum(p - off), ref)
        upper = pos >= h
        q_rows = jnp.logical_not(upper) if rev else upper
        eq = jnp.where(q_rows, jnp.minimum(cum - ref, 0.0), NEG_BIG)
        ek = jnp.where(q_rows, NEG_BIG, jnp.minimum(ref - cum, 0.0))
        sl = _dot_nt(q * jnp.exp(eq), k * jnp.exp(ek))
        scores = scores + jnp.where((ri // blk) == (ci // blk), sl, 0.0)
        h //= 2

    o = _dot(scores, v) + _dot_nt(q * jnp.exp(cum), st)
    st_new = st * jnp.exp(tot) + _dot_tn(v, k * jnp.exp(tot - cum))
    return o, st_new


def _hgscan_kernel(qf, fff, i_f, qb, ffb, i_b, lb_ref, of_ref, ob_ref, sf, sb):
    @pl.when(pl.program_id(2) == 0)
    def _():
        sf[...] = jnp.zeros_like(sf)
        sb[...] = jnp.zeros_like(sb)

    def run(q_ref, ff_ref, i_ref, lbv, st_ref, o_ref, rev):
        f = lbv + (1.0 - lbv) * _sigmoid(ff_ref[...])
        o, st = _hg_dir(_silu(q_ref[...]), 1.0 - f, i_ref[...], jnp.log(f), st_ref[...], rev)
        o_ref[...] = o
        st_ref[...] = st

    lb = lb_ref[0]
    run(qf, fff, i_f, lb[0:1], sf, of_ref, False)
    run(qb, ffb, i_b, lb[1:2], sb, ob_ref, True)


def _hgscan(nb, seq, z, lb3, hg_off, hw):
    t = z.shape[0]
    nc = seq // CHUNK
    dk = hw // HG_HEADS
    base = hg_off // dk
    nh = HG_HEADS

    def fw(comp):
        return pl.BlockSpec((CHUNK, dk), lambda bi, h, c: (bi * nc + c, base + comp * nh + h))

    def bw(comp):
        return pl.BlockSpec((CHUNK, dk), lambda bi, h, c: (bi * nc + nc - 1 - c, base + comp * nh + h))

    out = jax.ShapeDtypeStruct((t, hw), F32)
    return pl.pallas_call(
        _hgscan_kernel,
        grid=(nb, nh, nc),
        in_specs=[fw(0), fw(1), fw(3), bw(0), bw(2), bw(3),
                  pl.BlockSpec((1, 2, dk), lambda bi, h, c: (h, 0, 0))],
        out_specs=[pl.BlockSpec((CHUNK, dk), lambda bi, h, c: (bi * nc + c, h)),
                   pl.BlockSpec((CHUNK, dk), lambda bi, h, c: (bi * nc + nc - 1 - c, h))],
        out_shape=[out, out],
        scratch_shapes=[pltpu.VMEM((dk, dk), F32), pltpu.VMEM((dk, dk), F32)],
        compiler_params=_cparams(("parallel", "parallel", "arbitrary")),
        name="hgscan",
    )(z, z, z, z, z, z, lb3)


def _blockdiag2(w):
    _, r, n = w.shape
    z = jnp.zeros((r, n), w.dtype)
    return jnp.concatenate([jnp.concatenate([w[0], z], axis=1), jnp.concatenate([z, w[1]], axis=1)], axis=0)


def _indicator(width, seg):
    e = (jnp.arange(width)[:, None] // seg == jnp.arange(width // seg)[None, :]).astype(BF16)
    return e, e.T


def _layer_consts(w_in, rw_mu, rw_w0, rw_w_up, rw_a0, rw_a_up, rw_g_up, rw_k_k, rw_k_a, rw_r_k,
                  hg_lb_gamma):
    rw = rw_k_k.shape[-1]
    d = w_in.shape[1]
    nlat = 2 * W_LORA + 2 * A_LORA + G_LORA
    w = w_in[0]
    rkv = 3 * rw
    hg_cols = w.shape[1] - rkv - nlat
    pad = 512 - nlat
    w_perm = jnp.concatenate([w[:, :rkv], w[:, rkv + nlat:], w[:, rkv:rkv + nlat],
                              jnp.zeros((d, pad), w.dtype)], axis=1).astype(BF16)
    mu = rw_mu[0]
    lower = jnp.cumsum(jax.nn.softmax(hg_lb_gamma.astype(F32), axis=0), axis=0)[0]
    hw = lower.shape[-1]
    dk = hw // HG_HEADS
    e64, e64t = _indicator(rw, RW_HEAD_DIM)
    return dict(
        rw=rw, hw=hw, hg_off=rkv, lat_off=rkv + hg_cols, w_in=w_perm,
        mu_rkv=mu[:rkv].reshape(1, rkv),
        mu_lat=jnp.pad(mu[rkv:rkv + nlat], (0, pad)).reshape(1, 512),
        wup=_blockdiag2(rw_w_up[0]).astype(BF16), aup=_blockdiag2(rw_a_up[0]).astype(BF16),
        gup=rw_g_up[0].astype(BF16),
        w0=rw_w0[0].reshape(1, 2 * rw), a0=rw_a0[0].reshape(1, 2 * rw),
        k_k=rw_k_k[0].reshape(1, rw), k_a=rw_k_a[0].reshape(1, rw), r_k=rw_r_k[0].reshape(1, rw),
        e64=e64, e64t=e64t,
        lb3=lower.reshape(2, HG_HEADS, dk).transpose(1, 0, 2),
    )


def _mixer(x2, nb, seq, mod, norm_pre_mix, wc):
    z = _inproj(x2, mod, norm_pre_mix, wc['w_in'], seq)
    (r, v, kk, g, bonus, lw0, lw1, b0, b1, kd0, kd1) = _rwprep(
        z, seq, wc['rw'], wc['lat_off'], wc['mu_rkv'], wc['mu_lat'], wc['wup'], wc['aup'], wc['gup'],
        wc['w0'], wc['a0'], wc['k_k'], wc['k_a'], wc['r_k'], wc['e64'], wc['e64t'])
    rw_of, rw_ob = _rwscan(nb, seq, r, v, kk, lw0, lw1, b0, b1, kd0, kd1)
    hg_of, hg_ob = _hgscan(nb, seq, z, wc['lb3'], wc['hg_off'], wc['hw'])
    return dict(z=z, r=r, v=v, kk=kk, g=g, bonus=bonus, lw0=lw0, rw_of=rw_of, rw_ob=rw_ob,
                hg_of=hg_of, hg_ob=hg_ob)


def _outproj_kernel(rw, x_ref, rf_ref, rb_ref, bon_ref, g_ref, hf_ref, hb_ref, hgg_ref, mod_ref,
                    lnw_ref, lnb_ref, hnw_ref, npm_ref, npf_ref, wout_ref, wrh_ref, wrl_ref,
                    e64_ref, e64t_ref, e128_ref, e128t_ref,
                    x1_ref, h2_ref, lg_ref):
    m6 = mod_ref[0]
    o = rf_ref[...] + rb_ref[...]
    mean = _seg_sum(o, e64_ref[...], e64t_ref[...]) * (1.0 / RW_HEAD_DIM)
    dlt = o - mean
    var = _seg_sum(dlt * dlt, e64_ref[...], e64t_ref[...]) * (1.0 / RW_HEAD_DIM)
    o_rw = (dlt * lax.rsqrt(var + RW_GN_EPS) * lnw_ref[...] + lnb_ref[...] + bon_ref[...]) * g_ref[...]
    oh = hf_ref[...] + hb_ref[...]
    hd = oh.shape[1] // HG_HEADS
    ms = _seg_sum(oh * oh, e128_ref[...], e128t_ref[...]) * (1.0 / hd)
    o_hg = oh * lax.rsqrt(ms + NORM_EPS) * hnw_ref[...] * _silu(hgg_ref[...])
    m = _dot(o_rw, wout_ref[0:rw, :]) + _dot(o_hg, wout_ref[rw:, :])
    x1 = x_ref[...] + m6[2:3] * _rms_rows(m, npm_ref[...])
    x1_ref[...] = x1
    h2 = _rms_rows(x1, npf_ref[...]) * (1.0 + m6[4:5]) + m6[3:4]
    hi, lo = _split2(h2)
    h2_ref[...] = hi
    nt = (((1,), (1,)), ((), ()))
    lg_ref[...] = (lax.dot_general(wrh_ref[...], hi, nt, preferred_element_type=F32)
                   + lax.dot_general(wrh_ref[...], lo, nt, preferred_element_type=F32)
                   + lax.dot_general(wrl_ref[...], hi, nt, preferred_element_type=F32))


def _outproj(x2, seq, mx, z, mod, wc, oc):
    t, d = x2.shape
    rw, hw = wc['rw'], wc['hw']
    tm = 256
    gblk = (wc['hg_off'] + 4 * hw) // hw

    def row(w):
        return pl.BlockSpec((tm, w), lambda i: (i, 0))

    def full(a):
        return pl.BlockSpec(a.shape, lambda i: (0,) * a.ndim)

    consts = [oc['ln_w'], oc['ln_b'], oc['hg_norm_w'], oc['npm'], oc['npf'], oc['w_out'], oc['wr_hi'],
              oc['wr_lo'], wc['e64'], wc['e64t'], oc['e128'], oc['e128t']]
    return pl.pallas_call(
        functools.partial(_outproj_kernel, rw),
        grid=(t // tm,),
        in_specs=[row(d), row(rw), row(rw), row(rw), row(rw), row(hw), row(hw),
                  pl.BlockSpec((tm, hw), lambda i: (i, gblk)),
                  pl.BlockSpec((1, 6, d), lambda i: ((i * tm) // seq, 0, 0))] + [full(a) for a in consts],
        out_specs=[row(d), row(d), pl.BlockSpec((N_EXPERTS, tm), lambda i: (0, i))],
        out_shape=[jax.ShapeDtypeStruct((t, d), F32), jax.ShapeDtypeStruct((t, d), BF16),
                   jax.ShapeDtypeStruct((N_EXPERTS, t), F32)],
        compiler_params=_cparams(("parallel",)),
        name="outproj",
    )(x2, mx['rw_of'], mx['rw_ob'], mx['bonus'], mx['g'], mx['hg_of'], mx['hg_ob'], z, mod, *consts)


ROUTE_TILE = 512


def _route_kernel(lg_ref, bias_ref, ut_ref, eidx_ref, wsel_ref, rank_ref, cnt_ref, carry):
    @pl.when(pl.program_id(0) == 0)
    def _():
        carry[...] = jnp.zeros_like(carry)

    ne, tt = lg_ref.shape
    gsz = ne // N_GROUPS
    neg = -jnp.inf
    s = _sigmoid(lg_ref[...])
    biased = s + bias_ref[...]
    io_g = lax.broadcasted_iota(I32, (gsz, tt), 0)
    gs_rows = []
    for gi in range(N_GROUPS):
        blk = biased[gi * gsz:(gi + 1) * gsz, :]
        m1 = jnp.max(blk, axis=0, keepdims=True)
        first = jnp.min(jnp.where(blk == m1, io_g, gsz), axis=0, keepdims=True)
        m2 = jnp.max(jnp.where(io_g == first, neg, blk), axis=0, keepdims=True)
        gs_rows.append(m1 + m2)
    gs = jnp.concatenate(gs_rows, axis=0)
    io_n = lax.broadcasted_iota(I32, (N_GROUPS, tt), 0)
    selg = jnp.zeros((N_GROUPS, tt), jnp.bool_)
    for _ in range(TOPK_GROUPS):
        m = jnp.max(gs, axis=0, keepdims=True)
        first = jnp.min(jnp.where(gs == m, io_n, N_GROUPS), axis=0, keepdims=True)
        pick = io_n == first
        selg = jnp.logical_or(selg, pick)
        gs = jnp.where(pick, neg, gs)
    emask = jnp.concatenate([jnp.broadcast_to(selg[gi:gi + 1, :], (gsz, tt)) for gi in range(N_GROUPS)],
                            axis=0)
    mb = jnp.where(emask, biased, neg)
    io_e = lax.broadcasted_iota(I32, (ne, tt), 0)
    sel = jnp.zeros((ne, tt), jnp.bool_)
    picks, idxs, ws = [], [], []
    for _ in range(TOP_K):
        m = jnp.max(mb, axis=0, keepdims=True)
        first = jnp.min(jnp.where(mb == m, io_e, ne), axis=0, keepdims=True)
        pick = io_e == first
        picks.append(pick)
        idxs.append(first)
        ws.append(jnp.sum(jnp.where(pick, s, 0.0), axis=0, keepdims=True))
        sel = jnp.logical_or(sel, pick)
        mb = jnp.where(pick, neg, mb)
    wsum = ws[0]
    for w in ws[1:]:
        wsum = wsum + w
    pos = jnp.dot(jnp.where(sel, 1.0, 0.0).astype(BF16), ut_ref[...], preferred_element_type=F32) + carry[...]
    ranks = [jnp.sum(jnp.where(p, pos, 0.0), axis=0, keepdims=True).astype(I32) for p in picks]
    carry[...] = carry[...] + jnp.sum(jnp.where(sel, 1.0, 0.0), axis=1, keepdims=True)
    zi = jnp.zeros((SUBLANES - TOP_K, tt), I32)
    eidx_ref[...] = jnp.concatenate(idxs + [zi], axis=0)
    rank_ref[...] = jnp.concatenate(ranks + [zi], axis=0)
    wsel_ref[...] = jnp.concatenate([w / wsum * ROUTED_SCALE for w in ws] + [zi.astype(F32)], axis=0)
    cnt_ref[...] = jnp.broadcast_to(carry[...], cnt_ref.shape).astype(I32)


def _route(logits_t, e_bias):
    ne, t = logits_t.shape
    tt = ROUTE_TILE
    ut = (jnp.arange(tt)[:, None] < jnp.arange(tt)[None, :]).astype(BF16)
    tok = pl.BlockSpec((SUBLANES, tt), lambda i: (0, i))
    return pl.pallas_call(
        _route_kernel,
        grid=(t // tt,),
        in_specs=[pl.BlockSpec((ne, tt), lambda i: (0, i)),
                  pl.BlockSpec((ne, 1), lambda i: (0, 0)),
                  pl.BlockSpec((tt, tt), lambda i: (0, 0))],
        out_specs=[tok, tok, tok, pl.BlockSpec((ne, LANES), lambda i: (0, 0))],
        out_shape=[jax.ShapeDtypeStruct((SUBLANES, t), I32), jax.ShapeDtypeStruct((SUBLANES, t), F32),
                   jax.ShapeDtypeStruct((SUBLANES, t), I32), jax.ShapeDtypeStruct((ne, LANES), I32)],
        scratch_shapes=[pltpu.VMEM((ne, 1), F32)],
        compiler_params=_cparams(("arbitrary",)),
        name="route",
    )(logits_t, e_bias.reshape(ne, 1), ut)


DISPATCH_TILE = 128


def _dispatch_kernel(tt, dest_ref, h_ref, xs_in_ref, xs_ref, idx_smem, isem, sem):
    del xs_in_ref
    i = pl.program_id(0)
    n = tt * TOP_K
    icp = pltpu.make_async_copy(dest_ref.at[pl.ds(i * n, n)], idx_smem, isem)
    icp.start()
    icp.wait()

    def row_copy(k):
        return pltpu.make_async_copy(h_ref.at[i * tt + k % tt], xs_ref.at[idx_smem[k]], sem)

    def issue(k, c):
        row_copy(k).start()
        return c

    def drain(k, c):
        row_copy(k).wait()
        return c

    lax.fori_loop(0, n, issue, 0)
    lax.fori_loop(0, n, drain, 0)


def _dispatch(dest_tiles, h3, p_rows):
    t = h3.shape[0]
    tt = DISPATCH_TILE
    xs0 = jnp.zeros((p_rows,) + h3.shape[1:], h3.dtype)
    anyspec = pl.BlockSpec(memory_space=pl.ANY)
    return pl.pallas_call(
        functools.partial(_dispatch_kernel, tt),
        grid=(t // tt,),
        in_specs=[anyspec, anyspec, anyspec],
        out_specs=anyspec,
        out_shape=jax.ShapeDtypeStruct(xs0.shape, xs0.dtype),
        scratch_shapes=[pltpu.SMEM((tt * TOP_K,), I32), pltpu.SemaphoreType.DMA(()),
                        pltpu.SemaphoreType.DMA(())],
        input_output_aliases={2: 0},
        compiler_params=_cparams(("arbitrary",)),
        name="dispatch",
    )(dest_tiles, h3, xs0)


def _experts_kernel(be_ref, nu_ref, x_ref, wg_ref, wu_ref, wd_ref, y_ref):
    del be_ref
    b = pl.program_id(0)

    @pl.when(b < nu_ref[0])
    def _():
        x = x_ref[...]
        gate = jnp.dot(x, wg_ref[0], preferred_element_type=F32)
        up = jnp.dot(x, wu_ref[0], preferred_element_type=F32)
        y_ref[...] = _dot(_silu(gate) * up, wd_ref[0])

    @pl.when(b >= nu_ref[0])
    def _():
        y_ref[...] = jnp.zeros_like(y_ref)


def _experts(block_e, n_used, xs, wg, wu, wd):
    p_rows, d = xs.shape
    de = wg.shape[2]
    nblk = p_rows // EXPERT_BLOCK

    def live(b, nu):
        return jnp.minimum(b, jnp.maximum(nu[0] - 1, 0))

    grid_spec = pltpu.PrefetchScalarGridSpec(
        num_scalar_prefetch=2,
        grid=(nblk,),
        in_specs=[pl.BlockSpec((EXPERT_BLOCK, d), lambda b, be, nu: (live(b, nu), 0)),
                  pl.BlockSpec((1, d, de), lambda b, be, nu: (be[live(b, nu)], 0, 0)),
                  pl.BlockSpec((1, d, de), lambda b, be, nu: (be[live(b, nu)], 0, 0)),
                  pl.BlockSpec((1, de, d), lambda b, be, nu: (be[live(b, nu)], 0, 0))],
        out_specs=pl.BlockSpec((EXPERT_BLOCK, d), lambda b, be, nu: (b, 0)),
    )
    return pl.pallas_call(
        _experts_kernel,
        grid_spec=grid_spec,
        out_shape=jax.ShapeDtypeStruct((p_rows, d), F32),
        compiler_params=_cparams(("arbitrary",)),
        name="experts",
    )(block_e, n_used, xs, wg, wu, wd)


def _combine_kernel(tt, dest_ref, y_ref, x1_ref, h2_ref, w_ref, mod_ref, npf_ref, sg_ref, su_ref, sd_ref,
                    o_ref, ybuf, idx_smem, isem, sem):
    i = pl.program_id(0)
    n = tt * TOP_K
    icp = pltpu.make_async_copy(dest_ref.at[pl.ds(i * n, n)], idx_smem, isem)
    icp.start()
    icp.wait()

    def row_copy(k):
        return pltpu.make_async_copy(y_ref.at[idx_smem[k]], ybuf.at[k], sem)

    def issue(k, c):
        row_copy(k).start()
        return c

    def drain(k, c):
        row_copy(k).wait()
        return c

    lax.fori_loop(0, n, issue, 0)
    h2 = h2_ref[...]
    gate = jnp.dot(h2, sg_ref[...], preferred_element_type=F32)
    up = jnp.dot(h2, su_ref[...], preferred_element_type=F32)
    shared = _dot(_silu(gate) * up, sd_ref[...])
    lax.fori_loop(0, n, drain, 0)
    w = w_ref[...]
    routed = w[:, 0:1] * ybuf[0:tt, :]
    for j in range(1, TOP_K):
        routed = routed + w[:, j:j + 1] * ybuf[j * tt:(j + 1) * tt, :]
    m6 = mod_ref[0]
    o_ref[...] = x1_ref[...] + m6[5:6] * _rms_rows(routed + shared, npf_ref[...])


def _combine(dest_tiles, y, x1, h2, wsel_t, mod, seq, npf, sg, su, sd):
    t, d = x1.shape
    tt = DISPATCH_TILE
    anyspec = pl.BlockSpec(memory_space=pl.ANY)

    def full(a):
        return pl.BlockSpec(a.shape, lambda i: (0,) * a.ndim)

    return pl.pallas_call(
        functools.partial(_combine_kernel, tt),
        grid=(t // tt,),
        in_specs=[anyspec, anyspec,
                  pl.BlockSpec((tt, d), lambda i: (i, 0)),
                  pl.BlockSpec((tt, d), lambda i: (i, 0)),
                  pl.BlockSpec((tt, SUBLANES), lambda i: (i, 0)),
                  pl.BlockSpec((1, 6, d), lambda i: ((i * tt) // seq, 0, 0)),
                  full(npf), full(sg), full(su), full(sd)],
        out_specs=pl.BlockSpec((tt, d), lambda i: (i, 0)),
        out_shape=jax.ShapeDtypeStruct((t, d), F32),
        scratch_shapes=[pltpu.VMEM((tt * TOP_K, d), F32), pltpu.SMEM((tt * TOP_K,), I32),
                        pltpu.SemaphoreType.DMA(()), pltpu.SemaphoreType.DMA(())],
        compiler_params=_cparams(("arbitrary",)),
        name="combine",
    )(dest_tiles, y, x1, h2, wsel_t, mod, npf, sg, su, sd)


def _moe_plan(eidx, rank, cnt, t):
    counts = cnt[:, 0]
    padded = (counts + EXPERT_BLOCK - 1) // EXPERT_BLOCK * EXPERT_BLOCK
    pad_end = jnp.cumsum(padded)
    pad_start = pad_end - padded
    n_blocks = (t * TOP_K + EXPERT_BLOCK - 1) // EXPERT_BLOCK + N_EXPERTS
    block_e = jnp.minimum(jnp.searchsorted(pad_end, jnp.arange(n_blocks, dtype=I32) * EXPERT_BLOCK,
                                           side='right'), N_EXPERTS - 1).astype(I32)
    n_used = (pad_end[-1:] // EXPERT_BLOCK).astype(I32)
    dest = jnp.take(pad_start, eidx[:TOP_K], axis=0).astype(I32) + rank[:TOP_K]
    tt = DISPATCH_TILE
    dest_tiles = dest.reshape(TOP_K, t // tt, tt).transpose(1, 0, 2).reshape(-1)
    return block_e, n_used, dest_tiles, n_blocks * EXPERT_BLOCK


def _trunk(x, mod, wc, oc, ec, norm_pre_mix):
    nb, seq, d = x.shape
    t = nb * seq
    x2 = x.reshape(t, d)
    mx = _mixer(x2, nb, seq, mod, norm_pre_mix, wc)
    x1, h2, logits_t = _outproj(x2, seq, mx, mx['z'], mod, wc, oc)
    eidx, wsel, rank, cnt = _route(logits_t, ec['e_bias'])
    block_e, n_used, dest_tiles, p_rows = _moe_plan(eidx, rank, cnt, t)
    xs = _dispatch(dest_tiles, h2.reshape(t, d // LANES, LANES), p_rows)
    y = _experts(block_e, n_used, xs.reshape(p_rows, d), ec['wg'], ec['wu'], ec['wd'])
    out = _combine(dest_tiles, y, x1, h2, wsel.T, mod, seq, oc['npo'], ec['sg'], ec['su'], ec['sd'])
    return out.reshape(nb, seq, d)


def kernel(x_prompt, x_sample, c_prompt, c_sample, w_ada, b_ada, norm_pre_mix, norm_post_mix, norm_pre_ffn, norm_post_ffn, w_in, rw_mu, rw_w0, rw_w_up, rw_a0, rw_a_up, rw_g_up, rw_k_k, rw_k_a, rw_r_k, rw_ln_w, rw_ln_b, hg_lb_gamma, hg_norm_w, w_out, w_router, e_bias, w_exp_gate, w_exp_up, w_exp_down, w_sh_gate, w_sh_up, w_sh_down):
    d = x_prompt.shape[-1]
    wc = _layer_consts(w_in, rw_mu, rw_w0, rw_w_up, rw_a0, rw_a_up, rw_g_up, rw_k_k, rw_k_a, rw_r_k,
                       hg_lb_gamma)
    rw, hw = wc['rw'], wc['hw']
    e128, e128t = _indicator(hw, hw // HG_HEADS)
    wr_hi, wr_lo = _split2(w_router[0].T)
    oc = dict(ln_w=rw_ln_w[0].reshape(1, rw), ln_b=rw_ln_b[0].reshape(1, rw),
              hg_norm_w=hg_norm_w[0].reshape(1, hw), npm=norm_post_mix[0].reshape(1, d),
              npf=norm_pre_ffn[0].reshape(1, d), npo=norm_post_ffn[0].reshape(1, d),
              w_out=w_out[0].astype(BF16), wr_hi=wr_hi, wr_lo=wr_lo, e128=e128, e128t=e128t)
    ec = dict(e_bias=e_bias[0], wg=w_exp_gate[0].astype(BF16), wu=w_exp_up[0].astype(BF16),
              wd=w_exp_down[0].astype(BF16), sg=w_sh_gate[0].astype(BF16), su=w_sh_up[0].astype(BF16),
              sd=w_sh_down[0].astype(BF16))
    nbp = c_prompt.shape[0]
    mod = _ada(jnp.concatenate([c_prompt, c_sample], axis=0), w_ada[0], b_ada[0]).reshape(-1, 6, d)
    y_prompt = _trunk(x_prompt, mod[:nbp], wc, oc, ec, norm_pre_mix[0])
    y_sample = _trunk(x_sample, mod[nbp:], wc, oc, ec, norm_pre_mix[0])
    return (y_prompt, y_sample)
```

```python
import functools
import math

import jax
import jax.numpy as jnp
from jax import lax
from jax.experimental import pallas as pl
from jax.experimental.pallas import tpu as pltpu

F32 = jnp.float32
BF16 = jnp.bfloat16
I32 = jnp.int32

RW_HEAD_DIM = 64
W_LORA = 64
A_LORA = 64
G_LORA = 128
RW_GN_EPS = 64e-5
HG_HEADS = 8
N_EXPERTS = 64
TOP_K = 6
N_GROUPS = 8
TOPK_GROUPS = 4
ROUTED_SCALE = 2.5
EXPERT_BLOCK = 256
NORM_EPS = 1e-6

LANES = 128
SUBLANES = 8
VMEM_LIMIT = 56 * 1024 * 1024

CHUNK = 64
NEG_BIG = -1e30


def _cparams(sem, vmem=VMEM_LIMIT):
    return pltpu.CompilerParams(dimension_semantics=sem, vmem_limit_bytes=vmem)


def _sigmoid(x):
    return 1.0 / (1.0 + jnp.exp(-x))


def _silu(x):
    return x * _sigmoid(x)


def _dot(a, b):
    return jnp.dot(a.astype(BF16), b.astype(BF16), preferred_element_type=F32)


def _dot_nt(a, b):
    return lax.dot_general(a.astype(BF16), b.astype(BF16), (((1,), (1,)), ((), ())),
                           preferred_element_type=F32)


def _dot_tn(a, b):
    return lax.dot_general(a.astype(BF16), b.astype(BF16), (((0,), (0,)), ((), ())),
                           preferred_element_type=F32)


def _split2(x):
    hi = x.astype(BF16)
    lo = (x - hi.astype(F32)).astype(BF16)
    return hi, lo


def _split3(x):
    hi = x.astype(BF16)
    r1 = x - hi.astype(F32)
    mid = r1.astype(BF16)
    lo = (r1 - mid.astype(F32)).astype(BF16)
    return hi, mid, lo


def _seg_sum(x, e, et):
    hi, lo = _split2(x)
    s = jnp.dot(hi, e, preferred_element_type=F32) + jnp.dot(lo, e, preferred_element_type=F32)
    shi, slo = _split2(s)
    return jnp.dot(shi, et, preferred_element_type=F32) + jnp.dot(slo, et, preferred_element_type=F32)


PACK_W = 2 * LANES


def _pack_rows(x_bf, out_ref):
    for s in range(x_bf.shape[1] // PACK_W):
        a = lax.bitcast_convert_type(x_bf[:, s * PACK_W:s * PACK_W + LANES].astype(F32), jnp.uint32)
        b = lax.bitcast_convert_type(x_bf[:, s * PACK_W + LANES:(s + 1) * PACK_W].astype(F32), jnp.uint32)
        out_ref[:, s, :] = a | (b >> 16)


def _unpack_rows(x_ref):
    parts = []
    for s in range(x_ref.shape[1]):
        w = x_ref[:, s, :]
        parts.append(lax.bitcast_convert_type(w & jnp.uint32(0xFFFF0000), F32).astype(BF16))
        parts.append(lax.bitcast_convert_type(w << 16, F32).astype(BF16))
    return jnp.concatenate(parts, axis=1)


def _rms_rows(x, g):
    return x * lax.rsqrt(jnp.mean(x * x, axis=-1, keepdims=True) + NORM_EPS) * g


def _ada_kernel(c_ref, w_ref, b_ref, o_ref):
    c = c_ref[...]
    o_ref[...] = _dot(_silu(c), w_ref[...]) + b_ref[...]


def _ada(c, w_ada, b_ada):
    nb, d = c.shape
    n = w_ada.shape[1]
    tn = 512
    return pl.pallas_call(
        _ada_kernel,
        grid=(n // tn,),
        in_specs=[pl.BlockSpec((nb, d), lambda j: (0, 0)),
                  pl.BlockSpec((d, tn), lambda j: (0, j)),
                  pl.BlockSpec((1, tn), lambda j: (0, j))],
        out_specs=pl.BlockSpec((nb, tn), lambda j: (0, j)),
        out_shape=jax.ShapeDtypeStruct((nb, n), F32),
        compiler_params=_cparams(("parallel",)),
        name="ada",
    )(c, w_ada, b_ada.reshape(1, n))


def _inproj_kernel(x_ref, mod_ref, g_ref, w_ref, o_ref, h_scr):
    @pl.when(pl.program_id(1) == 0)
    def _():
        m = mod_ref[0]
        h = _rms_rows(x_ref[...], g_ref[...]) * (1.0 + m[1:2]) + m[0:1]
        h_scr[...] = h.astype(BF16)

    o_ref[...] = jnp.dot(h_scr[...], w_ref[...], preferred_element_type=F32)


def _inproj(x2, mod, g, w_bf, seq):
    t, d = x2.shape
    n = w_bf.shape[1]
    tm, tn = 512, 512
    return pl.pallas_call(
        _inproj_kernel,
        grid=(t // tm, n // tn),
        in_specs=[pl.BlockSpec((tm, d), lambda i, j: (i, 0)),
                  pl.BlockSpec((1, 6, d), lambda i, j: ((i * tm) // seq, 0, 0)),
                  pl.BlockSpec((1, d), lambda i, j: (0, 0)),
                  pl.BlockSpec((d, tn), lambda i, j: (0, j))],
        out_specs=pl.BlockSpec((tm, tn), lambda i, j: (i, j)),
        out_shape=jax.ShapeDtypeStruct((t, n), F32),
        scratch_shapes=[pltpu.VMEM((tm, d), BF16)],
        compiler_params=_cparams(("parallel", "arbitrary")),
        name="inproj",
    )(x2, mod, g.reshape(1, d), w_bf)


def _rwprep_kernel(seq, tm, rw,
                   z_ref, zp_ref, zn_ref, l_ref, lp_ref, ln_ref,
                   mu_ref, mul_ref, wup_ref, aup_ref, gup_ref, w0_ref, a0_ref,
                   kk_ref, ka_ref, rk_ref, e_ref, et_ref,
                   r_o, v_o, kk_o, g_o, bon_o, lw0_o, lw1_o, b0_o, b1_o, kd0_o, kd1_o):
    i = pl.program_id(0)
    first = (i * tm) % seq == 0
    last = ((i + 1) * tm) % seq == 0

    def shifted(cur, prev_blk, next_blk, mu):
        rows = lax.broadcasted_iota(I32, cur.shape, 0)
        prow = jnp.where(first, 0.0, prev_blk[SUBLANES - 1:SUBLANES, :])
        nrow = jnp.where(last, 0.0, next_blk[0:1, :])
        prev = jnp.where(rows == 0, prow, pltpu.roll(cur, 1, axis=0))
        nxt = jnp.where(rows == tm - 1, nrow, pltpu.roll(cur, tm - 1, axis=0))
        return cur + mu * (0.5 * (prev + nxt) - cur)

    lat = shifted(l_ref[...], lp_ref[...], ln_ref[...], mul_ref[...])
    w_lat = lat[:, 0:2 * W_LORA]
    a_lat = lat[:, 2 * W_LORA:2 * W_LORA + 2 * A_LORA]
    g_lat = lat[:, 2 * W_LORA + 2 * A_LORA:2 * W_LORA + 2 * A_LORA + G_LORA]
    w_raw = _dot(jnp.tanh(w_lat), wup_ref[...]) + w0_ref[...]
    a_all = _sigmoid(_dot(a_lat, aup_ref[...]) + a0_ref[...])
    g_o[...] = _dot(_sigmoid(g_lat), gup_ref[...])
    lw = (-math.exp(-0.5)) * _sigmoid(w_raw)
    lw0_o[...] = lw[:, :rw]
    lw1_o[...] = lw[:, rw:]

    r = shifted(z_ref[:, 0:rw], zp_ref[:, 0:rw], zn_ref[:, 0:rw], mu_ref[:, 0:rw])
    k = shifted(z_ref[:, rw:2 * rw], zp_ref[:, rw:2 * rw], zn_ref[:, rw:2 * rw], mu_ref[:, rw:2 * rw])
    v = shifted(z_ref[:, 2 * rw:3 * rw], zp_ref[:, 2 * rw:3 * rw], zn_ref[:, 2 * rw:3 * rw],
                mu_ref[:, 2 * rw:3 * rw])
    r_o[...] = r
    v_o[...] = v
    kk = k * kk_ref[...]
    ss = _seg_sum(kk * kk, e_ref[...], et_ref[...])
    kk = kk / jnp.maximum(jnp.sqrt(ss), 1e-12)
    kk_o[...] = kk
    a0 = a_all[:, :rw]
    a1 = a_all[:, rw:]
    b0_o[...] = kk * a0
    b1_o[...] = kk * a1
    kd0 = k * (1.0 + (a0 - 1.0) * ka_ref[...])
    kd1 = k * (1.0 + (a1 - 1.0) * ka_ref[...])
    kd0_o[...] = kd0
    kd1_o[...] = kd1
    kb = 0.5 * (kd0 + kd1)
    bon_o[...] = _seg_sum(r * kb * rk_ref[...], e_ref[...], et_ref[...]) * v


def _rwprep(z, seq, rw, lat_off, mu_rkv, mu_lat, wup, aup, gup, w0, a0, k_k, k_a, r_k, e64, e64t):
    t = z.shape[0]
    tm = 256
    nlat = 512
    nrkv = 3 * rw
    tb = tm // SUBLANES
    nblk8 = t // SUBLANES
    lat_blk = lat_off // nlat

    def cur(i):
        return (i, 0)

    def prv(i):
        return (jnp.maximum(i * tb - 1, 0), 0)

    def nxt(i):
        return (jnp.minimum((i + 1) * tb, nblk8 - 1), 0)

    def full(shape):
        return pl.BlockSpec(shape, lambda i: (0,) * len(shape))

    out = jax.ShapeDtypeStruct((t, rw), F32)
    ospec = pl.BlockSpec((tm, rw), lambda i: (i, 0))
    return pl.pallas_call(
        functools.partial(_rwprep_kernel, seq, tm, rw),
        grid=(t // tm,),
        in_specs=[pl.BlockSpec((tm, nrkv), cur),
                  pl.BlockSpec((SUBLANES, nrkv), prv),
                  pl.BlockSpec((SUBLANES, nrkv), nxt),
                  pl.BlockSpec((tm, nlat), lambda i: (i, lat_blk)),
                  pl.BlockSpec((SUBLANES, nlat), lambda i: (jnp.maximum(i * tb - 1, 0), lat_blk)),
                  pl.BlockSpec((SUBLANES, nlat), lambda i: (jnp.minimum((i + 1) * tb, nblk8 - 1), lat_blk)),
                  full((1, nrkv)), full((1, nlat)),
                  full(wup.shape), full(aup.shape), full(gup.shape),
                  full((1, 2 * rw)), full((1, 2 * rw)),
                  full((1, rw)), full((1, rw)), full((1, rw)),
                  full(e64.shape), full(e64t.shape)],
        out_specs=[ospec] * 11,
        out_shape=[out] * 11,
        compiler_params=_cparams(("parallel",)),
        name="rwprep",
    )(z, z, z, z, z, z, mu_rkv, mu_lat, wup, aup, gup, w0, a0, k_k, k_a, r_k, e64, e64t)


def _tri(n, rev):
    i = lax.broadcasted_iota(I32, (n, n), 0)
    j = lax.broadcasted_iota(I32, (n, n), 1)
    m = (j >= i) if rev else (j <= i)
    return jnp.where(m, 1.0, 0.0).astype(BF16)


def _rw_streams(streams):
    c = streams[0][0].shape[0]
    hd = RW_HEAD_DIM
    n2 = 2 * c
    ns = len(streams)
    revs = [s[7] for s in streams]
    lane = lax.broadcasted_iota(I32, (c, LANES), 1)
    head_a = lane < hd
    ri = lax.broadcasted_iota(I32, (n2, n2), 0)
    ci = lax.broadcasted_iota(I32, (n2, n2), 1)
    ti = ri % c
    si = ci % c
    same16 = (ri // 16) == (ci // 16)
    same32 = (ri // 32) == (ci // 32)
    mid32 = jnp.logical_and(same32, jnp.logical_not(same16))
    eye = jnp.where(ri == ci, 1.0, 0.0)
    strict = {False: si < ti, True: si > ti}
    incl = {False: si <= ti, True: si >= ti}
    tri = {rev: _tri(c, rev) for rev in set(revs)}
    ei = lax.broadcasted_iota(I32, (LANES, LANES), 0)
    ej = lax.broadcasted_iota(I32, (LANES, LANES), 1)
    eye_k = ei == ej

    def pair(x):
        return jnp.concatenate([jnp.where(head_a, x, 0.0), jnp.where(head_a, 0.0, x)], axis=0)

    cum = []
    for (r, v, kk, lw, b, kd, s_in, rev) in streams:
        hi, lo = _split2(lw)
        cs = jnp.dot(tri[rev], jnp.concatenate([hi, lo], axis=1), preferred_element_type=F32)
        cum.append(cs[:, :LANES] + cs[:, LANES:])
    ops = []
    for (r, v, kk, lw, b, kd, s_in, rev), cm in zip(streams, cum):
        tot = cm[0:1, :] if rev else cm[c - 1:c, :]
        g_inv = jnp.exp(-cm)
        g_tail = jnp.exp(tot - cm)
        ops.append(dict(p2=pair(-kk * jnp.exp(cm - lw)), r2=pair(r * jnp.exp(cm)),
                        bi2=pair(b * g_inv), ki2=pair(kd * g_inv), bt2=pair(b * g_tail),
                        kt2=pair(kd * g_tail), v2=pair(v), g_tot=jnp.exp(tot)))
    gm = [_dot_nt(jnp.concatenate([o['p2'], o['r2']], axis=0), jnp.concatenate([o['bi2'], o['ki2']], axis=0))
          for o in ops]
    a2 = [jnp.where(strict[rev], g[:n2, :n2], 0.0) for g, rev in zip(gm, revs)]
    b2 = [jnp.where(strict[rev], g[:n2, n2:], 0.0) for g, rev in zip(gm, revs)]
    ap2 = [jnp.where(incl[rev], g[n2:, :n2], 0.0) for g, rev in zip(gm, revs)]
    bp2 = [jnp.where(incl[rev], g[n2:, n2:], 0.0) for g, rev in zip(gm, revs)]
    bv = [_dot(x, o['v2']) for x, o in zip(b2, ops)]
    bpv = [_dot(x, o['v2']) for x, o in zip(bp2, ops)]

    x = [jnp.where(same16, a, 0.0) for a in a2]
    tinv = [eye + xi for xi in x]
    for _ in range(3):
        x = [_dot(xi, xi) for xi in x]
        tinv = [t + _dot(t, xi) for t, xi in zip(tinv, x)]
    for lvl in (mid32, jnp.logical_not(same32)):
        y = [_dot(t, jnp.where(lvl, a, 0.0)) for t, a in zip(tinv, a2)]
        tinv = [t + _dot(yi, t) for t, yi in zip(tinv, y)]

    wu = [_dot(t, jnp.concatenate([o['p2'], bvi], axis=1)) for t, o, bvi in zip(tinv, ops, bv)]
    qo = [_dot(a, w) for a, w in zip(ap2, wu)]
    m2 = [_dot_tn(w[:, :LANES], o['bt2']) + jnp.where(eye_k, jnp.broadcast_to(o['g_tot'], (LANES, LANES)), 0.0)
          for w, o in zip(wu, ops)]
    nn2 = [_dot_tn(w[:, LANES:], o['bt2']) + _dot_tn(o['v2'], o['kt2']) for w, o in zip(wu, ops)]

    nt = (((1,), (1,)), ((), ()))
    outs = []
    for i in range(ns):
        s_hi, s_lo = _split2(streams[i][6])
        q2b = (ops[i]['r2'] + qo[i][:, :LANES]).astype(BF16)
        oo = (lax.dot_general(q2b, s_hi, nt, preferred_element_type=F32)
              + lax.dot_general(q2b, s_lo, nt, preferred_element_type=F32)
              + qo[i][:, LANES:] + bpv[i])
        m2b = m2[i].astype(BF16)
        s_out = (jnp.dot(s_hi, m2b, preferred_element_type=F32)
                 + jnp.dot(s_lo, m2b, preferred_element_type=F32) + nn2[i])
        outs.append((oo[:c] + oo[c:], s_out))
    return outs


RW_PAIRS_PER_STEP = 2


def _rwscan_kernel(rf, vf, kkf, lwf, bf, kdf, rb, vb, kkb, lwb, bb, kdb, of_ref, ob_ref, sf, sb):
    @pl.when(pl.program_id(2) == 0)
    def _():
        sf[...] = jnp.zeros_like(sf)
        sb[...] = jnp.zeros_like(sb)

    streams = []
    for p in range(RW_PAIRS_PER_STEP):
        sl = slice(p * LANES, (p + 1) * LANES)
        streams.append((rf[:, sl], vf[:, sl], kkf[:, sl], lwf[:, sl], bf[:, sl], kdf[:, sl], sf[p], False))
        streams.append((rb[:, sl], vb[:, sl], kkb[:, sl], lwb[:, sl], bb[:, sl], kdb[:, sl], sb[p], True))
    outs = _rw_streams(streams)
    for p in range(RW_PAIRS_PER_STEP):
        sl = slice(p * LANES, (p + 1) * LANES)
        of_ref[:, sl], sf[p] = outs[2 * p]
        ob_ref[:, sl], sb[p] = outs[2 * p + 1]


def _rwscan(nb, seq, r, v, kk, lw0, lw1, b0, b1, kd0, kd1):
    t, rw = r.shape
    nc = seq // CHUNK
    wblk = RW_PAIRS_PER_STEP * LANES
    fw = pl.BlockSpec((CHUNK, wblk), lambda bi, hp, c: (bi * nc + c, hp))
    bw = pl.BlockSpec((CHUNK, wblk), lambda bi, hp, c: (bi * nc + nc - 1 - c, hp))
    out = jax.ShapeDtypeStruct((t, rw), F32)
    state = pltpu.VMEM((RW_PAIRS_PER_STEP, LANES, LANES), F32)
    return pl.pallas_call(
        _rwscan_kernel,
        grid=(nb, rw // wblk, nc),
        in_specs=[fw] * 6 + [bw] * 6,
        out_specs=[fw, bw],
        out_shape=[out, out],
        scratch_shapes=[state, state],
        compiler_params=_cparams(("parallel", "parallel", "arbitrary")),
        name="rwscan",
    )(r, v, kk, lw0, b0, kd0, r, v, kk, lw1, b1, kd1)


def _hg_streams(streams):
    c, dk = streams[0][0].shape
    revs = [s[5] for s in streams]
    tri = {rev: _tri(c, rev) for rev in set(revs)}
    row = lax.broadcasted_iota(I32, (c, dk), 0)
    ri = lax.broadcasted_iota(I32, (c, c), 0)
    ci = lax.broadcasted_iota(I32, (c, c), 1)

    cum = []
    for (q, k, v, lf, st, rev) in streams:
        hi, mid, lo = _split3(lf)
        cs = jnp.dot(tri[rev], jnp.concatenate([hi, mid, lo], axis=1), preferred_element_type=F32)
        cum.append(cs[:, :dk] + cs[:, dk:2 * dk] + cs[:, 2 * dk:])
    scores = [jnp.where(ri == ci, _dot_nt(s[0], s[1]), 0.0) for s in streams]
    rolled = [dict() for _ in streams]

    def rolled_cum(i, d):
        if d not in rolled[i]:
            rolled[i][d] = pltpu.roll(cum[i], d % c, axis=0)
        return rolled[i][d]

    h = c // 2
    while h >= 1:
        blk = 2 * h
        pos = row % blk
        upper = pos >= h
        same_blk = (ri // blk) == (ci // blk)
        sl = []
        for i, (q, k, v, lf, st, rev) in enumerate(streams):
            cm = cum[i]
            if h >= SUBLANES:
                pieces = []
                for m0 in range(0, c, blk):
                    idx = m0 + h if rev else m0 + h - 1
                    pieces.append(jnp.broadcast_to(cm[idx:idx + 1, :], (blk, dk)))
                ref = jnp.concatenate(pieces, axis=0) if len(pieces) > 1 else pieces[0]
            else:
                off = h if rev else h - 1
                ref = jnp.zeros_like(cm)
                for p in range(blk):
                    ref = jnp.where(pos == p, rolled_cum(i, p - off), ref)
            q_rows = jnp.logical_not(upper) if rev else upper
            eq = jnp.where(q_rows, jnp.minimum(cm - ref, 0.0), NEG_BIG)
            ek = jnp.where(q_rows, NEG_BIG, jnp.minimum(ref - cm, 0.0))
            sl.append(_dot_nt(q * jnp.exp(eq), k * jnp.exp(ek)))
        scores = [sc + jnp.where(same_blk, x, 0.0) for sc, x in zip(scores, sl)]
        h //= 2

    outs = []
    for (q, k, v, lf, st, rev), cm, sc in zip(streams, cum, scores):
        tot = cm[0:1, :] if rev else cm[c - 1:c, :]
        o = _dot(sc, v) + _dot_nt(q * jnp.exp(cm), st)
        st_new = st * jnp.exp(tot) + _dot_tn(v, k * jnp.exp(tot - cm))
        outs.append((o, st_new))
    return outs


HG_HEADS_PER_STEP = 2


def _hgscan_kernel(dk, qf, fff, i_f, qb, ffb, i_b, lb_ref, of_ref, ob_ref, sf, sb):
    @pl.when(pl.program_id(2) == 0)
    def _():
        sf[...] = jnp.zeros_like(sf)
        sb[...] = jnp.zeros_like(sb)

    streams = []
    for p in range(HG_HEADS_PER_STEP):
        sl = slice(p * dk, (p + 1) * dk)
        for (q_ref, ff_ref, i_ref, st_ref, d) in ((qf, fff, i_f, sf, 0), (qb, ffb, i_b, sb, 1)):
            lbv = lb_ref[d:d + 1, sl]
            f = lbv + (1.0 - lbv) * _sigmoid(ff_ref[:, sl])
            streams.append((_silu(q_ref[:, sl]), 1.0 - f, i_ref[:, sl], jnp.log(f), st_ref[p], d == 1))
    outs = _hg_streams(streams)
    for p in range(HG_HEADS_PER_STEP):
        sl = slice(p * dk, (p + 1) * dk)
        of_ref[:, sl], sf[p] = outs[2 * p]
        ob_ref[:, sl], sb[p] = outs[2 * p + 1]


def _hgscan(nb, seq, z, lb, hg_off, hw):
    t = z.shape[0]
    nc = seq // CHUNK
    dk = hw // HG_HEADS
    wblk = HG_HEADS_PER_STEP * dk
    base = hg_off // wblk
    nh = hw // wblk

    def fw(comp):
        return pl.BlockSpec((CHUNK, wblk), lambda bi, h, c: (bi * nc + c, base + comp * nh + h))

    def bw(comp):
        return pl.BlockSpec((CHUNK, wblk), lambda bi, h, c: (bi * nc + nc - 1 - c, base + comp * nh + h))

    out = jax.ShapeDtypeStruct((t, hw), F32)
    state = pltpu.VMEM((HG_HEADS_PER_STEP, dk, dk), F32)
    return pl.pallas_call(
        functools.partial(_hgscan_kernel, dk),
        grid=(nb, nh, nc),
        in_specs=[fw(0), fw(1), fw(3), bw(0), bw(2), bw(3),
                  pl.BlockSpec((2, wblk), lambda bi, h, c: (0, h))],
        out_specs=[pl.BlockSpec((CHUNK, wblk), lambda bi, h, c: (bi * nc + c, h)),
                   pl.BlockSpec((CHUNK, wblk), lambda bi, h, c: (bi * nc + nc - 1 - c, h))],
        out_shape=[out, out],
        scratch_shapes=[state, state],
        compiler_params=_cparams(("parallel", "parallel", "arbitrary")),
        name="hgscan",
    )(z, z, z, z, z, z, lb)


def _blockdiag2(w):
    _, r, n = w.shape
    z = jnp.zeros((r, n), w.dtype)
    return jnp.concatenate([jnp.concatenate([w[0], z], axis=1), jnp.concatenate([z, w[1]], axis=1)], axis=0)


def _indicator(width, seg):
    e = (jnp.arange(width)[:, None] // seg == jnp.arange(width // seg)[None, :]).astype(BF16)
    return e, e.T


def _layer_consts(w_in, rw_mu, rw_w0, rw_w_up, rw_a0, rw_a_up, rw_g_up, rw_k_k, rw_k_a, rw_r_k,
                  hg_lb_gamma):
    rw = rw_k_k.shape[-1]
    d = w_in.shape[1]
    nlat = 2 * W_LORA + 2 * A_LORA + G_LORA
    w = w_in[0]
    rkv = 3 * rw
    hg_cols = w.shape[1] - rkv - nlat
    pad = 512 - nlat
    w_perm = jnp.concatenate([w[:, :rkv], w[:, rkv + nlat:], w[:, rkv:rkv + nlat],
                              jnp.zeros((d, pad), w.dtype)], axis=1).astype(BF16)
    mu = rw_mu[0]
    lower = jnp.cumsum(jax.nn.softmax(hg_lb_gamma.astype(F32), axis=0), axis=0)[0]
    hw = lower.shape[-1]
    e64, e64t = _indicator(rw, RW_HEAD_DIM)
    return dict(
        rw=rw, hw=hw, hg_off=rkv, lat_off=rkv + hg_cols, w_in=w_perm,
        mu_rkv=mu[:rkv].reshape(1, rkv),
        mu_lat=jnp.pad(mu[rkv:rkv + nlat], (0, pad)).reshape(1, 512),
        wup=_blockdiag2(rw_w_up[0]).astype(BF16), aup=_blockdiag2(rw_a_up[0]).astype(BF16),
        gup=rw_g_up[0].astype(BF16),
        w0=rw_w0[0].reshape(1, 2 * rw), a0=rw_a0[0].reshape(1, 2 * rw),
        k_k=rw_k_k[0].reshape(1, rw), k_a=rw_k_a[0].reshape(1, rw), r_k=rw_r_k[0].reshape(1, rw),
        e64=e64, e64t=e64t,
        lb=lower,
    )


def _mixer(x2, nb, seq, mod, norm_pre_mix, wc):
    z = _inproj(x2, mod, norm_pre_mix, wc['w_in'], seq)
    (r, v, kk, g, bonus, lw0, lw1, b0, b1, kd0, kd1) = _rwprep(
        z, seq, wc['rw'], wc['lat_off'], wc['mu_rkv'], wc['mu_lat'], wc['wup'], wc['aup'], wc['gup'],
        wc['w0'], wc['a0'], wc['k_k'], wc['k_a'], wc['r_k'], wc['e64'], wc['e64t'])
    rw_of, rw_ob = _rwscan(nb, seq, r, v, kk, lw0, lw1, b0, b1, kd0, kd1)
    hg_of, hg_ob = _hgscan(nb, seq, z, wc['lb'], wc['hg_off'], wc['hw'])
    return dict(z=z, r=r, v=v, kk=kk, g=g, bonus=bonus, lw0=lw0, rw_of=rw_of, rw_ob=rw_ob,
                hg_of=hg_of, hg_ob=hg_ob)


def _outproj_kernel(rw, x_ref, rf_ref, rb_ref, bon_ref, g_ref, hf_ref, hb_ref, hgg_ref, mod_ref,
                    lnw_ref, lnb_ref, hnw_ref, npm_ref, npf_ref, wout_ref, wrh_ref, wrl_ref,
                    e64_ref, e64t_ref, e128_ref, e128t_ref,
                    x1_ref, h2_ref, lg_ref):
    m6 = mod_ref[0]
    o = rf_ref[...] + rb_ref[...]
    mean = _seg_sum(o, e64_ref[...], e64t_ref[...]) * (1.0 / RW_HEAD_DIM)
    dlt = o - mean
    var = _seg_sum(dlt * dlt, e64_ref[...], e64t_ref[...]) * (1.0 / RW_HEAD_DIM)
    o_rw = (dlt * lax.rsqrt(var + RW_GN_EPS) * lnw_ref[...] + lnb_ref[...] + bon_ref[...]) * g_ref[...]
    oh = hf_ref[...] + hb_ref[...]
    hd = oh.shape[1] // HG_HEADS
    ms = _seg_sum(oh * oh, e128_ref[...], e128t_ref[...]) * (1.0 / hd)
    o_hg = oh * lax.rsqrt(ms + NORM_EPS) * hnw_ref[...] * _silu(hgg_ref[...])
    m = _dot(o_rw, wout_ref[0:rw, :]) + _dot(o_hg, wout_ref[rw:, :])
    x1 = x_ref[...] + m6[2:3] * _rms_rows(m, npm_ref[...])
    x1_ref[...] = x1
    h2 = _rms_rows(x1, npf_ref[...]) * (1.0 + m6[4:5]) + m6[3:4]
    hi, lo = _split2(h2)
    _pack_rows(hi, h2_ref)
    nt = (((1,), (1,)), ((), ()))
    lg_ref[...] = (lax.dot_general(wrh_ref[...], hi, nt, preferred_element_type=F32)
                   + lax.dot_general(wrh_ref[...], lo, nt, preferred_element_type=F32)
                   + lax.dot_general(wrl_ref[...], hi, nt, preferred_element_type=F32))


def _outproj(x2, seq, mx, z, mod, wc, oc):
    t, d = x2.shape
    rw, hw = wc['rw'], wc['hw']
    tm = 256
    gblk = (wc['hg_off'] + 4 * hw) // hw

    def row(w):
        return pl.BlockSpec((tm, w), lambda i: (i, 0))

    def full(a):
        return pl.BlockSpec(a.shape, lambda i: (0,) * a.ndim)

    consts = [oc['ln_w'], oc['ln_b'], oc['hg_norm_w'], oc['npm'], oc['npf'], oc['w_out'], oc['wr_hi'],
              oc['wr_lo'], wc['e64'], wc['e64t'], oc['e128'], oc['e128t']]
    return pl.pallas_call(
        functools.partial(_outproj_kernel, rw),
        grid=(t // tm,),
        in_specs=[row(d), row(rw), row(rw), row(rw), row(rw), row(hw), row(hw),
                  pl.BlockSpec((tm, hw), lambda i: (i, gblk)),
                  pl.BlockSpec((1, 6, d), lambda i: ((i * tm) // seq, 0, 0))] + [full(a) for a in consts],
        out_specs=[row(d), pl.BlockSpec((tm, d // PACK_W, LANES), lambda i: (i, 0, 0)),
                   pl.BlockSpec((N_EXPERTS, tm), lambda i: (0, i))],
        out_shape=[jax.ShapeDtypeStruct((t, d), F32), jax.ShapeDtypeStruct((t, d // PACK_W, LANES), jnp.uint32),
                   jax.ShapeDtypeStruct((N_EXPERTS, t), F32)],
        compiler_params=_cparams(("parallel",)),
        name="outproj",
    )(x2, mx['rw_of'], mx['rw_ob'], mx['bonus'], mx['g'], mx['hg_of'], mx['hg_ob'], z, mod, *consts)


ROUTE_TILE = 512


def _route_kernel(lg_ref, bias_ref, ut_ref, eidx_ref, wsel_ref, rank_ref, cnt_ref, carry):
    @pl.when(pl.program_id(0) == 0)
    def _():
        carry[...] = jnp.zeros_like(carry)

    ne, tt = lg_ref.shape
    gsz = ne // N_GROUPS
    neg = -jnp.inf
    s = _sigmoid(lg_ref[...])
    biased = s + bias_ref[...]
    io_g = lax.broadcasted_iota(I32, (gsz, tt), 0)
    gs_rows = []
    for gi in range(N_GROUPS):
        blk = biased[gi * gsz:(gi + 1) * gsz, :]
        m1 = jnp.max(blk, axis=0, keepdims=True)
        first = jnp.min(jnp.where(blk == m1, io_g, gsz), axis=0, keepdims=True)
        m2 = jnp.max(jnp.where(io_g == first, neg, blk), axis=0, keepdims=True)
        gs_rows.append(m1 + m2)
    gs = jnp.concatenate(gs_rows, axis=0)
    io_n = lax.broadcasted_iota(I32, (N_GROUPS, tt), 0)
    selg = jnp.zeros((N_GROUPS, tt), jnp.bool_)
    for _ in range(TOPK_GROUPS):
        m = jnp.max(gs, axis=0, keepdims=True)
        first = jnp.min(jnp.where(gs == m, io_n, N_GROUPS), axis=0, keepdims=True)
        pick = io_n == first
        selg = jnp.logical_or(selg, pick)
        gs = jnp.where(pick, neg, gs)
    emask = jnp.concatenate([jnp.broadcast_to(selg[gi:gi + 1, :], (gsz, tt)) for gi in range(N_GROUPS)],
                            axis=0)
    mb = jnp.where(emask, biased, neg)
    io_e = lax.broadcasted_iota(I32, (ne, tt), 0)
    sel = jnp.zeros((ne, tt), jnp.bool_)
    picks, idxs, ws = [], [], []
    for _ in range(TOP_K):
        m = jnp.max(mb, axis=0, keepdims=True)
        first = jnp.min(jnp.where(mb == m, io_e, ne), axis=0, keepdims=True)
        pick = io_e == first
        picks.append(pick)
        idxs.append(first)
        ws.append(jnp.sum(jnp.where(pick, s, 0.0), axis=0, keepdims=True))
        sel = jnp.logical_or(sel, pick)
        mb = jnp.where(pick, neg, mb)
    wsum = ws[0]
    for w in ws[1:]:
        wsum = wsum + w
    pos = jnp.dot(jnp.where(sel, 1.0, 0.0).astype(BF16), ut_ref[...], preferred_element_type=F32) + carry[...]
    ranks = [jnp.sum(jnp.where(p, pos, 0.0), axis=0, keepdims=True).astype(I32) for p in picks]
    carry[...] = carry[...] + jnp.sum(jnp.where(sel, 1.0, 0.0), axis=1, keepdims=True)
    zi = jnp.zeros((SUBLANES - TOP_K, tt), I32)
    eidx_ref[...] = jnp.concatenate(idxs + [zi], axis=0)
    rank_ref[...] = jnp.concatenate(ranks + [zi], axis=0)
    wsel_ref[...] = jnp.concatenate([w / wsum * ROUTED_SCALE for w in ws] + [zi.astype(F32)], axis=0)
    cnt_ref[...] = jnp.broadcast_to(carry[...], cnt_ref.shape).astype(I32)


def _route(logits_t, e_bias):
    ne, t = logits_t.shape
    tt = ROUTE_TILE
    ut = (jnp.arange(tt)[:, None] < jnp.arange(tt)[None, :]).astype(BF16)
    tok = pl.BlockSpec((SUBLANES, tt), lambda i: (0, i))
    return pl.pallas_call(
        _route_kernel,
        grid=(t // tt,),
        in_specs=[pl.BlockSpec((ne, tt), lambda i: (0, i)),
                  pl.BlockSpec((ne, 1), lambda i: (0, 0)),
                  pl.BlockSpec((tt, tt), lambda i: (0, 0))],
        out_specs=[tok, tok, tok, pl.BlockSpec((ne, LANES), lambda i: (0, 0))],
        out_shape=[jax.ShapeDtypeStruct((SUBLANES, t), I32), jax.ShapeDtypeStruct((SUBLANES, t), F32),
                   jax.ShapeDtypeStruct((SUBLANES, t), I32), jax.ShapeDtypeStruct((ne, LANES), I32)],
        scratch_shapes=[pltpu.VMEM((ne, 1), F32)],
        compiler_params=_cparams(("arbitrary",)),
        name="route",
    )(logits_t, e_bias.reshape(ne, 1), ut)


DISPATCH_TILE = 128
PLAN_TILE = 1024


def _plan_kernel(eidx_ref, rank_ref, ps_ref, dest_ref):
    ne = ps_ref.shape[0]
    tp = eidx_ref.shape[1]
    io_e = lax.broadcasted_iota(I32, (ne, tp), 0)
    ps = ps_ref[...]
    rows = []
    for j in range(TOP_K):
        start = jnp.sum(jnp.where(io_e == eidx_ref[j:j + 1, :], ps, 0.0), axis=0, keepdims=True)
        rows.append(start.astype(I32) + rank_ref[j:j + 1, :])
    dest = jnp.concatenate(rows + [jnp.zeros((SUBLANES - TOP_K, tp), I32)], axis=0)
    for i in range(tp // DISPATCH_TILE):
        dest_ref[i] = dest[:, i * DISPATCH_TILE:(i + 1) * DISPATCH_TILE]


def _plan(eidx, rank, pad_start):
    t = eidx.shape[1]
    tp = PLAN_TILE
    ne = pad_start.shape[0]
    tok = pl.BlockSpec((SUBLANES, tp), lambda i: (0, i))
    ntile = tp // DISPATCH_TILE
    return pl.pallas_call(
        _plan_kernel,
        grid=(t // tp,),
        in_specs=[tok, tok, pl.BlockSpec((ne, 1), lambda i: (0, 0))],
        out_specs=pl.BlockSpec((ntile, SUBLANES, DISPATCH_TILE), lambda i: (i, 0, 0)),
        out_shape=jax.ShapeDtypeStruct((t // DISPATCH_TILE, SUBLANES, DISPATCH_TILE), I32),
        compiler_params=_cparams(("parallel",)),
        name="plan",
    )(eidx, rank, pad_start.astype(F32).reshape(ne, 1))


def _dispatch_kernel(tt, dest_ref, h_ref, xs_in_ref, xs_ref, idx_smem, isem, sem):
    del xs_in_ref
    i = pl.program_id(0)
    n = tt * TOP_K
    icp = pltpu.make_async_copy(dest_ref.at[pl.ds(i * tt * SUBLANES, n)], idx_smem, isem)
    icp.start()
    icp.wait()

    def row_copy(k):
        return pltpu.make_async_copy(h_ref.at[k % tt], xs_ref.at[idx_smem[k]], sem)

    def issue(k, c):
        row_copy(k).start()
        return c

    def drain(k, c):
        row_copy(k).wait()
        return c

    lax.fori_loop(0, n, issue, 0, unroll=8)
    lax.fori_loop(0, n, drain, 0, unroll=8)


def _dispatch(dest_flat, h2p, p_rows):
    t = h2p.shape[0]
    tt = DISPATCH_TILE
    xs0 = jnp.zeros((p_rows,) + h2p.shape[1:], h2p.dtype)
    anyspec = pl.BlockSpec(memory_space=pl.ANY)
    return pl.pallas_call(
        functools.partial(_dispatch_kernel, tt),
        grid=(t // tt,),
        in_specs=[anyspec, pl.BlockSpec((tt,) + h2p.shape[1:], lambda i: (i, 0, 0)), anyspec],
        out_specs=anyspec,
        out_shape=jax.ShapeDtypeStruct(xs0.shape, xs0.dtype),
        scratch_shapes=[pltpu.SMEM((tt * TOP_K,), I32), pltpu.SemaphoreType.DMA(()),
                        pltpu.SemaphoreType.DMA(())],
        input_output_aliases={2: 0},
        compiler_params=_cparams(("arbitrary",)),
        name="dispatch",
    )(dest_flat, h2p, xs0)


def _experts_kernel(be_ref, nu_ref, x_ref, wg_ref, wu_ref, wd_ref, y_ref):
    del be_ref
    b = pl.program_id(0)

    @pl.when(b < nu_ref[0])
    def _():
        x = _unpack_rows(x_ref)
        gate = jnp.dot(x, wg_ref[0], preferred_element_type=F32)
        up = jnp.dot(x, wu_ref[0], preferred_element_type=F32)
        y_ref[...] = _dot(_silu(gate) * up, wd_ref[0])

    @pl.when(b >= nu_ref[0])
    def _():
        y_ref[...] = jnp.zeros_like(y_ref)


def _experts(block_e, n_used, xs, wg, wu, wd):
    p_rows = xs.shape[0]
    d, de = wg.shape[1], wg.shape[2]
    nblk = p_rows // EXPERT_BLOCK

    def live(b, nu):
        return jnp.minimum(b, jnp.maximum(nu[0] - 1, 0))

    grid_spec = pltpu.PrefetchScalarGridSpec(
        num_scalar_prefetch=2,
        grid=(nblk,),
        in_specs=[pl.BlockSpec((EXPERT_BLOCK,) + xs.shape[1:], lambda b, be, nu: (live(b, nu), 0, 0)),
                  pl.BlockSpec((1, d, de), lambda b, be, nu: (be[live(b, nu)], 0, 0)),
                  pl.BlockSpec((1, d, de), lambda b, be, nu: (be[live(b, nu)], 0, 0)),
                  pl.BlockSpec((1, de, d), lambda b, be, nu: (be[live(b, nu)], 0, 0))],
        out_specs=pl.BlockSpec((EXPERT_BLOCK, d), lambda b, be, nu: (b, 0)),
    )
    return pl.pallas_call(
        _experts_kernel,
        grid_spec=grid_spec,
        out_shape=jax.ShapeDtypeStruct((p_rows, d), F32),
        compiler_params=_cparams(("arbitrary",)),
        name="experts",
    )(block_e, n_used, xs, wg, wu, wd)


def _combine_kernel(tt, dest_ref, y_ref, x1_ref, h2_ref, w_ref, mod_ref, npf_ref, sg_ref, su_ref, sd_ref,
                    o_ref, ybuf, idx_smem, isem, sem):
    i = pl.program_id(0)
    n = tt * TOP_K
    icp = pltpu.make_async_copy(dest_ref.at[pl.ds(i * tt * SUBLANES, n)], idx_smem, isem)
    icp.start()
    icp.wait()

    def row_copy(k):
        return pltpu.make_async_copy(y_ref.at[idx_smem[k]], ybuf.at[k], sem)

    def issue(k, c):
        row_copy(k).start()
        return c

    def drain(k, c):
        row_copy(k).wait()
        return c

    lax.fori_loop(0, n, issue, 0, unroll=8)
    h2 = _unpack_rows(h2_ref)
    gate = jnp.dot(h2, sg_ref[...], preferred_element_type=F32)
    up = jnp.dot(h2, su_ref[...], preferred_element_type=F32)
    shared = _dot(_silu(gate) * up, sd_ref[...])
    lax.fori_loop(0, n, drain, 0, unroll=8)
    w = w_ref[...]
    routed = w[:, 0:1] * ybuf[0:tt, :]
    for j in range(1, TOP_K):
        routed = routed + w[:, j:j + 1] * ybuf[j * tt:(j + 1) * tt, :]
    m6 = mod_ref[0]
    o_ref[...] = x1_ref[...] + m6[5:6] * _rms_rows(routed + shared, npf_ref[...])


def _combine(dest_flat, y, x1, h2, wsel_t, mod, seq, npf, sg, su, sd):
    t, d = x1.shape
    tt = DISPATCH_TILE
    anyspec = pl.BlockSpec(memory_space=pl.ANY)

    def full(a):
        return pl.BlockSpec(a.shape, lambda i: (0,) * a.ndim)

    return pl.pallas_call(
        functools.partial(_combine_kernel, tt),
        grid=(t // tt,),
        in_specs=[anyspec, anyspec,
                  pl.BlockSpec((tt, d), lambda i: (i, 0)),
                  pl.BlockSpec((tt,) + h2.shape[1:], lambda i: (i, 0, 0)),
                  pl.BlockSpec((tt, SUBLANES), lambda i: (i, 0)),
                  pl.BlockSpec((1, 6, d), lambda i: ((i * tt) // seq, 0, 0)),
                  full(npf), full(sg), full(su), full(sd)],
        out_specs=pl.BlockSpec((tt, d), lambda i: (i, 0)),
        out_shape=jax.ShapeDtypeStruct((t, d), F32),
        scratch_shapes=[pltpu.VMEM((tt * TOP_K, d), F32), pltpu.SMEM((tt * TOP_K,), I32),
                        pltpu.SemaphoreType.DMA(()), pltpu.SemaphoreType.DMA(())],
        compiler_params=_cparams(("arbitrary",)),
        name="combine",
    )(dest_flat, y, x1, h2, wsel_t, mod, npf, sg, su, sd)


def _moe_plan(eidx, rank, cnt, t):
    counts = cnt[:, 0]
    padded = (counts + EXPERT_BLOCK - 1) // EXPERT_BLOCK * EXPERT_BLOCK
    pad_end = jnp.cumsum(padded)
    pad_start = pad_end - padded
    n_blocks = (t * TOP_K + EXPERT_BLOCK - 1) // EXPERT_BLOCK + N_EXPERTS
    first_row = jnp.arange(n_blocks, dtype=I32) * EXPERT_BLOCK
    block_e = jnp.minimum(jnp.sum((pad_end[None, :] <= first_row[:, None]).astype(I32), axis=1),
                          N_EXPERTS - 1).astype(I32)
    n_used = (pad_end[-1:] // EXPERT_BLOCK).astype(I32)
    dest_flat = _plan(eidx, rank, pad_start).reshape(-1)
    return block_e, n_used, dest_flat, n_blocks * EXPERT_BLOCK


def _trunk(x, mod, wc, oc, ec, norm_pre_mix):
    nb, seq, d = x.shape
    t = nb * seq
    x2 = x.reshape(t, d)
    mx = _mixer(x2, nb, seq, mod, norm_pre_mix, wc)
    x1, h2, logits_t = _outproj(x2, seq, mx, mx['z'], mod, wc, oc)
    eidx, wsel, rank, cnt = _route(logits_t, ec['e_bias'])
    block_e, n_used, dest_flat, p_rows = _moe_plan(eidx, rank, cnt, t)
    xs = _dispatch(dest_flat, h2, p_rows)
    y = _experts(block_e, n_used, xs, ec['wg'], ec['wu'], ec['wd'])
    out = _combine(dest_flat, y, x1, h2, wsel.T, mod, seq, oc['npo'], ec['sg'], ec['su'], ec['sd'])
    return out.reshape(nb, seq, d)


def kernel(x_prompt, x_sample, c_prompt, c_sample, w_ada, b_ada, norm_pre_mix, norm_post_mix, norm_pre_ffn, norm_post_ffn, w_in, rw_mu, rw_w0, rw_w_up, rw_a0, rw_a_up, rw_g_up, rw_k_k, rw_k_a, rw_r_k, rw_ln_w, rw_ln_b, hg_lb_gamma, hg_norm_w, w_out, w_router, e_bias, w_exp_gate, w_exp_up, w_exp_down, w_sh_gate, w_sh_up, w_sh_down):
    d = x_prompt.shape[-1]
    wc = _layer_consts(w_in, rw_mu, rw_w0, rw_w_up, rw_a0, rw_a_up, rw_g_up, rw_k_k, rw_k_a, rw_r_k,
                       hg_lb_gamma)
    rw, hw = wc['rw'], wc['hw']
    e128, e128t = _indicator(hw, hw // HG_HEADS)
    wr_hi, wr_lo = _split2(w_router[0].T)
    oc = dict(ln_w=rw_ln_w[0].reshape(1, rw), ln_b=rw_ln_b[0].reshape(1, rw),
              hg_norm_w=hg_norm_w[0].reshape(1, hw), npm=norm_post_mix[0].reshape(1, d),
              npf=norm_pre_ffn[0].reshape(1, d), npo=norm_post_ffn[0].reshape(1, d),
              w_out=w_out[0].astype(BF16), wr_hi=wr_hi, wr_lo=wr_lo, e128=e128, e128t=e128t)
    ec = dict(e_bias=e_bias[0], wg=w_exp_gate[0].astype(BF16), wu=w_exp_up[0].astype(BF16),
              wd=w_exp_down[0].astype(BF16), sg=w_sh_gate[0].astype(BF16), su=w_sh_up[0].astype(BF16),
              sd=w_sh_down[0].astype(BF16))
    nbp = c_prompt.shape[0]
    mod = _ada(jnp.concatenate([c_prompt, c_sample], axis=0), w_ada[0], b_ada[0]).reshape(-1, 6, d)
    y_prompt = _trunk(x_prompt, mod[:nbp], wc, oc, ec, norm_pre_mix[0])
    y_sample = _trunk(x_sample, mod[nbp:], wc, oc, ec, norm_pre_mix[0])
    return (y_prompt, y_sample)
```

```python
import functools
import math

import jax
import jax.numpy as jnp
from jax import lax
from jax.experimental import pallas as pl
from jax.experimental.pallas import tpu as pltpu

F32 = jnp.float32
BF16 = jnp.bfloat16
I32 = jnp.int32

RW_HEAD_DIM = 64
W_LORA = 64
A_LORA = 64
G_LORA = 128
RW_GN_EPS = 64e-5
HG_HEADS = 8
N_EXPERTS = 64
TOP_K = 6
N_GROUPS = 8
TOPK_GROUPS = 4
ROUTED_SCALE = 2.5
EXPERT_BLOCK = 256
NORM_EPS = 1e-6

LANES = 128
SUBLANES = 8
VMEM_LIMIT = 56 * 1024 * 1024

CHUNK = 64
NEG_BIG = -1e30


def _cparams(sem, vmem=VMEM_LIMIT):
    return pltpu.CompilerParams(dimension_semantics=sem, vmem_limit_bytes=vmem)


def _sigmoid(x):
    return 1.0 / (1.0 + jnp.exp(-x))


def _silu(x):
    return x * _sigmoid(x)


def _dot(a, b):
    return jnp.dot(a.astype(BF16), b.astype(BF16), preferred_element_type=F32)


def _dot_nt(a, b):
    return lax.dot_general(a.astype(BF16), b.astype(BF16), (((1,), (1,)), ((), ())),
                           preferred_element_type=F32)


def _dot_tn(a, b):
    return lax.dot_general(a.astype(BF16), b.astype(BF16), (((0,), (0,)), ((), ())),
                           preferred_element_type=F32)


def _split2(x):
    hi = x.astype(BF16)
    lo = (x - hi.astype(F32)).astype(BF16)
    return hi, lo


def _split3(x):
    hi = x.astype(BF16)
    r1 = x - hi.astype(F32)
    mid = r1.astype(BF16)
    lo = (r1 - mid.astype(F32)).astype(BF16)
    return hi, mid, lo


def _seg_sum(x, e, et):
    hi, lo = _split2(x)
    s = jnp.dot(hi, e, preferred_element_type=F32) + jnp.dot(lo, e, preferred_element_type=F32)
    shi, slo = _split2(s)
    return jnp.dot(shi, et, preferred_element_type=F32) + jnp.dot(slo, et, preferred_element_type=F32)


PACK_W = 2 * LANES


def _pack_rows(x_bf, out_ref):
    n, d = x_bf.shape
    ns = d // PACK_W
    for s in range(ns):
        a = lax.bitcast_convert_type(x_bf[:, s * PACK_W:s * PACK_W + LANES].astype(F32), jnp.uint32)
        b = lax.bitcast_convert_type(x_bf[:, s * PACK_W + LANES:(s + 1) * PACK_W].astype(F32), jnp.uint32)
        out_ref[pl.ds(s, n, stride=ns), :] = a | (b >> 16)


def _unpack_rows(x_ref, ns):
    n = x_ref.shape[0] // ns
    parts = []
    for s in range(ns):
        w = x_ref[pl.ds(s, n, stride=ns), :]
        parts.append(lax.bitcast_convert_type(w & jnp.uint32(0xFFFF0000), F32).astype(BF16))
        parts.append(lax.bitcast_convert_type(w << 16, F32).astype(BF16))
    return jnp.concatenate(parts, axis=1)


def _rms_rows(x, g):
    return x * lax.rsqrt(jnp.mean(x * x, axis=-1, keepdims=True) + NORM_EPS) * g


def _ada_kernel(c_ref, w_ref, b_ref, o_ref):
    c = c_ref[...]
    o_ref[...] = _dot(_silu(c), w_ref[...]) + b_ref[...]


def _ada(c, w_ada, b_ada):
    nb, d = c.shape
    n = w_ada.shape[1]
    tn = 512
    return pl.pallas_call(
        _ada_kernel,
        grid=(n // tn,),
        in_specs=[pl.BlockSpec((nb, d), lambda j: (0, 0)),
                  pl.BlockSpec((d, tn), lambda j: (0, j)),
                  pl.BlockSpec((1, tn), lambda j: (0, j))],
        out_specs=pl.BlockSpec((nb, tn), lambda j: (0, j)),
        out_shape=jax.ShapeDtypeStruct((nb, n), F32),
        compiler_params=_cparams(("parallel",)),
        name="ada",
    )(c, w_ada, b_ada.reshape(1, n))


def _inproj_kernel(x_ref, mod_ref, g_ref, w_ref, o_ref, h_scr):
    @pl.when(pl.program_id(1) == 0)
    def _():
        m = mod_ref[0]
        h = _rms_rows(x_ref[...], g_ref[...]) * (1.0 + m[1:2]) + m[0:1]
        h_scr[...] = h.astype(BF16)

    o_ref[...] = jnp.dot(h_scr[...], w_ref[...], preferred_element_type=F32)


def _inproj(x2, mod, g, w_bf, seq):
    t, d = x2.shape
    n = w_bf.shape[1]
    tm, tn = 1024, 512
    return pl.pallas_call(
        _inproj_kernel,
        grid=(t // tm, n // tn),
        in_specs=[pl.BlockSpec((tm, d), lambda i, j: (i, 0)),
                  pl.BlockSpec((1, 6, d), lambda i, j: ((i * tm) // seq, 0, 0)),
                  pl.BlockSpec((1, d), lambda i, j: (0, 0)),
                  pl.BlockSpec((d, tn), lambda i, j: (0, j))],
        out_specs=pl.BlockSpec((tm, tn), lambda i, j: (i, j)),
        out_shape=jax.ShapeDtypeStruct((t, n), F32),
        scratch_shapes=[pltpu.VMEM((tm, d), BF16)],
        compiler_params=_cparams(("parallel", "arbitrary")),
        name="inproj",
    )(x2, mod, g.reshape(1, d), w_bf)


def _rwprep_kernel(seq, tm, rw,
                   z_ref, zp_ref, zn_ref, l_ref, lp_ref, ln_ref,
                   mu_ref, mul_ref, wup_ref, aup_ref, gup_ref, w0_ref, a0_ref,
                   kk_ref, ka_ref, rk_ref, e_ref, et_ref,
                   r_o, v_o, kk_o, g_o, bon_o, lw0_o, lw1_o, b0_o, b1_o, kd0_o, kd1_o):
    i = pl.program_id(0)
    first = (i * tm) % seq == 0
    last = ((i + 1) * tm) % seq == 0

    def shifted(cur, prev_blk, next_blk, mu):
        rows = lax.broadcasted_iota(I32, cur.shape, 0)
        prow = jnp.where(first, 0.0, prev_blk[SUBLANES - 1:SUBLANES, :])
        nrow = jnp.where(last, 0.0, next_blk[0:1, :])
        prev = jnp.where(rows == 0, prow, pltpu.roll(cur, 1, axis=0))
        nxt = jnp.where(rows == tm - 1, nrow, pltpu.roll(cur, tm - 1, axis=0))
        return cur + mu * (0.5 * (prev + nxt) - cur)

    lat = shifted(l_ref[...], lp_ref[...], ln_ref[...], mul_ref[...])
    w_lat = lat[:, 0:2 * W_LORA]
    a_lat = lat[:, 2 * W_LORA:2 * W_LORA + 2 * A_LORA]
    g_lat = lat[:, 2 * W_LORA + 2 * A_LORA:2 * W_LORA + 2 * A_LORA + G_LORA]
    w_raw = _dot(jnp.tanh(w_lat), wup_ref[...]) + w0_ref[...]
    a_all = _sigmoid(_dot(a_lat, aup_ref[...]) + a0_ref[...])
    g_o[...] = _dot(_sigmoid(g_lat), gup_ref[...])
    lw = (-math.exp(-0.5)) * _sigmoid(w_raw)
    lw0_o[...] = lw[:, :rw]
    lw1_o[...] = lw[:, rw:]

    r = shifted(z_ref[:, 0:rw], zp_ref[:, 0:rw], zn_ref[:, 0:rw], mu_ref[:, 0:rw])
    k = shifted(z_ref[:, rw:2 * rw], zp_ref[:, rw:2 * rw], zn_ref[:, rw:2 * rw], mu_ref[:, rw:2 * rw])
    v = shifted(z_ref[:, 2 * rw:3 * rw], zp_ref[:, 2 * rw:3 * rw], zn_ref[:, 2 * rw:3 * rw],
                mu_ref[:, 2 * rw:3 * rw])
    r_o[...] = r
    v_o[...] = v
    kk = k * kk_ref[...]
    ss = _seg_sum(kk * kk, e_ref[...], et_ref[...])
    kk = kk / jnp.maximum(jnp.sqrt(ss), 1e-12)
    kk_o[...] = kk
    a0 = a_all[:, :rw]
    a1 = a_all[:, rw:]
    b0_o[...] = kk * a0
    b1_o[...] = kk * a1
    kd0 = k * (1.0 + (a0 - 1.0) * ka_ref[...])
    kd1 = k * (1.0 + (a1 - 1.0) * ka_ref[...])
    kd0_o[...] = kd0
    kd1_o[...] = kd1
    kb = 0.5 * (kd0 + kd1)
    bon_o[...] = _seg_sum(r * kb * rk_ref[...], e_ref[...], et_ref[...]) * v


def _rwprep(z, seq, rw, lat_off, mu_rkv, mu_lat, wup, aup, gup, w0, a0, k_k, k_a, r_k, e64, e64t):
    t = z.shape[0]
    tm = 256
    nlat = 512
    nrkv = 3 * rw
    tb = tm // SUBLANES
    nblk8 = t // SUBLANES
    lat_blk = lat_off // nlat

    def cur(i):
        return (i, 0)

    def prv(i):
        return (jnp.maximum(i * tb - 1, 0), 0)

    def nxt(i):
        return (jnp.minimum((i + 1) * tb, nblk8 - 1), 0)

    def full(shape):
        return pl.BlockSpec(shape, lambda i: (0,) * len(shape))

    out = jax.ShapeDtypeStruct((t, rw), F32)
    ospec = pl.BlockSpec((tm, rw), lambda i: (i, 0))
    return pl.pallas_call(
        functools.partial(_rwprep_kernel, seq, tm, rw),
        grid=(t // tm,),
        in_specs=[pl.BlockSpec((tm, nrkv), cur),
                  pl.BlockSpec((SUBLANES, nrkv), prv),
                  pl.BlockSpec((SUBLANES, nrkv), nxt),
                  pl.BlockSpec((tm, nlat), lambda i: (i, lat_blk)),
                  pl.BlockSpec((SUBLANES, nlat), lambda i: (jnp.maximum(i * tb - 1, 0), lat_blk)),
                  pl.BlockSpec((SUBLANES, nlat), lambda i: (jnp.minimum((i + 1) * tb, nblk8 - 1), lat_blk)),
                  full((1, nrkv)), full((1, nlat)),
                  full(wup.shape), full(aup.shape), full(gup.shape),
                  full((1, 2 * rw)), full((1, 2 * rw)),
                  full((1, rw)), full((1, rw)), full((1, rw)),
                  full(e64.shape), full(e64t.shape)],
        out_specs=[ospec] * 11,
        out_shape=[out] * 11,
        compiler_params=_cparams(("parallel",)),
        name="rwprep",
    )(z, z, z, z, z, z, mu_rkv, mu_lat, wup, aup, gup, w0, a0, k_k, k_a, r_k, e64, e64t)


def _tri(n, rev):
    i = lax.broadcasted_iota(I32, (n, n), 0)
    j = lax.broadcasted_iota(I32, (n, n), 1)
    m = (j >= i) if rev else (j <= i)
    return jnp.where(m, 1.0, 0.0).astype(BF16)


def _rw_streams(streams):
    c = streams[0][0].shape[0]
    hd = RW_HEAD_DIM
    n2 = 2 * c
    ns = len(streams)
    revs = [s[7] for s in streams]
    lane = lax.broadcasted_iota(I32, (c, LANES), 1)
    head_a = lane < hd
    ri = lax.broadcasted_iota(I32, (n2, n2), 0)
    ci = lax.broadcasted_iota(I32, (n2, n2), 1)
    ti = ri % c
    si = ci % c
    same16 = (ri // 16) == (ci // 16)
    same32 = (ri // 32) == (ci // 32)
    mid32 = jnp.logical_and(same32, jnp.logical_not(same16))
    eye = jnp.where(ri == ci, 1.0, 0.0)
    strict = {False: si < ti, True: si > ti}
    incl = {False: si <= ti, True: si >= ti}
    tri = {rev: _tri(c, rev) for rev in set(revs)}
    ei = lax.broadcasted_iota(I32, (LANES, LANES), 0)
    ej = lax.broadcasted_iota(I32, (LANES, LANES), 1)
    eye_k = ei == ej

    def pair(x):
        return jnp.concatenate([jnp.where(head_a, x, 0.0), jnp.where(head_a, 0.0, x)], axis=0)

    cum = []
    for (r, v, kk, lw, b, kd, s_in, rev) in streams:
        hi, lo = _split2(lw)
        cs = jnp.dot(tri[rev], jnp.concatenate([hi, lo], axis=1), preferred_element_type=F32)
        cum.append(cs[:, :LANES] + cs[:, LANES:])
    ops = []
    for (r, v, kk, lw, b, kd, s_in, rev), cm in zip(streams, cum):
        tot = cm[0:1, :] if rev else cm[c - 1:c, :]
        g_inv = jnp.exp(-cm)
        g_tail = jnp.exp(tot - cm)
        ops.append(dict(p2=pair(-kk * jnp.exp(cm - lw)), r2=pair(r * jnp.exp(cm)),
                        bi2=pair(b * g_inv), ki2=pair(kd * g_inv), bt2=pair(b * g_tail),
                        kt2=pair(kd * g_tail), v2=pair(v), g_tot=jnp.exp(tot)))
    gm = [_dot_nt(jnp.concatenate([o['p2'], o['r2']], axis=0), jnp.concatenate([o['bi2'], o['ki2']], axis=0))
          for o in ops]
    a2 = [jnp.where(strict[rev], g[:n2, :n2], 0.0) for g, rev in zip(gm, revs)]
    b2 = [jnp.where(strict[rev], g[:n2, n2:], 0.0) for g, rev in zip(gm, revs)]
    ap2 = [jnp.where(incl[rev], g[n2:, :n2], 0.0) for g, rev in zip(gm, revs)]
    bp2 = [jnp.where(incl[rev], g[n2:, n2:], 0.0) for g, rev in zip(gm, revs)]
    bv = [_dot(x, o['v2']) for x, o in zip(b2, ops)]
    bpv = [_dot(x, o['v2']) for x, o in zip(bp2, ops)]

    x = [jnp.where(same16, a, 0.0) for a in a2]
    tinv = [eye + xi for xi in x]
    for _ in range(3):
        x = [_dot(xi, xi) for xi in x]
        tinv = [t + _dot(t, xi) for t, xi in zip(tinv, x)]
    for lvl in (mid32, jnp.logical_not(same32)):
        y = [_dot(t, jnp.where(lvl, a, 0.0)) for t, a in zip(tinv, a2)]
        tinv = [t + _dot(yi, t) for t, yi in zip(tinv, y)]

    wu = [_dot(t, jnp.concatenate([o['p2'], bvi], axis=1)) for t, o, bvi in zip(tinv, ops, bv)]
    qo = [_dot(a, w) for a, w in zip(ap2, wu)]
    m2 = [_dot_tn(w[:, :LANES], o['bt2']) + jnp.where(eye_k, jnp.broadcast_to(o['g_tot'], (LANES, LANES)), 0.0)
          for w, o in zip(wu, ops)]
    nn2 = [_dot_tn(w[:, LANES:], o['bt2']) + _dot_tn(o['v2'], o['kt2']) for w, o in zip(wu, ops)]

    nt = (((1,), (1,)), ((), ()))
    outs = []
    for i in range(ns):
        s_hi, s_lo = _split2(streams[i][6])
        q2b = (ops[i]['r2'] + qo[i][:, :LANES]).astype(BF16)
        oo = (lax.dot_general(q2b, s_hi, nt, preferred_element_type=F32)
              + lax.dot_general(q2b, s_lo, nt, preferred_element_type=F32)
              + qo[i][:, LANES:] + bpv[i])
        m2b = m2[i].astype(BF16)
        s_out = (jnp.dot(s_hi, m2b, preferred_element_type=F32)
                 + jnp.dot(s_lo, m2b, preferred_element_type=F32) + nn2[i])
        outs.append((oo[:c] + oo[c:], s_out))
    return outs


RW_PAIRS_PER_STEP = 4


def _rwscan_kernel(rf, vf, kkf, lwf, bf, kdf, rb, vb, kkb, lwb, bb, kdb, of_ref, ob_ref, sf, sb):
    @pl.when(pl.program_id(2) == 0)
    def _():
        sf[...] = jnp.zeros_like(sf)
        sb[...] = jnp.zeros_like(sb)

    streams = []
    for p in range(RW_PAIRS_PER_STEP):
        sl = slice(p * LANES, (p + 1) * LANES)
        streams.append((rf[:, sl], vf[:, sl], kkf[:, sl], lwf[:, sl], bf[:, sl], kdf[:, sl], sf[p], False))
        streams.append((rb[:, sl], vb[:, sl], kkb[:, sl], lwb[:, sl], bb[:, sl], kdb[:, sl], sb[p], True))
    outs = _rw_streams(streams)
    for p in range(RW_PAIRS_PER_STEP):
        sl = slice(p * LANES, (p + 1) * LANES)
        of_ref[:, sl], sf[p] = outs[2 * p]
        ob_ref[:, sl], sb[p] = outs[2 * p + 1]


def _rwscan(nb, seq, r, v, kk, lw0, lw1, b0, b1, kd0, kd1):
    t, rw = r.shape
    nc = seq // CHUNK
    wblk = RW_PAIRS_PER_STEP * LANES
    fw = pl.BlockSpec((CHUNK, wblk), lambda bi, hp, c: (bi * nc + c, hp))
    bw = pl.BlockSpec((CHUNK, wblk), lambda bi, hp, c: (bi * nc + nc - 1 - c, hp))
    out = jax.ShapeDtypeStruct((t, rw), F32)
    state = pltpu.VMEM((RW_PAIRS_PER_STEP, LANES, LANES), F32)
    return pl.pallas_call(
        _rwscan_kernel,
        grid=(nb, rw // wblk, nc),
        in_specs=[fw] * 6 + [bw] * 6,
        out_specs=[fw, bw],
        out_shape=[out, out],
        scratch_shapes=[state, state],
        compiler_params=_cparams(("parallel", "parallel", "arbitrary")),
        name="rwscan",
    )(r, v, kk, lw0, b0, kd0, r, v, kk, lw1, b1, kd1)


def _hg_streams(streams):
    c, dk = streams[0][0].shape
    revs = [s[5] for s in streams]
    tri = {rev: _tri(c, rev) for rev in set(revs)}
    row = lax.broadcasted_iota(I32, (c, dk), 0)
    ri = lax.broadcasted_iota(I32, (c, c), 0)
    ci = lax.broadcasted_iota(I32, (c, c), 1)

    cum = []
    for (q, k, v, lf, st, rev) in streams:
        hi, mid, lo = _split3(lf)
        cs = jnp.dot(tri[rev], jnp.concatenate([hi, mid, lo], axis=1), preferred_element_type=F32)
        cum.append(cs[:, :dk] + cs[:, dk:2 * dk] + cs[:, 2 * dk:])
    scores = [jnp.where(ri == ci, _dot_nt(s[0], s[1]), 0.0) for s in streams]
    rolled = [dict() for _ in streams]

    def rolled_cum(i, d):
        if d not in rolled[i]:
            rolled[i][d] = pltpu.roll(cum[i], d % c, axis=0)
        return rolled[i][d]

    h = c // 2
    while h >= 1:
        blk = 2 * h
        pos = row % blk
        upper = pos >= h
        same_blk = (ri // blk) == (ci // blk)
        sl = []
        for i, (q, k, v, lf, st, rev) in enumerate(streams):
            cm = cum[i]
            if h >= SUBLANES:
                pieces = []
                for m0 in range(0, c, blk):
                    idx = m0 + h if rev else m0 + h - 1
                    pieces.append(jnp.broadcast_to(cm[idx:idx + 1, :], (blk, dk)))
                ref = jnp.concatenate(pieces, axis=0) if len(pieces) > 1 else pieces[0]
            else:
                off = h if rev else h - 1
                ref = jnp.zeros_like(cm)
                for p in range(blk):
                    ref = jnp.where(pos == p, rolled_cum(i, p - off), ref)
            q_rows = jnp.logical_not(upper) if rev else upper
            eq = jnp.where(q_rows, jnp.minimum(cm - ref, 0.0), NEG_BIG)
            ek = jnp.where(q_rows, NEG_BIG, jnp.minimum(ref - cm, 0.0))
            sl.append(_dot_nt(q * jnp.exp(eq), k * jnp.exp(ek)))
        scores = [sc + jnp.where(same_blk, x, 0.0) for sc, x in zip(scores, sl)]
        h //= 2

    outs = []
    for (q, k, v, lf, st, rev), cm, sc in zip(streams, cum, scores):
        tot = cm[0:1, :] if rev else cm[c - 1:c, :]
        o = _dot(sc, v) + _dot_nt(q * jnp.exp(cm), st)
        st_new = st * jnp.exp(tot) + _dot_tn(v, k * jnp.exp(tot - cm))
        outs.append((o, st_new))
    return outs


HG_HEADS_PER_STEP = 4


def _hgscan_kernel(dk, qf, fff, i_f, qb, ffb, i_b, lb_ref, of_ref, ob_ref, sf, sb):
    @pl.when(pl.program_id(2) == 0)
    def _():
        sf[...] = jnp.zeros_like(sf)
        sb[...] = jnp.zeros_like(sb)

    streams = []
    for p in range(HG_HEADS_PER_STEP):
        sl = slice(p * dk, (p + 1) * dk)
        for (q_ref, ff_ref, i_ref, st_ref, d) in ((qf, fff, i_f, sf, 0), (qb, ffb, i_b, sb, 1)):
            lbv = lb_ref[d:d + 1, sl]
            f = lbv + (1.0 - lbv) * _sigmoid(ff_ref[:, sl])
            streams.append((_silu(q_ref[:, sl]), 1.0 - f, i_ref[:, sl], jnp.log(f), st_ref[p], d == 1))
    outs = _hg_streams(streams)
    for p in range(HG_HEADS_PER_STEP):
        sl = slice(p * dk, (p + 1) * dk)
        of_ref[:, sl], sf[p] = outs[2 * p]
        ob_ref[:, sl], sb[p] = outs[2 * p + 1]


def _hgscan(nb, seq, z, lb, hg_off, hw):
    t = z.shape[0]
    nc = seq // CHUNK
    dk = hw // HG_HEADS
    wblk = HG_HEADS_PER_STEP * dk
    base = hg_off // wblk
    nh = hw // wblk

    def fw(comp):
        return pl.BlockSpec((CHUNK, wblk), lambda bi, h, c: (bi * nc + c, base + comp * nh + h))

    def bw(comp):
        return pl.BlockSpec((CHUNK, wblk), lambda bi, h, c: (bi * nc + nc - 1 - c, base + comp * nh + h))

    out = jax.ShapeDtypeStruct((t, hw), F32)
    state = pltpu.VMEM((HG_HEADS_PER_STEP, dk, dk), F32)
    return pl.pallas_call(
        functools.partial(_hgscan_kernel, dk),
        grid=(nb, nh, nc),
        in_specs=[fw(0), fw(1), fw(3), bw(0), bw(2), bw(3),
                  pl.BlockSpec((2, wblk), lambda bi, h, c: (0, h))],
        out_specs=[pl.BlockSpec((CHUNK, wblk), lambda bi, h, c: (bi * nc + c, h)),
                   pl.BlockSpec((CHUNK, wblk), lambda bi, h, c: (bi * nc + nc - 1 - c, h))],
        out_shape=[out, out],
        scratch_shapes=[state, state],
        compiler_params=_cparams(("parallel", "parallel", "arbitrary")),
        name="hgscan",
    )(z, z, z, z, z, z, lb)


def _blockdiag2(w):
    _, r, n = w.shape
    z = jnp.zeros((r, n), w.dtype)
    return jnp.concatenate([jnp.concatenate([w[0], z], axis=1), jnp.concatenate([z, w[1]], axis=1)], axis=0)


def _indicator(width, seg):
    e = (jnp.arange(width)[:, None] // seg == jnp.arange(width // seg)[None, :]).astype(BF16)
    return e, e.T


def _layer_consts(w_in, rw_mu, rw_w0, rw_w_up, rw_a0, rw_a_up, rw_g_up, rw_k_k, rw_k_a, rw_r_k,
                  hg_lb_gamma):
    rw = rw_k_k.shape[-1]
    d = w_in.shape[1]
    nlat = 2 * W_LORA + 2 * A_LORA + G_LORA
    w = w_in[0]
    rkv = 3 * rw
    hg_cols = w.shape[1] - rkv - nlat
    pad = 512 - nlat
    w_perm = jnp.concatenate([w[:, :rkv], w[:, rkv + nlat:], w[:, rkv:rkv + nlat],
                              jnp.zeros((d, pad), w.dtype)], axis=1).astype(BF16)
    mu = rw_mu[0]
    lower = jnp.cumsum(jax.nn.softmax(hg_lb_gamma.astype(F32), axis=0), axis=0)[0]
    hw = lower.shape[-1]
    e64, e64t = _indicator(rw, RW_HEAD_DIM)
    return dict(
        rw=rw, hw=hw, hg_off=rkv, lat_off=rkv + hg_cols, w_in=w_perm,
        mu_rkv=mu[:rkv].reshape(1, rkv),
        mu_lat=jnp.pad(mu[rkv:rkv + nlat], (0, pad)).reshape(1, 512),
        wup=_blockdiag2(rw_w_up[0]).astype(BF16), aup=_blockdiag2(rw_a_up[0]).astype(BF16),
        gup=rw_g_up[0].astype(BF16),
        w0=rw_w0[0].reshape(1, 2 * rw), a0=rw_a0[0].reshape(1, 2 * rw),
        k_k=rw_k_k[0].reshape(1, rw), k_a=rw_k_a[0].reshape(1, rw), r_k=rw_r_k[0].reshape(1, rw),
        e64=e64, e64t=e64t,
        lb=lower,
    )


def _mixer(x2, nb, seq, mod, norm_pre_mix, wc):
    z = _inproj(x2, mod, norm_pre_mix, wc['w_in'], seq)
    (r, v, kk, g, bonus, lw0, lw1, b0, b1, kd0, kd1) = _rwprep(
        z, seq, wc['rw'], wc['lat_off'], wc['mu_rkv'], wc['mu_lat'], wc['wup'], wc['aup'], wc['gup'],
        wc['w0'], wc['a0'], wc['k_k'], wc['k_a'], wc['r_k'], wc['e64'], wc['e64t'])
    rw_of, rw_ob = _rwscan(nb, seq, r, v, kk, lw0, lw1, b0, b1, kd0, kd1)
    hg_of, hg_ob = _hgscan(nb, seq, z, wc['lb'], wc['hg_off'], wc['hw'])
    return dict(z=z, r=r, v=v, kk=kk, g=g, bonus=bonus, lw0=lw0, rw_of=rw_of, rw_ob=rw_ob,
                hg_of=hg_of, hg_ob=hg_ob)


def _outproj_kernel(rw, x_ref, rf_ref, rb_ref, bon_ref, g_ref, hf_ref, hb_ref, hgg_ref, mod_ref,
                    lnw_ref, lnb_ref, hnw_ref, npm_ref, npf_ref, wout_ref, wrh_ref, wrl_ref,
                    e64_ref, e64t_ref, e128_ref, e128t_ref,
                    x1_ref, h2_ref, lg_ref):
    m6 = mod_ref[0]
    o = rf_ref[...] + rb_ref[...]
    mean = _seg_sum(o, e64_ref[...], e64t_ref[...]) * (1.0 / RW_HEAD_DIM)
    dlt = o - mean
    var = _seg_sum(dlt * dlt, e64_ref[...], e64t_ref[...]) * (1.0 / RW_HEAD_DIM)
    o_rw = (dlt * lax.rsqrt(var + RW_GN_EPS) * lnw_ref[...] + lnb_ref[...] + bon_ref[...]) * g_ref[...]
    oh = hf_ref[...] + hb_ref[...]
    hd = oh.shape[1] // HG_HEADS
    ms = _seg_sum(oh * oh, e128_ref[...], e128t_ref[...]) * (1.0 / hd)
    o_hg = oh * lax.rsqrt(ms + NORM_EPS) * hnw_ref[...] * _silu(hgg_ref[...])
    m = _dot(o_rw, wout_ref[0:rw, :]) + _dot(o_hg, wout_ref[rw:, :])
    x1 = x_ref[...] + m6[2:3] * _rms_rows(m, npm_ref[...])
    x1_ref[...] = x1
    h2 = _rms_rows(x1, npf_ref[...]) * (1.0 + m6[4:5]) + m6[3:4]
    hi, lo = _split2(h2)
    _pack_rows(hi, h2_ref)
    nt = (((1,), (1,)), ((), ()))
    lg_ref[...] = (lax.dot_general(wrh_ref[...], hi, nt, preferred_element_type=F32)
                   + lax.dot_general(wrh_ref[...], lo, nt, preferred_element_type=F32)
                   + lax.dot_general(wrl_ref[...], hi, nt, preferred_element_type=F32))


def _outproj(x2, seq, mx, z, mod, wc, oc):
    t, d = x2.shape
    rw, hw = wc['rw'], wc['hw']
    tm = 256
    gblk = (wc['hg_off'] + 4 * hw) // hw

    def row(w):
        return pl.BlockSpec((tm, w), lambda i: (i, 0))

    def full(a):
        return pl.BlockSpec(a.shape, lambda i: (0,) * a.ndim)

    consts = [oc['ln_w'], oc['ln_b'], oc['hg_norm_w'], oc['npm'], oc['npf'], oc['w_out'], oc['wr_hi'],
              oc['wr_lo'], wc['e64'], wc['e64t'], oc['e128'], oc['e128t']]
    return pl.pallas_call(
        functools.partial(_outproj_kernel, rw),
        grid=(t // tm,),
        in_specs=[row(d), row(rw), row(rw), row(rw), row(rw), row(hw), row(hw),
                  pl.BlockSpec((tm, hw), lambda i: (i, gblk)),
                  pl.BlockSpec((1, 6, d), lambda i: ((i * tm) // seq, 0, 0))] + [full(a) for a in consts],
        out_specs=[row(d), pl.BlockSpec((tm * (d // PACK_W), LANES), lambda i: (i, 0)),
                   pl.BlockSpec((N_EXPERTS, tm), lambda i: (0, i))],
        out_shape=[jax.ShapeDtypeStruct((t, d), F32), jax.ShapeDtypeStruct((t * (d // PACK_W), LANES), jnp.uint32),
                   jax.ShapeDtypeStruct((N_EXPERTS, t), F32)],
        compiler_params=_cparams(("parallel",)),
        name="outproj",
    )(x2, mx['rw_of'], mx['rw_ob'], mx['bonus'], mx['g'], mx['hg_of'], mx['hg_ob'], z, mod, *consts)


ROUTE_TILE = 512


def _route_kernel(lg_ref, bias_ref, ut_ref, eidx_ref, wsel_ref, rank_ref, cnt_ref, carry):
    @pl.when(pl.program_id(0) == 0)
    def _():
        carry[...] = jnp.zeros_like(carry)

    ne, tt = lg_ref.shape
    gsz = ne // N_GROUPS
    neg = -jnp.inf
    s = _sigmoid(lg_ref[...])
    biased = s + bias_ref[...]
    io_g = lax.broadcasted_iota(I32, (gsz, tt), 0)
    gs_rows = []
    for gi in range(N_GROUPS):
        blk = biased[gi * gsz:(gi + 1) * gsz, :]
        m1 = jnp.max(blk, axis=0, keepdims=True)
        first = jnp.min(jnp.where(blk == m1, io_g, gsz), axis=0, keepdims=True)
        m2 = jnp.max(jnp.where(io_g == first, neg, blk), axis=0, keepdims=True)
        gs_rows.append(m1 + m2)
    gs = jnp.concatenate(gs_rows, axis=0)
    io_n = lax.broadcasted_iota(I32, (N_GROUPS, tt), 0)
    selg = jnp.zeros((N_GROUPS, tt), jnp.bool_)
    for _ in range(TOPK_GROUPS):
        m = jnp.max(gs, axis=0, keepdims=True)
        first = jnp.min(jnp.where(gs == m, io_n, N_GROUPS), axis=0, keepdims=True)
        pick = io_n == first
        selg = jnp.logical_or(selg, pick)
        gs = jnp.where(pick, neg, gs)
    emask = jnp.concatenate([jnp.broadcast_to(selg[gi:gi + 1, :], (gsz, tt)) for gi in range(N_GROUPS)],
                            axis=0)
    mb = jnp.where(emask, biased, neg)
    io_e = lax.broadcasted_iota(I32, (ne, tt), 0)
    sel = jnp.zeros((ne, tt), jnp.bool_)
    picks, idxs, ws = [], [], []
    for _ in range(TOP_K):
        m = jnp.max(mb, axis=0, keepdims=True)
        first = jnp.min(jnp.where(mb == m, io_e, ne), axis=0, keepdims=True)
        pick = io_e == first
        picks.append(pick)
        idxs.append(first)
        ws.append(jnp.sum(jnp.where(pick, s, 0.0), axis=0, keepdims=True))
        sel = jnp.logical_or(sel, pick)
        mb = jnp.where(pick, neg, mb)
    wsum = ws[0]
    for w in ws[1:]:
        wsum = wsum + w
    pos = jnp.dot(jnp.where(sel, 1.0, 0.0).astype(BF16), ut_ref[...], preferred_element_type=F32) + carry[...]
    ranks = [jnp.sum(jnp.where(p, pos, 0.0), axis=0, keepdims=True).astype(I32) for p in picks]
    carry[...] = carry[...] + jnp.sum(jnp.where(sel, 1.0, 0.0), axis=1, keepdims=True)
    zi = jnp.zeros((SUBLANES - TOP_K, tt), I32)
    eidx_ref[...] = jnp.concatenate(idxs + [zi], axis=0)
    rank_ref[...] = jnp.concatenate(ranks + [zi], axis=0)
    wsel_ref[...] = jnp.concatenate([w / wsum * ROUTED_SCALE for w in ws] + [zi.astype(F32)], axis=0)
    cnt_ref[...] = jnp.broadcast_to(carry[...], cnt_ref.shape).astype(I32)


def _route(logits_t, e_bias):
    ne, t = logits_t.shape
    tt = ROUTE_TILE
    ut = (jnp.arange(tt)[:, None] < jnp.arange(tt)[None, :]).astype(BF16)
    tok = pl.BlockSpec((SUBLANES, tt), lambda i: (0, i))
    return pl.pallas_call(
        _route_kernel,
        grid=(t // tt,),
        in_specs=[pl.BlockSpec((ne, tt), lambda i: (0, i)),
                  pl.BlockSpec((ne, 1), lambda i: (0, 0)),
                  pl.BlockSpec((tt, tt), lambda i: (0, 0))],
        out_specs=[tok, tok, tok, pl.BlockSpec((ne, LANES), lambda i: (0, 0))],
        out_shape=[jax.ShapeDtypeStruct((SUBLANES, t), I32), jax.ShapeDtypeStruct((SUBLANES, t), F32),
                   jax.ShapeDtypeStruct((SUBLANES, t), I32), jax.ShapeDtypeStruct((ne, LANES), I32)],
        scratch_shapes=[pltpu.VMEM((ne, 1), F32)],
        compiler_params=_cparams(("arbitrary",)),
        name="route",
    )(logits_t, e_bias.reshape(ne, 1), ut)


DISPATCH_TILE = 128
PLAN_TILE = 1024


def _plan_kernel(eidx_ref, rank_ref, ps_ref, dest_ref):
    ne = ps_ref.shape[0]
    tp = eidx_ref.shape[1]
    io_e = lax.broadcasted_iota(I32, (ne, tp), 0)
    ps = ps_ref[...]
    rows = []
    for j in range(TOP_K):
        start = jnp.sum(jnp.where(io_e == eidx_ref[j:j + 1, :], ps, 0.0), axis=0, keepdims=True)
        rows.append(start.astype(I32) + rank_ref[j:j + 1, :])
    dest = jnp.concatenate(rows + [jnp.zeros((SUBLANES - TOP_K, tp), I32)], axis=0)
    for i in range(tp // DISPATCH_TILE):
        dest_ref[i] = dest[:, i * DISPATCH_TILE:(i + 1) * DISPATCH_TILE]


def _plan(eidx, rank, pad_start):
    t = eidx.shape[1]
    tp = PLAN_TILE
    ne = pad_start.shape[0]
    tok = pl.BlockSpec((SUBLANES, tp), lambda i: (0, i))
    ntile = tp // DISPATCH_TILE
    return pl.pallas_call(
        _plan_kernel,
        grid=(t // tp,),
        in_specs=[tok, tok, pl.BlockSpec((ne, 1), lambda i: (0, 0))],
        out_specs=pl.BlockSpec((ntile, SUBLANES, DISPATCH_TILE), lambda i: (i, 0, 0)),
        out_shape=jax.ShapeDtypeStruct((t // DISPATCH_TILE, SUBLANES, DISPATCH_TILE), I32),
        compiler_params=_cparams(("parallel",)),
        name="plan",
    )(eidx, rank, pad_start.astype(F32).reshape(ne, 1))


def _dispatch_kernel(tt, dest_ref, h_ref, xs_in_ref, xs_ref, idx_smem, isem, sem):
    del xs_in_ref
    i = pl.program_id(0)
    n = tt * TOP_K
    icp = pltpu.make_async_copy(dest_ref.at[pl.ds(i * tt * SUBLANES, n)], idx_smem, isem)
    icp.start()
    icp.wait()

    def row_copy(k):
        return pltpu.make_async_copy(h_ref.at[k % tt], xs_ref.at[idx_smem[k]], sem)

    def issue(k, c):
        row_copy(k).start()
        return c

    def drain(k, c):
        row_copy(k).wait()
        return c

    lax.fori_loop(0, n, issue, 0, unroll=8)
    lax.fori_loop(0, n, drain, 0, unroll=8)


def _dispatch(dest_flat, h2p, p_rows):
    t = h2p.shape[0]
    tt = DISPATCH_TILE
    xs0 = jnp.zeros((p_rows,) + h2p.shape[1:], h2p.dtype)
    anyspec = pl.BlockSpec(memory_space=pl.ANY)
    return pl.pallas_call(
        functools.partial(_dispatch_kernel, tt),
        grid=(t // tt,),
        in_specs=[anyspec, pl.BlockSpec((tt,) + h2p.shape[1:], lambda i: (i, 0, 0)), anyspec],
        out_specs=anyspec,
        out_shape=jax.ShapeDtypeStruct(xs0.shape, xs0.dtype),
        scratch_shapes=[pltpu.SMEM((tt * TOP_K,), I32), pltpu.SemaphoreType.DMA(()),
                        pltpu.SemaphoreType.DMA(())],
        input_output_aliases={2: 0},
        compiler_params=_cparams(("arbitrary",)),
        name="dispatch",
    )(dest_flat, h2p, xs0)


def _experts_kernel(be_ref, nu_ref, x_ref, wg_ref, wu_ref, wd_ref, y_ref):
    del be_ref
    b = pl.program_id(0)

    @pl.when(b < nu_ref[0])
    def _():
        x = _unpack_rows(x_ref, wg_ref.shape[1] // PACK_W)
        gate = jnp.dot(x, wg_ref[0], preferred_element_type=F32)
        up = jnp.dot(x, wu_ref[0], preferred_element_type=F32)
        y_ref[...] = _dot(_silu(gate) * up, wd_ref[0])

    @pl.when(b >= nu_ref[0])
    def _():
        y_ref[...] = jnp.zeros_like(y_ref)


def _experts(block_e, n_used, xs, wg, wu, wd):
    d, de = wg.shape[1], wg.shape[2]
    ns = d // PACK_W
    p_rows = xs.shape[0] // ns
    nblk = p_rows // EXPERT_BLOCK

    def live(b, nu):
        return jnp.minimum(b, jnp.maximum(nu[0] - 1, 0))

    grid_spec = pltpu.PrefetchScalarGridSpec(
        num_scalar_prefetch=2,
        grid=(nblk,),
        in_specs=[pl.BlockSpec((EXPERT_BLOCK * ns, LANES), lambda b, be, nu: (live(b, nu), 0)),
                  pl.BlockSpec((1, d, de), lambda b, be, nu: (be[live(b, nu)], 0, 0)),
                  pl.BlockSpec((1, d, de), lambda b, be, nu: (be[live(b, nu)], 0, 0)),
                  pl.BlockSpec((1, de, d), lambda b, be, nu: (be[live(b, nu)], 0, 0))],
        out_specs=pl.BlockSpec((EXPERT_BLOCK, d), lambda b, be, nu: (b, 0)),
    )
    return pl.pallas_call(
        _experts_kernel,
        grid_spec=grid_spec,
        out_shape=jax.ShapeDtypeStruct((p_rows, d), F32),
        compiler_params=_cparams(("arbitrary",)),
        name="experts",
    )(block_e, n_used, xs, wg, wu, wd)


def _combine_kernel(tt, dest_ref, y_ref, x1_ref, h2_ref, w_ref, mod_ref, npf_ref, sg_ref, su_ref, sd_ref,
                    o_ref, ybuf, idx_smem, isem, sem):
    i = pl.program_id(0)
    n = tt * TOP_K
    icp = pltpu.make_async_copy(dest_ref.at[pl.ds(i * tt * SUBLANES, n)], idx_smem, isem)
    icp.start()
    icp.wait()

    def row_copy(k):
        return pltpu.make_async_copy(y_ref.at[idx_smem[k]], ybuf.at[k], sem)

    def issue(k, c):
        row_copy(k).start()
        return c

    def drain(k, c):
        row_copy(k).wait()
        return c

    lax.fori_loop(0, n, issue, 0, unroll=8)
    h2 = _unpack_rows(h2_ref, sg_ref.shape[0] // PACK_W)
    gate = jnp.dot(h2, sg_ref[...], preferred_element_type=F32)
    up = jnp.dot(h2, su_ref[...], preferred_element_type=F32)
    shared = _dot(_silu(gate) * up, sd_ref[...])
    lax.fori_loop(0, n, drain, 0, unroll=8)
    w = w_ref[...]
    routed = w[:, 0:1] * ybuf[0:tt, :]
    for j in range(1, TOP_K):
        routed = routed + w[:, j:j + 1] * ybuf[j * tt:(j + 1) * tt, :]
    m6 = mod_ref[0]
    o_ref[...] = x1_ref[...] + m6[5:6] * _rms_rows(routed + shared, npf_ref[...])


def _combine(dest_flat, y, x1, h2, wsel_t, mod, seq, npf, sg, su, sd):
    t, d = x1.shape
    tt = DISPATCH_TILE
    anyspec = pl.BlockSpec(memory_space=pl.ANY)

    def full(a):
        return pl.BlockSpec(a.shape, lambda i: (0,) * a.ndim)

    return pl.pallas_call(
        functools.partial(_combine_kernel, tt),
        grid=(t // tt,),
        in_specs=[anyspec, anyspec,
                  pl.BlockSpec((tt, d), lambda i: (i, 0)),
                  pl.BlockSpec((tt * (d // PACK_W), LANES), lambda i: (i, 0)),
                  pl.BlockSpec((tt, SUBLANES), lambda i: (i, 0)),
                  pl.BlockSpec((1, 6, d), lambda i: ((i * tt) // seq, 0, 0)),
                  full(npf), full(sg), full(su), full(sd)],
        out_specs=pl.BlockSpec((tt, d), lambda i: (i, 0)),
        out_shape=jax.ShapeDtypeStruct((t, d), F32),
        scratch_shapes=[pltpu.VMEM((tt * TOP_K, d), F32), pltpu.SMEM((tt * TOP_K,), I32),
                        pltpu.SemaphoreType.DMA(()), pltpu.SemaphoreType.DMA(())],
        compiler_params=_cparams(("arbitrary",)),
        name="combine",
    )(dest_flat, y, x1, h2, wsel_t, mod, npf, sg, su, sd)


def _moe_plan(eidx, rank, cnt, t):
    counts = cnt[:, 0]
    padded = (counts + EXPERT_BLOCK - 1) // EXPERT_BLOCK * EXPERT_BLOCK
    pad_end = jnp.cumsum(padded)
    pad_start = pad_end - padded
    n_blocks = (t * TOP_K + EXPERT_BLOCK - 1) // EXPERT_BLOCK + N_EXPERTS
    first_row = jnp.arange(n_blocks, dtype=I32) * EXPERT_BLOCK
    block_e = jnp.minimum(jnp.sum((pad_end[None, :] <= first_row[:, None]).astype(I32), axis=1),
                          N_EXPERTS - 1).astype(I32)
    n_used = (pad_end[-1:] // EXPERT_BLOCK).astype(I32)
    dest_flat = _plan(eidx, rank, pad_start).reshape(-1)
    return block_e, n_used, dest_flat, n_blocks * EXPERT_BLOCK


def _trunk(x, mod, wc, oc, ec, norm_pre_mix):
    nb, seq, d = x.shape
    t = nb * seq
    x2 = x.reshape(t, d)
    mx = _mixer(x2, nb, seq, mod, norm_pre_mix, wc)
    x1, h2, logits_t = _outproj(x2, seq, mx, mx['z'], mod, wc, oc)
    eidx, wsel, rank, cnt = _route(logits_t, ec['e_bias'])
    block_e, n_used, dest_flat, p_rows = _moe_plan(eidx, rank, cnt, t)
    ns = d // PACK_W
    xs = _dispatch(dest_flat, h2.reshape(t, ns, LANES), p_rows)
    y = _experts(block_e, n_used, xs.reshape(p_rows * ns, LANES), ec['wg'], ec['wu'], ec['wd'])
    out = _combine(dest_flat, y, x1, h2, wsel.T, mod, seq, oc['npo'], ec['sg'], ec['su'], ec['sd'])
    return out.reshape(nb, seq, d)


def kernel(x_prompt, x_sample, c_prompt, c_sample, w_ada, b_ada, norm_pre_mix, norm_post_mix, norm_pre_ffn, norm_post_ffn, w_in, rw_mu, rw_w0, rw_w_up, rw_a0, rw_a_up, rw_g_up, rw_k_k, rw_k_a, rw_r_k, rw_ln_w, rw_ln_b, hg_lb_gamma, hg_norm_w, w_out, w_router, e_bias, w_exp_gate, w_exp_up, w_exp_down, w_sh_gate, w_sh_up, w_sh_down):
    d = x_prompt.shape[-1]
    wc = _layer_consts(w_in, rw_mu, rw_w0, rw_w_up, rw_a0, rw_a_up, rw_g_up, rw_k_k, rw_k_a, rw_r_k,
                       hg_lb_gamma)
    rw, hw = wc['rw'], wc['hw']
    e128, e128t = _indicator(hw, hw // HG_HEADS)
    wr_hi, wr_lo = _split2(w_router[0].T)
    oc = dict(ln_w=rw_ln_w[0].reshape(1, rw), ln_b=rw_ln_b[0].reshape(1, rw),
              hg_norm_w=hg_norm_w[0].reshape(1, hw), npm=norm_post_mix[0].reshape(1, d),
              npf=norm_pre_ffn[0].reshape(1, d), npo=norm_post_ffn[0].reshape(1, d),
              w_out=w_out[0].astype(BF16), wr_hi=wr_hi, wr_lo=wr_lo, e128=e128, e128t=e128t)
    ec = dict(e_bias=e_bias[0], wg=w_exp_gate[0].astype(BF16), wu=w_exp_up[0].astype(BF16),
              wd=w_exp_down[0].astype(BF16), sg=w_sh_gate[0].astype(BF16), su=w_sh_up[0].astype(BF16),
              sd=w_sh_down[0].astype(BF16))
    nbp = c_prompt.shape[0]
    mod = _ada(jnp.concatenate([c_prompt, c_sample], axis=0), w_ada[0], b_ada[0]).reshape(-1, 6, d)
    y_prompt = _trunk(x_prompt, mod[:nbp], wc, oc, ec, norm_pre_mix[0])
    y_sample = _trunk(x_sample, mod[nbp:], wc, oc, ec, norm_pre_mix[0])
    return (y_prompt, y_sample)
```

```python
import functools
import math

import jax
import jax.numpy as jnp
from jax import lax
from jax.experimental import pallas as pl
from jax.experimental.pallas import tpu as pltpu

F32 = jnp.float32
BF16 = jnp.bfloat16
I32 = jnp.int32

RW_HEAD_DIM = 64
W_LORA = 64
A_LORA = 64
G_LORA = 128
RW_GN_EPS = 64e-5
HG_HEADS = 8
N_EXPERTS = 64
TOP_K = 6
N_GROUPS = 8
TOPK_GROUPS = 4
ROUTED_SCALE = 2.5
EXPERT_BLOCK = 256
NORM_EPS = 1e-6

LANES = 128
SUBLANES = 8
VMEM_LIMIT = 56 * 1024 * 1024

CHUNK = 64


def _cparams(sem, vmem=VMEM_LIMIT):
    return pltpu.CompilerParams(dimension_semantics=sem, vmem_limit_bytes=vmem)


def _sigmoid(x):
    return 1.0 / (1.0 + jnp.exp(-x))


def _silu(x):
    return x * _sigmoid(x)


def _dot(a, b):
    return jnp.dot(a.astype(BF16), b.astype(BF16), preferred_element_type=F32)


def _dot_nt(a, b):
    return lax.dot_general(a.astype(BF16), b.astype(BF16), (((1,), (1,)), ((), ())),
                           preferred_element_type=F32)


def _dot_tn(a, b):
    return lax.dot_general(a.astype(BF16), b.astype(BF16), (((0,), (0,)), ((), ())),
                           preferred_element_type=F32)


def _split2(x):
    hi = x.astype(BF16)
    lo = (x - hi.astype(F32)).astype(BF16)
    return hi, lo


def _split3(x):
    hi = x.astype(BF16)
    r1 = x - hi.astype(F32)
    mid = r1.astype(BF16)
    lo = (r1 - mid.astype(F32)).astype(BF16)
    return hi, mid, lo


def _seg_sum(x, e, et):
    hi, lo = _split2(x)
    s = jnp.dot(hi, e, preferred_element_type=F32) + jnp.dot(lo, e, preferred_element_type=F32)
    shi, slo = _split2(s)
    return jnp.dot(shi, et, preferred_element_type=F32) + jnp.dot(slo, et, preferred_element_type=F32)


PACK_W = 2 * LANES


def _pack_rows(x_bf, out_ref):
    n, d = x_bf.shape
    ns = d // PACK_W
    for s in range(ns):
        a = lax.bitcast_convert_type(x_bf[:, s * PACK_W:s * PACK_W + LANES].astype(F32), jnp.uint32)
        b = lax.bitcast_convert_type(x_bf[:, s * PACK_W + LANES:(s + 1) * PACK_W].astype(F32), jnp.uint32)
        out_ref[pl.ds(s, n, stride=ns), :] = a | (b >> 16)


def _unpack_rows(x_ref, ns):
    n = x_ref.shape[0] // ns
    parts = []
    for s in range(ns):
        w = x_ref[pl.ds(s, n, stride=ns), :]
        parts.append(lax.bitcast_convert_type(w & jnp.uint32(0xFFFF0000), F32).astype(BF16))
        parts.append(lax.bitcast_convert_type(w << 16, F32).astype(BF16))
    return jnp.concatenate(parts, axis=1)


def _rms_rows(x, g):
    return x * lax.rsqrt(jnp.mean(x * x, axis=-1, keepdims=True) + NORM_EPS) * g


def _ada_kernel(c_ref, w_ref, b_ref, o_ref):
    c = c_ref[...]
    o_ref[...] = _dot(_silu(c), w_ref[...]) + b_ref[...]


def _ada(c, w_ada, b_ada):
    nb, d = c.shape
    n = w_ada.shape[1]
    tn = 512
    return pl.pallas_call(
        _ada_kernel,
        grid=(n // tn,),
        in_specs=[pl.BlockSpec((nb, d), lambda j: (0, 0)),
                  pl.BlockSpec((d, tn), lambda j: (0, j)),
                  pl.BlockSpec((1, tn), lambda j: (0, j))],
        out_specs=pl.BlockSpec((nb, tn), lambda j: (0, j)),
        out_shape=jax.ShapeDtypeStruct((nb, n), F32),
        compiler_params=_cparams(("parallel",)),
        name="ada",
    )(c, w_ada, b_ada.reshape(1, n))


def _inproj_kernel(x_ref, mod_ref, g_ref, w_ref, o_ref, h_scr):
    @pl.when(pl.program_id(1) == 0)
    def _():
        m = mod_ref[0]
        h = _rms_rows(x_ref[...], g_ref[...]) * (1.0 + m[1:2]) + m[0:1]
        h_scr[...] = h.astype(BF16)

    o_ref[...] = jnp.dot(h_scr[...], w_ref[...], preferred_element_type=F32)


def _inproj(x2, mod, g, w_bf, seq):
    t, d = x2.shape
    n = w_bf.shape[1]
    tm, tn = 1024, 512
    return pl.pallas_call(
        _inproj_kernel,
        grid=(t // tm, n // tn),
        in_specs=[pl.BlockSpec((tm, d), lambda i, j: (i, 0)),
                  pl.BlockSpec((1, 6, d), lambda i, j: ((i * tm) // seq, 0, 0)),
                  pl.BlockSpec((1, d), lambda i, j: (0, 0)),
                  pl.BlockSpec((d, tn), lambda i, j: (0, j))],
        out_specs=pl.BlockSpec((tm, tn), lambda i, j: (i, j)),
        out_shape=jax.ShapeDtypeStruct((t, n), F32),
        scratch_shapes=[pltpu.VMEM((tm, d), BF16)],
        compiler_params=_cparams(("parallel", "arbitrary")),
        name="inproj",
    )(x2, mod, g.reshape(1, d), w_bf)


def _rwprep_kernel(seq, tm, rw,
                   z_ref, zp_ref, zn_ref, l_ref, lp_ref, ln_ref,
                   mu_ref, mul_ref, wup_ref, aup_ref, gup_ref, w0_ref, a0_ref,
                   kk_ref, ka_ref, rk_ref, e_ref, et_ref,
                   r_o, v_o, kk_o, g_o, bon_o, lw0_o, lw1_o, b0_o, b1_o, kd0_o, kd1_o):
    i = pl.program_id(0)
    first = (i * tm) % seq == 0
    last = ((i + 1) * tm) % seq == 0

    def shifted(cur, prev_blk, next_blk, mu):
        rows = lax.broadcasted_iota(I32, cur.shape, 0)
        prow = jnp.where(first, 0.0, prev_blk[SUBLANES - 1:SUBLANES, :])
        nrow = jnp.where(last, 0.0, next_blk[0:1, :])
        prev = jnp.where(rows == 0, prow, pltpu.roll(cur, 1, axis=0))
        nxt = jnp.where(rows == tm - 1, nrow, pltpu.roll(cur, tm - 1, axis=0))
        return cur + mu * (0.5 * (prev + nxt) - cur)

    lat = shifted(l_ref[...], lp_ref[...], ln_ref[...], mul_ref[...])
    w_lat = lat[:, 0:2 * W_LORA]
    a_lat = lat[:, 2 * W_LORA:2 * W_LORA + 2 * A_LORA]
    g_lat = lat[:, 2 * W_LORA + 2 * A_LORA:2 * W_LORA + 2 * A_LORA + G_LORA]
    w_raw = _dot(jnp.tanh(w_lat), wup_ref[...]) + w0_ref[...]
    a_all = _sigmoid(_dot(a_lat, aup_ref[...]) + a0_ref[...])
    g_o[...] = _dot(_sigmoid(g_lat), gup_ref[...])
    lw = (-math.exp(-0.5)) * _sigmoid(w_raw)
    lw0_o[...] = lw[:, :rw]
    lw1_o[...] = lw[:, rw:]

    r = shifted(z_ref[:, 0:rw], zp_ref[:, 0:rw], zn_ref[:, 0:rw], mu_ref[:, 0:rw])
    k = shifted(z_ref[:, rw:2 * rw], zp_ref[:, rw:2 * rw], zn_ref[:, rw:2 * rw], mu_ref[:, rw:2 * rw])
    v = shifted(z_ref[:, 2 * rw:3 * rw], zp_ref[:, 2 * rw:3 * rw], zn_ref[:, 2 * rw:3 * rw],
                mu_ref[:, 2 * rw:3 * rw])
    r_o[...] = r
    v_o[...] = v
    kk = k * kk_ref[...]
    ss = _seg_sum(kk * kk, e_ref[...], et_ref[...])
    kk = kk / jnp.maximum(jnp.sqrt(ss), 1e-12)
    kk_o[...] = kk
    a0 = a_all[:, :rw]
    a1 = a_all[:, rw:]
    b0_o[...] = kk * a0
    b1_o[...] = kk * a1
    kd0 = k * (1.0 + (a0 - 1.0) * ka_ref[...])
    kd1 = k * (1.0 + (a1 - 1.0) * ka_ref[...])
    kd0_o[...] = kd0
    kd1_o[...] = kd1
    kb = 0.5 * (kd0 + kd1)
    bon_o[...] = _seg_sum(r * kb * rk_ref[...], e_ref[...], et_ref[...]) * v


def _rwprep(z, seq, rw, lat_off, mu_rkv, mu_lat, wup, aup, gup, w0, a0, k_k, k_a, r_k, e64, e64t):
    t = z.shape[0]
    tm = 256
    nlat = 512
    nrkv = 3 * rw
    tb = tm // SUBLANES
    nblk8 = t // SUBLANES
    lat_blk = lat_off // nlat

    def cur(i):
        return (i, 0)

    def prv(i):
        return (jnp.maximum(i * tb - 1, 0), 0)

    def nxt(i):
        return (jnp.minimum((i + 1) * tb, nblk8 - 1), 0)

    def full(shape):
        return pl.BlockSpec(shape, lambda i: (0,) * len(shape))

    out = jax.ShapeDtypeStruct((t, rw), F32)
    ospec = pl.BlockSpec((tm, rw), lambda i: (i, 0))
    return pl.pallas_call(
        functools.partial(_rwprep_kernel, seq, tm, rw),
        grid=(t // tm,),
        in_specs=[pl.BlockSpec((tm, nrkv), cur),
                  pl.BlockSpec((SUBLANES, nrkv), prv),
                  pl.BlockSpec((SUBLANES, nrkv), nxt),
                  pl.BlockSpec((tm, nlat), lambda i: (i, lat_blk)),
                  pl.BlockSpec((SUBLANES, nlat), lambda i: (jnp.maximum(i * tb - 1, 0), lat_blk)),
                  pl.BlockSpec((SUBLANES, nlat), lambda i: (jnp.minimum((i + 1) * tb, nblk8 - 1), lat_blk)),
                  full((1, nrkv)), full((1, nlat)),
                  full(wup.shape), full(aup.shape), full(gup.shape),
                  full((1, 2 * rw)), full((1, 2 * rw)),
                  full((1, rw)), full((1, rw)), full((1, rw)),
                  full(e64.shape), full(e64t.shape)],
        out_specs=[ospec] * 11,
        out_shape=[out] * 11,
        compiler_params=_cparams(("parallel",)),
        name="rwprep",
    )(z, z, z, z, z, z, mu_rkv, mu_lat, wup, aup, gup, w0, a0, k_k, k_a, r_k, e64, e64t)


def _tri(n, rev):
    i = lax.broadcasted_iota(I32, (n, n), 0)
    j = lax.broadcasted_iota(I32, (n, n), 1)
    m = (j >= i) if rev else (j <= i)
    return jnp.where(m, 1.0, 0.0).astype(BF16)


def _rw_streams(streams):
    c = streams[0][0].shape[0]
    hd = RW_HEAD_DIM
    n2 = 2 * c
    ns = len(streams)
    revs = [s[7] for s in streams]
    lane = lax.broadcasted_iota(I32, (c, LANES), 1)
    head_a = lane < hd
    ri = lax.broadcasted_iota(I32, (n2, n2), 0)
    ci = lax.broadcasted_iota(I32, (n2, n2), 1)
    ti = ri % c
    si = ci % c
    same16 = (ri // 16) == (ci // 16)
    same32 = (ri // 32) == (ci // 32)
    mid32 = jnp.logical_and(same32, jnp.logical_not(same16))
    eye = jnp.where(ri == ci, 1.0, 0.0)
    strict = {False: si < ti, True: si > ti}
    incl = {False: si <= ti, True: si >= ti}
    tri = {rev: _tri(c, rev) for rev in set(revs)}
    ei = lax.broadcasted_iota(I32, (LANES, LANES), 0)
    ej = lax.broadcasted_iota(I32, (LANES, LANES), 1)
    eye_k = ei == ej

    def pair(x):
        return jnp.concatenate([jnp.where(head_a, x, 0.0), jnp.where(head_a, 0.0, x)], axis=0)

    cum = []
    for (r, v, kk, lw, b, kd, s_in, rev) in streams:
        hi, lo = _split2(lw)
        cs = jnp.dot(tri[rev], jnp.concatenate([hi, lo], axis=1), preferred_element_type=F32)
        cum.append(cs[:, :LANES] + cs[:, LANES:])
    ops = []
    for (r, v, kk, lw, b, kd, s_in, rev), cm in zip(streams, cum):
        tot = cm[0:1, :] if rev else cm[c - 1:c, :]
        g_inv = jnp.exp(-cm)
        g_tail = jnp.exp(tot - cm)
        ops.append(dict(p2=pair(-kk * jnp.exp(cm - lw)), r2=pair(r * jnp.exp(cm)),
                        bi2=pair(b * g_inv), ki2=pair(kd * g_inv), bt2=pair(b * g_tail),
                        kt2=pair(kd * g_tail), v2=pair(v), g_tot=jnp.exp(tot)))
    gm = [_dot_nt(jnp.concatenate([o['p2'], o['r2']], axis=0), jnp.concatenate([o['bi2'], o['ki2']], axis=0))
          for o in ops]
    a2 = [jnp.where(strict[rev], g[:n2, :n2], 0.0) for g, rev in zip(gm, revs)]
    b2 = [jnp.where(strict[rev], g[:n2, n2:], 0.0) for g, rev in zip(gm, revs)]
    ap2 = [jnp.where(incl[rev], g[n2:, :n2], 0.0) for g, rev in zip(gm, revs)]
    bp2 = [jnp.where(incl[rev], g[n2:, n2:], 0.0) for g, rev in zip(gm, revs)]
    bv = [_dot(x, o['v2']) for x, o in zip(b2, ops)]
    bpv = [_dot(x, o['v2']) for x, o in zip(bp2, ops)]

    x = [jnp.where(same16, a, 0.0) for a in a2]
    tinv = [eye + xi for xi in x]
    for _ in range(3):
        x = [_dot(xi, xi) for xi in x]
        tinv = [t + _dot(t, xi) for t, xi in zip(tinv, x)]
    for lvl in (mid32, jnp.logical_not(same32)):
        y = [_dot(t, jnp.where(lvl, a, 0.0)) for t, a in zip(tinv, a2)]
        tinv = [t + _dot(yi, t) for t, yi in zip(tinv, y)]

    wu = [_dot(t, jnp.concatenate([o['p2'], bvi], axis=1)) for t, o, bvi in zip(tinv, ops, bv)]
    qo = [_dot(a, w) for a, w in zip(ap2, wu)]
    m2 = [_dot_tn(w[:, :LANES], o['bt2']) + jnp.where(eye_k, jnp.broadcast_to(o['g_tot'], (LANES, LANES)), 0.0)
          for w, o in zip(wu, ops)]
    nn2 = [_dot_tn(w[:, LANES:], o['bt2']) + _dot_tn(o['v2'], o['kt2']) for w, o in zip(wu, ops)]

    nt = (((1,), (1,)), ((), ()))
    outs = []
    for i in range(ns):
        s_hi, s_lo = _split2(streams[i][6])
        q2b = (ops[i]['r2'] + qo[i][:, :LANES]).astype(BF16)
        oo = (lax.dot_general(q2b, s_hi, nt, preferred_element_type=F32)
              + lax.dot_general(q2b, s_lo, nt, preferred_element_type=F32)
              + qo[i][:, LANES:] + bpv[i])
        m2b = m2[i].astype(BF16)
        s_out = (jnp.dot(s_hi, m2b, preferred_element_type=F32)
                 + jnp.dot(s_lo, m2b, preferred_element_type=F32) + nn2[i])
        outs.append((oo[:c] + oo[c:], s_out))
    return outs


RW_PAIRS_PER_STEP = 8


def _rwscan_kernel(rf, vf, kkf, lwf, bf, kdf, rb, vb, kkb, lwb, bb, kdb, of_ref, ob_ref, sf, sb):
    @pl.when(pl.program_id(2) == 0)
    def _():
        sf[...] = jnp.zeros_like(sf)
        sb[...] = jnp.zeros_like(sb)

    streams = []
    for p in range(RW_PAIRS_PER_STEP):
        sl = slice(p * LANES, (p + 1) * LANES)
        streams.append((rf[:, sl], vf[:, sl], kkf[:, sl], lwf[:, sl], bf[:, sl], kdf[:, sl], sf[p], False))
        streams.append((rb[:, sl], vb[:, sl], kkb[:, sl], lwb[:, sl], bb[:, sl], kdb[:, sl], sb[p], True))
    outs = _rw_streams(streams)
    for p in range(RW_PAIRS_PER_STEP):
        sl = slice(p * LANES, (p + 1) * LANES)
        of_ref[:, sl], sf[p] = outs[2 * p]
        ob_ref[:, sl], sb[p] = outs[2 * p + 1]


def _rwscan(nb, seq, r, v, kk, lw0, lw1, b0, b1, kd0, kd1):
    t, rw = r.shape
    nc = seq // CHUNK
    wblk = RW_PAIRS_PER_STEP * LANES
    fw = pl.BlockSpec((CHUNK, wblk), lambda bi, hp, c: (bi * nc + c, hp))
    bw = pl.BlockSpec((CHUNK, wblk), lambda bi, hp, c: (bi * nc + nc - 1 - c, hp))
    out = jax.ShapeDtypeStruct((t, rw), F32)
    state = pltpu.VMEM((RW_PAIRS_PER_STEP, LANES, LANES), F32)
    return pl.pallas_call(
        _rwscan_kernel,
        grid=(nb, rw // wblk, nc),
        in_specs=[fw] * 6 + [bw] * 6,
        out_specs=[fw, bw],
        out_shape=[out, out],
        scratch_shapes=[state, state],
        compiler_params=_cparams(("parallel", "parallel", "arbitrary")),
        name="rwscan",
    )(r, v, kk, lw0, b0, kd0, r, v, kk, lw1, b1, kd1)


def _hg_streams(streams):
    c, dk = streams[0][0].shape
    revs = [s[5] for s in streams]
    tri = {rev: _tri(c, rev) for rev in set(revs)}
    row = lax.broadcasted_iota(I32, (c, dk), 0)
    ri = lax.broadcasted_iota(I32, (c, c), 0)
    ci = lax.broadcasted_iota(I32, (c, c), 1)

    cum = []
    for (q, k, v, lf, st, rev) in streams:
        hi, mid, lo = _split3(lf)
        cs = jnp.dot(tri[rev], jnp.concatenate([hi, mid, lo], axis=1), preferred_element_type=F32)
        cum.append(cs[:, :dk] + cs[:, dk:2 * dk] + cs[:, 2 * dk:])
    scores = [jnp.where(ri == ci, jnp.sum(s[0] * s[1], axis=1, keepdims=True), 0.0) for s in streams]
    sub = row % SUBLANES
    nt = (((1,), (1,)), ((), ()))

    def sub_bcast(x, idx):
        x3 = x.reshape(c // SUBLANES, SUBLANES, dk)
        return jnp.broadcast_to(x3[:, idx:idx + 1, :], x3.shape).reshape(c, dk)

    h = c // 2
    while h >= 1:
        blk = 2 * h
        upper = (row % blk) >= h
        same_blk = (ri // blk) == (ci // blk)
        r_up = (ri % blk) >= h
        c_up = (ci % blk) >= h
        q_rows = {False: upper, True: jnp.logical_not(upper)}
        pmask = {False: jnp.logical_and(same_blk, jnp.logical_and(r_up, jnp.logical_not(c_up))),
                 True: jnp.logical_and(same_blk, jnp.logical_and(jnp.logical_not(r_up), c_up))}
        sl = []
        for i, (q, k, v, lf, st, rev) in enumerate(streams):
            cm = cum[i]
            off = h if rev else h - 1
            if h >= SUBLANES:
                pieces = [jnp.broadcast_to(cm[m0 + off:m0 + off + 1, :], (blk, dk)) for m0 in range(0, c, blk)]
                ref = jnp.concatenate(pieces, axis=0) if len(pieces) > 1 else pieces[0]
            elif blk == SUBLANES:
                ref = sub_bcast(cm, off)
            elif 2 * blk == SUBLANES:
                ref = jnp.where(sub < blk, sub_bcast(cm, off), sub_bcast(cm, off + blk))
            else:
                ref = jnp.where(q_rows[rev], pltpu.roll(cm, c - 1 if rev else 1, axis=0), cm)
            dlt = cm - ref
            e = jnp.minimum(jnp.where(q_rows[rev], dlt, -dlt), 0.0)
            x = (jnp.where(q_rows[rev], q, k) * jnp.exp(e)).astype(BF16)
            sl.append(lax.dot_general(x, x, nt, preferred_element_type=F32))
        scores = [sc + jnp.where(pmask[rev], x, 0.0) for sc, x, rev in zip(scores, sl, revs)]
        h //= 2

    outs = []
    for (q, k, v, lf, st, rev), cm, sc in zip(streams, cum, scores):
        tot = cm[0:1, :] if rev else cm[c - 1:c, :]
        o = _dot(sc, v) + _dot_nt(q * jnp.exp(cm), st)
        st_new = st * jnp.exp(tot) + _dot_tn(v, k * jnp.exp(tot - cm))
        outs.append((o, st_new))
    return outs


HG_HEADS_PER_STEP = 4


def _hgscan_kernel(dk, qf, fff, i_f, qb, ffb, i_b, lb_ref, of_ref, ob_ref, sf, sb):
    @pl.when(pl.program_id(2) == 0)
    def _():
        sf[...] = jnp.zeros_like(sf)
        sb[...] = jnp.zeros_like(sb)

    streams = []
    for p in range(HG_HEADS_PER_STEP):
        sl = slice(p * dk, (p + 1) * dk)
        for (q_ref, ff_ref, i_ref, st_ref, d) in ((qf, fff, i_f, sf, 0), (qb, ffb, i_b, sb, 1)):
            lbv = lb_ref[d:d + 1, sl]
            f = lbv + (1.0 - lbv) * _sigmoid(ff_ref[:, sl])
            streams.append((_silu(q_ref[:, sl]), 1.0 - f, i_ref[:, sl], jnp.log(f), st_ref[p], d == 1))
    outs = _hg_streams(streams)
    for p in range(HG_HEADS_PER_STEP):
        sl = slice(p * dk, (p + 1) * dk)
        of_ref[:, sl], sf[p] = outs[2 * p]
        ob_ref[:, sl], sb[p] = outs[2 * p + 1]


def _hgscan(nb, seq, z, lb, hg_off, hw):
    t = z.shape[0]
    nc = seq // CHUNK
    dk = hw // HG_HEADS
    wblk = HG_HEADS_PER_STEP * dk
    base = hg_off // wblk
    nh = hw // wblk

    def fw(comp):
        return pl.BlockSpec((CHUNK, wblk), lambda bi, h, c: (bi * nc + c, base + comp * nh + h))

    def bw(comp):
        return pl.BlockSpec((CHUNK, wblk), lambda bi, h, c: (bi * nc + nc - 1 - c, base + comp * nh + h))

    out = jax.ShapeDtypeStruct((t, hw), F32)
    state = pltpu.VMEM((HG_HEADS_PER_STEP, dk, dk), F32)
    return pl.pallas_call(
        functools.partial(_hgscan_kernel, dk),
        grid=(nb, nh, nc),
        in_specs=[fw(0), fw(1), fw(3), bw(0), bw(2), bw(3),
                  pl.BlockSpec((2, wblk), lambda bi, h, c: (0, h))],
        out_specs=[pl.BlockSpec((CHUNK, wblk), lambda bi, h, c: (bi * nc + c, h)),
                   pl.BlockSpec((CHUNK, wblk), lambda bi, h, c: (bi * nc + nc - 1 - c, h))],
        out_shape=[out, out],
        scratch_shapes=[state, state],
        compiler_params=_cparams(("parallel", "parallel", "arbitrary")),
        name="hgscan",
    )(z, z, z, z, z, z, lb)


def _blockdiag2(w):
    _, r, n = w.shape
    z = jnp.zeros((r, n), w.dtype)
    return jnp.concatenate([jnp.concatenate([w[0], z], axis=1), jnp.concatenate([z, w[1]], axis=1)], axis=0)


def _indicator(width, seg):
    e = (jnp.arange(width)[:, None] // seg == jnp.arange(width // seg)[None, :]).astype(BF16)
    return e, e.T


def _layer_consts(w_in, rw_mu, rw_w0, rw_w_up, rw_a0, rw_a_up, rw_g_up, rw_k_k, rw_k_a, rw_r_k,
                  hg_lb_gamma):
    rw = rw_k_k.shape[-1]
    d = w_in.shape[1]
    nlat = 2 * W_LORA + 2 * A_LORA + G_LORA
    w = w_in[0]
    rkv = 3 * rw
    hg_cols = w.shape[1] - rkv - nlat
    pad = 512 - nlat
    w_perm = jnp.concatenate([w[:, :rkv], w[:, rkv + nlat:], w[:, rkv:rkv + nlat],
                              jnp.zeros((d, pad), w.dtype)], axis=1).astype(BF16)
    mu = rw_mu[0]
    lower = jnp.cumsum(jax.nn.softmax(hg_lb_gamma.astype(F32), axis=0), axis=0)[0]
    hw = lower.shape[-1]
    e64, e64t = _indicator(rw, RW_HEAD_DIM)
    return dict(
        rw=rw, hw=hw, hg_off=rkv, lat_off=rkv + hg_cols, w_in=w_perm,
        mu_rkv=mu[:rkv].reshape(1, rkv),
        mu_lat=jnp.pad(mu[rkv:rkv + nlat], (0, pad)).reshape(1, 512),
        wup=_blockdiag2(rw_w_up[0]).astype(BF16), aup=_blockdiag2(rw_a_up[0]).astype(BF16),
        gup=rw_g_up[0].astype(BF16),
        w0=rw_w0[0].reshape(1, 2 * rw), a0=rw_a0[0].reshape(1, 2 * rw),
        k_k=rw_k_k[0].reshape(1, rw), k_a=rw_k_a[0].reshape(1, rw), r_k=rw_r_k[0].reshape(1, rw),
        e64=e64, e64t=e64t,
        lb=lower,
    )


def _mixer(x2, nb, seq, mod, norm_pre_mix, wc):
    z = _inproj(x2, mod, norm_pre_mix, wc['w_in'], seq)
    (r, v, kk, g, bonus, lw0, lw1, b0, b1, kd0, kd1) = _rwprep(
        z, seq, wc['rw'], wc['lat_off'], wc['mu_rkv'], wc['mu_lat'], wc['wup'], wc['aup'], wc['gup'],
        wc['w0'], wc['a0'], wc['k_k'], wc['k_a'], wc['r_k'], wc['e64'], wc['e64t'])
    rw_of, rw_ob = _rwscan(nb, seq, r, v, kk, lw0, lw1, b0, b1, kd0, kd1)
    hg_of, hg_ob = _hgscan(nb, seq, z, wc['lb'], wc['hg_off'], wc['hw'])
    return dict(z=z, r=r, v=v, kk=kk, g=g, bonus=bonus, lw0=lw0, rw_of=rw_of, rw_ob=rw_ob,
                hg_of=hg_of, hg_ob=hg_ob)


def _outproj_kernel(rw, x_ref, rf_ref, rb_ref, bon_ref, g_ref, hf_ref, hb_ref, hgg_ref, mod_ref,
                    lnw_ref, lnb_ref, hnw_ref, npm_ref, npf_ref, wout_ref, wrh_ref, wrl_ref,
                    e64_ref, e64t_ref, e128_ref, e128t_ref,
                    x1_ref, h2_ref, lg_ref):
    m6 = mod_ref[0]
    o = rf_ref[...] + rb_ref[...]
    mean = _seg_sum(o, e64_ref[...], e64t_ref[...]) * (1.0 / RW_HEAD_DIM)
    dlt = o - mean
    var = _seg_sum(dlt * dlt, e64_ref[...], e64t_ref[...]) * (1.0 / RW_HEAD_DIM)
    o_rw = (dlt * lax.rsqrt(var + RW_GN_EPS) * lnw_ref[...] + lnb_ref[...] + bon_ref[...]) * g_ref[...]
    oh = hf_ref[...] + hb_ref[...]
    hd = oh.shape[1] // HG_HEADS
    ms = _seg_sum(oh * oh, e128_ref[...], e128t_ref[...]) * (1.0 / hd)
    o_hg = oh * lax.rsqrt(ms + NORM_EPS) * hnw_ref[...] * _silu(hgg_ref[...])
    m = _dot(o_rw, wout_ref[0:rw, :]) + _dot(o_hg, wout_ref[rw:, :])
    x1 = x_ref[...] + m6[2:3] * _rms_rows(m, npm_ref[...])
    x1_ref[...] = x1
    h2 = _rms_rows(x1, npf_ref[...]) * (1.0 + m6[4:5]) + m6[3:4]
    hi, lo = _split2(h2)
    _pack_rows(hi, h2_ref)
    nt = (((1,), (1,)), ((), ()))
    lg_ref[...] = (lax.dot_general(wrh_ref[...], hi, nt, preferred_element_type=F32)
                   + lax.dot_general(wrh_ref[...], lo, nt, preferred_element_type=F32)
                   + lax.dot_general(wrl_ref[...], hi, nt, preferred_element_type=F32))


def _outproj(x2, seq, mx, z, mod, wc, oc):
    t, d = x2.shape
    rw, hw = wc['rw'], wc['hw']
    tm = 256
    gblk = (wc['hg_off'] + 4 * hw) // hw

    def row(w):
        return pl.BlockSpec((tm, w), lambda i: (i, 0))

    def full(a):
        return pl.BlockSpec(a.shape, lambda i: (0,) * a.ndim)

    consts = [oc['ln_w'], oc['ln_b'], oc['hg_norm_w'], oc['npm'], oc['npf'], oc['w_out'], oc['wr_hi'],
              oc['wr_lo'], wc['e64'], wc['e64t'], oc['e128'], oc['e128t']]
    return pl.pallas_call(
        functools.partial(_outproj_kernel, rw),
        grid=(t // tm,),
        in_specs=[row(d), row(rw), row(rw), row(rw), row(rw), row(hw), row(hw),
                  pl.BlockSpec((tm, hw), lambda i: (i, gblk)),
                  pl.BlockSpec((1, 6, d), lambda i: ((i * tm) // seq, 0, 0))] + [full(a) for a in consts],
        out_specs=[row(d), pl.BlockSpec((tm * (d // PACK_W), LANES), lambda i: (i, 0)),
                   pl.BlockSpec((N_EXPERTS, tm), lambda i: (0, i))],
        out_shape=[jax.ShapeDtypeStruct((t, d), F32), jax.ShapeDtypeStruct((t * (d // PACK_W), LANES), jnp.uint32),
                   jax.ShapeDtypeStruct((N_EXPERTS, t), F32)],
        compiler_params=_cparams(("parallel",)),
        name="outproj",
    )(x2, mx['rw_of'], mx['rw_ob'], mx['bonus'], mx['g'], mx['hg_of'], mx['hg_ob'], z, mod, *consts)


ROUTE_TILE = 512


def _route_kernel(lg_ref, bias_ref, ut_ref, eidx_ref, wsel_ref, rank_ref, cnt_ref, carry):
    @pl.when(pl.program_id(0) == 0)
    def _():
        carry[...] = jnp.zeros_like(carry)

    ne, tt = lg_ref.shape
    gsz = ne // N_GROUPS
    neg = -jnp.inf
    s = _sigmoid(lg_ref[...])
    biased = s + bias_ref[...]
    io_g = lax.broadcasted_iota(I32, (gsz, tt), 0)
    gs_rows = []
    for gi in range(N_GROUPS):
        blk = biased[gi * gsz:(gi + 1) * gsz, :]
        m1 = jnp.max(blk, axis=0, keepdims=True)
        first = jnp.min(jnp.where(blk == m1, io_g, gsz), axis=0, keepdims=True)
        m2 = jnp.max(jnp.where(io_g == first, neg, blk), axis=0, keepdims=True)
        gs_rows.append(m1 + m2)
    gs = jnp.concatenate(gs_rows, axis=0)
    io_n = lax.broadcasted_iota(I32, (N_GROUPS, tt), 0)
    selg = jnp.zeros((N_GROUPS, tt), jnp.bool_)
    for _ in range(TOPK_GROUPS):
        m = jnp.max(gs, axis=0, keepdims=True)
        first = jnp.min(jnp.where(gs == m, io_n, N_GROUPS), axis=0, keepdims=True)
        pick = io_n == first
        selg = jnp.logical_or(selg, pick)
        gs = jnp.where(pick, neg, gs)
    emask = jnp.concatenate([jnp.broadcast_to(selg[gi:gi + 1, :], (gsz, tt)) for gi in range(N_GROUPS)],
                            axis=0)
    mb = jnp.where(emask, biased, neg)
    io_e = lax.broadcasted_iota(I32, (ne, tt), 0)
    sel = jnp.zeros((ne, tt), jnp.bool_)
    picks, idxs, ws = [], [], []
    for _ in range(TOP_K):
        m = jnp.max(mb, axis=0, keepdims=True)
        first = jnp.min(jnp.where(mb == m, io_e, ne), axis=0, keepdims=True)
        pick = io_e == first
        picks.append(pick)
        idxs.append(first)
        ws.append(jnp.sum(jnp.where(pick, s, 0.0), axis=0, keepdims=True))
        sel = jnp.logical_or(sel, pick)
        mb = jnp.where(pick, neg, mb)
    wsum = ws[0]
    for w in ws[1:]:
        wsum = wsum + w
    pos = jnp.dot(jnp.where(sel, 1.0, 0.0).astype(BF16), ut_ref[...], preferred_element_type=F32) + carry[...]
    ranks = [jnp.sum(jnp.where(p, pos, 0.0), axis=0, keepdims=True).astype(I32) for p in picks]
    carry[...] = carry[...] + jnp.sum(jnp.where(sel, 1.0, 0.0), axis=1, keepdims=True)
    zi = jnp.zeros((SUBLANES - TOP_K, tt), I32)
    eidx_ref[...] = jnp.concatenate(idxs + [zi], axis=0)
    rank_ref[...] = jnp.concatenate(ranks + [zi], axis=0)
    wsel_ref[...] = jnp.concatenate([w / wsum * ROUTED_SCALE for w in ws] + [zi.astype(F32)], axis=0)
    cnt_ref[...] = jnp.broadcast_to(carry[...], cnt_ref.shape).astype(I32)


def _route(logits_t, e_bias):
    ne, t = logits_t.shape
    tt = ROUTE_TILE
    ut = (jnp.arange(tt)[:, None] < jnp.arange(tt)[None, :]).astype(BF16)
    tok = pl.BlockSpec((SUBLANES, tt), lambda i: (0, i))
    return pl.pallas_call(
        _route_kernel,
        grid=(t // tt,),
        in_specs=[pl.BlockSpec((ne, tt), lambda i: (0, i)),
                  pl.BlockSpec((ne, 1), lambda i: (0, 0)),
                  pl.BlockSpec((tt, tt), lambda i: (0, 0))],
        out_specs=[tok, tok, tok, pl.BlockSpec((ne, LANES), lambda i: (0, 0))],
        out_shape=[jax.ShapeDtypeStruct((SUBLANES, t), I32), jax.ShapeDtypeStruct((SUBLANES, t), F32),
                   jax.ShapeDtypeStruct((SUBLANES, t), I32), jax.ShapeDtypeStruct((ne, LANES), I32)],
        scratch_shapes=[pltpu.VMEM((ne, 1), F32)],
        compiler_params=_cparams(("arbitrary",)),
        name="route",
    )(logits_t, e_bias.reshape(ne, 1), ut)


DISPATCH_TILE = 128
PLAN_TILE = 1024


def _plan_kernel(eidx_ref, rank_ref, ps_ref, dest_ref):
    ne = ps_ref.shape[0]
    tp = eidx_ref.shape[1]
    io_e = lax.broadcasted_iota(I32, (ne, tp), 0)
    ps = ps_ref[...]
    rows = []
    for j in range(TOP_K):
        start = jnp.sum(jnp.where(io_e == eidx_ref[j:j + 1, :], ps, 0.0), axis=0, keepdims=True)
        rows.append(start.astype(I32) + rank_ref[j:j + 1, :])
    dest = jnp.concatenate(rows + [jnp.zeros((SUBLANES - TOP_K, tp), I32)], axis=0)
    for i in range(tp // DISPATCH_TILE):
        dest_ref[i] = dest[:, i * DISPATCH_TILE:(i + 1) * DISPATCH_TILE]


def _plan(eidx, rank, pad_start):
    t = eidx.shape[1]
    tp = PLAN_TILE
    ne = pad_start.shape[0]
    tok = pl.BlockSpec((SUBLANES, tp), lambda i: (0, i))
    ntile = tp // DISPATCH_TILE
    return pl.pallas_call(
        _plan_kernel,
        grid=(t // tp,),
        in_specs=[tok, tok, pl.BlockSpec((ne, 1), lambda i: (0, 0))],
        out_specs=pl.BlockSpec((ntile, SUBLANES, DISPATCH_TILE), lambda i: (i, 0, 0)),
        out_shape=jax.ShapeDtypeStruct((t // DISPATCH_TILE, SUBLANES, DISPATCH_TILE), I32),
        compiler_params=_cparams(("parallel",)),
        name="plan",
    )(eidx, rank, pad_start.astype(F32).reshape(ne, 1))


def _dispatch_kernel(tt, dest_ref, h_ref, xs_in_ref, xs_ref, idx_smem, isem, sem):
    del xs_in_ref
    i = pl.program_id(0)
    n = tt * TOP_K
    icp = pltpu.make_async_copy(dest_ref.at[pl.ds(i * tt * SUBLANES, n)], idx_smem, isem)
    icp.start()
    icp.wait()

    def row_copy(k):
        return pltpu.make_async_copy(h_ref.at[k % tt], xs_ref.at[idx_smem[k]], sem)

    def issue(k, c):
        row_copy(k).start()
        return c

    def drain(k, c):
        row_copy(k).wait()
        return c

    lax.fori_loop(0, n, issue, 0, unroll=8)
    lax.fori_loop(0, n, drain, 0, unroll=8)


def _dispatch(dest_flat, h2p, p_rows):
    t = h2p.shape[0]
    tt = DISPATCH_TILE
    xs0 = jnp.zeros((p_rows,) + h2p.shape[1:], h2p.dtype)
    anyspec = pl.BlockSpec(memory_space=pl.ANY)
    return pl.pallas_call(
        functools.partial(_dispatch_kernel, tt),
        grid=(t // tt,),
        in_specs=[anyspec, pl.BlockSpec((tt,) + h2p.shape[1:], lambda i: (i, 0, 0)), anyspec],
        out_specs=anyspec,
        out_shape=jax.ShapeDtypeStruct(xs0.shape, xs0.dtype),
        scratch_shapes=[pltpu.SMEM((tt * TOP_K,), I32), pltpu.SemaphoreType.DMA(()),
                        pltpu.SemaphoreType.DMA(())],
        input_output_aliases={2: 0},
        compiler_params=_cparams(("arbitrary",)),
        name="dispatch",
    )(dest_flat, h2p, xs0)


def _experts_kernel(be_ref, nu_ref, x_ref, wg_ref, wu_ref, wd_ref, y_ref, wg_s, wu_s, wd_s):
    b = pl.program_id(0)
    live = b < nu_ref[0]
    new_expert = jnp.logical_or(b == 0, be_ref[b] != be_ref[jnp.maximum(b - 1, 0)])

    @pl.when(jnp.logical_and(live, new_expert))
    def _():
        wg_s[...] = wg_ref[0].astype(BF16)
        wu_s[...] = wu_ref[0].astype(BF16)
        wd_s[...] = wd_ref[0].astype(BF16)

    @pl.when(live)
    def _():
        x = _unpack_rows(x_ref, wg_ref.shape[1] // PACK_W)
        gate = jnp.dot(x, wg_s[...], preferred_element_type=F32)
        up = jnp.dot(x, wu_s[...], preferred_element_type=F32)
        y_ref[...] = _dot(_silu(gate) * up, wd_s[...])

    @pl.when(b >= nu_ref[0])
    def _():
        y_ref[...] = jnp.zeros_like(y_ref)


def _experts(block_e, n_used, xs, wg, wu, wd):
    d, de = wg.shape[1], wg.shape[2]
    ns = d // PACK_W
    p_rows = xs.shape[0] // ns
    nblk = p_rows // EXPERT_BLOCK

    def live(b, nu):
        return jnp.minimum(b, jnp.maximum(nu[0] - 1, 0))

    grid_spec = pltpu.PrefetchScalarGridSpec(
        num_scalar_prefetch=2,
        grid=(nblk,),
        in_specs=[pl.BlockSpec((EXPERT_BLOCK * ns, LANES), lambda b, be, nu: (live(b, nu), 0)),
                  pl.BlockSpec((1, d, de), lambda b, be, nu: (be[live(b, nu)], 0, 0)),
                  pl.BlockSpec((1, d, de), lambda b, be, nu: (be[live(b, nu)], 0, 0)),
                  pl.BlockSpec((1, de, d), lambda b, be, nu: (be[live(b, nu)], 0, 0))],
        out_specs=pl.BlockSpec((EXPERT_BLOCK, d), lambda b, be, nu: (b, 0)),
        scratch_shapes=[pltpu.VMEM((d, de), BF16), pltpu.VMEM((d, de), BF16), pltpu.VMEM((de, d), BF16)],
    )
    return pl.pallas_call(
        _experts_kernel,
        grid_spec=grid_spec,
        out_shape=jax.ShapeDtypeStruct((p_rows, d), F32),
        compiler_params=_cparams(("arbitrary",)),
        name="experts",
    )(block_e, n_used, xs, wg, wu, wd)


def _combine_kernel(tt, dest_ref, y_ref, x1_ref, h2_ref, w_ref, mod_ref, npf_ref, sg_ref, su_ref, sd_ref,
                    o_ref, ybuf, idx_smem, isem, sem):
    i = pl.program_id(0)
    n = tt * TOP_K
    icp = pltpu.make_async_copy(dest_ref.at[pl.ds(i * tt * SUBLANES, n)], idx_smem, isem)
    icp.start()
    icp.wait()

    def row_copy(k):
        return pltpu.make_async_copy(y_ref.at[idx_smem[k]], ybuf.at[k], sem)

    def issue(k, c):
        row_copy(k).start()
        return c

    def drain(k, c):
        row_copy(k).wait()
        return c

    lax.fori_loop(0, n, issue, 0, unroll=8)
    h2 = _unpack_rows(h2_ref, sg_ref.shape[0] // PACK_W)
    gate = jnp.dot(h2, sg_ref[...], preferred_element_type=F32)
    up = jnp.dot(h2, su_ref[...], preferred_element_type=F32)
    shared = _dot(_silu(gate) * up, sd_ref[...])
    lax.fori_loop(0, n, drain, 0, unroll=8)
    w = w_ref[...]
    routed = w[:, 0:1] * ybuf[0:tt, :]
    for j in range(1, TOP_K):
        routed = routed + w[:, j:j + 1] * ybuf[j * tt:(j + 1) * tt, :]
    m6 = mod_ref[0]
    o_ref[...] = x1_ref[...] + m6[5:6] * _rms_rows(routed + shared, npf_ref[...])


def _combine(dest_flat, y, x1, h2, wsel_t, mod, seq, npf, sg, su, sd):
    t, d = x1.shape
    tt = DISPATCH_TILE
    anyspec = pl.BlockSpec(memory_space=pl.ANY)

    def full(a):
        return pl.BlockSpec(a.shape, lambda i: (0,) * a.ndim)

    return pl.pallas_call(
        functools.partial(_combine_kernel, tt),
        grid=(t // tt,),
        in_specs=[anyspec, anyspec,
                  pl.BlockSpec((tt, d), lambda i: (i, 0)),
                  pl.BlockSpec((tt * (d // PACK_W), LANES), lambda i: (i, 0)),
                  pl.BlockSpec((tt, SUBLANES), lambda i: (i, 0)),
                  pl.BlockSpec((1, 6, d), lambda i: ((i * tt) // seq, 0, 0)),
                  full(npf), full(sg), full(su), full(sd)],
        out_specs=pl.BlockSpec((tt, d), lambda i: (i, 0)),
        out_shape=jax.ShapeDtypeStruct((t, d), F32),
        scratch_shapes=[pltpu.VMEM((tt * TOP_K, d), F32), pltpu.SMEM((tt * TOP_K,), I32),
                        pltpu.SemaphoreType.DMA(()), pltpu.SemaphoreType.DMA(())],
        compiler_params=_cparams(("arbitrary",)),
        name="combine",
    )(dest_flat, y, x1, h2, wsel_t, mod, npf, sg, su, sd)


def _moe_plan(eidx, rank, cnt, t):
    counts = cnt[:, 0]
    padded = (counts + EXPERT_BLOCK - 1) // EXPERT_BLOCK * EXPERT_BLOCK
    pad_end = jnp.cumsum(padded)
    pad_start = pad_end - padded
    n_blocks = (t * TOP_K + EXPERT_BLOCK - 1) // EXPERT_BLOCK + N_EXPERTS
    first_row = jnp.arange(n_blocks, dtype=I32) * EXPERT_BLOCK
    block_e = jnp.minimum(jnp.sum((pad_end[None, :] <= first_row[:, None]).astype(I32), axis=1),
                          N_EXPERTS - 1).astype(I32)
    n_used = (pad_end[-1:] // EXPERT_BLOCK).astype(I32)
    dest_flat = _plan(eidx, rank, pad_start).reshape(-1)
    return block_e, n_used, dest_flat, n_blocks * EXPERT_BLOCK


def _trunk(x, mod, wc, oc, ec, norm_pre_mix):
    nb, seq, d = x.shape
    t = nb * seq
    x2 = x.reshape(t, d)
    mx = _mixer(x2, nb, seq, mod, norm_pre_mix, wc)
    x1, h2, logits_t = _outproj(x2, seq, mx, mx['z'], mod, wc, oc)
    eidx, wsel, rank, cnt = _route(logits_t, ec['e_bias'])
    block_e, n_used, dest_flat, p_rows = _moe_plan(eidx, rank, cnt, t)
    ns = d // PACK_W
    xs = _dispatch(dest_flat, h2.reshape(t, ns, LANES), p_rows)
    y = _experts(block_e, n_used, xs.reshape(p_rows * ns, LANES), ec['wg'], ec['wu'], ec['wd'])
    out = _combine(dest_flat, y, x1, h2, wsel.T, mod, seq, oc['npo'], ec['sg'], ec['su'], ec['sd'])
    return out.reshape(nb, seq, d)


def kernel(x_prompt, x_sample, c_prompt, c_sample, w_ada, b_ada, norm_pre_mix, norm_post_mix, norm_pre_ffn, norm_post_ffn, w_in, rw_mu, rw_w0, rw_w_up, rw_a0, rw_a_up, rw_g_up, rw_k_k, rw_k_a, rw_r_k, rw_ln_w, rw_ln_b, hg_lb_gamma, hg_norm_w, w_out, w_router, e_bias, w_exp_gate, w_exp_up, w_exp_down, w_sh_gate, w_sh_up, w_sh_down):
    d = x_prompt.shape[-1]
    wc = _layer_consts(w_in, rw_mu, rw_w0, rw_w_up, rw_a0, rw_a_up, rw_g_up, rw_k_k, rw_k_a, rw_r_k,
                       hg_lb_gamma)
    rw, hw = wc['rw'], wc['hw']
    e128, e128t = _indicator(hw, hw // HG_HEADS)
    wr_hi, wr_lo = _split2(w_router[0].T)
    oc = dict(ln_w=rw_ln_w[0].reshape(1, rw), ln_b=rw_ln_b[0].reshape(1, rw),
              hg_norm_w=hg_norm_w[0].reshape(1, hw), npm=norm_post_mix[0].reshape(1, d),
              npf=norm_pre_ffn[0].reshape(1, d), npo=norm_post_ffn[0].reshape(1, d),
              w_out=w_out[0].astype(BF16), wr_hi=wr_hi, wr_lo=wr_lo, e128=e128, e128t=e128t)
    ec = dict(e_bias=e_bias[0], wg=w_exp_gate[0], wu=w_exp_up[0], wd=w_exp_down[0], sg=w_sh_gate[0].astype(BF16), su=w_sh_up[0].astype(BF16),
              sd=w_sh_down[0].astype(BF16))
    nbp = c_prompt.shape[0]
    mod = _ada(jnp.concatenate([c_prompt, c_sample], axis=0), w_ada[0], b_ada[0]).reshape(-1, 6, d)
    y_prompt = _trunk(x_prompt, mod[:nbp], wc, oc, ec, norm_pre_mix[0])
    y_sample = _trunk(x_sample, mod[nbp:], wc, oc, ec, norm_pre_mix[0])
    return (y_prompt, y_sample)
```

```python
import functools
import math

import jax
import jax.numpy as jnp
from jax import lax
from jax.experimental import pallas as pl
from jax.experimental.pallas import tpu as pltpu

F32 = jnp.float32
BF16 = jnp.bfloat16
I32 = jnp.int32

RW_HEAD_DIM = 64
W_LORA = 64
A_LORA = 64
G_LORA = 128
RW_GN_EPS = 64e-5
HG_HEADS = 8
N_EXPERTS = 64
TOP_K = 6
N_GROUPS = 8
TOPK_GROUPS = 4
ROUTED_SCALE = 2.5
EXPERT_BLOCK = 256
NORM_EPS = 1e-6

LANES = 128
SUBLANES = 8
VMEM_LIMIT = 56 * 1024 * 1024

CHUNK = 64


def _cparams(sem, vmem=VMEM_LIMIT):
    return pltpu.CompilerParams(dimension_semantics=sem, vmem_limit_bytes=vmem)


def _sigmoid(x):
    return 1.0 / (1.0 + jnp.exp(-x))


def _silu(x):
    return x * _sigmoid(x)


def _dot(a, b):
    return jnp.dot(a.astype(BF16), b.astype(BF16), preferred_element_type=F32)


def _dot_nt(a, b):
    return lax.dot_general(a.astype(BF16), b.astype(BF16), (((1,), (1,)), ((), ())),
                           preferred_element_type=F32)


def _dot_tn(a, b):
    return lax.dot_general(a.astype(BF16), b.astype(BF16), (((0,), (0,)), ((), ())),
                           preferred_element_type=F32)


def _split2(x):
    hi = x.astype(BF16)
    lo = (x - hi.astype(F32)).astype(BF16)
    return hi, lo


def _split3(x):
    hi = x.astype(BF16)
    r1 = x - hi.astype(F32)
    mid = r1.astype(BF16)
    lo = (r1 - mid.astype(F32)).astype(BF16)
    return hi, mid, lo


def _seg_sum(x, e, et):
    hi, lo = _split2(x)
    s = jnp.dot(hi, e, preferred_element_type=F32) + jnp.dot(lo, e, preferred_element_type=F32)
    shi, slo = _split2(s)
    return jnp.dot(shi, et, preferred_element_type=F32) + jnp.dot(slo, et, preferred_element_type=F32)


PACK_W = 2 * LANES


def _pack_rows(x_bf, out_ref):
    n, d = x_bf.shape
    ns = d // PACK_W
    for s in range(ns):
        a = lax.bitcast_convert_type(x_bf[:, s * PACK_W:s * PACK_W + LANES].astype(F32), jnp.uint32)
        b = lax.bitcast_convert_type(x_bf[:, s * PACK_W + LANES:(s + 1) * PACK_W].astype(F32), jnp.uint32)
        out_ref[pl.ds(s, n, stride=ns), :] = a | (b >> 16)


def _unpack_rows(x_ref, ns):
    n = x_ref.shape[0] // ns
    parts = []
    for s in range(ns):
        w = x_ref[pl.ds(s, n, stride=ns), :]
        parts.append(lax.bitcast_convert_type(w & jnp.uint32(0xFFFF0000), F32).astype(BF16))
        parts.append(lax.bitcast_convert_type(w << 16, F32).astype(BF16))
    return jnp.concatenate(parts, axis=1)


def _rms_rows(x, g):
    return x * lax.rsqrt(jnp.mean(x * x, axis=-1, keepdims=True) + NORM_EPS) * g


def _ada_kernel(c_ref, w_ref, b_ref, o_ref):
    c = c_ref[...]
    o_ref[...] = _dot(_silu(c), w_ref[...]) + b_ref[...]


def _ada(c, w_ada, b_ada):
    nb, d = c.shape
    n = w_ada.shape[1]
    tn = 512
    return pl.pallas_call(
        _ada_kernel,
        grid=(n // tn,),
        in_specs=[pl.BlockSpec((nb, d), lambda j: (0, 0)),
                  pl.BlockSpec((d, tn), lambda j: (0, j)),
                  pl.BlockSpec((1, tn), lambda j: (0, j))],
        out_specs=pl.BlockSpec((nb, tn), lambda j: (0, j)),
        out_shape=jax.ShapeDtypeStruct((nb, n), F32),
        compiler_params=_cparams(("parallel",)),
        name="ada",
    )(c, w_ada, b_ada.reshape(1, n))


def _inproj_kernel(x_ref, mod_ref, g_ref, w_ref, o_ref, h_scr):
    @pl.when(pl.program_id(1) == 0)
    def _():
        m = mod_ref[0]
        h = _rms_rows(x_ref[...], g_ref[...]) * (1.0 + m[1:2]) + m[0:1]
        h_scr[...] = h.astype(BF16)

    o_ref[...] = jnp.dot(h_scr[...], w_ref[...], preferred_element_type=F32)


def _inproj(x2, mod, g, w_bf, seq):
    t, d = x2.shape
    n = w_bf.shape[1]
    tm, tn = 1024, 512
    return pl.pallas_call(
        _inproj_kernel,
        grid=(t // tm, n // tn),
        in_specs=[pl.BlockSpec((tm, d), lambda i, j: (i, 0)),
                  pl.BlockSpec((1, 6, d), lambda i, j: ((i * tm) // seq, 0, 0)),
                  pl.BlockSpec((1, d), lambda i, j: (0, 0)),
                  pl.BlockSpec((d, tn), lambda i, j: (0, j))],
        out_specs=pl.BlockSpec((tm, tn), lambda i, j: (i, j)),
        out_shape=jax.ShapeDtypeStruct((t, n), F32),
        scratch_shapes=[pltpu.VMEM((tm, d), BF16)],
        compiler_params=_cparams(("parallel", "arbitrary")),
        name="inproj",
    )(x2, mod, g.reshape(1, d), w_bf)


def _rwprep_kernel(seq, tm, rw,
                   z_ref, zp_ref, zn_ref, l_ref, lp_ref, ln_ref,
                   mu_ref, mul_ref, wup_ref, aup_ref, gup_ref, w0_ref, a0_ref,
                   kk_ref, ka_ref, rk_ref, e_ref, et_ref,
                   r_o, v_o, kk_o, g_o, bon_o, lw0_o, lw1_o, b0_o, b1_o, kd0_o, kd1_o):
    i = pl.program_id(0)
    first = (i * tm) % seq == 0
    last = ((i + 1) * tm) % seq == 0

    def shifted(cur, prev_blk, next_blk, mu):
        rows = lax.broadcasted_iota(I32, cur.shape, 0)
        prow = jnp.where(first, 0.0, prev_blk[SUBLANES - 1:SUBLANES, :])
        nrow = jnp.where(last, 0.0, next_blk[0:1, :])
        prev = jnp.where(rows == 0, prow, pltpu.roll(cur, 1, axis=0))
        nxt = jnp.where(rows == tm - 1, nrow, pltpu.roll(cur, tm - 1, axis=0))
        return cur + mu * (0.5 * (prev + nxt) - cur)

    lat = shifted(l_ref[...], lp_ref[...], ln_ref[...], mul_ref[...])
    w_lat = lat[:, 0:2 * W_LORA]
    a_lat = lat[:, 2 * W_LORA:2 * W_LORA + 2 * A_LORA]
    g_lat = lat[:, 2 * W_LORA + 2 * A_LORA:2 * W_LORA + 2 * A_LORA + G_LORA]
    w_raw = _dot(jnp.tanh(w_lat), wup_ref[...]) + w0_ref[...]
    a_all = _sigmoid(_dot(a_lat, aup_ref[...]) + a0_ref[...])
    g_o[...] = _dot(_sigmoid(g_lat), gup_ref[...])
    lw = (-math.exp(-0.5)) * _sigmoid(w_raw)
    lw0_o[...] = lw[:, :rw]
    lw1_o[...] = lw[:, rw:]

    r = shifted(z_ref[:, 0:rw], zp_ref[:, 0:rw], zn_ref[:, 0:rw], mu_ref[:, 0:rw])
    k = shifted(z_ref[:, rw:2 * rw], zp_ref[:, rw:2 * rw], zn_ref[:, rw:2 * rw], mu_ref[:, rw:2 * rw])
    v = shifted(z_ref[:, 2 * rw:3 * rw], zp_ref[:, 2 * rw:3 * rw], zn_ref[:, 2 * rw:3 * rw],
                mu_ref[:, 2 * rw:3 * rw])
    r_o[...] = r
    v_o[...] = v
    kk = k * kk_ref[...]
    ss = _seg_sum(kk * kk, e_ref[...], et_ref[...])
    kk = kk / jnp.maximum(jnp.sqrt(ss), 1e-12)
    kk_o[...] = kk
    a0 = a_all[:, :rw]
    a1 = a_all[:, rw:]
    b0_o[...] = kk * a0
    b1_o[...] = kk * a1
    kd0 = k * (1.0 + (a0 - 1.0) * ka_ref[...])
    kd1 = k * (1.0 + (a1 - 1.0) * ka_ref[...])
    kd0_o[...] = kd0
    kd1_o[...] = kd1
    kb = 0.5 * (kd0 + kd1)
    bon_o[...] = _seg_sum(r * kb * rk_ref[...], e_ref[...], et_ref[...]) * v


def _rwprep(z, seq, rw, lat_off, mu_rkv, mu_lat, wup, aup, gup, w0, a0, k_k, k_a, r_k, e64, e64t):
    t = z.shape[0]
    tm = 256
    nlat = 512
    nrkv = 3 * rw
    tb = tm // SUBLANES
    nblk8 = t // SUBLANES
    lat_blk = lat_off // nlat

    def cur(i):
        return (i, 0)

    def prv(i):
        return (jnp.maximum(i * tb - 1, 0), 0)

    def nxt(i):
        return (jnp.minimum((i + 1) * tb, nblk8 - 1), 0)

    def full(shape):
        return pl.BlockSpec(shape, lambda i: (0,) * len(shape))

    out = jax.ShapeDtypeStruct((t, rw), F32)
    ospec = pl.BlockSpec((tm, rw), lambda i: (i, 0))
    return pl.pallas_call(
        functools.partial(_rwprep_kernel, seq, tm, rw),
        grid=(t // tm,),
        in_specs=[pl.BlockSpec((tm, nrkv), cur),
                  pl.BlockSpec((SUBLANES, nrkv), prv),
                  pl.BlockSpec((SUBLANES, nrkv), nxt),
                  pl.BlockSpec((tm, nlat), lambda i: (i, lat_blk)),
                  pl.BlockSpec((SUBLANES, nlat), lambda i: (jnp.maximum(i * tb - 1, 0), lat_blk)),
                  pl.BlockSpec((SUBLANES, nlat), lambda i: (jnp.minimum((i + 1) * tb, nblk8 - 1), lat_blk)),
                  full((1, nrkv)), full((1, nlat)),
                  full(wup.shape), full(aup.shape), full(gup.shape),
                  full((1, 2 * rw)), full((1, 2 * rw)),
                  full((1, rw)), full((1, rw)), full((1, rw)),
                  full(e64.shape), full(e64t.shape)],
        out_specs=[ospec] * 11,
        out_shape=[out] * 11,
        compiler_params=_cparams(("parallel",)),
        name="rwprep",
    )(z, z, z, z, z, z, mu_rkv, mu_lat, wup, aup, gup, w0, a0, k_k, k_a, r_k, e64, e64t)


def _tri(n, rev):
    i = lax.broadcasted_iota(I32, (n, n), 0)
    j = lax.broadcasted_iota(I32, (n, n), 1)
    m = (j >= i) if rev else (j <= i)
    return jnp.where(m, 1.0, 0.0).astype(BF16)


def _rw_streams(streams):
    c = streams[0][0].shape[0]
    hd = RW_HEAD_DIM
    n2 = 2 * c
    ns = len(streams)
    revs = [s[7] for s in streams]
    lane = lax.broadcasted_iota(I32, (c, LANES), 1)
    head_a = lane < hd
    ri = lax.broadcasted_iota(I32, (n2, n2), 0)
    ci = lax.broadcasted_iota(I32, (n2, n2), 1)
    ti = ri % c
    si = ci % c
    same16 = (ri // 16) == (ci // 16)
    same32 = (ri // 32) == (ci // 32)
    mid32 = jnp.logical_and(same32, jnp.logical_not(same16))
    eye = jnp.where(ri == ci, 1.0, 0.0)
    strict = {False: si < ti, True: si > ti}
    incl = {False: si <= ti, True: si >= ti}
    tri = {rev: _tri(c, rev) for rev in set(revs)}
    ei = lax.broadcasted_iota(I32, (LANES, LANES), 0)
    ej = lax.broadcasted_iota(I32, (LANES, LANES), 1)
    eye_k = ei == ej

    def pair(x):
        return jnp.concatenate([jnp.where(head_a, x, 0.0), jnp.where(head_a, 0.0, x)], axis=0)

    cum = []
    for (r, v, kk, lw, b, kd, s_in, rev) in streams:
        hi, lo = _split2(lw)
        cs = jnp.dot(tri[rev], jnp.concatenate([hi, lo], axis=1), preferred_element_type=F32)
        cum.append(cs[:, :LANES] + cs[:, LANES:])
    ops = []
    for (r, v, kk, lw, b, kd, s_in, rev), cm in zip(streams, cum):
        tot = cm[0:1, :] if rev else cm[c - 1:c, :]
        g_inv = jnp.exp(-cm)
        g_tail = jnp.exp(tot - cm)
        ops.append(dict(p2=pair(-kk * jnp.exp(cm - lw)), r2=pair(r * jnp.exp(cm)),
                        bi2=pair(b * g_inv), ki2=pair(kd * g_inv), bt2=pair(b * g_tail),
                        kt2=pair(kd * g_tail), v2=pair(v), g_tot=jnp.exp(tot)))
    gm = [_dot_nt(jnp.concatenate([o['p2'], o['r2']], axis=0), jnp.concatenate([o['bi2'], o['ki2']], axis=0))
          for o in ops]
    a2 = [jnp.where(strict[rev], g[:n2, :n2], 0.0) for g, rev in zip(gm, revs)]
    b2 = [jnp.where(strict[rev], g[:n2, n2:], 0.0) for g, rev in zip(gm, revs)]
    ap2 = [jnp.where(incl[rev], g[n2:, :n2], 0.0) for g, rev in zip(gm, revs)]
    bp2 = [jnp.where(incl[rev], g[n2:, n2:], 0.0) for g, rev in zip(gm, revs)]
    bv = [_dot(x, o['v2']) for x, o in zip(b2, ops)]
    bpv = [_dot(x, o['v2']) for x, o in zip(bp2, ops)]

    x = [jnp.where(same16, a, 0.0) for a in a2]
    tinv = [eye + xi for xi in x]
    for _ in range(3):
        x = [_dot(xi, xi) for xi in x]
        tinv = [t + _dot(t, xi) for t, xi in zip(tinv, x)]
    for lvl in (mid32, jnp.logical_not(same32)):
        y = [_dot(t, jnp.where(lvl, a, 0.0)) for t, a in zip(tinv, a2)]
        tinv = [t + _dot(yi, t) for t, yi in zip(tinv, y)]

    wu = [_dot(t, jnp.concatenate([o['p2'], bvi], axis=1)) for t, o, bvi in zip(tinv, ops, bv)]
    qo = [_dot(a, w) for a, w in zip(ap2, wu)]
    m2 = [_dot_tn(w[:, :LANES], o['bt2']) + jnp.where(eye_k, jnp.broadcast_to(o['g_tot'], (LANES, LANES)), 0.0)
          for w, o in zip(wu, ops)]
    nn2 = [_dot_tn(w[:, LANES:], o['bt2']) + _dot_tn(o['v2'], o['kt2']) for w, o in zip(wu, ops)]

    nt = (((1,), (1,)), ((), ()))
    outs = []
    for i in range(ns):
        s_hi, s_lo = _split2(streams[i][6])
        q2b = (ops[i]['r2'] + qo[i][:, :LANES]).astype(BF16)
        oo = (lax.dot_general(q2b, s_hi, nt, preferred_element_type=F32)
              + lax.dot_general(q2b, s_lo, nt, preferred_element_type=F32)
              + qo[i][:, LANES:] + bpv[i])
        m2b = m2[i].astype(BF16)
        s_out = (jnp.dot(s_hi, m2b, preferred_element_type=F32)
                 + jnp.dot(s_lo, m2b, preferred_element_type=F32) + nn2[i])
        outs.append((oo[:c] + oo[c:], s_out))
    return outs


RW_PAIRS_PER_STEP = 8


def _rwscan_kernel(rf, vf, kkf, lwf, bf, kdf, rb, vb, kkb, lwb, bb, kdb, of_ref, ob_ref, sf, sb):
    @pl.when(pl.program_id(2) == 0)
    def _():
        sf[...] = jnp.zeros_like(sf)
        sb[...] = jnp.zeros_like(sb)

    streams = []
    for p in range(RW_PAIRS_PER_STEP):
        sl = slice(p * LANES, (p + 1) * LANES)
        streams.append((rf[:, sl], vf[:, sl], kkf[:, sl], lwf[:, sl], bf[:, sl], kdf[:, sl], sf[p], False))
        streams.append((rb[:, sl], vb[:, sl], kkb[:, sl], lwb[:, sl], bb[:, sl], kdb[:, sl], sb[p], True))
    outs = _rw_streams(streams)
    for p in range(RW_PAIRS_PER_STEP):
        sl = slice(p * LANES, (p + 1) * LANES)
        of_ref[:, sl], sf[p] = outs[2 * p]
        ob_ref[:, sl], sb[p] = outs[2 * p + 1]


def _rwscan(nb, seq, r, v, kk, lw0, lw1, b0, b1, kd0, kd1):
    t, rw = r.shape
    nc = seq // CHUNK
    wblk = RW_PAIRS_PER_STEP * LANES
    fw = pl.BlockSpec((CHUNK, wblk), lambda bi, hp, c: (bi * nc + c, hp))
    bw = pl.BlockSpec((CHUNK, wblk), lambda bi, hp, c: (bi * nc + nc - 1 - c, hp))
    out = jax.ShapeDtypeStruct((t, rw), F32)
    state = pltpu.VMEM((RW_PAIRS_PER_STEP, LANES, LANES), F32)
    return pl.pallas_call(
        _rwscan_kernel,
        grid=(nb, rw // wblk, nc),
        in_specs=[fw] * 6 + [bw] * 6,
        out_specs=[fw, bw],
        out_shape=[out, out],
        scratch_shapes=[state, state],
        compiler_params=_cparams(("parallel", "parallel", "arbitrary")),
        name="rwscan",
    )(r, v, kk, lw0, b0, kd0, r, v, kk, lw1, b1, kd1)


def _hg_streams(streams):
    c, dk = streams[0][0].shape
    revs = [s[5] for s in streams]
    tri = {rev: _tri(c, rev) for rev in set(revs)}
    row = lax.broadcasted_iota(I32, (c, dk), 0)
    ri = lax.broadcasted_iota(I32, (c, c), 0)
    ci = lax.broadcasted_iota(I32, (c, c), 1)

    cum = []
    for (q, k, v, lf, st, rev) in streams:
        hi, mid, lo = _split3(lf)
        cs = jnp.dot(tri[rev], jnp.concatenate([hi, mid, lo], axis=1), preferred_element_type=F32)
        cum.append(cs[:, :dk] + cs[:, dk:2 * dk] + cs[:, 2 * dk:])
    scores = [jnp.where(ri == ci, jnp.sum(s[0] * s[1], axis=1, keepdims=True), 0.0) for s in streams]
    sub = row % SUBLANES
    nt = (((1,), (1,)), ((), ()))

    def sub_bcast(x, idx):
        x3 = x.reshape(c // SUBLANES, SUBLANES, dk)
        return jnp.broadcast_to(x3[:, idx:idx + 1, :], x3.shape).reshape(c, dk)

    h = c // 2
    while h >= 1:
        blk = 2 * h
        upper = (row % blk) >= h
        same_blk = (ri // blk) == (ci // blk)
        r_up = (ri % blk) >= h
        c_up = (ci % blk) >= h
        q_rows = {False: upper, True: jnp.logical_not(upper)}
        pmask = {False: jnp.logical_and(same_blk, jnp.logical_and(r_up, jnp.logical_not(c_up))),
                 True: jnp.logical_and(same_blk, jnp.logical_and(jnp.logical_not(r_up), c_up))}
        sl = []
        for i, (q, k, v, lf, st, rev) in enumerate(streams):
            cm = cum[i]
            off = h if rev else h - 1
            if h >= SUBLANES:
                pieces = [jnp.broadcast_to(cm[m0 + off:m0 + off + 1, :], (blk, dk)) for m0 in range(0, c, blk)]
                ref = jnp.concatenate(pieces, axis=0) if len(pieces) > 1 else pieces[0]
            elif blk == SUBLANES:
                ref = sub_bcast(cm, off)
            elif 2 * blk == SUBLANES:
                ref = jnp.where(sub < blk, sub_bcast(cm, off), sub_bcast(cm, off + blk))
            else:
                ref = jnp.where(q_rows[rev], pltpu.roll(cm, c - 1 if rev else 1, axis=0), cm)
            dlt = cm - ref
            e = jnp.minimum(jnp.where(q_rows[rev], dlt, -dlt), 0.0)
            x = (jnp.where(q_rows[rev], q, k) * jnp.exp(e)).astype(BF16)
            sl.append(lax.dot_general(x, x, nt, preferred_element_type=F32))
        scores = [sc + jnp.where(pmask[rev], x, 0.0) for sc, x, rev in zip(scores, sl, revs)]
        h //= 2

    outs = []
    for (q, k, v, lf, st, rev), cm, sc in zip(streams, cum, scores):
        tot = cm[0:1, :] if rev else cm[c - 1:c, :]
        o = _dot(sc, v) + _dot_nt(q * jnp.exp(cm), st)
        st_new = st * jnp.exp(tot) + _dot_tn(v, k * jnp.exp(tot - cm))
        outs.append((o, st_new))
    return outs


HG_HEADS_PER_STEP = 4


def _hgscan_kernel(dk, qf, fff, i_f, qb, ffb, i_b, lb_ref, of_ref, ob_ref, sf, sb):
    @pl.when(pl.program_id(2) == 0)
    def _():
        sf[...] = jnp.zeros_like(sf)
        sb[...] = jnp.zeros_like(sb)

    streams = []
    for p in range(HG_HEADS_PER_STEP):
        sl = slice(p * dk, (p + 1) * dk)
        for (q_ref, ff_ref, i_ref, st_ref, d) in ((qf, fff, i_f, sf, 0), (qb, ffb, i_b, sb, 1)):
            lbv = lb_ref[d:d + 1, sl]
            f = lbv + (1.0 - lbv) * _sigmoid(ff_ref[:, sl])
            streams.append((_silu(q_ref[:, sl]), 1.0 - f, i_ref[:, sl], jnp.log(f), st_ref[p], d == 1))
    outs = _hg_streams(streams)
    for p in range(HG_HEADS_PER_STEP):
        sl = slice(p * dk, (p + 1) * dk)
        of_ref[:, sl], sf[p] = outs[2 * p]
        ob_ref[:, sl], sb[p] = outs[2 * p + 1]


def _hgscan(nb, seq, z, lb, hg_off, hw):
    t = z.shape[0]
    nc = seq // CHUNK
    dk = hw // HG_HEADS
    wblk = HG_HEADS_PER_STEP * dk
    base = hg_off // wblk
    nh = hw // wblk

    def fw(comp):
        return pl.BlockSpec((CHUNK, wblk), lambda bi, h, c: (bi * nc + c, base + comp * nh + h))

    def bw(comp):
        return pl.BlockSpec((CHUNK, wblk), lambda bi, h, c: (bi * nc + nc - 1 - c, base + comp * nh + h))

    out = jax.ShapeDtypeStruct((t, hw), F32)
    state = pltpu.VMEM((HG_HEADS_PER_STEP, dk, dk), F32)
    return pl.pallas_call(
        functools.partial(_hgscan_kernel, dk),
        grid=(nb, nh, nc),
        in_specs=[fw(0), fw(1), fw(3), bw(0), bw(2), bw(3),
                  pl.BlockSpec((2, wblk), lambda bi, h, c: (0, h))],
        out_specs=[pl.BlockSpec((CHUNK, wblk), lambda bi, h, c: (bi * nc + c, h)),
                   pl.BlockSpec((CHUNK, wblk), lambda bi, h, c: (bi * nc + nc - 1 - c, h))],
        out_shape=[out, out],
        scratch_shapes=[state, state],
        compiler_params=_cparams(("parallel", "parallel", "arbitrary")),
        name="hgscan",
    )(z, z, z, z, z, z, lb)


def _blockdiag2(w):
    _, r, n = w.shape
    z = jnp.zeros((r, n), w.dtype)
    return jnp.concatenate([jnp.concatenate([w[0], z], axis=1), jnp.concatenate([z, w[1]], axis=1)], axis=0)


def _indicator(width, seg):
    e = (jnp.arange(width)[:, None] // seg == jnp.arange(width // seg)[None, :]).astype(BF16)
    return e, e.T


def _layer_consts(w_in, rw_mu, rw_w0, rw_w_up, rw_a0, rw_a_up, rw_g_up, rw_k_k, rw_k_a, rw_r_k,
                  hg_lb_gamma):
    rw = rw_k_k.shape[-1]
    d = w_in.shape[1]
    nlat = 2 * W_LORA + 2 * A_LORA + G_LORA
    w = w_in[0]
    rkv = 3 * rw
    hg_cols = w.shape[1] - rkv - nlat
    pad = 512 - nlat
    w_perm = jnp.concatenate([w[:, :rkv], w[:, rkv + nlat:], w[:, rkv:rkv + nlat],
                              jnp.zeros((d, pad), w.dtype)], axis=1).astype(BF16)
    mu = rw_mu[0]
    lower = jnp.cumsum(jax.nn.softmax(hg_lb_gamma.astype(F32), axis=0), axis=0)[0]
    hw = lower.shape[-1]
    e64, e64t = _indicator(rw, RW_HEAD_DIM)
    return dict(
        rw=rw, hw=hw, hg_off=rkv, lat_off=rkv + hg_cols, w_in=w_perm,
        mu_rkv=mu[:rkv].reshape(1, rkv),
        mu_lat=jnp.pad(mu[rkv:rkv + nlat], (0, pad)).reshape(1, 512),
        wup=_blockdiag2(rw_w_up[0]).astype(BF16), aup=_blockdiag2(rw_a_up[0]).astype(BF16),
        gup=rw_g_up[0].astype(BF16),
        w0=rw_w0[0].reshape(1, 2 * rw), a0=rw_a0[0].reshape(1, 2 * rw),
        k_k=rw_k_k[0].reshape(1, rw), k_a=rw_k_a[0].reshape(1, rw), r_k=rw_r_k[0].reshape(1, rw),
        e64=e64, e64t=e64t,
        lb=lower,
    )


def _mixer(x2, nb, seq, mod, norm_pre_mix, wc):
    z = _inproj(x2, mod, norm_pre_mix, wc['w_in'], seq)
    (r, v, kk, g, bonus, lw0, lw1, b0, b1, kd0, kd1) = _rwprep(
        z, seq, wc['rw'], wc['lat_off'], wc['mu_rkv'], wc['mu_lat'], wc['wup'], wc['aup'], wc['gup'],
        wc['w0'], wc['a0'], wc['k_k'], wc['k_a'], wc['r_k'], wc['e64'], wc['e64t'])
    rw_of, rw_ob = _rwscan(nb, seq, r, v, kk, lw0, lw1, b0, b1, kd0, kd1)
    hg_of, hg_ob = _hgscan(nb, seq, z, wc['lb'], wc['hg_off'], wc['hw'])
    return dict(z=z, r=r, v=v, kk=kk, g=g, bonus=bonus, lw0=lw0, rw_of=rw_of, rw_ob=rw_ob,
                hg_of=hg_of, hg_ob=hg_ob)


def _outproj_kernel(rw, x_ref, rf_ref, rb_ref, bon_ref, g_ref, hf_ref, hb_ref, hgg_ref, mod_ref,
                    lnw_ref, lnb_ref, hnw_ref, npm_ref, npf_ref, wout_ref, wrh_ref, wrl_ref,
                    e64_ref, e64t_ref, e128_ref, e128t_ref,
                    x1_ref, h2_ref, lg_ref):
    m6 = mod_ref[0]
    o = rf_ref[...] + rb_ref[...]
    mean = _seg_sum(o, e64_ref[...], e64t_ref[...]) * (1.0 / RW_HEAD_DIM)
    dlt = o - mean
    var = _seg_sum(dlt * dlt, e64_ref[...], e64t_ref[...]) * (1.0 / RW_HEAD_DIM)
    o_rw = (dlt * lax.rsqrt(var + RW_GN_EPS) * lnw_ref[...] + lnb_ref[...] + bon_ref[...]) * g_ref[...]
    oh = hf_ref[...] + hb_ref[...]
    hd = oh.shape[1] // HG_HEADS
    ms = _seg_sum(oh * oh, e128_ref[...], e128t_ref[...]) * (1.0 / hd)
    o_hg = oh * lax.rsqrt(ms + NORM_EPS) * hnw_ref[...] * _silu(hgg_ref[...])
    m = _dot(o_rw, wout_ref[0:rw, :]) + _dot(o_hg, wout_ref[rw:, :])
    x1 = x_ref[...] + m6[2:3] * _rms_rows(m, npm_ref[...])
    x1_ref[...] = x1
    h2 = _rms_rows(x1, npf_ref[...]) * (1.0 + m6[4:5]) + m6[3:4]
    hi, lo = _split2(h2)
    _pack_rows(hi, h2_ref)
    nt = (((1,), (1,)), ((), ()))
    lg_ref[...] = (lax.dot_general(wrh_ref[...], hi, nt, preferred_element_type=F32)
                   + lax.dot_general(wrh_ref[...], lo, nt, preferred_element_type=F32)
                   + lax.dot_general(wrl_ref[...], hi, nt, preferred_element_type=F32))


def _outproj(x2, seq, mx, z, mod, wc, oc):
    t, d = x2.shape
    rw, hw = wc['rw'], wc['hw']
    tm = 256
    gblk = (wc['hg_off'] + 4 * hw) // hw

    def row(w):
        return pl.BlockSpec((tm, w), lambda i: (i, 0))

    def full(a):
        return pl.BlockSpec(a.shape, lambda i: (0,) * a.ndim)

    consts = [oc['ln_w'], oc['ln_b'], oc['hg_norm_w'], oc['npm'], oc['npf'], oc['w_out'], oc['wr_hi'],
              oc['wr_lo'], wc['e64'], wc['e64t'], oc['e128'], oc['e128t']]
    return pl.pallas_call(
        functools.partial(_outproj_kernel, rw),
        grid=(t // tm,),
        in_specs=[row(d), row(rw), row(rw), row(rw), row(rw), row(hw), row(hw),
                  pl.BlockSpec((tm, hw), lambda i: (i, gblk)),
                  pl.BlockSpec((1, 6, d), lambda i: ((i * tm) // seq, 0, 0))] + [full(a) for a in consts],
        out_specs=[row(d), pl.BlockSpec((tm * (d // PACK_W), LANES), lambda i: (i, 0)),
                   pl.BlockSpec((N_EXPERTS, tm), lambda i: (0, i))],
        out_shape=[jax.ShapeDtypeStruct((t, d), F32), jax.ShapeDtypeStruct((t * (d // PACK_W), LANES), jnp.uint32),
                   jax.ShapeDtypeStruct((N_EXPERTS, t), F32)],
        compiler_params=_cparams(("parallel",)),
        name="outproj",
    )(x2, mx['rw_of'], mx['rw_ob'], mx['bonus'], mx['g'], mx['hg_of'], mx['hg_ob'], z, mod, *consts)


ROUTE_TILE = 512


def _route_kernel(lg_ref, bias_ref, ut_ref, eidx_ref, wsel_ref, rank_ref, cnt_ref, carry):
    @pl.when(pl.program_id(0) == 0)
    def _():
        carry[...] = jnp.zeros_like(carry)

    ne, tt = lg_ref.shape
    gsz = ne // N_GROUPS
    neg = -jnp.inf
    s = _sigmoid(lg_ref[...])
    biased = s + bias_ref[...]
    io_g = lax.broadcasted_iota(I32, (gsz, tt), 0)
    gs_rows = []
    for gi in range(N_GROUPS):
        blk = biased[gi * gsz:(gi + 1) * gsz, :]
        m1 = jnp.max(blk, axis=0, keepdims=True)
        first = jnp.min(jnp.where(blk == m1, io_g, gsz), axis=0, keepdims=True)
        m2 = jnp.max(jnp.where(io_g == first, neg, blk), axis=0, keepdims=True)
        gs_rows.append(m1 + m2)
    gs = jnp.concatenate(gs_rows, axis=0)
    io_n = lax.broadcasted_iota(I32, (N_GROUPS, tt), 0)
    selg = jnp.zeros((N_GROUPS, tt), jnp.bool_)
    for _ in range(TOPK_GROUPS):
        m = jnp.max(gs, axis=0, keepdims=True)
        first = jnp.min(jnp.where(gs == m, io_n, N_GROUPS), axis=0, keepdims=True)
        pick = io_n == first
        selg = jnp.logical_or(selg, pick)
        gs = jnp.where(pick, neg, gs)
    emask = jnp.concatenate([jnp.broadcast_to(selg[gi:gi + 1, :], (gsz, tt)) for gi in range(N_GROUPS)],
                            axis=0)
    mb = jnp.where(emask, biased, neg)
    io_e = lax.broadcasted_iota(I32, (ne, tt), 0)
    sel = jnp.zeros((ne, tt), jnp.bool_)
    picks, idxs, ws = [], [], []
    for _ in range(TOP_K):
        m = jnp.max(mb, axis=0, keepdims=True)
        first = jnp.min(jnp.where(mb == m, io_e, ne), axis=0, keepdims=True)
        pick = io_e == first
        picks.append(pick)
        idxs.append(first)
        ws.append(jnp.sum(jnp.where(pick, s, 0.0), axis=0, keepdims=True))
        sel = jnp.logical_or(sel, pick)
        mb = jnp.where(pick, neg, mb)
    wsum = ws[0]
    for w in ws[1:]:
        wsum = wsum + w
    pos = jnp.dot(jnp.where(sel, 1.0, 0.0).astype(BF16), ut_ref[...], preferred_element_type=F32) + carry[...]
    ranks = [jnp.sum(jnp.where(p, pos, 0.0), axis=0, keepdims=True).astype(I32) for p in picks]
    carry[...] = carry[...] + jnp.sum(jnp.where(sel, 1.0, 0.0), axis=1, keepdims=True)
    zi = jnp.zeros((SUBLANES - TOP_K, tt), I32)
    eidx_ref[...] = jnp.concatenate(idxs + [zi], axis=0)
    rank_ref[...] = jnp.concatenate(ranks + [zi], axis=0)
    wsel_ref[...] = jnp.concatenate([w / wsum * ROUTED_SCALE for w in ws] + [zi.astype(F32)], axis=0)
    cnt_ref[...] = jnp.broadcast_to(carry[...], cnt_ref.shape).astype(I32)


def _route(logits_t, e_bias):
    ne, t = logits_t.shape
    tt = ROUTE_TILE
    ut = (jnp.arange(tt)[:, None] < jnp.arange(tt)[None, :]).astype(BF16)
    tok = pl.BlockSpec((SUBLANES, tt), lambda i: (0, i))
    return pl.pallas_call(
        _route_kernel,
        grid=(t // tt,),
        in_specs=[pl.BlockSpec((ne, tt), lambda i: (0, i)),
                  pl.BlockSpec((ne, 1), lambda i: (0, 0)),
                  pl.BlockSpec((tt, tt), lambda i: (0, 0))],
        out_specs=[tok, tok, tok, pl.BlockSpec((ne, LANES), lambda i: (0, 0))],
        out_shape=[jax.ShapeDtypeStruct((SUBLANES, t), I32), jax.ShapeDtypeStruct((SUBLANES, t), F32),
                   jax.ShapeDtypeStruct((SUBLANES, t), I32), jax.ShapeDtypeStruct((ne, LANES), I32)],
        scratch_shapes=[pltpu.VMEM((ne, 1), F32)],
        compiler_params=_cparams(("arbitrary",)),
        name="route",
    )(logits_t, e_bias.reshape(ne, 1), ut)


DISPATCH_TILE = 128
PLAN_TILE = 1024


def _plan_kernel(eidx_ref, rank_ref, ps_ref, dest_ref):
    ne = ps_ref.shape[0]
    tp = eidx_ref.shape[1]
    io_e = lax.broadcasted_iota(I32, (ne, tp), 0)
    ps = ps_ref[...]
    rows = []
    for j in range(TOP_K):
        start = jnp.sum(jnp.where(io_e == eidx_ref[j:j + 1, :], ps, 0.0), axis=0, keepdims=True)
        rows.append(start.astype(I32) + rank_ref[j:j + 1, :])
    dest = jnp.concatenate(rows + [jnp.zeros((SUBLANES - TOP_K, tp), I32)], axis=0)
    for i in range(tp // DISPATCH_TILE):
        dest_ref[i] = dest[:, i * DISPATCH_TILE:(i + 1) * DISPATCH_TILE]


def _plan(eidx, rank, pad_start):
    t = eidx.shape[1]
    tp = PLAN_TILE
    ne = pad_start.shape[0]
    tok = pl.BlockSpec((SUBLANES, tp), lambda i: (0, i))
    ntile = tp // DISPATCH_TILE
    return pl.pallas_call(
        _plan_kernel,
        grid=(t // tp,),
        in_specs=[tok, tok, pl.BlockSpec((ne, 1), lambda i: (0, 0))],
        out_specs=pl.BlockSpec((ntile, SUBLANES, DISPATCH_TILE), lambda i: (i, 0, 0)),
        out_shape=jax.ShapeDtypeStruct((t // DISPATCH_TILE, SUBLANES, DISPATCH_TILE), I32),
        compiler_params=_cparams(("parallel",)),
        name="plan",
    )(eidx, rank, pad_start.astype(F32).reshape(ne, 1))


def _dispatch_kernel(tt, dest_ref, h_ref, xs_in_ref, xs_ref, idx_smem, isem, sem):
    del xs_in_ref
    i = pl.program_id(0)
    n = tt * TOP_K
    icp = pltpu.make_async_copy(dest_ref.at[pl.ds(i * tt * SUBLANES, n)], idx_smem, isem)
    icp.start()
    icp.wait()

    def row_copy(k):
        return pltpu.make_async_copy(h_ref.at[k % tt], xs_ref.at[idx_smem[k]], sem)

    def drain(k, c):
        row_copy(k).wait()
        return c

    for k in range(n):
        row_copy(k).start(priority=k % 2)
    lax.fori_loop(0, n, drain, 0, unroll=8)


def _dispatch(dest_flat, h2p, p_rows):
    t = h2p.shape[0]
    tt = DISPATCH_TILE
    xs0 = jnp.zeros((p_rows,) + h2p.shape[1:], h2p.dtype)
    anyspec = pl.BlockSpec(memory_space=pl.ANY)
    return pl.pallas_call(
        functools.partial(_dispatch_kernel, tt),
        grid=(t // tt,),
        in_specs=[anyspec, pl.BlockSpec((tt,) + h2p.shape[1:], lambda i: (i, 0, 0)), anyspec],
        out_specs=anyspec,
        out_shape=jax.ShapeDtypeStruct(xs0.shape, xs0.dtype),
        scratch_shapes=[pltpu.SMEM((tt * TOP_K,), I32), pltpu.SemaphoreType.DMA(()),
                        pltpu.SemaphoreType.DMA(())],
        input_output_aliases={2: 0},
        compiler_params=_cparams(("arbitrary",)),
        name="dispatch",
    )(dest_flat, h2p, xs0)


def _experts_kernel(be_ref, nu_ref, x_ref, wg_ref, wu_ref, wd_ref, y_ref, wg_s, wu_s, wd_s):
    b = pl.program_id(0)
    live = b < nu_ref[0]
    new_expert = jnp.logical_or(b == 0, be_ref[b] != be_ref[jnp.maximum(b - 1, 0)])

    @pl.when(jnp.logical_and(live, new_expert))
    def _():
        wg_s[...] = wg_ref[0].astype(BF16)
        wu_s[...] = wu_ref[0].astype(BF16)
        wd_s[...] = wd_ref[0].astype(BF16)

    @pl.when(live)
    def _():
        x = _unpack_rows(x_ref, wg_ref.shape[1] // PACK_W)
        gate = jnp.dot(x, wg_s[...], preferred_element_type=F32)
        up = jnp.dot(x, wu_s[...], preferred_element_type=F32)
        y_ref[...] = _dot(_silu(gate) * up, wd_s[...])

    @pl.when(b >= nu_ref[0])
    def _():
        y_ref[...] = jnp.zeros_like(y_ref)


def _experts(block_e, n_used, xs, wg, wu, wd):
    d, de = wg.shape[1], wg.shape[2]
    ns = d // PACK_W
    p_rows = xs.shape[0] // ns
    nblk = p_rows // EXPERT_BLOCK

    def live(b, nu):
        return jnp.minimum(b, jnp.maximum(nu[0] - 1, 0))

    grid_spec = pltpu.PrefetchScalarGridSpec(
        num_scalar_prefetch=2,
        grid=(nblk,),
        in_specs=[pl.BlockSpec((EXPERT_BLOCK * ns, LANES), lambda b, be, nu: (live(b, nu), 0)),
                  pl.BlockSpec((1, d, de), lambda b, be, nu: (be[live(b, nu)], 0, 0)),
                  pl.BlockSpec((1, d, de), lambda b, be, nu: (be[live(b, nu)], 0, 0)),
                  pl.BlockSpec((1, de, d), lambda b, be, nu: (be[live(b, nu)], 0, 0))],
        out_specs=pl.BlockSpec((EXPERT_BLOCK, d), lambda b, be, nu: (b, 0)),
        scratch_shapes=[pltpu.VMEM((d, de), BF16), pltpu.VMEM((d, de), BF16), pltpu.VMEM((de, d), BF16)],
    )
    return pl.pallas_call(
        _experts_kernel,
        grid_spec=grid_spec,
        out_shape=jax.ShapeDtypeStruct((p_rows, d), F32),
        compiler_params=_cparams(("arbitrary",)),
        name="experts",
    )(block_e, n_used, xs, wg, wu, wd)


def _combine_kernel(tt, dest_ref, y_ref, x1_ref, h2_ref, w_ref, mod_ref, npf_ref, sg_ref, su_ref, sd_ref,
                    o_ref, ybuf, idx_smem, isem, sem):
    i = pl.program_id(0)
    n = tt * TOP_K
    icp = pltpu.make_async_copy(dest_ref.at[pl.ds(i * tt * SUBLANES, n)], idx_smem, isem)
    icp.start()
    icp.wait()

    def row_copy(k):
        return pltpu.make_async_copy(y_ref.at[idx_smem[k]], ybuf.at[k], sem)

    def drain(k, c):
        row_copy(k).wait()
        return c

    for k in range(n):
        row_copy(k).start(priority=k % 2)
    h2 = _unpack_rows(h2_ref, sg_ref.shape[0] // PACK_W)
    gate = jnp.dot(h2, sg_ref[...], preferred_element_type=F32)
    up = jnp.dot(h2, su_ref[...], preferred_element_type=F32)
    shared = _dot(_silu(gate) * up, sd_ref[...])
    lax.fori_loop(0, n, drain, 0, unroll=8)
    w = w_ref[...]
    routed = w[:, 0:1] * ybuf[0:tt, :]
    for j in range(1, TOP_K):
        routed = routed + w[:, j:j + 1] * ybuf[j * tt:(j + 1) * tt, :]
    m6 = mod_ref[0]
    o_ref[...] = x1_ref[...] + m6[5:6] * _rms_rows(routed + shared, npf_ref[...])


def _combine(dest_flat, y, x1, h2, wsel_t, mod, seq, npf, sg, su, sd):
    t, d = x1.shape
    tt = DISPATCH_TILE
    anyspec = pl.BlockSpec(memory_space=pl.ANY)

    def full(a):
        return pl.BlockSpec(a.shape, lambda i: (0,) * a.ndim)

    return pl.pallas_call(
        functools.partial(_combine_kernel, tt),
        grid=(t // tt,),
        in_specs=[anyspec, anyspec,
                  pl.BlockSpec((tt, d), lambda i: (i, 0)),
                  pl.BlockSpec((tt * (d // PACK_W), LANES), lambda i: (i, 0)),
                  pl.BlockSpec((tt, SUBLANES), lambda i: (i, 0)),
                  pl.BlockSpec((1, 6, d), lambda i: ((i * tt) // seq, 0, 0)),
                  full(npf), full(sg), full(su), full(sd)],
        out_specs=pl.BlockSpec((tt, d), lambda i: (i, 0)),
        out_shape=jax.ShapeDtypeStruct((t, d), F32),
        scratch_shapes=[pltpu.VMEM((tt * TOP_K, d), F32), pltpu.SMEM((tt * TOP_K,), I32),
                        pltpu.SemaphoreType.DMA(()), pltpu.SemaphoreType.DMA(())],
        compiler_params=_cparams(("arbitrary",)),
        name="combine",
    )(dest_flat, y, x1, h2, wsel_t, mod, npf, sg, su, sd)


def _moe_plan(eidx, rank, cnt, t):
    counts = cnt[:, 0]
    padded = (counts + EXPERT_BLOCK - 1) // EXPERT_BLOCK * EXPERT_BLOCK
    pad_end = jnp.cumsum(padded)
    pad_start = pad_end - padded
    n_blocks = (t * TOP_K + EXPERT_BLOCK - 1) // EXPERT_BLOCK + N_EXPERTS
    first_row = jnp.arange(n_blocks, dtype=I32) * EXPERT_BLOCK
    block_e = jnp.minimum(jnp.sum((pad_end[None, :] <= first_row[:, None]).astype(I32), axis=1),
                          N_EXPERTS - 1).astype(I32)
    n_used = (pad_end[-1:] // EXPERT_BLOCK).astype(I32)
    dest_flat = _plan(eidx, rank, pad_start).reshape(-1)
    return block_e, n_used, dest_flat, n_blocks * EXPERT_BLOCK


def _trunk(x, mod, wc, oc, ec, norm_pre_mix):
    nb, seq, d = x.shape
    t = nb * seq
    x2 = x.reshape(t, d)
    mx = _mixer(x2, nb, seq, mod, norm_pre_mix, wc)
    x1, h2, logits_t = _outproj(x2, seq, mx, mx['z'], mod, wc, oc)
    eidx, wsel, rank, cnt = _route(logits_t, ec['e_bias'])
    block_e, n_used, dest_flat, p_rows = _moe_plan(eidx, rank, cnt, t)
    ns = d // PACK_W
    xs = _dispatch(dest_flat, h2.reshape(t, ns, LANES), p_rows)
    y = _experts(block_e, n_used, xs.reshape(p_rows * ns, LANES), ec['wg'], ec['wu'], ec['wd'])
    out = _combine(dest_flat, y, x1, h2, wsel.T, mod, seq, oc['npo'], ec['sg'], ec['su'], ec['sd'])
    return out.reshape(nb, seq, d)


def kernel(x_prompt, x_sample, c_prompt, c_sample, w_ada, b_ada, norm_pre_mix, norm_post_mix, norm_pre_ffn, norm_post_ffn, w_in, rw_mu, rw_w0, rw_w_up, rw_a0, rw_a_up, rw_g_up, rw_k_k, rw_k_a, rw_r_k, rw_ln_w, rw_ln_b, hg_lb_gamma, hg_norm_w, w_out, w_router, e_bias, w_exp_gate, w_exp_up, w_exp_down, w_sh_gate, w_sh_up, w_sh_down):
    d = x_prompt.shape[-1]
    wc = _layer_consts(w_in, rw_mu, rw_w0, rw_w_up, rw_a0, rw_a_up, rw_g_up, rw_k_k, rw_k_a, rw_r_k,
                       hg_lb_gamma)
    rw, hw = wc['rw'], wc['hw']
    e128, e128t = _indicator(hw, hw // HG_HEADS)
    wr_hi, wr_lo = _split2(w_router[0].T)
    oc = dict(ln_w=rw_ln_w[0].reshape(1, rw), ln_b=rw_ln_b[0].reshape(1, rw),
              hg_norm_w=hg_norm_w[0].reshape(1, hw), npm=norm_post_mix[0].reshape(1, d),
              npf=norm_pre_ffn[0].reshape(1, d), npo=norm_post_ffn[0].reshape(1, d),
              w_out=w_out[0].astype(BF16), wr_hi=wr_hi, wr_lo=wr_lo, e128=e128, e128t=e128t)
    ec = dict(e_bias=e_bias[0], wg=w_exp_gate[0], wu=w_exp_up[0], wd=w_exp_down[0], sg=w_sh_gate[0].astype(BF16), su=w_sh_up[0].astype(BF16),
              sd=w_sh_down[0].astype(BF16))
    nbp = c_prompt.shape[0]
    mod = _ada(jnp.concatenate([c_prompt, c_sample], axis=0), w_ada[0], b_ada[0]).reshape(-1, 6, d)
    y_prompt = _trunk(x_prompt, mod[:nbp], wc, oc, ec, norm_pre_mix[0])
    y_sample = _trunk(x_sample, mod[nbp:], wc, oc, ec, norm_pre_mix[0])
    return (y_prompt, y_sample)
```

```python
import functools
import math

import jax
import jax.numpy as jnp
from jax import lax
from jax.experimental import pallas as pl
from jax.experimental.pallas import tpu as pltpu

F32 = jnp.float32
BF16 = jnp.bfloat16
I32 = jnp.int32

RW_HEAD_DIM = 64
W_LORA = 64
A_LORA = 64
G_LORA = 128
RW_GN_EPS = 64e-5
HG_HEADS = 8
N_EXPERTS = 64
TOP_K = 6
N_GROUPS = 8
TOPK_GROUPS = 4
ROUTED_SCALE = 2.5
EXPERT_BLOCK = 256
NORM_EPS = 1e-6

LANES = 128
SUBLANES = 8
VMEM_LIMIT = 56 * 1024 * 1024

CHUNK = 64


def _cparams(sem, vmem=VMEM_LIMIT):
    return pltpu.CompilerParams(dimension_semantics=sem, vmem_limit_bytes=vmem)


def _sigmoid(x):
    return 1.0 / (1.0 + jnp.exp(-x))


def _silu(x):
    return x * _sigmoid(x)


def _dot(a, b):
    return jnp.dot(a.astype(BF16), b.astype(BF16), preferred_element_type=F32)


def _dot_nt(a, b):
    return lax.dot_general(a.astype(BF16), b.astype(BF16), (((1,), (1,)), ((), ())),
                           preferred_element_type=F32)


def _dot_tn(a, b):
    return lax.dot_general(a.astype(BF16), b.astype(BF16), (((0,), (0,)), ((), ())),
                           preferred_element_type=F32)


def _split2(x):
    hi = x.astype(BF16)
    lo = (x - hi.astype(F32)).astype(BF16)
    return hi, lo


def _split3(x):
    hi = x.astype(BF16)
    r1 = x - hi.astype(F32)
    mid = r1.astype(BF16)
    lo = (r1 - mid.astype(F32)).astype(BF16)
    return hi, mid, lo


def _seg_sum(x, e, et):
    hi, lo = _split2(x)
    s = jnp.dot(hi, e, preferred_element_type=F32) + jnp.dot(lo, e, preferred_element_type=F32)
    shi, slo = _split2(s)
    return jnp.dot(shi, et, preferred_element_type=F32) + jnp.dot(slo, et, preferred_element_type=F32)


PACK_W = 2 * LANES


def _pack_rows(x_bf, out_ref):
    n, d = x_bf.shape
    ns = d // PACK_W
    for s in range(ns):
        a = lax.bitcast_convert_type(x_bf[:, s * PACK_W:s * PACK_W + LANES].astype(F32), jnp.uint32)
        b = lax.bitcast_convert_type(x_bf[:, s * PACK_W + LANES:(s + 1) * PACK_W].astype(F32), jnp.uint32)
        out_ref[pl.ds(s, n, stride=ns), :] = a | (b >> 16)


def _unpack_rows(x_ref, ns):
    n = x_ref.shape[0] // ns
    parts = []
    for s in range(ns):
        w = x_ref[pl.ds(s, n, stride=ns), :]
        parts.append(lax.bitcast_convert_type(w & jnp.uint32(0xFFFF0000), F32).astype(BF16))
        parts.append(lax.bitcast_convert_type(w << 16, F32).astype(BF16))
    return jnp.concatenate(parts, axis=1)


def _rms_rows(x, g):
    return x * lax.rsqrt(jnp.mean(x * x, axis=-1, keepdims=True) + NORM_EPS) * g


def _ada_kernel(c_ref, w_ref, b_ref, o_ref):
    c = c_ref[...]
    o_ref[...] = _dot(_silu(c), w_ref[...]) + b_ref[...]


def _ada(c, w_ada, b_ada):
    nb, d = c.shape
    n = w_ada.shape[1]
    tn = 512
    return pl.pallas_call(
        _ada_kernel,
        grid=(n // tn,),
        in_specs=[pl.BlockSpec((nb, d), lambda j: (0, 0)),
                  pl.BlockSpec((d, tn), lambda j: (0, j)),
                  pl.BlockSpec((1, tn), lambda j: (0, j))],
        out_specs=pl.BlockSpec((nb, tn), lambda j: (0, j)),
        out_shape=jax.ShapeDtypeStruct((nb, n), F32),
        compiler_params=_cparams(("parallel",)),
        name="ada",
    )(c, w_ada, b_ada.reshape(1, n))


def _inproj_kernel(x_ref, mod_ref, g_ref, w_ref, o_ref, h_scr):
    @pl.when(pl.program_id(1) == 0)
    def _():
        m = mod_ref[0]
        h = _rms_rows(x_ref[...], g_ref[...]) * (1.0 + m[1:2]) + m[0:1]
        h_scr[...] = h.astype(BF16)

    o_ref[...] = jnp.dot(h_scr[...], w_ref[...], preferred_element_type=F32)


def _inproj(x2, mod, g, w_bf, seq):
    t, d = x2.shape
    n = w_bf.shape[1]
    tm, tn = 1024, 512
    return pl.pallas_call(
        _inproj_kernel,
        grid=(t // tm, n // tn),
        in_specs=[pl.BlockSpec((tm, d), lambda i, j: (i, 0)),
                  pl.BlockSpec((1, 6, d), lambda i, j: ((i * tm) // seq, 0, 0)),
                  pl.BlockSpec((1, d), lambda i, j: (0, 0)),
                  pl.BlockSpec((d, tn), lambda i, j: (0, j))],
        out_specs=pl.BlockSpec((tm, tn), lambda i, j: (i, j)),
        out_shape=jax.ShapeDtypeStruct((t, n), F32),
        scratch_shapes=[pltpu.VMEM((tm, d), BF16)],
        compiler_params=_cparams(("parallel", "arbitrary")),
        name="inproj",
    )(x2, mod, g.reshape(1, d), w_bf)


def _rwprep_kernel(seq, tm, rw,
                   z_ref, zp_ref, zn_ref, l_ref, lp_ref, ln_ref,
                   mu_ref, mul_ref, wup_ref, aup_ref, gup_ref, w0_ref, a0_ref,
                   kk_ref, ka_ref, rk_ref, e_ref, et_ref,
                   r_o, v_o, kk_o, g_o, bon_o, lw0_o, lw1_o, b0_o, b1_o, kd0_o, kd1_o):
    i = pl.program_id(0)
    first = (i * tm) % seq == 0
    last = ((i + 1) * tm) % seq == 0

    def shifted(cur, prev_blk, next_blk, mu):
        rows = lax.broadcasted_iota(I32, cur.shape, 0)
        prow = jnp.where(first, 0.0, prev_blk[SUBLANES - 1:SUBLANES, :])
        nrow = jnp.where(last, 0.0, next_blk[0:1, :])
        prev = jnp.where(rows == 0, prow, pltpu.roll(cur, 1, axis=0))
        nxt = jnp.where(rows == tm - 1, nrow, pltpu.roll(cur, tm - 1, axis=0))
        return cur + mu * (0.5 * (prev + nxt) - cur)

    lat = shifted(l_ref[...], lp_ref[...], ln_ref[...], mul_ref[...])
    w_lat = lat[:, 0:2 * W_LORA]
    a_lat = lat[:, 2 * W_LORA:2 * W_LORA + 2 * A_LORA]
    g_lat = lat[:, 2 * W_LORA + 2 * A_LORA:2 * W_LORA + 2 * A_LORA + G_LORA]
    w_raw = _dot(jnp.tanh(w_lat), wup_ref[...]) + w0_ref[...]
    a_all = _sigmoid(_dot(a_lat, aup_ref[...]) + a0_ref[...])
    g_o[...] = _dot(_sigmoid(g_lat), gup_ref[...])
    lw = (-math.exp(-0.5)) * _sigmoid(w_raw)
    lw0_o[...] = lw[:, :rw]
    lw1_o[...] = lw[:, rw:]

    r = shifted(z_ref[:, 0:rw], zp_ref[:, 0:rw], zn_ref[:, 0:rw], mu_ref[:, 0:rw])
    k = shifted(z_ref[:, rw:2 * rw], zp_ref[:, rw:2 * rw], zn_ref[:, rw:2 * rw], mu_ref[:, rw:2 * rw])
    v = shifted(z_ref[:, 2 * rw:3 * rw], zp_ref[:, 2 * rw:3 * rw], zn_ref[:, 2 * rw:3 * rw],
                mu_ref[:, 2 * rw:3 * rw])
    r_o[...] = r
    v_o[...] = v
    kk = k * kk_ref[...]
    ss = _seg_sum(kk * kk, e_ref[...], et_ref[...])
    kk = kk / jnp.maximum(jnp.sqrt(ss), 1e-12)
    kk_o[...] = kk
    a0 = a_all[:, :rw]
    a1 = a_all[:, rw:]
    b0_o[...] = kk * a0
    b1_o[...] = kk * a1
    kd0 = k * (1.0 + (a0 - 1.0) * ka_ref[...])
    kd1 = k * (1.0 + (a1 - 1.0) * ka_ref[...])
    kd0_o[...] = kd0
    kd1_o[...] = kd1
    kb = 0.5 * (kd0 + kd1)
    bon_o[...] = _seg_sum(r * kb * rk_ref[...], e_ref[...], et_ref[...]) * v


def _rwprep(z, seq, rw, lat_off, mu_rkv, mu_lat, wup, aup, gup, w0, a0, k_k, k_a, r_k, e64, e64t):
    t = z.shape[0]
    tm = 256
    nlat = 512
    nrkv = 3 * rw
    tb = tm // SUBLANES
    nblk8 = t // SUBLANES
    lat_blk = lat_off // nlat

    def cur(i):
        return (i, 0)

    def prv(i):
        return (jnp.maximum(i * tb - 1, 0), 0)

    def nxt(i):
        return (jnp.minimum((i + 1) * tb, nblk8 - 1), 0)

    def full(shape):
        return pl.BlockSpec(shape, lambda i: (0,) * len(shape))

    out = jax.ShapeDtypeStruct((t, rw), F32)
    ospec = pl.BlockSpec((tm, rw), lambda i: (i, 0))
    return pl.pallas_call(
        functools.partial(_rwprep_kernel, seq, tm, rw),
        grid=(t // tm,),
        in_specs=[pl.BlockSpec((tm, nrkv), cur),
                  pl.BlockSpec((SUBLANES, nrkv), prv),
                  pl.BlockSpec((SUBLANES, nrkv), nxt),
                  pl.BlockSpec((tm, nlat), lambda i: (i, lat_blk)),
                  pl.BlockSpec((SUBLANES, nlat), lambda i: (jnp.maximum(i * tb - 1, 0), lat_blk)),
                  pl.BlockSpec((SUBLANES, nlat), lambda i: (jnp.minimum((i + 1) * tb, nblk8 - 1), lat_blk)),
                  full((1, nrkv)), full((1, nlat)),
                  full(wup.shape), full(aup.shape), full(gup.shape),
                  full((1, 2 * rw)), full((1, 2 * rw)),
                  full((1, rw)), full((1, rw)), full((1, rw)),
                  full(e64.shape), full(e64t.shape)],
        out_specs=[ospec] * 11,
        out_shape=[out] * 11,
        compiler_params=_cparams(("parallel",)),
        name="rwprep",
    )(z, z, z, z, z, z, mu_rkv, mu_lat, wup, aup, gup, w0, a0, k_k, k_a, r_k, e64, e64t)


def _tri(n, rev):
    i = lax.broadcasted_iota(I32, (n, n), 0)
    j = lax.broadcasted_iota(I32, (n, n), 1)
    m = (j >= i) if rev else (j <= i)
    return jnp.where(m, 1.0, 0.0).astype(BF16)


def _rw_streams(streams):
    c = streams[0][0].shape[0]
    hd = RW_HEAD_DIM
    n2 = 2 * c
    ns = len(streams)
    revs = [s[7] for s in streams]
    lane = lax.broadcasted_iota(I32, (c, LANES), 1)
    head_a = lane < hd
    ri = lax.broadcasted_iota(I32, (n2, n2), 0)
    ci = lax.broadcasted_iota(I32, (n2, n2), 1)
    ti = ri % c
    si = ci % c
    same16 = (ri // 16) == (ci // 16)
    same32 = (ri // 32) == (ci // 32)
    mid32 = jnp.logical_and(same32, jnp.logical_not(same16))
    eye = jnp.where(ri == ci, 1.0, 0.0)
    strict = {False: si < ti, True: si > ti}
    incl = {False: si <= ti, True: si >= ti}
    tri = {rev: _tri(c, rev) for rev in set(revs)}
    ei = lax.broadcasted_iota(I32, (LANES, LANES), 0)
    ej = lax.broadcasted_iota(I32, (LANES, LANES), 1)
    eye_k = ei == ej

    def pair(x):
        return jnp.concatenate([jnp.where(head_a, x, 0.0), jnp.where(head_a, 0.0, x)], axis=0)

    cum = []
    for (r, v, kk, lw, b, kd, s_in, rev) in streams:
        hi, lo = _split2(lw)
        cs = jnp.dot(tri[rev], jnp.concatenate([hi, lo], axis=1), preferred_element_type=F32)
        cum.append(cs[:, :LANES] + cs[:, LANES:])
    ops = []
    for (r, v, kk, lw, b, kd, s_in, rev), cm in zip(streams, cum):
        tot = cm[0:1, :] if rev else cm[c - 1:c, :]
        g_inv = jnp.exp(-cm)
        g_tail = jnp.exp(tot - cm)
        ops.append(dict(p2=pair(-kk * jnp.exp(cm - lw)), r2=pair(r * jnp.exp(cm)),
                        bi2=pair(b * g_inv), ki2=pair(kd * g_inv), bt2=pair(b * g_tail),
                        kt2=pair(kd * g_tail), v2=pair(v), g_tot=jnp.exp(tot)))
    gm = [_dot_nt(jnp.concatenate([o['p2'], o['r2']], axis=0), jnp.concatenate([o['bi2'], o['ki2']], axis=0))
          for o in ops]
    a2 = [jnp.where(strict[rev], g[:n2, :n2], 0.0) for g, rev in zip(gm, revs)]
    b2 = [jnp.where(strict[rev], g[:n2, n2:], 0.0) for g, rev in zip(gm, revs)]
    ap2 = [jnp.where(incl[rev], g[n2:, :n2], 0.0) for g, rev in zip(gm, revs)]
    bp2 = [jnp.where(incl[rev], g[n2:, n2:], 0.0) for g, rev in zip(gm, revs)]
    bv = [_dot(x, o['v2']) for x, o in zip(b2, ops)]
    bpv = [_dot(x, o['v2']) for x, o in zip(bp2, ops)]

    x = [jnp.where(same16, a, 0.0) for a in a2]
    tinv = [eye + xi for xi in x]
    for _ in range(3):
        x = [_dot(xi, xi) for xi in x]
        tinv = [t + _dot(t, xi) for t, xi in zip(tinv, x)]
    for lvl in (mid32, jnp.logical_not(same32)):
        y = [_dot(t, jnp.where(lvl, a, 0.0)) for t, a in zip(tinv, a2)]
        tinv = [t + _dot(yi, t) for t, yi in zip(tinv, y)]

    wu = [_dot(t, jnp.concatenate([o['p2'], bvi], axis=1)) for t, o, bvi in zip(tinv, ops, bv)]
    qo = [_dot(a, w) for a, w in zip(ap2, wu)]
    m2 = [_dot_tn(w[:, :LANES], o['bt2']) + jnp.where(eye_k, jnp.broadcast_to(o['g_tot'], (LANES, LANES)), 0.0)
          for w, o in zip(wu, ops)]
    nn2 = [_dot_tn(w[:, LANES:], o['bt2']) + _dot_tn(o['v2'], o['kt2']) for w, o in zip(wu, ops)]

    nt = (((1,), (1,)), ((), ()))
    outs = []
    for i in range(ns):
        s_hi, s_lo = _split2(streams[i][6])
        q2b = (ops[i]['r2'] + qo[i][:, :LANES]).astype(BF16)
        oo = (lax.dot_general(q2b, s_hi, nt, preferred_element_type=F32)
              + lax.dot_general(q2b, s_lo, nt, preferred_element_type=F32)
              + qo[i][:, LANES:] + bpv[i])
        m2b = m2[i].astype(BF16)
        s_out = (jnp.dot(s_hi, m2b, preferred_element_type=F32)
                 + jnp.dot(s_lo, m2b, preferred_element_type=F32) + nn2[i])
        outs.append((oo[:c] + oo[c:], s_out))
    return outs


RW_PAIRS_PER_STEP = 8


def _rwscan_kernel(rf, vf, kkf, lwf, bf, kdf, rb, vb, kkb, lwb, bb, kdb, of_ref, ob_ref, sf, sb):
    @pl.when(pl.program_id(2) == 0)
    def _():
        sf[...] = jnp.zeros_like(sf)
        sb[...] = jnp.zeros_like(sb)

    streams = []
    for p in range(RW_PAIRS_PER_STEP):
        sl = slice(p * LANES, (p + 1) * LANES)
        streams.append((rf[:, sl], vf[:, sl], kkf[:, sl], lwf[:, sl], bf[:, sl], kdf[:, sl], sf[p], False))
        streams.append((rb[:, sl], vb[:, sl], kkb[:, sl], lwb[:, sl], bb[:, sl], kdb[:, sl], sb[p], True))
    outs = _rw_streams(streams)
    for p in range(RW_PAIRS_PER_STEP):
        sl = slice(p * LANES, (p + 1) * LANES)
        of_ref[:, sl], sf[p] = outs[2 * p]
        ob_ref[:, sl], sb[p] = outs[2 * p + 1]


def _rwscan(nb, seq, r, v, kk, lw0, lw1, b0, b1, kd0, kd1):
    t, rw = r.shape
    nc = seq // CHUNK
    wblk = RW_PAIRS_PER_STEP * LANES
    fw = pl.BlockSpec((CHUNK, wblk), lambda bi, hp, c: (bi * nc + c, hp))
    bw = pl.BlockSpec((CHUNK, wblk), lambda bi, hp, c: (bi * nc + nc - 1 - c, hp))
    out = jax.ShapeDtypeStruct((t, rw), F32)
    state = pltpu.VMEM((RW_PAIRS_PER_STEP, LANES, LANES), F32)
    return pl.pallas_call(
        _rwscan_kernel,
        grid=(nb, rw // wblk, nc),
        in_specs=[fw] * 6 + [bw] * 6,
        out_specs=[fw, bw],
        out_shape=[out, out],
        scratch_shapes=[state, state],
        compiler_params=_cparams(("parallel", "parallel", "arbitrary")),
        name="rwscan",
    )(r, v, kk, lw0, b0, kd0, r, v, kk, lw1, b1, kd1)


def _hg_streams(streams):
    c, dk = streams[0][0].shape
    revs = [s[5] for s in streams]
    tri = {rev: _tri(c, rev) for rev in set(revs)}
    row = lax.broadcasted_iota(I32, (c, dk), 0)
    ri = lax.broadcasted_iota(I32, (c, c), 0)
    ci = lax.broadcasted_iota(I32, (c, c), 1)

    cum = []
    for (q, k, v, lf, st, rev) in streams:
        hi, mid, lo = _split3(lf)
        cs = jnp.dot(tri[rev], jnp.concatenate([hi, mid, lo], axis=1), preferred_element_type=F32)
        cum.append(cs[:, :dk] + cs[:, dk:2 * dk] + cs[:, 2 * dk:])
    scores = [jnp.where(ri == ci, jnp.sum(s[0] * s[1], axis=1, keepdims=True), 0.0) for s in streams]
    sub = row % SUBLANES
    nt = (((1,), (1,)), ((), ()))

    def sub_bcast(x, idx):
        x3 = x.reshape(c // SUBLANES, SUBLANES, dk)
        return jnp.broadcast_to(x3[:, idx:idx + 1, :], x3.shape).reshape(c, dk)

    h = c // 2
    while h >= 1:
        blk = 2 * h
        upper = (row % blk) >= h
        same_blk = (ri // blk) == (ci // blk)
        r_up = (ri % blk) >= h
        c_up = (ci % blk) >= h
        q_rows = {False: upper, True: jnp.logical_not(upper)}
        pmask = {False: jnp.logical_and(same_blk, jnp.logical_and(r_up, jnp.logical_not(c_up))),
                 True: jnp.logical_and(same_blk, jnp.logical_and(jnp.logical_not(r_up), c_up))}
        sl = []
        for i, (q, k, v, lf, st, rev) in enumerate(streams):
            cm = cum[i]
            off = h if rev else h - 1
            if h >= SUBLANES:
                pieces = [jnp.broadcast_to(cm[m0 + off:m0 + off + 1, :], (blk, dk)) for m0 in range(0, c, blk)]
                ref = jnp.concatenate(pieces, axis=0) if len(pieces) > 1 else pieces[0]
            elif blk == SUBLANES:
                ref = sub_bcast(cm, off)
            elif 2 * blk == SUBLANES:
                ref = jnp.where(sub < blk, sub_bcast(cm, off), sub_bcast(cm, off + blk))
            else:
                ref = jnp.where(q_rows[rev], pltpu.roll(cm, c - 1 if rev else 1, axis=0), cm)
            dlt = cm - ref
            e = jnp.minimum(jnp.where(q_rows[rev], dlt, -dlt), 0.0)
            x = (jnp.where(q_rows[rev], q, k) * jnp.exp(e)).astype(BF16)
            sl.append(lax.dot_general(x, x, nt, preferred_element_type=F32))
        scores = [sc + jnp.where(pmask[rev], x, 0.0) for sc, x, rev in zip(scores, sl, revs)]
        h //= 2

    outs = []
    for (q, k, v, lf, st, rev), cm, sc in zip(streams, cum, scores):
        tot = cm[0:1, :] if rev else cm[c - 1:c, :]
        o = _dot(sc, v) + _dot_nt(q * jnp.exp(cm), st)
        st_new = st * jnp.exp(tot) + _dot_tn(v, k * jnp.exp(tot - cm))
        outs.append((o, st_new))
    return outs


HG_HEADS_PER_STEP = 4


def _hgscan_kernel(dk, qf, fff, i_f, qb, ffb, i_b, lb_ref, of_ref, ob_ref, sf, sb):
    @pl.when(pl.program_id(2) == 0)
    def _():
        sf[...] = jnp.zeros_like(sf)
        sb[...] = jnp.zeros_like(sb)

    streams = []
    for p in range(HG_HEADS_PER_STEP):
        sl = slice(p * dk, (p + 1) * dk)
        for (q_ref, ff_ref, i_ref, st_ref, d) in ((qf, fff, i_f, sf, 0), (qb, ffb, i_b, sb, 1)):
            lbv = lb_ref[d:d + 1, sl]
            f = lbv + (1.0 - lbv) * _sigmoid(ff_ref[:, sl])
            streams.append((_silu(q_ref[:, sl]), 1.0 - f, i_ref[:, sl], jnp.log(f), st_ref[p], d == 1))
    outs = _hg_streams(streams)
    for p in range(HG_HEADS_PER_STEP):
        sl = slice(p * dk, (p + 1) * dk)
        of_ref[:, sl], sf[p] = outs[2 * p]
        ob_ref[:, sl], sb[p] = outs[2 * p + 1]


def _hgscan(nb, seq, z, lb, hg_off, hw):
    t = z.shape[0]
    nc = seq // CHUNK
    dk = hw // HG_HEADS
    wblk = HG_HEADS_PER_STEP * dk
    base = hg_off // wblk
    nh = hw // wblk

    def fw(comp):
        return pl.BlockSpec((CHUNK, wblk), lambda bi, h, c: (bi * nc + c, base + comp * nh + h))

    def bw(comp):
        return pl.BlockSpec((CHUNK, wblk), lambda bi, h, c: (bi * nc + nc - 1 - c, base + comp * nh + h))

    out = jax.ShapeDtypeStruct((t, hw), F32)
    state = pltpu.VMEM((HG_HEADS_PER_STEP, dk, dk), F32)
    return pl.pallas_call(
        functools.partial(_hgscan_kernel, dk),
        grid=(nb, nh, nc),
        in_specs=[fw(0), fw(1), fw(3), bw(0), bw(2), bw(3),
                  pl.BlockSpec((2, wblk), lambda bi, h, c: (0, h))],
        out_specs=[pl.BlockSpec((CHUNK, wblk), lambda bi, h, c: (bi * nc + c, h)),
                   pl.BlockSpec((CHUNK, wblk), lambda bi, h, c: (bi * nc + nc - 1 - c, h))],
        out_shape=[out, out],
        scratch_shapes=[state, state],
        compiler_params=_cparams(("parallel", "parallel", "arbitrary")),
        name="hgscan",
    )(z, z, z, z, z, z, lb)


def _blockdiag2(w):
    _, r, n = w.shape
    z = jnp.zeros((r, n), w.dtype)
    return jnp.concatenate([jnp.concatenate([w[0], z], axis=1), jnp.concatenate([z, w[1]], axis=1)], axis=0)


def _indicator(width, seg):
    e = (jnp.arange(width)[:, None] // seg == jnp.arange(width // seg)[None, :]).astype(BF16)
    return e, e.T


def _layer_consts(w_in, rw_mu, rw_w0, rw_w_up, rw_a0, rw_a_up, rw_g_up, rw_k_k, rw_k_a, rw_r_k,
                  hg_lb_gamma):
    rw = rw_k_k.shape[-1]
    d = w_in.shape[1]
    nlat = 2 * W_LORA + 2 * A_LORA + G_LORA
    w = w_in[0]
    rkv = 3 * rw
    hg_cols = w.shape[1] - rkv - nlat
    pad = 512 - nlat
    w_perm = jnp.concatenate([w[:, :rkv], w[:, rkv + nlat:], w[:, rkv:rkv + nlat],
                              jnp.zeros((d, pad), w.dtype)], axis=1).astype(BF16)
    mu = rw_mu[0]
    lower = jnp.cumsum(jax.nn.softmax(hg_lb_gamma.astype(F32), axis=0), axis=0)[0]
    hw = lower.shape[-1]
    e64, e64t = _indicator(rw, RW_HEAD_DIM)
    return dict(
        rw=rw, hw=hw, hg_off=rkv, lat_off=rkv + hg_cols, w_in=w_perm,
        mu_rkv=mu[:rkv].reshape(1, rkv),
        mu_lat=jnp.pad(mu[rkv:rkv + nlat], (0, pad)).reshape(1, 512),
        wup=_blockdiag2(rw_w_up[0]).astype(BF16), aup=_blockdiag2(rw_a_up[0]).astype(BF16),
        gup=rw_g_up[0].astype(BF16),
        w0=rw_w0[0].reshape(1, 2 * rw), a0=rw_a0[0].reshape(1, 2 * rw),
        k_k=rw_k_k[0].reshape(1, rw), k_a=rw_k_a[0].reshape(1, rw), r_k=rw_r_k[0].reshape(1, rw),
        e64=e64, e64t=e64t,
        lb=lower,
    )


def _mixer(x2, nb, seq, mod, norm_pre_mix, wc):
    z = _inproj(x2, mod, norm_pre_mix, wc['w_in'], seq)
    (r, v, kk, g, bonus, lw0, lw1, b0, b1, kd0, kd1) = _rwprep(
        z, seq, wc['rw'], wc['lat_off'], wc['mu_rkv'], wc['mu_lat'], wc['wup'], wc['aup'], wc['gup'],
        wc['w0'], wc['a0'], wc['k_k'], wc['k_a'], wc['r_k'], wc['e64'], wc['e64t'])
    rw_of, rw_ob = _rwscan(nb, seq, r, v, kk, lw0, lw1, b0, b1, kd0, kd1)
    hg_of, hg_ob = _hgscan(nb, seq, z, wc['lb'], wc['hg_off'], wc['hw'])
    return dict(z=z, r=r, v=v, kk=kk, g=g, bonus=bonus, lw0=lw0, rw_of=rw_of, rw_ob=rw_ob,
                hg_of=hg_of, hg_ob=hg_ob)


def _outproj_kernel(rw, x_ref, rf_ref, rb_ref, bon_ref, g_ref, hf_ref, hb_ref, hgg_ref, mod_ref,
                    lnw_ref, lnb_ref, hnw_ref, npm_ref, npf_ref, wout_ref, wrh_ref, wrl_ref,
                    e64_ref, e64t_ref, e128_ref, e128t_ref,
                    x1_ref, h2_ref, lg_ref):
    m6 = mod_ref[0]
    o = rf_ref[...] + rb_ref[...]
    mean = _seg_sum(o, e64_ref[...], e64t_ref[...]) * (1.0 / RW_HEAD_DIM)
    dlt = o - mean
    var = _seg_sum(dlt * dlt, e64_ref[...], e64t_ref[...]) * (1.0 / RW_HEAD_DIM)
    o_rw = (dlt * lax.rsqrt(var + RW_GN_EPS) * lnw_ref[...] + lnb_ref[...] + bon_ref[...]) * g_ref[...]
    oh = hf_ref[...] + hb_ref[...]
    hd = oh.shape[1] // HG_HEADS
    ms = _seg_sum(oh * oh, e128_ref[...], e128t_ref[...]) * (1.0 / hd)
    o_hg = oh * lax.rsqrt(ms + NORM_EPS) * hnw_ref[...] * _silu(hgg_ref[...])
    m = _dot(o_rw, wout_ref[0:rw, :]) + _dot(o_hg, wout_ref[rw:, :])
    x1 = x_ref[...] + m6[2:3] * _rms_rows(m, npm_ref[...])
    x1_ref[...] = x1
    h2 = _rms_rows(x1, npf_ref[...]) * (1.0 + m6[4:5]) + m6[3:4]
    hi, lo = _split2(h2)
    _pack_rows(hi, h2_ref)
    nt = (((1,), (1,)), ((), ()))
    lg_ref[...] = (lax.dot_general(wrh_ref[...], hi, nt, preferred_element_type=F32)
                   + lax.dot_general(wrh_ref[...], lo, nt, preferred_element_type=F32)
                   + lax.dot_general(wrl_ref[...], hi, nt, preferred_element_type=F32))


def _outproj(x2, seq, mx, z, mod, wc, oc):
    t, d = x2.shape
    rw, hw = wc['rw'], wc['hw']
    tm = 256
    gblk = (wc['hg_off'] + 4 * hw) // hw

    def row(w):
        return pl.BlockSpec((tm, w), lambda i: (i, 0))

    def full(a):
        return pl.BlockSpec(a.shape, lambda i: (0,) * a.ndim)

    consts = [oc['ln_w'], oc['ln_b'], oc['hg_norm_w'], oc['npm'], oc['npf'], oc['w_out'], oc['wr_hi'],
              oc['wr_lo'], wc['e64'], wc['e64t'], oc['e128'], oc['e128t']]
    return pl.pallas_call(
        functools.partial(_outproj_kernel, rw),
        grid=(t // tm,),
        in_specs=[row(d), row(rw), row(rw), row(rw), row(rw), row(hw), row(hw),
                  pl.BlockSpec((tm, hw), lambda i: (i, gblk)),
                  pl.BlockSpec((1, 6, d), lambda i: ((i * tm) // seq, 0, 0))] + [full(a) for a in consts],
        out_specs=[row(d), pl.BlockSpec((tm * (d // PACK_W), LANES), lambda i: (i, 0)),
                   pl.BlockSpec((N_EXPERTS, tm), lambda i: (0, i))],
        out_shape=[jax.ShapeDtypeStruct((t, d), F32), jax.ShapeDtypeStruct((t * (d // PACK_W), LANES), jnp.uint32),
                   jax.ShapeDtypeStruct((N_EXPERTS, t), F32)],
        compiler_params=_cparams(("parallel",)),
        name="outproj",
    )(x2, mx['rw_of'], mx['rw_ob'], mx['bonus'], mx['g'], mx['hg_of'], mx['hg_ob'], z, mod, *consts)


ROUTE_TILE = 512


def _route_kernel(lg_ref, bias_ref, ut_ref, eidx_ref, wsel_ref, rank_ref, cnt_ref, carry):
    @pl.when(pl.program_id(0) == 0)
    def _():
        carry[...] = jnp.zeros_like(carry)

    ne, tt = lg_ref.shape
    gsz = ne // N_GROUPS
    neg = -jnp.inf
    s = _sigmoid(lg_ref[...])
    biased = s + bias_ref[...]
    io_g = lax.broadcasted_iota(I32, (gsz, tt), 0)
    gs_rows = []
    for gi in range(N_GROUPS):
        blk = biased[gi * gsz:(gi + 1) * gsz, :]
        m1 = jnp.max(blk, axis=0, keepdims=True)
        first = jnp.min(jnp.where(blk == m1, io_g, gsz), axis=0, keepdims=True)
        m2 = jnp.max(jnp.where(io_g == first, neg, blk), axis=0, keepdims=True)
        gs_rows.append(m1 + m2)
    gs = jnp.concatenate(gs_rows, axis=0)
    io_n = lax.broadcasted_iota(I32, (N_GROUPS, tt), 0)
    selg = jnp.zeros((N_GROUPS, tt), jnp.bool_)
    for _ in range(TOPK_GROUPS):
        m = jnp.max(gs, axis=0, keepdims=True)
        first = jnp.min(jnp.where(gs == m, io_n, N_GROUPS), axis=0, keepdims=True)
        pick = io_n == first
        selg = jnp.logical_or(selg, pick)
        gs = jnp.where(pick, neg, gs)
    emask = jnp.concatenate([jnp.broadcast_to(selg[gi:gi + 1, :], (gsz, tt)) for gi in range(N_GROUPS)],
                            axis=0)
    mb = jnp.where(emask, biased, neg)
    io_e = lax.broadcasted_iota(I32, (ne, tt), 0)
    sel = jnp.zeros((ne, tt), jnp.bool_)
    picks, idxs, ws = [], [], []
    for _ in range(TOP_K):
        m = jnp.max(mb, axis=0, keepdims=True)
        first = jnp.min(jnp.where(mb == m, io_e, ne), axis=0, keepdims=True)
        pick = io_e == first
        picks.append(pick)
        idxs.append(first)
        ws.append(jnp.sum(jnp.where(pick, s, 0.0), axis=0, keepdims=True))
        sel = jnp.logical_or(sel, pick)
        mb = jnp.where(pick, neg, mb)
    wsum = ws[0]
    for w in ws[1:]:
        wsum = wsum + w
    pos = jnp.dot(jnp.where(sel, 1.0, 0.0).astype(BF16), ut_ref[...], preferred_element_type=F32) + carry[...]
    ranks = [jnp.sum(jnp.where(p, pos, 0.0), axis=0, keepdims=True).astype(I32) for p in picks]
    carry[...] = carry[...] + jnp.sum(jnp.where(sel, 1.0, 0.0), axis=1, keepdims=True)
    zi = jnp.zeros((SUBLANES - TOP_K, tt), I32)
    eidx_ref[...] = jnp.concatenate(idxs + [zi], axis=0)
    rank_ref[...] = jnp.concatenate(ranks + [zi], axis=0)
    wsel_ref[...] = jnp.concatenate([w / wsum * ROUTED_SCALE for w in ws] + [zi.astype(F32)], axis=0)
    cnt_ref[...] = jnp.broadcast_to(carry[...], cnt_ref.shape).astype(I32)


def _route(logits_t, e_bias):
    ne, t = logits_t.shape
    tt = ROUTE_TILE
    ut = (jnp.arange(tt)[:, None] < jnp.arange(tt)[None, :]).astype(BF16)
    tok = pl.BlockSpec((SUBLANES, tt), lambda i: (0, i))
    return pl.pallas_call(
        _route_kernel,
        grid=(t // tt,),
        in_specs=[pl.BlockSpec((ne, tt), lambda i: (0, i)),
                  pl.BlockSpec((ne, 1), lambda i: (0, 0)),
                  pl.BlockSpec((tt, tt), lambda i: (0, 0))],
        out_specs=[tok, tok, tok, pl.BlockSpec((ne, LANES), lambda i: (0, 0))],
        out_shape=[jax.ShapeDtypeStruct((SUBLANES, t), I32), jax.ShapeDtypeStruct((SUBLANES, t), F32),
                   jax.ShapeDtypeStruct((SUBLANES, t), I32), jax.ShapeDtypeStruct((ne, LANES), I32)],
        scratch_shapes=[pltpu.VMEM((ne, 1), F32)],
        compiler_params=_cparams(("arbitrary",)),
        name="route",
    )(logits_t, e_bias.reshape(ne, 1), ut)


DISPATCH_TILE = 128
PLAN_TILE = 1024


def _plan_kernel(eidx_ref, rank_ref, ps_ref, dest_ref):
    ne = ps_ref.shape[0]
    tp = eidx_ref.shape[1]
    io_e = lax.broadcasted_iota(I32, (ne, tp), 0)
    ps = ps_ref[...]
    rows = []
    for j in range(TOP_K):
        start = jnp.sum(jnp.where(io_e == eidx_ref[j:j + 1, :], ps, 0.0), axis=0, keepdims=True)
        rows.append(start.astype(I32) + rank_ref[j:j + 1, :])
    dest = jnp.concatenate(rows + [jnp.zeros((SUBLANES - TOP_K, tp), I32)], axis=0)
    for i in range(tp // DISPATCH_TILE):
        dest_ref[i] = dest[:, i * DISPATCH_TILE:(i + 1) * DISPATCH_TILE]


def _plan(eidx, rank, pad_start):
    t = eidx.shape[1]
    tp = PLAN_TILE
    ne = pad_start.shape[0]
    tok = pl.BlockSpec((SUBLANES, tp), lambda i: (0, i))
    ntile = tp // DISPATCH_TILE
    return pl.pallas_call(
        _plan_kernel,
        grid=(t // tp,),
        in_specs=[tok, tok, pl.BlockSpec((ne, 1), lambda i: (0, 0))],
        out_specs=pl.BlockSpec((ntile, SUBLANES, DISPATCH_TILE), lambda i: (i, 0, 0)),
        out_shape=jax.ShapeDtypeStruct((t // DISPATCH_TILE, SUBLANES, DISPATCH_TILE), I32),
        compiler_params=_cparams(("parallel",)),
        name="plan",
    )(eidx, rank, pad_start.astype(F32).reshape(ne, 1))


def _dispatch_kernel(tt, dest_ref, h_ref, xs_in_ref, xs_ref, idx_smem, isem, sem):
    del xs_in_ref
    i = pl.program_id(0)
    n = tt * TOP_K
    icp = pltpu.make_async_copy(dest_ref.at[pl.ds(i * tt * SUBLANES, n)], idx_smem, isem)
    icp.start()
    icp.wait()

    def row_copy(k):
        return pltpu.make_async_copy(h_ref.at[k % tt], xs_ref.at[idx_smem[k]], sem)

    def drain(k, c):
        row_copy(k).wait()
        return c

    for k in range(n):
        row_copy(k).start(priority=k % 2)
    lax.fori_loop(0, n, drain, 0, unroll=8)


def _dispatch(dest_flat, h2p, p_rows):
    t = h2p.shape[0]
    tt = DISPATCH_TILE
    xs0 = jnp.zeros((p_rows,) + h2p.shape[1:], h2p.dtype)
    anyspec = pl.BlockSpec(memory_space=pl.ANY)
    return pl.pallas_call(
        functools.partial(_dispatch_kernel, tt),
        grid=(t // tt,),
        in_specs=[anyspec, pl.BlockSpec((tt,) + h2p.shape[1:], lambda i: (i, 0, 0)), anyspec],
        out_specs=anyspec,
        out_shape=jax.ShapeDtypeStruct(xs0.shape, xs0.dtype),
        scratch_shapes=[pltpu.SMEM((tt * TOP_K,), I32), pltpu.SemaphoreType.DMA(()),
                        pltpu.SemaphoreType.DMA(())],
        input_output_aliases={2: 0},
        compiler_params=_cparams(("arbitrary",)),
        name="dispatch",
    )(dest_flat, h2p, xs0)


def _experts_kernel(be_ref, nu_ref, x_ref, wg_ref, wu_ref, wd_ref, y_ref, wg_s, wu_s, wd_s):
    b = pl.program_id(0)
    live = b < nu_ref[0]
    new_expert = jnp.logical_or(b == 0, be_ref[b] != be_ref[jnp.maximum(b - 1, 0)])

    @pl.when(jnp.logical_and(live, new_expert))
    def _():
        wg_s[...] = wg_ref[0].astype(BF16)
        wu_s[...] = wu_ref[0].astype(BF16)
        wd_s[...] = wd_ref[0].astype(BF16)

    @pl.when(live)
    def _():
        x = _unpack_rows(x_ref, wg_ref.shape[1] // PACK_W)
        gate = jnp.dot(x, wg_s[...], preferred_element_type=F32)
        up = jnp.dot(x, wu_s[...], preferred_element_type=F32)
        _pack_rows(_dot(_silu(gate) * up, wd_s[...]).astype(BF16), y_ref)

    @pl.when(b >= nu_ref[0])
    def _():
        y_ref[...] = jnp.zeros_like(y_ref)


def _experts(block_e, n_used, xs, wg, wu, wd):
    d, de = wg.shape[1], wg.shape[2]
    ns = d // PACK_W
    p_rows = xs.shape[0] // ns
    nblk = p_rows // EXPERT_BLOCK

    def live(b, nu):
        return jnp.minimum(b, jnp.maximum(nu[0] - 1, 0))

    grid_spec = pltpu.PrefetchScalarGridSpec(
        num_scalar_prefetch=2,
        grid=(nblk,),
        in_specs=[pl.BlockSpec((EXPERT_BLOCK * ns, LANES), lambda b, be, nu: (live(b, nu), 0)),
                  pl.BlockSpec((1, d, de), lambda b, be, nu: (be[live(b, nu)], 0, 0)),
                  pl.BlockSpec((1, d, de), lambda b, be, nu: (be[live(b, nu)], 0, 0)),
                  pl.BlockSpec((1, de, d), lambda b, be, nu: (be[live(b, nu)], 0, 0))],
        out_specs=pl.BlockSpec((EXPERT_BLOCK * ns, LANES), lambda b, be, nu: (b, 0)),
        scratch_shapes=[pltpu.VMEM((d, de), BF16), pltpu.VMEM((d, de), BF16), pltpu.VMEM((de, d), BF16)],
    )
    return pl.pallas_call(
        _experts_kernel,
        grid_spec=grid_spec,
        out_shape=jax.ShapeDtypeStruct((p_rows * ns, LANES), jnp.uint32),
        compiler_params=_cparams(("arbitrary",)),
        name="experts",
    )(block_e, n_used, xs, wg, wu, wd)


def _combine_kernel(tt, nsteps, dest_ref, y_ref, x1_ref, h2_ref, w_ref, mod_ref, npf_ref, sg_ref, su_ref, sd_ref,
                    o_ref, ybuf0, ybuf1, idx_smem, isem, sem0, sem1):
    i = pl.program_id(0)
    n = tt * TOP_K
    ns = sg_ref.shape[0] // PACK_W
    bufs = ((ybuf0, sem0), (ybuf1, sem1))

    def row_copy(slot, k):
        buf, sem = bufs[slot]
        return pltpu.make_async_copy(y_ref.at[idx_smem[slot * n + k]], buf.at[pl.ds(k * ns, ns)], sem)

    def issue(step, slot):
        icp = pltpu.make_async_copy(dest_ref.at[pl.ds(step * tt * SUBLANES, n)],
                                    idx_smem.at[pl.ds(slot * n, n)], isem)
        icp.start()
        icp.wait()
        for k in range(n):
            row_copy(slot, k).start(priority=k % 2)

    @pl.when(i == 0)
    def _():
        issue(0, 0)

    for slot in (0, 1):
        @pl.when(jnp.logical_and(i + 1 < nsteps, (i + 1) % 2 == slot))
        def _():
            issue(i + 1, slot)

    h2 = _unpack_rows(h2_ref, ns)
    gate = jnp.dot(h2, sg_ref[...], preferred_element_type=F32)
    up = jnp.dot(h2, su_ref[...], preferred_element_type=F32)
    shared = _dot(_silu(gate) * up, sd_ref[...])
    w = w_ref[...]
    m6 = mod_ref[0]

    def finish(slot):
        buf, _ = bufs[slot]

        def drain(k, c):
            row_copy(slot, k).wait()
            return c

        lax.fori_loop(0, n, drain, 0, unroll=8)
        acc = [None] * (2 * ns)
        for j in range(TOP_K):
            wj = w[:, j:j + 1]
            for s in range(ns):
                word = buf[pl.ds(j * tt * ns + s, tt, stride=ns), :]
                parts = (lax.bitcast_convert_type(word & jnp.uint32(0xFFFF0000), F32),
                         lax.bitcast_convert_type(word << 16, F32))
                for h, part in enumerate(parts):
                    term = wj * part
                    acc[2 * s + h] = term if acc[2 * s + h] is None else acc[2 * s + h] + term
        routed = jnp.concatenate(acc, axis=1)
        o_ref[...] = x1_ref[...] + m6[5:6] * _rms_rows(routed + shared, npf_ref[...])

    for slot in (0, 1):
        @pl.when(i % 2 == slot)
        def _():
            finish(slot)


def _combine(dest_flat, y3, x1, h2, wsel_t, mod, seq, npf, sg, su, sd):
    t, d = x1.shape
    tt = DISPATCH_TILE
    ns = d // PACK_W
    n = tt * TOP_K
    anyspec = pl.BlockSpec(memory_space=pl.ANY)

    def full(a):
        return pl.BlockSpec(a.shape, lambda i: (0,) * a.ndim)

    rows = pltpu.VMEM((n * ns, LANES), jnp.uint32)
    return pl.pallas_call(
        functools.partial(_combine_kernel, tt, t // tt),
        grid=(t // tt,),
        in_specs=[anyspec, anyspec,
                  pl.BlockSpec((tt, d), lambda i: (i, 0)),
                  pl.BlockSpec((tt * ns, LANES), lambda i: (i, 0)),
                  pl.BlockSpec((tt, SUBLANES), lambda i: (i, 0)),
                  pl.BlockSpec((1, 6, d), lambda i: ((i * tt) // seq, 0, 0)),
                  full(npf), full(sg), full(su), full(sd)],
        out_specs=pl.BlockSpec((tt, d), lambda i: (i, 0)),
        out_shape=jax.ShapeDtypeStruct((t, d), F32),
        scratch_shapes=[rows, rows, pltpu.SMEM((2 * n,), I32),
                        pltpu.SemaphoreType.DMA(()), pltpu.SemaphoreType.DMA(()), pltpu.SemaphoreType.DMA(())],
        compiler_params=_cparams(("arbitrary",)),
        name="combine",
    )(dest_flat, y3, x1, h2, wsel_t, mod, npf, sg, su, sd)


def _moe_plan(eidx, rank, cnt, t):
    counts = cnt[:, 0]
    padded = (counts + EXPERT_BLOCK - 1) // EXPERT_BLOCK * EXPERT_BLOCK
    pad_end = jnp.cumsum(padded)
    pad_start = pad_end - padded
    n_blocks = (t * TOP_K + EXPERT_BLOCK - 1) // EXPERT_BLOCK + N_EXPERTS
    first_row = jnp.arange(n_blocks, dtype=I32) * EXPERT_BLOCK
    block_e = jnp.minimum(jnp.sum((pad_end[None, :] <= first_row[:, None]).astype(I32), axis=1),
                          N_EXPERTS - 1).astype(I32)
    n_used = (pad_end[-1:] // EXPERT_BLOCK).astype(I32)
    dest_flat = _plan(eidx, rank, pad_start).reshape(-1)
    return block_e, n_used, dest_flat, n_blocks * EXPERT_BLOCK


def _trunk(x, mod, wc, oc, ec, norm_pre_mix):
    nb, seq, d = x.shape
    t = nb * seq
    x2 = x.reshape(t, d)
    mx = _mixer(x2, nb, seq, mod, norm_pre_mix, wc)
    x1, h2, logits_t = _outproj(x2, seq, mx, mx['z'], mod, wc, oc)
    eidx, wsel, rank, cnt = _route(logits_t, ec['e_bias'])
    block_e, n_used, dest_flat, p_rows = _moe_plan(eidx, rank, cnt, t)
    ns = d // PACK_W
    xs = _dispatch(dest_flat, h2.reshape(t, ns, LANES), p_rows)
    y = _experts(block_e, n_used, xs.reshape(p_rows * ns, LANES), ec['wg'], ec['wu'], ec['wd'])
    out = _combine(dest_flat, y.reshape(p_rows, ns, LANES), x1, h2, wsel.T, mod, seq, oc['npo'],
                   ec['sg'], ec['su'], ec['sd'])
    return out.reshape(nb, seq, d)


def kernel(x_prompt, x_sample, c_prompt, c_sample, w_ada, b_ada, norm_pre_mix, norm_post_mix, norm_pre_ffn, norm_post_ffn, w_in, rw_mu, rw_w0, rw_w_up, rw_a0, rw_a_up, rw_g_up, rw_k_k, rw_k_a, rw_r_k, rw_ln_w, rw_ln_b, hg_lb_gamma, hg_norm_w, w_out, w_router, e_bias, w_exp_gate, w_exp_up, w_exp_down, w_sh_gate, w_sh_up, w_sh_down):
    d = x_prompt.shape[-1]
    wc = _layer_consts(w_in, rw_mu, rw_w0, rw_w_up, rw_a0, rw_a_up, rw_g_up, rw_k_k, rw_k_a, rw_r_k,
                       hg_lb_gamma)
    rw, hw = wc['rw'], wc['hw']
    e128, e128t = _indicator(hw, hw // HG_HEADS)
    wr_hi, wr_lo = _split2(w_router[0].T)
    oc = dict(ln_w=rw_ln_w[0].reshape(1, rw), ln_b=rw_ln_b[0].reshape(1, rw),
              hg_norm_w=hg_norm_w[0].reshape(1, hw), npm=norm_post_mix[0].reshape(1, d),
              npf=norm_pre_ffn[0].reshape(1, d), npo=norm_post_ffn[0].reshape(1, d),
              w_out=w_out[0].astype(BF16), wr_hi=wr_hi, wr_lo=wr_lo, e128=e128, e128t=e128t)
    ec = dict(e_bias=e_bias[0], wg=w_exp_gate[0], wu=w_exp_up[0], wd=w_exp_down[0], sg=w_sh_gate[0].astype(BF16), su=w_sh_up[0].astype(BF16),
              sd=w_sh_down[0].astype(BF16))
    nbp = c_prompt.shape[0]
    mod = _ada(jnp.concatenate([c_prompt, c_sample], axis=0), w_ada[0], b_ada[0]).reshape(-1, 6, d)
    y_prompt = _trunk(x_prompt, mod[:nbp], wc, oc, ec, norm_pre_mix[0])
    y_sample = _trunk(x_sample, mod[nbp:], wc, oc, ec, norm_pre_mix[0])
    return (y_prompt, y_sample)
```

```python
import functools
import math

import jax
import jax.numpy as jnp
from jax import lax
from jax.experimental import pallas as pl
from jax.experimental.pallas import tpu as pltpu

F32 = jnp.float32
BF16 = jnp.bfloat16
I32 = jnp.int32

RW_HEAD_DIM = 64
W_LORA = 64
A_LORA = 64
G_LORA = 128
RW_GN_EPS = 64e-5
HG_HEADS = 8
N_EXPERTS = 64
TOP_K = 6
N_GROUPS = 8
TOPK_GROUPS = 4
ROUTED_SCALE = 2.5
EXPERT_BLOCK = 256
NORM_EPS = 1e-6

LANES = 128
SUBLANES = 8
VMEM_LIMIT = 56 * 1024 * 1024

CHUNK = 64


def _cparams(sem, vmem=VMEM_LIMIT):
    return pltpu.CompilerParams(dimension_semantics=sem, vmem_limit_bytes=vmem)


def _sigmoid(x):
    return 1.0 / (1.0 + jnp.exp(-x))


def _silu(x):
    return x * _sigmoid(x)


def _dot(a, b):
    return jnp.dot(a.astype(BF16), b.astype(BF16), preferred_element_type=F32)


def _dot_nt(a, b):
    return lax.dot_general(a.astype(BF16), b.astype(BF16), (((1,), (1,)), ((), ())),
                           preferred_element_type=F32)


def _dot_tn(a, b):
    return lax.dot_general(a.astype(BF16), b.astype(BF16), (((0,), (0,)), ((), ())),
                           preferred_element_type=F32)


def _split2(x):
    hi = x.astype(BF16)
    lo = (x - hi.astype(F32)).astype(BF16)
    return hi, lo


def _split3(x):
    hi = x.astype(BF16)
    r1 = x - hi.astype(F32)
    mid = r1.astype(BF16)
    lo = (r1 - mid.astype(F32)).astype(BF16)
    return hi, mid, lo


def _seg_sum(x, e, et):
    hi, lo = _split2(x)
    s = jnp.dot(hi, e, preferred_element_type=F32) + jnp.dot(lo, e, preferred_element_type=F32)
    shi, slo = _split2(s)
    return jnp.dot(shi, et, preferred_element_type=F32) + jnp.dot(slo, et, preferred_element_type=F32)


PACK_W = 2 * LANES


def _pack_rows(x_bf, out_ref):
    n, d = x_bf.shape
    ns = d // PACK_W
    for s in range(ns):
        a = lax.bitcast_convert_type(x_bf[:, s * PACK_W:s * PACK_W + LANES].astype(F32), jnp.uint32)
        b = lax.bitcast_convert_type(x_bf[:, s * PACK_W + LANES:(s + 1) * PACK_W].astype(F32), jnp.uint32)
        out_ref[pl.ds(s, n, stride=ns), :] = a | (b >> 16)


def _unpack_rows(x_ref, ns):
    n = x_ref.shape[0] // ns
    parts = []
    for s in range(ns):
        w = x_ref[pl.ds(s, n, stride=ns), :]
        parts.append(lax.bitcast_convert_type(w & jnp.uint32(0xFFFF0000), F32).astype(BF16))
        parts.append(lax.bitcast_convert_type(w << 16, F32).astype(BF16))
    return jnp.concatenate(parts, axis=1)


def _rms_rows(x, g):
    return x * lax.rsqrt(jnp.mean(x * x, axis=-1, keepdims=True) + NORM_EPS) * g


def _ada_kernel(c_ref, w_ref, b_ref, o_ref):
    c = c_ref[...]
    o_ref[...] = _dot(_silu(c), w_ref[...]) + b_ref[...]


def _ada(c, w_ada, b_ada):
    nb, d = c.shape
    n = w_ada.shape[1]
    tn = 512
    return pl.pallas_call(
        _ada_kernel,
        grid=(n // tn,),
        in_specs=[pl.BlockSpec((nb, d), lambda j: (0, 0)),
                  pl.BlockSpec((d, tn), lambda j: (0, j)),
                  pl.BlockSpec((1, tn), lambda j: (0, j))],
        out_specs=pl.BlockSpec((nb, tn), lambda j: (0, j)),
        out_shape=jax.ShapeDtypeStruct((nb, n), F32),
        compiler_params=_cparams(("parallel",)),
        name="ada",
    )(c, w_ada, b_ada.reshape(1, n))


def _inproj_kernel(x_ref, mod_ref, g_ref, w_ref, o_ref, h_scr):
    @pl.when(pl.program_id(1) == 0)
    def _():
        m = mod_ref[0]
        h = _rms_rows(x_ref[...], g_ref[...]) * (1.0 + m[1:2]) + m[0:1]
        h_scr[...] = h.astype(BF16)

    o_ref[...] = jnp.dot(h_scr[...], w_ref[...], preferred_element_type=F32)


def _inproj(x2, mod, g, w_bf, seq):
    t, d = x2.shape
    n = w_bf.shape[1]
    tm, tn = 1024, 512
    return pl.pallas_call(
        _inproj_kernel,
        grid=(t // tm, n // tn),
        in_specs=[pl.BlockSpec((tm, d), lambda i, j: (i, 0)),
                  pl.BlockSpec((1, 6, d), lambda i, j: ((i * tm) // seq, 0, 0)),
                  pl.BlockSpec((1, d), lambda i, j: (0, 0)),
                  pl.BlockSpec((d, tn), lambda i, j: (0, j))],
        out_specs=pl.BlockSpec((tm, tn), lambda i, j: (i, j)),
        out_shape=jax.ShapeDtypeStruct((t, n), F32),
        scratch_shapes=[pltpu.VMEM((tm, d), BF16)],
        compiler_params=_cparams(("parallel", "arbitrary")),
        name="inproj",
    )(x2, mod, g.reshape(1, d), w_bf)


def _rwprep_kernel(seq, tm, rw,
                   z_ref, zp_ref, zn_ref, l_ref, lp_ref, ln_ref,
                   mu_ref, mul_ref, wup_ref, aup_ref, gup_ref, w0_ref, a0_ref,
                   kk_ref, ka_ref, rk_ref, e_ref, et_ref,
                   r_o, v_o, kk_o, g_o, bon_o, lw0_o, lw1_o, b0_o, b1_o, kd0_o, kd1_o):
    i = pl.program_id(0)
    first = (i * tm) % seq == 0
    last = ((i + 1) * tm) % seq == 0

    def shifted(cur, prev_blk, next_blk, mu):
        rows = lax.broadcasted_iota(I32, cur.shape, 0)
        prow = jnp.where(first, 0.0, prev_blk[SUBLANES - 1:SUBLANES, :])
        nrow = jnp.where(last, 0.0, next_blk[0:1, :])
        prev = jnp.where(rows == 0, prow, pltpu.roll(cur, 1, axis=0))
        nxt = jnp.where(rows == tm - 1, nrow, pltpu.roll(cur, tm - 1, axis=0))
        return cur + mu * (0.5 * (prev + nxt) - cur)

    lat = shifted(l_ref[...], lp_ref[...], ln_ref[...], mul_ref[...])
    w_lat = lat[:, 0:2 * W_LORA]
    a_lat = lat[:, 2 * W_LORA:2 * W_LORA + 2 * A_LORA]
    g_lat = lat[:, 2 * W_LORA + 2 * A_LORA:2 * W_LORA + 2 * A_LORA + G_LORA]
    w_raw = _dot(jnp.tanh(w_lat), wup_ref[...]) + w0_ref[...]
    a_all = _sigmoid(_dot(a_lat, aup_ref[...]) + a0_ref[...])
    g_o[...] = _dot(_sigmoid(g_lat), gup_ref[...])
    lw = (-math.exp(-0.5)) * _sigmoid(w_raw)
    lw0_o[...] = lw[:, :rw]
    lw1_o[...] = lw[:, rw:]

    r = shifted(z_ref[:, 0:rw], zp_ref[:, 0:rw], zn_ref[:, 0:rw], mu_ref[:, 0:rw])
    k = shifted(z_ref[:, rw:2 * rw], zp_ref[:, rw:2 * rw], zn_ref[:, rw:2 * rw], mu_ref[:, rw:2 * rw])
    v = shifted(z_ref[:, 2 * rw:3 * rw], zp_ref[:, 2 * rw:3 * rw], zn_ref[:, 2 * rw:3 * rw],
                mu_ref[:, 2 * rw:3 * rw])
    r_o[...] = r
    v_o[...] = v
    kk = k * kk_ref[...]
    ss = _seg_sum(kk * kk, e_ref[...], et_ref[...])
    kk = kk / jnp.maximum(jnp.sqrt(ss), 1e-12)
    kk_o[...] = kk
    a0 = a_all[:, :rw]
    a1 = a_all[:, rw:]
    b0_o[...] = kk * a0
    b1_o[...] = kk * a1
    kd0 = k * (1.0 + (a0 - 1.0) * ka_ref[...])
    kd1 = k * (1.0 + (a1 - 1.0) * ka_ref[...])
    kd0_o[...] = kd0
    kd1_o[...] = kd1
    kb = 0.5 * (kd0 + kd1)
    bon_o[...] = _seg_sum(r * kb * rk_ref[...], e_ref[...], et_ref[...]) * v


def _rwprep(z, seq, rw, lat_off, mu_rkv, mu_lat, wup, aup, gup, w0, a0, k_k, k_a, r_k, e64, e64t):
    t = z.shape[0]
    tm = 256
    nlat = 512
    nrkv = 3 * rw
    tb = tm // SUBLANES
    nblk8 = t // SUBLANES
    lat_blk = lat_off // nlat

    def cur(i):
        return (i, 0)

    def prv(i):
        return (jnp.maximum(i * tb - 1, 0), 0)

    def nxt(i):
        return (jnp.minimum((i + 1) * tb, nblk8 - 1), 0)

    def full(shape):
        return pl.BlockSpec(shape, lambda i: (0,) * len(shape))

    out = jax.ShapeDtypeStruct((t, rw), F32)
    ospec = pl.BlockSpec((tm, rw), lambda i: (i, 0))
    return pl.pallas_call(
        functools.partial(_rwprep_kernel, seq, tm, rw),
        grid=(t // tm,),
        in_specs=[pl.BlockSpec((tm, nrkv), cur),
                  pl.BlockSpec((SUBLANES, nrkv), prv),
                  pl.BlockSpec((SUBLANES, nrkv), nxt),
                  pl.BlockSpec((tm, nlat), lambda i: (i, lat_blk)),
                  pl.BlockSpec((SUBLANES, nlat), lambda i: (jnp.maximum(i * tb - 1, 0), lat_blk)),
                  pl.BlockSpec((SUBLANES, nlat), lambda i: (jnp.minimum((i + 1) * tb, nblk8 - 1), lat_blk)),
                  full((1, nrkv)), full((1, nlat)),
                  full(wup.shape), full(aup.shape), full(gup.shape),
                  full((1, 2 * rw)), full((1, 2 * rw)),
                  full((1, rw)), full((1, rw)), full((1, rw)),
                  full(e64.shape), full(e64t.shape)],
        out_specs=[ospec] * 11,
        out_shape=[out] * 11,
        compiler_params=_cparams(("parallel",)),
        name="rwprep",
    )(z, z, z, z, z, z, mu_rkv, mu_lat, wup, aup, gup, w0, a0, k_k, k_a, r_k, e64, e64t)


def _tri(n, rev):
    i = lax.broadcasted_iota(I32, (n, n), 0)
    j = lax.broadcasted_iota(I32, (n, n), 1)
    m = (j >= i) if rev else (j <= i)
    return jnp.where(m, 1.0, 0.0).astype(BF16)


def _rw_streams(streams):
    c = streams[0][0].shape[0]
    hd = RW_HEAD_DIM
    n2 = 2 * c
    ns = len(streams)
    revs = [s[7] for s in streams]
    lane = lax.broadcasted_iota(I32, (c, LANES), 1)
    head_a = lane < hd
    ri = lax.broadcasted_iota(I32, (n2, n2), 0)
    ci = lax.broadcasted_iota(I32, (n2, n2), 1)
    ti = ri % c
    si = ci % c
    same16 = (ri // 16) == (ci // 16)
    same32 = (ri // 32) == (ci // 32)
    mid32 = jnp.logical_and(same32, jnp.logical_not(same16))
    eye = jnp.where(ri == ci, 1.0, 0.0)
    strict = {False: si < ti, True: si > ti}
    incl = {False: si <= ti, True: si >= ti}
    tri = {rev: _tri(c, rev) for rev in set(revs)}
    ei = lax.broadcasted_iota(I32, (LANES, LANES), 0)
    ej = lax.broadcasted_iota(I32, (LANES, LANES), 1)
    eye_k = ei == ej

    def pair(x):
        return jnp.concatenate([jnp.where(head_a, x, 0.0), jnp.where(head_a, 0.0, x)], axis=0)

    cum = []
    for (r, v, kk, lw, b, kd, s_in, rev) in streams:
        hi, lo = _split2(lw)
        cs = jnp.dot(tri[rev], jnp.concatenate([hi, lo], axis=1), preferred_element_type=F32)
        cum.append(cs[:, :LANES] + cs[:, LANES:])
    ops = []
    for (r, v, kk, lw, b, kd, s_in, rev), cm in zip(streams, cum):
        tot = cm[0:1, :] if rev else cm[c - 1:c, :]
        g_inv = jnp.exp(-cm)
        g_tail = jnp.exp(tot - cm)
        ops.append(dict(p2=pair(-kk * jnp.exp(cm - lw)), r2=pair(r * jnp.exp(cm)),
                        bi2=pair(b * g_inv), ki2=pair(kd * g_inv), bt2=pair(b * g_tail),
                        kt2=pair(kd * g_tail), v2=pair(v), g_tot=jnp.exp(tot)))
    gm = [_dot_nt(jnp.concatenate([o['p2'], o['r2']], axis=0), jnp.concatenate([o['bi2'], o['ki2']], axis=0))
          for o in ops]
    a2 = [jnp.where(strict[rev], g[:n2, :n2], 0.0) for g, rev in zip(gm, revs)]
    b2 = [jnp.where(strict[rev], g[:n2, n2:], 0.0) for g, rev in zip(gm, revs)]
    ap2 = [jnp.where(incl[rev], g[n2:, :n2], 0.0) for g, rev in zip(gm, revs)]
    bp2 = [jnp.where(incl[rev], g[n2:, n2:], 0.0) for g, rev in zip(gm, revs)]
    bv = [_dot(x, o['v2']) for x, o in zip(b2, ops)]
    bpv = [_dot(x, o['v2']) for x, o in zip(bp2, ops)]

    x = [jnp.where(same16, a, 0.0) for a in a2]
    tinv = [eye + xi for xi in x]
    for _ in range(3):
        x = [_dot(xi, xi) for xi in x]
        tinv = [t + _dot(t, xi) for t, xi in zip(tinv, x)]
    for lvl in (mid32, jnp.logical_not(same32)):
        y = [_dot(t, jnp.where(lvl, a, 0.0)) for t, a in zip(tinv, a2)]
        tinv = [t + _dot(yi, t) for t, yi in zip(tinv, y)]

    wu = [_dot(t, jnp.concatenate([o['p2'], bvi], axis=1)) for t, o, bvi in zip(tinv, ops, bv)]
    qo = [_dot(a, w) for a, w in zip(ap2, wu)]
    m2 = [_dot_tn(w[:, :LANES], o['bt2']) + jnp.where(eye_k, jnp.broadcast_to(o['g_tot'], (LANES, LANES)), 0.0)
          for w, o in zip(wu, ops)]
    nn2 = [_dot_tn(w[:, LANES:], o['bt2']) + _dot_tn(o['v2'], o['kt2']) for w, o in zip(wu, ops)]

    nt = (((1,), (1,)), ((), ()))
    outs = []
    for i in range(ns):
        s_hi, s_lo = _split2(streams[i][6])
        q2b = (ops[i]['r2'] + qo[i][:, :LANES]).astype(BF16)
        oo = (lax.dot_general(q2b, s_hi, nt, preferred_element_type=F32)
              + lax.dot_general(q2b, s_lo, nt, preferred_element_type=F32)
              + qo[i][:, LANES:] + bpv[i])
        m2b = m2[i].astype(BF16)
        s_out = (jnp.dot(s_hi, m2b, preferred_element_type=F32)
                 + jnp.dot(s_lo, m2b, preferred_element_type=F32) + nn2[i])
        outs.append((oo[:c] + oo[c:], s_out))
    return outs


RW_PAIRS_PER_STEP = 8


def _rwscan_kernel(rf, vf, kkf, lwf, bf, kdf, rb, vb, kkb, lwb, bb, kdb, of_ref, ob_ref, sf, sb):
    @pl.when(pl.program_id(2) == 0)
    def _():
        sf[...] = jnp.zeros_like(sf)
        sb[...] = jnp.zeros_like(sb)

    streams = []
    for p in range(RW_PAIRS_PER_STEP):
        sl = slice(p * LANES, (p + 1) * LANES)
        streams.append((rf[:, sl], vf[:, sl], kkf[:, sl], lwf[:, sl], bf[:, sl], kdf[:, sl], sf[p], False))
        streams.append((rb[:, sl], vb[:, sl], kkb[:, sl], lwb[:, sl], bb[:, sl], kdb[:, sl], sb[p], True))
    outs = _rw_streams(streams)
    for p in range(RW_PAIRS_PER_STEP):
        sl = slice(p * LANES, (p + 1) * LANES)
        of_ref[:, sl], sf[p] = outs[2 * p]
        ob_ref[:, sl], sb[p] = outs[2 * p + 1]


def _rwscan(nb, seq, r, v, kk, lw0, lw1, b0, b1, kd0, kd1):
    t, rw = r.shape
    nc = seq // CHUNK
    wblk = RW_PAIRS_PER_STEP * LANES
    fw = pl.BlockSpec((CHUNK, wblk), lambda bi, hp, c: (bi * nc + c, hp))
    bw = pl.BlockSpec((CHUNK, wblk), lambda bi, hp, c: (bi * nc + nc - 1 - c, hp))
    out = jax.ShapeDtypeStruct((t, rw), F32)
    state = pltpu.VMEM((RW_PAIRS_PER_STEP, LANES, LANES), F32)
    return pl.pallas_call(
        _rwscan_kernel,
        grid=(nb, rw // wblk, nc),
        in_specs=[fw] * 6 + [bw] * 6,
        out_specs=[fw, bw],
        out_shape=[out, out],
        scratch_shapes=[state, state],
        compiler_params=_cparams(("parallel", "parallel", "arbitrary")),
        name="rwscan",
    )(r, v, kk, lw0, b0, kd0, r, v, kk, lw1, b1, kd1)


def _hg_streams(streams):
    c, dk = streams[0][0].shape
    revs = [s[5] for s in streams]
    tri = {rev: _tri(c, rev) for rev in set(revs)}
    row = lax.broadcasted_iota(I32, (c, dk), 0)
    ri = lax.broadcasted_iota(I32, (c, c), 0)
    ci = lax.broadcasted_iota(I32, (c, c), 1)

    cum = []
    for (q, k, v, lf, st, rev) in streams:
        hi, mid, lo = _split3(lf)
        cs = jnp.dot(tri[rev], jnp.concatenate([hi, mid, lo], axis=1), preferred_element_type=F32)
        cum.append(cs[:, :dk] + cs[:, dk:2 * dk] + cs[:, 2 * dk:])
    scores = [jnp.where(ri == ci, jnp.sum(s[0] * s[1], axis=1, keepdims=True), 0.0) for s in streams]
    sub = row % SUBLANES
    nt = (((1,), (1,)), ((), ()))

    def sub_bcast(x, idx):
        x3 = x.reshape(c // SUBLANES, SUBLANES, dk)
        return jnp.broadcast_to(x3[:, idx:idx + 1, :], x3.shape).reshape(c, dk)

    h = c // 2
    while h >= 1:
        blk = 2 * h
        upper = (row % blk) >= h
        same_blk = (ri // blk) == (ci // blk)
        r_up = (ri % blk) >= h
        c_up = (ci % blk) >= h
        q_rows = {False: upper, True: jnp.logical_not(upper)}
        pmask = {False: jnp.logical_and(same_blk, jnp.logical_and(r_up, jnp.logical_not(c_up))),
                 True: jnp.logical_and(same_blk, jnp.logical_and(jnp.logical_not(r_up), c_up))}
        sl = []
        for i, (q, k, v, lf, st, rev) in enumerate(streams):
            cm = cum[i]
            off = h if rev else h - 1
            if h >= SUBLANES:
                pieces = [jnp.broadcast_to(cm[m0 + off:m0 + off + 1, :], (blk, dk)) for m0 in range(0, c, blk)]
                ref = jnp.concatenate(pieces, axis=0) if len(pieces) > 1 else pieces[0]
            elif blk == SUBLANES:
                ref = sub_bcast(cm, off)
            elif 2 * blk == SUBLANES:
                ref = jnp.where(sub < blk, sub_bcast(cm, off), sub_bcast(cm, off + blk))
            else:
                ref = jnp.where(q_rows[rev], pltpu.roll(cm, c - 1 if rev else 1, axis=0), cm)
            dlt = cm - ref
            e = jnp.minimum(jnp.where(q_rows[rev], dlt, -dlt), 0.0)
            x = (jnp.where(q_rows[rev], q, k) * jnp.exp(e)).astype(BF16)
            sl.append(lax.dot_general(x, x, nt, preferred_element_type=F32))
        scores = [sc + jnp.where(pmask[rev], x, 0.0) for sc, x, rev in zip(scores, sl, revs)]
        h //= 2

    outs = []
    for (q, k, v, lf, st, rev), cm, sc in zip(streams, cum, scores):
        tot = cm[0:1, :] if rev else cm[c - 1:c, :]
        o = _dot(sc, v) + _dot_nt(q * jnp.exp(cm), st)
        st_new = st * jnp.exp(tot) + _dot_tn(v, k * jnp.exp(tot - cm))
        outs.append((o, st_new))
    return outs


HG_HEADS_PER_STEP = 4


def _hgscan_kernel(dk, qf, fff, i_f, qb, ffb, i_b, lb_ref, of_ref, ob_ref, sf, sb):
    @pl.when(pl.program_id(2) == 0)
    def _():
        sf[...] = jnp.zeros_like(sf)
        sb[...] = jnp.zeros_like(sb)

    streams = []
    for p in range(HG_HEADS_PER_STEP):
        sl = slice(p * dk, (p + 1) * dk)
        for (q_ref, ff_ref, i_ref, st_ref, d) in ((qf, fff, i_f, sf, 0), (qb, ffb, i_b, sb, 1)):
            lbv = lb_ref[d:d + 1, sl]
            f = lbv + (1.0 - lbv) * _sigmoid(ff_ref[:, sl])
            streams.append((_silu(q_ref[:, sl]), 1.0 - f, i_ref[:, sl], jnp.log(f), st_ref[p], d == 1))
    outs = _hg_streams(streams)
    for p in range(HG_HEADS_PER_STEP):
        sl = slice(p * dk, (p + 1) * dk)
        of_ref[:, sl], sf[p] = outs[2 * p]
        ob_ref[:, sl], sb[p] = outs[2 * p + 1]


def _hgscan(nb, seq, z, lb, hg_off, hw):
    t = z.shape[0]
    nc = seq // CHUNK
    dk = hw // HG_HEADS
    wblk = HG_HEADS_PER_STEP * dk
    base = hg_off // wblk
    nh = hw // wblk

    def fw(comp):
        return pl.BlockSpec((CHUNK, wblk), lambda bi, h, c: (bi * nc + c, base + comp * nh + h))

    def bw(comp):
        return pl.BlockSpec((CHUNK, wblk), lambda bi, h, c: (bi * nc + nc - 1 - c, base + comp * nh + h))

    out = jax.ShapeDtypeStruct((t, hw), F32)
    state = pltpu.VMEM((HG_HEADS_PER_STEP, dk, dk), F32)
    return pl.pallas_call(
        functools.partial(_hgscan_kernel, dk),
        grid=(nb, nh, nc),
        in_specs=[fw(0), fw(1), fw(3), bw(0), bw(2), bw(3),
                  pl.BlockSpec((2, wblk), lambda bi, h, c: (0, h))],
        out_specs=[pl.BlockSpec((CHUNK, wblk), lambda bi, h, c: (bi * nc + c, h)),
                   pl.BlockSpec((CHUNK, wblk), lambda bi, h, c: (bi * nc + nc - 1 - c, h))],
        out_shape=[out, out],
        scratch_shapes=[state, state],
        compiler_params=_cparams(("parallel", "parallel", "arbitrary")),
        name="hgscan",
    )(z, z, z, z, z, z, lb)


def _blockdiag2(w):
    _, r, n = w.shape
    z = jnp.zeros((r, n), w.dtype)
    return jnp.concatenate([jnp.concatenate([w[0], z], axis=1), jnp.concatenate([z, w[1]], axis=1)], axis=0)


def _indicator(width, seg):
    e = (jnp.arange(width)[:, None] // seg == jnp.arange(width // seg)[None, :]).astype(BF16)
    return e, e.T


def _layer_consts(w_in, rw_mu, rw_w0, rw_w_up, rw_a0, rw_a_up, rw_g_up, rw_k_k, rw_k_a, rw_r_k,
                  hg_lb_gamma):
    rw = rw_k_k.shape[-1]
    d = w_in.shape[1]
    nlat = 2 * W_LORA + 2 * A_LORA + G_LORA
    w = w_in[0]
    rkv = 3 * rw
    hg_cols = w.shape[1] - rkv - nlat
    pad = 512 - nlat
    w_perm = jnp.concatenate([w[:, :rkv], w[:, rkv + nlat:], w[:, rkv:rkv + nlat],
                              jnp.zeros((d, pad), w.dtype)], axis=1).astype(BF16)
    mu = rw_mu[0]
    lower = jnp.cumsum(jax.nn.softmax(hg_lb_gamma.astype(F32), axis=0), axis=0)[0]
    hw = lower.shape[-1]
    e64, e64t = _indicator(rw, RW_HEAD_DIM)
    return dict(
        rw=rw, hw=hw, hg_off=rkv, lat_off=rkv + hg_cols, w_in=w_perm,
        mu_rkv=mu[:rkv].reshape(1, rkv),
        mu_lat=jnp.pad(mu[rkv:rkv + nlat], (0, pad)).reshape(1, 512),
        wup=_blockdiag2(rw_w_up[0]).astype(BF16), aup=_blockdiag2(rw_a_up[0]).astype(BF16),
        gup=rw_g_up[0].astype(BF16),
        w0=rw_w0[0].reshape(1, 2 * rw), a0=rw_a0[0].reshape(1, 2 * rw),
        k_k=rw_k_k[0].reshape(1, rw), k_a=rw_k_a[0].reshape(1, rw), r_k=rw_r_k[0].reshape(1, rw),
        e64=e64, e64t=e64t,
        lb=lower,
    )


def _mixer(x2, nb, seq, mod, norm_pre_mix, wc):
    z = _inproj(x2, mod, norm_pre_mix, wc['w_in'], seq)
    (r, v, kk, g, bonus, lw0, lw1, b0, b1, kd0, kd1) = _rwprep(
        z, seq, wc['rw'], wc['lat_off'], wc['mu_rkv'], wc['mu_lat'], wc['wup'], wc['aup'], wc['gup'],
        wc['w0'], wc['a0'], wc['k_k'], wc['k_a'], wc['r_k'], wc['e64'], wc['e64t'])
    rw_of, rw_ob = _rwscan(nb, seq, r, v, kk, lw0, lw1, b0, b1, kd0, kd1)
    hg_of, hg_ob = _hgscan(nb, seq, z, wc['lb'], wc['hg_off'], wc['hw'])
    return dict(z=z, r=r, v=v, kk=kk, g=g, bonus=bonus, lw0=lw0, rw_of=rw_of, rw_ob=rw_ob,
                hg_of=hg_of, hg_ob=hg_ob)


def _outproj_kernel(rw, x_ref, rf_ref, rb_ref, bon_ref, g_ref, hf_ref, hb_ref, hgg_ref, mod_ref,
                    lnw_ref, lnb_ref, hnw_ref, npm_ref, npf_ref, wout_ref, wrh_ref, wrl_ref,
                    e64_ref, e64t_ref, e128_ref, e128t_ref,
                    x1_ref, h2_ref, lg_ref):
    m6 = mod_ref[0]
    o = rf_ref[...] + rb_ref[...]
    mean = _seg_sum(o, e64_ref[...], e64t_ref[...]) * (1.0 / RW_HEAD_DIM)
    dlt = o - mean
    var = _seg_sum(dlt * dlt, e64_ref[...], e64t_ref[...]) * (1.0 / RW_HEAD_DIM)
    o_rw = (dlt * lax.rsqrt(var + RW_GN_EPS) * lnw_ref[...] + lnb_ref[...] + bon_ref[...]) * g_ref[...]
    oh = hf_ref[...] + hb_ref[...]
    hd = oh.shape[1] // HG_HEADS
    ms = _seg_sum(oh * oh, e128_ref[...], e128t_ref[...]) * (1.0 / hd)
    o_hg = oh * lax.rsqrt(ms + NORM_EPS) * hnw_ref[...] * _silu(hgg_ref[...])
    m = _dot(o_rw, wout_ref[0:rw, :]) + _dot(o_hg, wout_ref[rw:, :])
    x1 = x_ref[...] + m6[2:3] * _rms_rows(m, npm_ref[...])
    x1_ref[...] = x1
    h2 = _rms_rows(x1, npf_ref[...]) * (1.0 + m6[4:5]) + m6[3:4]
    hi, lo = _split2(h2)
    _pack_rows(hi, h2_ref)
    nt = (((1,), (1,)), ((), ()))
    lg_ref[...] = (lax.dot_general(wrh_ref[...], hi, nt, preferred_element_type=F32)
                   + lax.dot_general(wrh_ref[...], lo, nt, preferred_element_type=F32)
                   + lax.dot_general(wrl_ref[...], hi, nt, preferred_element_type=F32))


def _outproj(x2, seq, mx, z, mod, wc, oc):
    t, d = x2.shape
    rw, hw = wc['rw'], wc['hw']
    tm = 256
    gblk = (wc['hg_off'] + 4 * hw) // hw

    def row(w):
        return pl.BlockSpec((tm, w), lambda i: (i, 0))

    def full(a):
        return pl.BlockSpec(a.shape, lambda i: (0,) * a.ndim)

    consts = [oc['ln_w'], oc['ln_b'], oc['hg_norm_w'], oc['npm'], oc['npf'], oc['w_out'], oc['wr_hi'],
              oc['wr_lo'], wc['e64'], wc['e64t'], oc['e128'], oc['e128t']]
    return pl.pallas_call(
        functools.partial(_outproj_kernel, rw),
        grid=(t // tm,),
        in_specs=[row(d), row(rw), row(rw), row(rw), row(rw), row(hw), row(hw),
                  pl.BlockSpec((tm, hw), lambda i: (i, gblk)),
                  pl.BlockSpec((1, 6, d), lambda i: ((i * tm) // seq, 0, 0))] + [full(a) for a in consts],
        out_specs=[row(d), pl.BlockSpec((tm * (d // PACK_W), LANES), lambda i: (i, 0)),
                   pl.BlockSpec((N_EXPERTS, tm), lambda i: (0, i))],
        out_shape=[jax.ShapeDtypeStruct((t, d), F32), jax.ShapeDtypeStruct((t * (d // PACK_W), LANES), jnp.uint32),
                   jax.ShapeDtypeStruct((N_EXPERTS, t), F32)],
        compiler_params=_cparams(("parallel",)),
        name="outproj",
    )(x2, mx['rw_of'], mx['rw_ob'], mx['bonus'], mx['g'], mx['hg_of'], mx['hg_ob'], z, mod, *consts)


ROUTE_TILE = 512


def _route_kernel(lg_ref, bias_ref, ut_ref, eidx_ref, wsel_ref, rank_ref, cnt_ref, carry):
    @pl.when(pl.program_id(0) == 0)
    def _():
        carry[...] = jnp.zeros_like(carry)

    ne, tt = lg_ref.shape
    gsz = ne // N_GROUPS
    neg = -jnp.inf
    s = _sigmoid(lg_ref[...])
    biased = s + bias_ref[...]
    io_g = lax.broadcasted_iota(I32, (gsz, tt), 0)
    gs_rows = []
    for gi in range(N_GROUPS):
        blk = biased[gi * gsz:(gi + 1) * gsz, :]
        m1 = jnp.max(blk, axis=0, keepdims=True)
        first = jnp.min(jnp.where(blk == m1, io_g, gsz), axis=0, keepdims=True)
        m2 = jnp.max(jnp.where(io_g == first, neg, blk), axis=0, keepdims=True)
        gs_rows.append(m1 + m2)
    gs = jnp.concatenate(gs_rows, axis=0)
    io_n = lax.broadcasted_iota(I32, (N_GROUPS, tt), 0)
    selg = jnp.zeros((N_GROUPS, tt), jnp.bool_)
    for _ in range(TOPK_GROUPS):
        m = jnp.max(gs, axis=0, keepdims=True)
        first = jnp.min(jnp.where(gs == m, io_n, N_GROUPS), axis=0, keepdims=True)
        pick = io_n == first
        selg = jnp.logical_or(selg, pick)
        gs = jnp.where(pick, neg, gs)
    emask = jnp.concatenate([jnp.broadcast_to(selg[gi:gi + 1, :], (gsz, tt)) for gi in range(N_GROUPS)],
                            axis=0)
    mb = jnp.where(emask, biased, neg)
    io_e = lax.broadcasted_iota(I32, (ne, tt), 0)
    sel = jnp.zeros((ne, tt), jnp.bool_)
    picks, idxs, ws = [], [], []
    for _ in range(TOP_K):
        m = jnp.max(mb, axis=0, keepdims=True)
        first = jnp.min(jnp.where(mb == m, io_e, ne), axis=0, keepdims=True)
        pick = io_e == first
        picks.append(pick)
        idxs.append(first)
        ws.append(jnp.sum(jnp.where(pick, s, 0.0), axis=0, keepdims=True))
        sel = jnp.logical_or(sel, pick)
        mb = jnp.where(pick, neg, mb)
    wsum = ws[0]
    for w in ws[1:]:
        wsum = wsum + w
    pos = jnp.dot(jnp.where(sel, 1.0, 0.0).astype(BF16), ut_ref[...], preferred_element_type=F32) + carry[...]
    ranks = [jnp.sum(jnp.where(p, pos, 0.0), axis=0, keepdims=True).astype(I32) for p in picks]
    carry[...] = carry[...] + jnp.sum(jnp.where(sel, 1.0, 0.0), axis=1, keepdims=True)
    zi = jnp.zeros((SUBLANES - TOP_K, tt), I32)
    eidx_ref[...] = jnp.concatenate(idxs + [zi], axis=0)
    rank_ref[...] = jnp.concatenate(ranks + [zi], axis=0)
    wsel_ref[...] = jnp.concatenate([w / wsum * ROUTED_SCALE for w in ws] + [zi.astype(F32)], axis=0)
    cnt_ref[...] = jnp.broadcast_to(carry[...], cnt_ref.shape).astype(I32)


def _route(logits_t, e_bias):
    ne, t = logits_t.shape
    tt = ROUTE_TILE
    ut = (jnp.arange(tt)[:, None] < jnp.arange(tt)[None, :]).astype(BF16)
    tok = pl.BlockSpec((SUBLANES, tt), lambda i: (0, i))
    return pl.pallas_call(
        _route_kernel,
        grid=(t // tt,),
        in_specs=[pl.BlockSpec((ne, tt), lambda i: (0, i)),
                  pl.BlockSpec((ne, 1), lambda i: (0, 0)),
                  pl.BlockSpec((tt, tt), lambda i: (0, 0))],
        out_specs=[tok, tok, tok, pl.BlockSpec((ne, LANES), lambda i: (0, 0))],
        out_shape=[jax.ShapeDtypeStruct((SUBLANES, t), I32), jax.ShapeDtypeStruct((SUBLANES, t), F32),
                   jax.ShapeDtypeStruct((SUBLANES, t), I32), jax.ShapeDtypeStruct((ne, LANES), I32)],
        scratch_shapes=[pltpu.VMEM((ne, 1), F32)],
        compiler_params=_cparams(("arbitrary",)),
        name="route",
    )(logits_t, e_bias.reshape(ne, 1), ut)


DISPATCH_TILE = 128
PLAN_TILE = 1024


def _plan_kernel(eidx_ref, rank_ref, ps_ref, dest_ref):
    ne = ps_ref.shape[0]
    tp = eidx_ref.shape[1]
    io_e = lax.broadcasted_iota(I32, (ne, tp), 0)
    ps = ps_ref[...]
    rows = []
    for j in range(TOP_K):
        start = jnp.sum(jnp.where(io_e == eidx_ref[j:j + 1, :], ps, 0.0), axis=0, keepdims=True)
        rows.append(start.astype(I32) + rank_ref[j:j + 1, :])
    dest = jnp.concatenate(rows + [jnp.zeros((SUBLANES - TOP_K, tp), I32)], axis=0)
    for i in range(tp // DISPATCH_TILE):
        dest_ref[i] = dest[:, i * DISPATCH_TILE:(i + 1) * DISPATCH_TILE]


def _plan(eidx, rank, pad_start):
    t = eidx.shape[1]
    tp = PLAN_TILE
    ne = pad_start.shape[0]
    tok = pl.BlockSpec((SUBLANES, tp), lambda i: (0, i))
    ntile = tp // DISPATCH_TILE
    return pl.pallas_call(
        _plan_kernel,
        grid=(t // tp,),
        in_specs=[tok, tok, pl.BlockSpec((ne, 1), lambda i: (0, 0))],
        out_specs=pl.BlockSpec((ntile, SUBLANES, DISPATCH_TILE), lambda i: (i, 0, 0)),
        out_shape=jax.ShapeDtypeStruct((t // DISPATCH_TILE, SUBLANES, DISPATCH_TILE), I32),
        compiler_params=_cparams(("parallel",)),
        name="plan",
    )(eidx, rank, pad_start.astype(F32).reshape(ne, 1))


def _dispatch_kernel(tt, nsteps, zs_ref, zc_ref, dest_ref, h_ref, xs_ref, idx_smem, zero_tile, isem, sem, zsem):
    i = pl.program_id(0)
    n = tt * TOP_K
    ne = zs_ref.shape[0]
    icp = pltpu.make_async_copy(dest_ref.at[pl.ds(i * tt * SUBLANES, n)], idx_smem, isem)
    icp.start()
    zero_tile[...] = jnp.zeros_like(zero_tile)
    icp.wait()

    def row_copy(k):
        return pltpu.make_async_copy(h_ref.at[k % tt], xs_ref.at[idx_smem[k]], sem)

    def drain(k, c):
        row_copy(k).wait()
        return c

    for k in range(n):
        row_copy(k).start(priority=k % 2)

    for q in range(pl.cdiv(ne, nsteps)):
        e = i + q * nsteps
        ec = jnp.minimum(e, ne - 1)
        start = zs_ref[ec]
        count = jnp.where(e < ne, zc_ref[ec], 0)

        def zero_copy(r):
            return pltpu.make_async_copy(zero_tile, xs_ref.at[start + r], zsem)

        def z_issue(r, c):
            zero_copy(r).start()
            return c

        def z_drain(r, c):
            zero_copy(r).wait()
            return c

        lax.fori_loop(0, count, z_issue, 0)
        lax.fori_loop(0, count, z_drain, 0)
    lax.fori_loop(0, n, drain, 0, unroll=8)


def _dispatch(dest_flat, h2p, p_rows, zero_start, zero_count):
    t = h2p.shape[0]
    tt = DISPATCH_TILE
    anyspec = pl.BlockSpec(memory_space=pl.ANY)
    grid_spec = pltpu.PrefetchScalarGridSpec(
        num_scalar_prefetch=2,
        grid=(t // tt,),
        in_specs=[anyspec, pl.BlockSpec((tt,) + h2p.shape[1:], lambda i, zs, zc: (i, 0, 0))],
        out_specs=anyspec,
        scratch_shapes=[pltpu.SMEM((tt * TOP_K,), I32), pltpu.VMEM(h2p.shape[1:], h2p.dtype),
                        pltpu.SemaphoreType.DMA(()), pltpu.SemaphoreType.DMA(()), pltpu.SemaphoreType.DMA(())],
    )
    return pl.pallas_call(
        functools.partial(_dispatch_kernel, tt, t // tt),
        grid_spec=grid_spec,
        out_shape=jax.ShapeDtypeStruct((p_rows,) + h2p.shape[1:], h2p.dtype),
        compiler_params=_cparams(("arbitrary",)),
        name="dispatch",
    )(zero_start, zero_count, dest_flat, h2p)


def _experts_kernel(be_ref, nu_ref, x_ref, wg_ref, wu_ref, wd_ref, y_ref, wg_s, wu_s, wd_s):
    b = pl.program_id(0)
    live = b < nu_ref[0]
    new_expert = jnp.logical_or(b == 0, be_ref[b] != be_ref[jnp.maximum(b - 1, 0)])

    @pl.when(jnp.logical_and(live, new_expert))
    def _():
        wg_s[...] = wg_ref[0].astype(BF16)
        wu_s[...] = wu_ref[0].astype(BF16)
        wd_s[...] = wd_ref[0].astype(BF16)

    @pl.when(live)
    def _():
        x = _unpack_rows(x_ref, wg_ref.shape[1] // PACK_W)
        gate = jnp.dot(x, wg_s[...], preferred_element_type=F32)
        up = jnp.dot(x, wu_s[...], preferred_element_type=F32)
        _pack_rows(_dot(_silu(gate) * up, wd_s[...]).astype(BF16), y_ref)

    @pl.when(b >= nu_ref[0])
    def _():
        y_ref[...] = jnp.zeros_like(y_ref)


def _experts(block_e, n_used, xs, wg, wu, wd):
    d, de = wg.shape[1], wg.shape[2]
    ns = d // PACK_W
    p_rows = xs.shape[0] // ns
    nblk = p_rows // EXPERT_BLOCK

    def live(b, nu):
        return jnp.minimum(b, jnp.maximum(nu[0] - 1, 0))

    grid_spec = pltpu.PrefetchScalarGridSpec(
        num_scalar_prefetch=2,
        grid=(nblk,),
        in_specs=[pl.BlockSpec((EXPERT_BLOCK * ns, LANES), lambda b, be, nu: (live(b, nu), 0)),
                  pl.BlockSpec((1, d, de), lambda b, be, nu: (be[live(b, nu)], 0, 0)),
                  pl.BlockSpec((1, d, de), lambda b, be, nu: (be[live(b, nu)], 0, 0)),
                  pl.BlockSpec((1, de, d), lambda b, be, nu: (be[live(b, nu)], 0, 0))],
        out_specs=pl.BlockSpec((EXPERT_BLOCK * ns, LANES), lambda b, be, nu: (b, 0)),
        scratch_shapes=[pltpu.VMEM((d, de), BF16), pltpu.VMEM((d, de), BF16), pltpu.VMEM((de, d), BF16)],
    )
    return pl.pallas_call(
        _experts_kernel,
        grid_spec=grid_spec,
        out_shape=jax.ShapeDtypeStruct((p_rows * ns, LANES), jnp.uint32),
        compiler_params=_cparams(("arbitrary",)),
        name="experts",
    )(block_e, n_used, xs, wg, wu, wd)


def _combine_kernel(tt, nsteps, dest_ref, y_ref, x1_ref, h2_ref, w_ref, mod_ref, npf_ref, sg_ref, su_ref, sd_ref,
                    o_ref, ybuf, idx_smem, isem, sems):
    i = pl.program_id(0)
    n = tt * TOP_K
    ns = sg_ref.shape[0] // PACK_W

    def row_copy(slot, k):
        dst = ybuf.at[pl.ds(pl.multiple_of((slot * n + k) * ns, ns), ns)]
        return pltpu.make_async_copy(y_ref.at[idx_smem[slot * n + k]], dst, sems.at[slot])

    def issue(step, slot):
        icp = pltpu.make_async_copy(dest_ref.at[pl.ds(step * tt * SUBLANES, n)],
                                    idx_smem.at[pl.ds(slot * n, n)], isem)
        icp.start()
        icp.wait()
        for k in range(n):
            row_copy(slot, k).start(priority=k % 2)

    def drain(slot):
        def body(k, c):
            row_copy(slot, k).wait()
            return c

        lax.fori_loop(0, n, body, 0, unroll=8)

    @pl.when(i == 0)
    def _():
        issue(0, 0)

    other = (i + 1) % 2
    issue(jnp.minimum(i + 1, nsteps - 1), other)

    h2 = _unpack_rows(h2_ref, ns)
    gate = jnp.dot(h2, sg_ref[...], preferred_element_type=F32)
    up = jnp.dot(h2, su_ref[...], preferred_element_type=F32)
    shared = _dot(_silu(gate) * up, sd_ref[...])
    w = w_ref[...]
    m6 = mod_ref[0]

    def finish(slot):
        drain(slot)
        acc = [None] * (2 * ns)
        for j in range(TOP_K):
            wj = w[:, j:j + 1]
            for s in range(ns):
                word = ybuf[pl.ds((slot * n + j * tt) * ns + s, tt, stride=ns), :]
                parts = (lax.bitcast_convert_type(word & jnp.uint32(0xFFFF0000), F32),
                         lax.bitcast_convert_type(word << 16, F32))
                for h, part in enumerate(parts):
                    term = wj * part
                    acc[2 * s + h] = term if acc[2 * s + h] is None else acc[2 * s + h] + term
        routed = jnp.concatenate(acc, axis=1)
        o_ref[...] = x1_ref[...] + m6[5:6] * _rms_rows(routed + shared, npf_ref[...])

    for slot in (0, 1):
        @pl.when(i % 2 == slot)
        def _():
            finish(slot)

    @pl.when(i == nsteps - 1)
    def _():
        drain(other)


def _combine(dest_flat, y3, x1, h2, wsel_t, mod, seq, npf, sg, su, sd):
    t, d = x1.shape
    tt = DISPATCH_TILE
    ns = d // PACK_W
    n = tt * TOP_K
    anyspec = pl.BlockSpec(memory_space=pl.ANY)

    def full(a):
        return pl.BlockSpec(a.shape, lambda i: (0,) * a.ndim)

    rows = pltpu.VMEM((2 * n * ns, LANES), jnp.uint32)
    return pl.pallas_call(
        functools.partial(_combine_kernel, tt, t // tt),
        grid=(t // tt,),
        in_specs=[anyspec, anyspec,
                  pl.BlockSpec((tt, d), lambda i: (i, 0)),
                  pl.BlockSpec((tt * ns, LANES), lambda i: (i, 0)),
                  pl.BlockSpec((tt, SUBLANES), lambda i: (i, 0)),
                  pl.BlockSpec((1, 6, d), lambda i: ((i * tt) // seq, 0, 0)),
                  full(npf), full(sg), full(su), full(sd)],
        out_specs=pl.BlockSpec((tt, d), lambda i: (i, 0)),
        out_shape=jax.ShapeDtypeStruct((t, d), F32),
        scratch_shapes=[rows, pltpu.SMEM((2 * n,), I32), pltpu.SemaphoreType.DMA(()),
                        pltpu.SemaphoreType.DMA((2,))],
        compiler_params=_cparams(("arbitrary",)),
        name="combine",
    )(dest_flat, y3, x1, h2, wsel_t, mod, npf, sg, su, sd)


def _moe_plan(eidx, rank, cnt, t):
    counts = cnt[:, 0]
    padded = (counts + EXPERT_BLOCK - 1) // EXPERT_BLOCK * EXPERT_BLOCK
    pad_end = jnp.cumsum(padded)
    pad_start = pad_end - padded
    n_blocks = (t * TOP_K + EXPERT_BLOCK - 1) // EXPERT_BLOCK + N_EXPERTS
    first_row = jnp.arange(n_blocks, dtype=I32) * EXPERT_BLOCK
    block_e = jnp.minimum(jnp.sum((pad_end[None, :] <= first_row[:, None]).astype(I32), axis=1),
                          N_EXPERTS - 1).astype(I32)
    n_used = (pad_end[-1:] // EXPERT_BLOCK).astype(I32)
    dest_flat = _plan(eidx, rank, pad_start).reshape(-1)
    zero_rows = ((pad_start + counts).astype(I32), (padded - counts).astype(I32))
    return block_e, n_used, dest_flat, n_blocks * EXPERT_BLOCK, zero_rows


def _trunk(x, mod, wc, oc, ec, norm_pre_mix):
    nb, seq, d = x.shape
    t = nb * seq
    x2 = x.reshape(t, d)
    mx = _mixer(x2, nb, seq, mod, norm_pre_mix, wc)
    x1, h2, logits_t = _outproj(x2, seq, mx, mx['z'], mod, wc, oc)
    eidx, wsel, rank, cnt = _route(logits_t, ec['e_bias'])
    block_e, n_used, dest_flat, p_rows, zero_rows = _moe_plan(eidx, rank, cnt, t)
    ns = d // PACK_W
    xs = _dispatch(dest_flat, h2.reshape(t, ns, LANES), p_rows, *zero_rows)
    y = _experts(block_e, n_used, xs.reshape(p_rows * ns, LANES), ec['wg'], ec['wu'], ec['wd'])
    out = _combine(dest_flat, y.reshape(p_rows, ns, LANES), x1, h2, wsel.T, mod, seq, oc['npo'],
                   ec['sg'], ec['su'], ec['sd'])
    return out.reshape(nb, seq, d)


def kernel(x_prompt, x_sample, c_prompt, c_sample, w_ada, b_ada, norm_pre_mix, norm_post_mix, norm_pre_ffn, norm_post_ffn, w_in, rw_mu, rw_w0, rw_w_up, rw_a0, rw_a_up, rw_g_up, rw_k_k, rw_k_a, rw_r_k, rw_ln_w, rw_ln_b, hg_lb_gamma, hg_norm_w, w_out, w_router, e_bias, w_exp_gate, w_exp_up, w_exp_down, w_sh_gate, w_sh_up, w_sh_down):
    d = x_prompt.shape[-1]
    wc = _layer_consts(w_in, rw_mu, rw_w0, rw_w_up, rw_a0, rw_a_up, rw_g_up, rw_k_k, rw_k_a, rw_r_k,
                       hg_lb_gamma)
    rw, hw = wc['rw'], wc['hw']
    e128, e128t = _indicator(hw, hw // HG_HEADS)
    wr_hi, wr_lo = _split2(w_router[0].T)
    oc = dict(ln_w=rw_ln_w[0].reshape(1, rw), ln_b=rw_ln_b[0].reshape(1, rw),
              hg_norm_w=hg_norm_w[0].reshape(1, hw), npm=norm_post_mix[0].reshape(1, d),
              npf=norm_pre_ffn[0].reshape(1, d), npo=norm_post_ffn[0].reshape(1, d),
              w_out=w_out[0].astype(BF16), wr_hi=wr_hi, wr_lo=wr_lo, e128=e128, e128t=e128t)
    ec = dict(e_bias=e_bias[0], wg=w_exp_gate[0], wu=w_exp_up[0], wd=w_exp_down[0], sg=w_sh_gate[0].astype(BF16), su=w_sh_up[0].astype(BF16),
              sd=w_sh_down[0].astype(BF16))
    nbp = c_prompt.shape[0]
    mod = _ada(jnp.concatenate([c_prompt, c_sample], axis=0), w_ada[0], b_ada[0]).reshape(-1, 6, d)
    y_prompt = _trunk(x_prompt, mod[:nbp], wc, oc, ec, norm_pre_mix[0])
    y_sample = _trunk(x_sample, mod[nbp:], wc, oc, ec, norm_pre_mix[0])
    return (y_prompt, y_sample)
```

```python
import functools
import math

import jax
import jax.numpy as jnp
from jax import lax
from jax.experimental import pallas as pl
from jax.experimental.pallas import tpu as pltpu

F32 = jnp.float32
BF16 = jnp.bfloat16
I32 = jnp.int32

RW_HEAD_DIM = 64
W_LORA = 64
A_LORA = 64
G_LORA = 128
RW_GN_EPS = 64e-5
HG_HEADS = 8
N_EXPERTS = 64
TOP_K = 6
N_GROUPS = 8
TOPK_GROUPS = 4
ROUTED_SCALE = 2.5
EXPERT_BLOCK = 256
NORM_EPS = 1e-6

LANES = 128
SUBLANES = 8
VMEM_LIMIT = 56 * 1024 * 1024

CHUNK = 64


def _cparams(sem, vmem=VMEM_LIMIT):
    return pltpu.CompilerParams(dimension_semantics=sem, vmem_limit_bytes=vmem)


def _sigmoid(x):
    return 1.0 / (1.0 + jnp.exp(-x))


def _silu(x):
    return x * _sigmoid(x)


def _dot(a, b):
    return jnp.dot(a.astype(BF16), b.astype(BF16), preferred_element_type=F32)


def _dot_nt(a, b):
    return lax.dot_general(a.astype(BF16), b.astype(BF16), (((1,), (1,)), ((), ())),
                           preferred_element_type=F32)


def _dot_tn(a, b):
    return lax.dot_general(a.astype(BF16), b.astype(BF16), (((0,), (0,)), ((), ())),
                           preferred_element_type=F32)


def _split2(x):
    hi = x.astype(BF16)
    lo = (x - hi.astype(F32)).astype(BF16)
    return hi, lo


def _split3(x):
    hi = x.astype(BF16)
    r1 = x - hi.astype(F32)
    mid = r1.astype(BF16)
    lo = (r1 - mid.astype(F32)).astype(BF16)
    return hi, mid, lo


def _seg_sum(x, e, et):
    hi, lo = _split2(x)
    s = jnp.dot(hi, e, preferred_element_type=F32) + jnp.dot(lo, e, preferred_element_type=F32)
    shi, slo = _split2(s)
    return jnp.dot(shi, et, preferred_element_type=F32) + jnp.dot(slo, et, preferred_element_type=F32)


PACK_W = 2 * LANES


def _pack_rows(x_bf, out_ref):
    n, d = x_bf.shape
    ns = d // PACK_W
    for s in range(ns):
        a = lax.bitcast_convert_type(x_bf[:, s * PACK_W:s * PACK_W + LANES].astype(F32), jnp.uint32)
        b = lax.bitcast_convert_type(x_bf[:, s * PACK_W + LANES:(s + 1) * PACK_W].astype(F32), jnp.uint32)
        out_ref[pl.ds(s, n, stride=ns), :] = a | (b >> 16)


def _unpack_rows(x_ref, ns):
    n = x_ref.shape[0] // ns
    parts = []
    for s in range(ns):
        w = x_ref[pl.ds(s, n, stride=ns), :]
        parts.append(lax.bitcast_convert_type(w & jnp.uint32(0xFFFF0000), F32).astype(BF16))
        parts.append(lax.bitcast_convert_type(w << 16, F32).astype(BF16))
    return jnp.concatenate(parts, axis=1)


def _rms_rows(x, g):
    return x * lax.rsqrt(jnp.mean(x * x, axis=-1, keepdims=True) + NORM_EPS) * g


def _ada_kernel(c_ref, w_ref, b_ref, o_ref):
    c = c_ref[...]
    o_ref[...] = _dot(_silu(c), w_ref[...]) + b_ref[...]


def _ada(c, w_ada, b_ada):
    nb, d = c.shape
    n = w_ada.shape[1]
    tn = 512
    return pl.pallas_call(
        _ada_kernel,
        grid=(n // tn,),
        in_specs=[pl.BlockSpec((nb, d), lambda j: (0, 0)),
                  pl.BlockSpec((d, tn), lambda j: (0, j)),
                  pl.BlockSpec((1, tn), lambda j: (0, j))],
        out_specs=pl.BlockSpec((nb, tn), lambda j: (0, j)),
        out_shape=jax.ShapeDtypeStruct((nb, n), F32),
        compiler_params=_cparams(("parallel",)),
        name="ada",
    )(c, w_ada, b_ada.reshape(1, n))


def _inproj_kernel(x_ref, mod_ref, g_ref, w_ref, o_ref, h_scr):
    @pl.when(pl.program_id(1) == 0)
    def _():
        m = mod_ref[0]
        h = _rms_rows(x_ref[...], g_ref[...]) * (1.0 + m[1:2]) + m[0:1]
        h_scr[...] = h.astype(BF16)

    o_ref[...] = jnp.dot(h_scr[...], w_ref[...], preferred_element_type=F32)


def _inproj(x2, mod, g, w_bf, seq):
    t, d = x2.shape
    n = w_bf.shape[1]
    tm, tn = 1024, 512
    return pl.pallas_call(
        _inproj_kernel,
        grid=(t // tm, n // tn),
        in_specs=[pl.BlockSpec((tm, d), lambda i, j: (i, 0)),
                  pl.BlockSpec((1, 6, d), lambda i, j: ((i * tm) // seq, 0, 0)),
                  pl.BlockSpec((1, d), lambda i, j: (0, 0)),
                  pl.BlockSpec((d, tn), lambda i, j: (0, j))],
        out_specs=pl.BlockSpec((tm, tn), lambda i, j: (i, j)),
        out_shape=jax.ShapeDtypeStruct((t, n), F32),
        scratch_shapes=[pltpu.VMEM((tm, d), BF16)],
        compiler_params=_cparams(("parallel", "arbitrary")),
        name="inproj",
    )(x2, mod, g.reshape(1, d), w_bf)


def _rwprep_kernel(seq, tm, rw,
                   z_ref, zp_ref, zn_ref, l_ref, lp_ref, ln_ref,
                   mu_ref, mul_ref, wup_ref, aup_ref, gup_ref, w0_ref, a0_ref,
                   kk_ref, ka_ref, rk_ref, e_ref, et_ref,
                   r_o, v_o, kk_o, g_o, bon_o, lw0_o, lw1_o, b0_o, b1_o, kd0_o, kd1_o):
    i = pl.program_id(0)
    first = (i * tm) % seq == 0
    last = ((i + 1) * tm) % seq == 0

    def shifted(cur, prev_blk, next_blk, mu):
        rows = lax.broadcasted_iota(I32, cur.shape, 0)
        prow = jnp.where(first, 0.0, prev_blk[SUBLANES - 1:SUBLANES, :])
        nrow = jnp.where(last, 0.0, next_blk[0:1, :])
        prev = jnp.where(rows == 0, prow, pltpu.roll(cur, 1, axis=0))
        nxt = jnp.where(rows == tm - 1, nrow, pltpu.roll(cur, tm - 1, axis=0))
        return cur + mu * (0.5 * (prev + nxt) - cur)

    lat = shifted(l_ref[...], lp_ref[...], ln_ref[...], mul_ref[...])
    w_lat = lat[:, 0:2 * W_LORA]
    a_lat = lat[:, 2 * W_LORA:2 * W_LORA + 2 * A_LORA]
    g_lat = lat[:, 2 * W_LORA + 2 * A_LORA:2 * W_LORA + 2 * A_LORA + G_LORA]
    w_raw = _dot(jnp.tanh(w_lat), wup_ref[...]) + w0_ref[...]
    a_all = _sigmoid(_dot(a_lat, aup_ref[...]) + a0_ref[...])
    g_o[...] = _dot(_sigmoid(g_lat), gup_ref[...])
    lw = (-math.exp(-0.5)) * _sigmoid(w_raw)
    lw0_o[...] = lw[:, :rw]
    lw1_o[...] = lw[:, rw:]

    r = shifted(z_ref[:, 0:rw], zp_ref[:, 0:rw], zn_ref[:, 0:rw], mu_ref[:, 0:rw])
    k = shifted(z_ref[:, rw:2 * rw], zp_ref[:, rw:2 * rw], zn_ref[:, rw:2 * rw], mu_ref[:, rw:2 * rw])
    v = shifted(z_ref[:, 2 * rw:3 * rw], zp_ref[:, 2 * rw:3 * rw], zn_ref[:, 2 * rw:3 * rw],
                mu_ref[:, 2 * rw:3 * rw])
    r_o[...] = r
    v_o[...] = v
    kk = k * kk_ref[...]
    ss = _seg_sum(kk * kk, e_ref[...], et_ref[...])
    kk = kk / jnp.maximum(jnp.sqrt(ss), 1e-12)
    kk_o[...] = kk
    a0 = a_all[:, :rw]
    a1 = a_all[:, rw:]
    b0_o[...] = kk * a0
    b1_o[...] = kk * a1
    kd0 = k * (1.0 + (a0 - 1.0) * ka_ref[...])
    kd1 = k * (1.0 + (a1 - 1.0) * ka_ref[...])
    kd0_o[...] = kd0
    kd1_o[...] = kd1
    kb = 0.5 * (kd0 + kd1)
    bon_o[...] = _seg_sum(r * kb * rk_ref[...], e_ref[...], et_ref[...]) * v


def _rwprep(z, seq, rw, lat_off, mu_rkv, mu_lat, wup, aup, gup, w0, a0, k_k, k_a, r_k, e64, e64t):
    t = z.shape[0]
    tm = 256
    nlat = 512
    nrkv = 3 * rw
    tb = tm // SUBLANES
    nblk8 = t // SUBLANES
    lat_blk = lat_off // nlat

    def cur(i):
        return (i, 0)

    def prv(i):
        return (jnp.maximum(i * tb - 1, 0), 0)

    def nxt(i):
        return (jnp.minimum((i + 1) * tb, nblk8 - 1), 0)

    def full(shape):
        return pl.BlockSpec(shape, lambda i: (0,) * len(shape))

    out = jax.ShapeDtypeStruct((t, rw), F32)
    ospec = pl.BlockSpec((tm, rw), lambda i: (i, 0))
    return pl.pallas_call(
        functools.partial(_rwprep_kernel, seq, tm, rw),
        grid=(t // tm,),
        in_specs=[pl.BlockSpec((tm, nrkv), cur),
                  pl.BlockSpec((SUBLANES, nrkv), prv),
                  pl.BlockSpec((SUBLANES, nrkv), nxt),
                  pl.BlockSpec((tm, nlat), lambda i: (i, lat_blk)),
                  pl.BlockSpec((SUBLANES, nlat), lambda i: (jnp.maximum(i * tb - 1, 0), lat_blk)),
                  pl.BlockSpec((SUBLANES, nlat), lambda i: (jnp.minimum((i + 1) * tb, nblk8 - 1), lat_blk)),
                  full((1, nrkv)), full((1, nlat)),
                  full(wup.shape), full(aup.shape), full(gup.shape),
                  full((1, 2 * rw)), full((1, 2 * rw)),
                  full((1, rw)), full((1, rw)), full((1, rw)),
                  full(e64.shape), full(e64t.shape)],
        out_specs=[ospec] * 11,
        out_shape=[out] * 11,
        compiler_params=_cparams(("parallel",)),
        name="rwprep",
    )(z, z, z, z, z, z, mu_rkv, mu_lat, wup, aup, gup, w0, a0, k_k, k_a, r_k, e64, e64t)


def _tri(n, rev):
    i = lax.broadcasted_iota(I32, (n, n), 0)
    j = lax.broadcasted_iota(I32, (n, n), 1)
    m = (j >= i) if rev else (j <= i)
    return jnp.where(m, 1.0, 0.0).astype(BF16)


def _rw_streams(streams):
    c = streams[0][0].shape[0]
    hd = RW_HEAD_DIM
    n2 = 2 * c
    ns = len(streams)
    revs = [s[7] for s in streams]
    lane = lax.broadcasted_iota(I32, (c, LANES), 1)
    head_a = lane < hd
    ri = lax.broadcasted_iota(I32, (n2, n2), 0)
    ci = lax.broadcasted_iota(I32, (n2, n2), 1)
    ti = ri % c
    si = ci % c
    same16 = (ri // 16) == (ci // 16)
    same32 = (ri // 32) == (ci // 32)
    mid32 = jnp.logical_and(same32, jnp.logical_not(same16))
    eye = jnp.where(ri == ci, 1.0, 0.0)
    strict = {False: si < ti, True: si > ti}
    incl = {False: si <= ti, True: si >= ti}
    tri = {rev: _tri(c, rev) for rev in set(revs)}
    ei = lax.broadcasted_iota(I32, (LANES, LANES), 0)
    ej = lax.broadcasted_iota(I32, (LANES, LANES), 1)
    eye_k = ei == ej

    def pair(x):
        return jnp.concatenate([jnp.where(head_a, x, 0.0), jnp.where(head_a, 0.0, x)], axis=0)

    cum = []
    for (r, v, kk, lw, b, kd, s_in, rev) in streams:
        hi, lo = _split2(lw)
        cs = jnp.dot(tri[rev], jnp.concatenate([hi, lo], axis=1), preferred_element_type=F32)
        cum.append(cs[:, :LANES] + cs[:, LANES:])
    ops = []
    for (r, v, kk, lw, b, kd, s_in, rev), cm in zip(streams, cum):
        tot = cm[0:1, :] if rev else cm[c - 1:c, :]
        g_inv = jnp.exp(-cm)
        g_tail = jnp.exp(tot - cm)
        ops.append(dict(p2=pair(-kk * jnp.exp(cm - lw)), r2=pair(r * jnp.exp(cm)),
                        bi2=pair(b * g_inv), ki2=pair(kd * g_inv), bt2=pair(b * g_tail),
                        kt2=pair(kd * g_tail), v2=pair(v), g_tot=jnp.exp(tot)))
    gm = [_dot_nt(jnp.concatenate([o['p2'], o['r2']], axis=0), jnp.concatenate([o['bi2'], o['ki2']], axis=0))
          for o in ops]
    a2 = [jnp.where(strict[rev], g[:n2, :n2], 0.0) for g, rev in zip(gm, revs)]
    b2 = [jnp.where(strict[rev], g[:n2, n2:], 0.0) for g, rev in zip(gm, revs)]
    ap2 = [jnp.where(incl[rev], g[n2:, :n2], 0.0) for g, rev in zip(gm, revs)]
    bp2 = [jnp.where(incl[rev], g[n2:, n2:], 0.0) for g, rev in zip(gm, revs)]
    vv = [_dot(jnp.concatenate([x, y], axis=0), o['v2']) for x, y, o in zip(b2, bp2, ops)]
    bv = [x[:n2] for x in vv]
    bpv = [x[n2:] for x in vv]

    x = [jnp.where(same16, a, 0.0) for a in a2]
    tinv = [eye + xi for xi in x]
    for _ in range(3):
        x = [_dot(xi, xi) for xi in x]
        tinv = [t + _dot(t, xi) for t, xi in zip(tinv, x)]
    for lvl in (mid32, jnp.logical_not(same32)):
        y = [_dot(t, jnp.where(lvl, a, 0.0)) for t, a in zip(tinv, a2)]
        tinv = [t + _dot(yi, t) for t, yi in zip(tinv, y)]

    wu = [_dot(t, jnp.concatenate([o['p2'], bvi], axis=1)) for t, o, bvi in zip(tinv, ops, bv)]
    qo = [_dot(a, w) for a, w in zip(ap2, wu)]
    m2 = [_dot_tn(w[:, :LANES], o['bt2']) + jnp.where(eye_k, jnp.broadcast_to(o['g_tot'], (LANES, LANES)), 0.0)
          for w, o in zip(wu, ops)]
    nn2 = [_dot_tn(w[:, LANES:], o['bt2']) + _dot_tn(o['v2'], o['kt2']) for w, o in zip(wu, ops)]

    nt = (((1,), (1,)), ((), ()))
    outs = []
    for i in range(ns):
        s_hi, s_lo = _split2(streams[i][6])
        q2b = (ops[i]['r2'] + qo[i][:, :LANES]).astype(BF16)
        oo = (lax.dot_general(q2b, s_hi, nt, preferred_element_type=F32)
              + lax.dot_general(q2b, s_lo, nt, preferred_element_type=F32)
              + qo[i][:, LANES:] + bpv[i])
        m2b = m2[i].astype(BF16)
        s_out = (jnp.dot(s_hi, m2b, preferred_element_type=F32)
                 + jnp.dot(s_lo, m2b, preferred_element_type=F32) + nn2[i])
        outs.append((oo[:c] + oo[c:], s_out))
    return outs


RW_PAIRS_PER_STEP = 8


def _rwscan_kernel(rf, vf, kkf, lwf, bf, kdf, rb, vb, kkb, lwb, bb, kdb, of_ref, ob_ref, sf, sb):
    @pl.when(pl.program_id(2) == 0)
    def _():
        sf[...] = jnp.zeros_like(sf)
        sb[...] = jnp.zeros_like(sb)

    streams = []
    for p in range(RW_PAIRS_PER_STEP):
        sl = slice(p * LANES, (p + 1) * LANES)
        streams.append((rf[:, sl], vf[:, sl], kkf[:, sl], lwf[:, sl], bf[:, sl], kdf[:, sl], sf[p], False))
        streams.append((rb[:, sl], vb[:, sl], kkb[:, sl], lwb[:, sl], bb[:, sl], kdb[:, sl], sb[p], True))
    outs = _rw_streams(streams)
    for p in range(RW_PAIRS_PER_STEP):
        sl = slice(p * LANES, (p + 1) * LANES)
        of_ref[:, sl], sf[p] = outs[2 * p]
        ob_ref[:, sl], sb[p] = outs[2 * p + 1]


def _rwscan(nb, seq, r, v, kk, lw0, lw1, b0, b1, kd0, kd1):
    t, rw = r.shape
    nc = seq // CHUNK
    wblk = RW_PAIRS_PER_STEP * LANES
    fw = pl.BlockSpec((CHUNK, wblk), lambda bi, hp, c: (bi * nc + c, hp))
    bw = pl.BlockSpec((CHUNK, wblk), lambda bi, hp, c: (bi * nc + nc - 1 - c, hp))
    out = jax.ShapeDtypeStruct((t, rw), F32)
    state = pltpu.VMEM((RW_PAIRS_PER_STEP, LANES, LANES), F32)
    return pl.pallas_call(
        _rwscan_kernel,
        grid=(nb, rw // wblk, nc),
        in_specs=[fw] * 6 + [bw] * 6,
        out_specs=[fw, bw],
        out_shape=[out, out],
        scratch_shapes=[state, state],
        compiler_params=_cparams(("parallel", "parallel", "arbitrary")),
        name="rwscan",
    )(r, v, kk, lw0, b0, kd0, r, v, kk, lw1, b1, kd1)


def _hg_streams(streams):
    c, dk = streams[0][0].shape
    revs = [s[5] for s in streams]
    tri = {rev: _tri(c, rev) for rev in set(revs)}
    row = lax.broadcasted_iota(I32, (c, dk), 0)
    ri = lax.broadcasted_iota(I32, (c, c), 0)
    ci = lax.broadcasted_iota(I32, (c, c), 1)

    cum = []
    for (q, k, v, lf, st, rev) in streams:
        hi, mid, lo = _split3(lf)
        cs = jnp.dot(tri[rev], jnp.concatenate([hi, mid, lo], axis=1), preferred_element_type=F32)
        cum.append(cs[:, :dk] + cs[:, dk:2 * dk] + cs[:, 2 * dk:])
    scores = [jnp.where(ri == ci, jnp.sum(s[0] * s[1], axis=1, keepdims=True), 0.0) for s in streams]
    sub = row % SUBLANES
    nt = (((1,), (1,)), ((), ()))

    def sub_bcast(x, idx):
        x3 = x.reshape(c // SUBLANES, SUBLANES, dk)
        return jnp.broadcast_to(x3[:, idx:idx + 1, :], x3.shape).reshape(c, dk)

    h = c // 2
    while h >= 1:
        blk = 2 * h
        upper = (row % blk) >= h
        same_blk = (ri // blk) == (ci // blk)
        r_up = (ri % blk) >= h
        c_up = (ci % blk) >= h
        q_rows = {False: upper, True: jnp.logical_not(upper)}
        sgn = {rev: jnp.where(q_rows[rev], 1.0, -1.0) for rev in (False, True)}
        pmask = {False: jnp.logical_and(same_blk, jnp.logical_and(r_up, jnp.logical_not(c_up))),
                 True: jnp.logical_and(same_blk, jnp.logical_and(jnp.logical_not(r_up), c_up))}
        sl = []
        for i, (q, k, v, lf, st, rev) in enumerate(streams):
            cm = cum[i]
            off = h if rev else h - 1
            if h >= SUBLANES:
                pieces = [jnp.broadcast_to(cm[m0 + off:m0 + off + 1, :], (blk, dk)) for m0 in range(0, c, blk)]
                ref = jnp.concatenate(pieces, axis=0) if len(pieces) > 1 else pieces[0]
            elif blk == SUBLANES:
                ref = sub_bcast(cm, off)
            elif 2 * blk == SUBLANES:
                ref = jnp.where(sub < blk, sub_bcast(cm, off), sub_bcast(cm, off + blk))
            else:
                ref = jnp.where(q_rows[rev], pltpu.roll(cm, c - 1 if rev else 1, axis=0), cm)
            e = jnp.minimum((cm - ref) * sgn[rev], 0.0)
            x = (jnp.where(q_rows[rev], q, k) * jnp.exp(e)).astype(BF16)
            sl.append(lax.dot_general(x, x, nt, preferred_element_type=F32))
        scores = [sc + jnp.where(pmask[rev], x, 0.0) for sc, x, rev in zip(scores, sl, revs)]
        h //= 2

    outs = []
    for (q, k, v, lf, st, rev), cm, sc in zip(streams, cum, scores):
        tot = cm[0:1, :] if rev else cm[c - 1:c, :]
        o = _dot(sc, v) + _dot_nt(q * jnp.exp(cm), st)
        st_new = st * jnp.exp(tot) + _dot_tn(v, k * jnp.exp(tot - cm))
        outs.append((o, st_new))
    return outs


HG_HEADS_PER_STEP = 4


def _hgscan_kernel(dk, qf, fff, i_f, qb, ffb, i_b, lb_ref, of_ref, ob_ref, sf, sb):
    @pl.when(pl.program_id(2) == 0)
    def _():
        sf[...] = jnp.zeros_like(sf)
        sb[...] = jnp.zeros_like(sb)

    streams = []
    for p in range(HG_HEADS_PER_STEP):
        sl = slice(p * dk, (p + 1) * dk)
        for (q_ref, ff_ref, i_ref, st_ref, d) in ((qf, fff, i_f, sf, 0), (qb, ffb, i_b, sb, 1)):
            lbv = lb_ref[d:d + 1, sl]
            f = lbv + (1.0 - lbv) * _sigmoid(ff_ref[:, sl])
            streams.append((_silu(q_ref[:, sl]), 1.0 - f, i_ref[:, sl], jnp.log(f), st_ref[p], d == 1))
    outs = _hg_streams(streams)
    for p in range(HG_HEADS_PER_STEP):
        sl = slice(p * dk, (p + 1) * dk)
        of_ref[:, sl], sf[p] = outs[2 * p]
        ob_ref[:, sl], sb[p] = outs[2 * p + 1]


def _hgscan(nb, seq, z, lb, hg_off, hw):
    t = z.shape[0]
    nc = seq // CHUNK
    dk = hw // HG_HEADS
    wblk = HG_HEADS_PER_STEP * dk
    base = hg_off // wblk
    nh = hw // wblk

    def fw(comp):
        return pl.BlockSpec((CHUNK, wblk), lambda bi, h, c: (bi * nc + c, base + comp * nh + h))

    def bw(comp):
        return pl.BlockSpec((CHUNK, wblk), lambda bi, h, c: (bi * nc + nc - 1 - c, base + comp * nh + h))

    out = jax.ShapeDtypeStruct((t, hw), F32)
    state = pltpu.VMEM((HG_HEADS_PER_STEP, dk, dk), F32)
    return pl.pallas_call(
        functools.partial(_hgscan_kernel, dk),
        grid=(nb, nh, nc),
        in_specs=[fw(0), fw(1), fw(3), bw(0), bw(2), bw(3),
                  pl.BlockSpec((2, wblk), lambda bi, h, c: (0, h))],
        out_specs=[pl.BlockSpec((CHUNK, wblk), lambda bi, h, c: (bi * nc + c, h)),
                   pl.BlockSpec((CHUNK, wblk), lambda bi, h, c: (bi * nc + nc - 1 - c, h))],
        out_shape=[out, out],
        scratch_shapes=[state, state],
        compiler_params=_cparams(("parallel", "parallel", "arbitrary")),
        name="hgscan",
    )(z, z, z, z, z, z, lb)


def _blockdiag2(w):
    _, r, n = w.shape
    z = jnp.zeros((r, n), w.dtype)
    return jnp.concatenate([jnp.concatenate([w[0], z], axis=1), jnp.concatenate([z, w[1]], axis=1)], axis=0)


def _indicator(width, seg):
    e = (jnp.arange(width)[:, None] // seg == jnp.arange(width // seg)[None, :]).astype(BF16)
    return e, e.T


def _layer_consts(w_in, rw_mu, rw_w0, rw_w_up, rw_a0, rw_a_up, rw_g_up, rw_k_k, rw_k_a, rw_r_k,
                  hg_lb_gamma):
    rw = rw_k_k.shape[-1]
    d = w_in.shape[1]
    nlat = 2 * W_LORA + 2 * A_LORA + G_LORA
    w = w_in[0]
    rkv = 3 * rw
    hg_cols = w.shape[1] - rkv - nlat
    pad = 512 - nlat
    w_perm = jnp.concatenate([w[:, :rkv], w[:, rkv + nlat:], w[:, rkv:rkv + nlat],
                              jnp.zeros((d, pad), w.dtype)], axis=1).astype(BF16)
    mu = rw_mu[0]
    lower = jnp.cumsum(jax.nn.softmax(hg_lb_gamma.astype(F32), axis=0), axis=0)[0]
    hw = lower.shape[-1]
    e64, e64t = _indicator(rw, RW_HEAD_DIM)
    return dict(
        rw=rw, hw=hw, hg_off=rkv, lat_off=rkv + hg_cols, w_in=w_perm,
        mu_rkv=mu[:rkv].reshape(1, rkv),
        mu_lat=jnp.pad(mu[rkv:rkv + nlat], (0, pad)).reshape(1, 512),
        wup=_blockdiag2(rw_w_up[0]).astype(BF16), aup=_blockdiag2(rw_a_up[0]).astype(BF16),
        gup=rw_g_up[0].astype(BF16),
        w0=rw_w0[0].reshape(1, 2 * rw), a0=rw_a0[0].reshape(1, 2 * rw),
        k_k=rw_k_k[0].reshape(1, rw), k_a=rw_k_a[0].reshape(1, rw), r_k=rw_r_k[0].reshape(1, rw),
        e64=e64, e64t=e64t,
        lb=lower,
    )


def _mixer(x2, nb, seq, mod, norm_pre_mix, wc):
    z = _inproj(x2, mod, norm_pre_mix, wc['w_in'], seq)
    (r, v, kk, g, bonus, lw0, lw1, b0, b1, kd0, kd1) = _rwprep(
        z, seq, wc['rw'], wc['lat_off'], wc['mu_rkv'], wc['mu_lat'], wc['wup'], wc['aup'], wc['gup'],
        wc['w0'], wc['a0'], wc['k_k'], wc['k_a'], wc['r_k'], wc['e64'], wc['e64t'])
    rw_of, rw_ob = _rwscan(nb, seq, r, v, kk, lw0, lw1, b0, b1, kd0, kd1)
    hg_of, hg_ob = _hgscan(nb, seq, z, wc['lb'], wc['hg_off'], wc['hw'])
    return dict(z=z, r=r, v=v, kk=kk, g=g, bonus=bonus, lw0=lw0, rw_of=rw_of, rw_ob=rw_ob,
                hg_of=hg_of, hg_ob=hg_ob)


def _outproj_kernel(rw, x_ref, rf_ref, rb_ref, bon_ref, g_ref, hf_ref, hb_ref, hgg_ref, mod_ref,
                    lnw_ref, lnb_ref, hnw_ref, npm_ref, npf_ref, wout_ref, wrh_ref, wrl_ref,
                    e64_ref, e64t_ref, e128_ref, e128t_ref,
                    x1_ref, h2_ref, lg_ref):
    m6 = mod_ref[0]
    o = rf_ref[...] + rb_ref[...]
    mean = _seg_sum(o, e64_ref[...], e64t_ref[...]) * (1.0 / RW_HEAD_DIM)
    dlt = o - mean
    var = _seg_sum(dlt * dlt, e64_ref[...], e64t_ref[...]) * (1.0 / RW_HEAD_DIM)
    o_rw = (dlt * lax.rsqrt(var + RW_GN_EPS) * lnw_ref[...] + lnb_ref[...] + bon_ref[...]) * g_ref[...]
    oh = hf_ref[...] + hb_ref[...]
    hd = oh.shape[1] // HG_HEADS
    ms = _seg_sum(oh * oh, e128_ref[...], e128t_ref[...]) * (1.0 / hd)
    o_hg = oh * lax.rsqrt(ms + NORM_EPS) * hnw_ref[...] * _silu(hgg_ref[...])
    m = _dot(o_rw, wout_ref[0:rw, :]) + _dot(o_hg, wout_ref[rw:, :])
    x1 = x_ref[...] + m6[2:3] * _rms_rows(m, npm_ref[...])
    x1_ref[...] = x1
    h2 = _rms_rows(x1, npf_ref[...]) * (1.0 + m6[4:5]) + m6[3:4]
    hi, lo = _split2(h2)
    _pack_rows(hi, h2_ref)
    nt = (((1,), (1,)), ((), ()))
    lg_ref[...] = (lax.dot_general(wrh_ref[...], hi, nt, preferred_element_type=F32)
                   + lax.dot_general(wrh_ref[...], lo, nt, preferred_element_type=F32)
                   + lax.dot_general(wrl_ref[...], hi, nt, preferred_element_type=F32))


def _outproj(x2, seq, mx, z, mod, wc, oc):
    t, d = x2.shape
    rw, hw = wc['rw'], wc['hw']
    tm = 256
    gblk = (wc['hg_off'] + 4 * hw) // hw

    def row(w):
        return pl.BlockSpec((tm, w), lambda i: (i, 0))

    def full(a):
        return pl.BlockSpec(a.shape, lambda i: (0,) * a.ndim)

    consts = [oc['ln_w'], oc['ln_b'], oc['hg_norm_w'], oc['npm'], oc['npf'], oc['w_out'], oc['wr_hi'],
              oc['wr_lo'], wc['e64'], wc['e64t'], oc['e128'], oc['e128t']]
    return pl.pallas_call(
        functools.partial(_outproj_kernel, rw),
        grid=(t // tm,),
        in_specs=[row(d), row(rw), row(rw), row(rw), row(rw), row(hw), row(hw),
                  pl.BlockSpec((tm, hw), lambda i: (i, gblk)),
                  pl.BlockSpec((1, 6, d), lambda i: ((i * tm) // seq, 0, 0))] + [full(a) for a in consts],
        out_specs=[row(d), pl.BlockSpec((tm * (d // PACK_W), LANES), lambda i: (i, 0)),
                   pl.BlockSpec((N_EXPERTS, tm), lambda i: (0, i))],
        out_shape=[jax.ShapeDtypeStruct((t, d), F32), jax.ShapeDtypeStruct((t * (d // PACK_W), LANES), jnp.uint32),
                   jax.ShapeDtypeStruct((N_EXPERTS, t), F32)],
        compiler_params=_cparams(("parallel",)),
        name="outproj",
    )(x2, mx['rw_of'], mx['rw_ob'], mx['bonus'], mx['g'], mx['hg_of'], mx['hg_ob'], z, mod, *consts)


ROUTE_TILE = 512


def _route_kernel(lg_ref, bias_ref, ut_ref, eidx_ref, wsel_ref, rank_ref, cnt_ref, carry):
    @pl.when(pl.program_id(0) == 0)
    def _():
        carry[...] = jnp.zeros_like(carry)

    ne, tt = lg_ref.shape
    gsz = ne // N_GROUPS
    neg = -jnp.inf
    s = _sigmoid(lg_ref[...])
    biased = s + bias_ref[...]
    io_g = lax.broadcasted_iota(I32, (gsz, tt), 0)
    gs_rows = []
    for gi in range(N_GROUPS):
        blk = biased[gi * gsz:(gi + 1) * gsz, :]
        m1 = jnp.max(blk, axis=0, keepdims=True)
        first = jnp.min(jnp.where(blk == m1, io_g, gsz), axis=0, keepdims=True)
        m2 = jnp.max(jnp.where(io_g == first, neg, blk), axis=0, keepdims=True)
        gs_rows.append(m1 + m2)
    gs = jnp.concatenate(gs_rows, axis=0)
    io_n = lax.broadcasted_iota(I32, (N_GROUPS, tt), 0)
    selg = jnp.zeros((N_GROUPS, tt), jnp.bool_)
    for _ in range(TOPK_GROUPS):
        m = jnp.max(gs, axis=0, keepdims=True)
        first = jnp.min(jnp.where(gs == m, io_n, N_GROUPS), axis=0, keepdims=True)
        pick = io_n == first
        selg = jnp.logical_or(selg, pick)
        gs = jnp.where(pick, neg, gs)
    emask = jnp.concatenate([jnp.broadcast_to(selg[gi:gi + 1, :], (gsz, tt)) for gi in range(N_GROUPS)],
                            axis=0)
    mb = jnp.where(emask, biased, neg)
    io_e = lax.broadcasted_iota(I32, (ne, tt), 0)
    sel = jnp.zeros((ne, tt), jnp.bool_)
    picks, idxs, ws = [], [], []
    for _ in range(TOP_K):
        m = jnp.max(mb, axis=0, keepdims=True)
        first = jnp.min(jnp.where(mb == m, io_e, ne), axis=0, keepdims=True)
        pick = io_e == first
        picks.append(pick)
        idxs.append(first)
        ws.append(jnp.sum(jnp.where(pick, s, 0.0), axis=0, keepdims=True))
        sel = jnp.logical_or(sel, pick)
        mb = jnp.where(pick, neg, mb)
    wsum = ws[0]
    for w in ws[1:]:
        wsum = wsum + w
    pos = jnp.dot(jnp.where(sel, 1.0, 0.0).astype(BF16), ut_ref[...], preferred_element_type=F32) + carry[...]
    ranks = [jnp.sum(jnp.where(p, pos, 0.0), axis=0, keepdims=True).astype(I32) for p in picks]
    carry[...] = carry[...] + jnp.sum(jnp.where(sel, 1.0, 0.0), axis=1, keepdims=True)
    zi = jnp.zeros((SUBLANES - TOP_K, tt), I32)
    eidx_ref[...] = jnp.concatenate(idxs + [zi], axis=0)
    rank_ref[...] = jnp.concatenate(ranks + [zi], axis=0)
    wsel_ref[...] = jnp.concatenate([w / wsum * ROUTED_SCALE for w in ws] + [zi.astype(F32)], axis=0)
    cnt_ref[...] = jnp.broadcast_to(carry[...], cnt_ref.shape).astype(I32)


def _route(logits_t, e_bias):
    ne, t = logits_t.shape
    tt = ROUTE_TILE
    ut = (jnp.arange(tt)[:, None] < jnp.arange(tt)[None, :]).astype(BF16)
    tok = pl.BlockSpec((SUBLANES, tt), lambda i: (0, i))
    return pl.pallas_call(
        _route_kernel,
        grid=(t // tt,),
        in_specs=[pl.BlockSpec((ne, tt), lambda i: (0, i)),
                  pl.BlockSpec((ne, 1), lambda i: (0, 0)),
                  pl.BlockSpec((tt, tt), lambda i: (0, 0))],
        out_specs=[tok, tok, tok, pl.BlockSpec((ne, LANES), lambda i: (0, 0))],
        out_shape=[jax.ShapeDtypeStruct((SUBLANES, t), I32), jax.ShapeDtypeStruct((SUBLANES, t), F32),
                   jax.ShapeDtypeStruct((SUBLANES, t), I32), jax.ShapeDtypeStruct((ne, LANES), I32)],
        scratch_shapes=[pltpu.VMEM((ne, 1), F32)],
        compiler_params=_cparams(("arbitrary",)),
        name="route",
    )(logits_t, e_bias.reshape(ne, 1), ut)


DISPATCH_TILE = 128
PLAN_TILE = 1024


def _plan_kernel(eidx_ref, rank_ref, ps_ref, dest_ref):
    ne = ps_ref.shape[0]
    tp = eidx_ref.shape[1]
    io_e = lax.broadcasted_iota(I32, (ne, tp), 0)
    ps = ps_ref[...]
    rows = []
    for j in range(TOP_K):
        start = jnp.sum(jnp.where(io_e == eidx_ref[j:j + 1, :], ps, 0.0), axis=0, keepdims=True)
        rows.append(start.astype(I32) + rank_ref[j:j + 1, :])
    dest = jnp.concatenate(rows + [jnp.zeros((SUBLANES - TOP_K, tp), I32)], axis=0)
    for i in range(tp // DISPATCH_TILE):
        dest_ref[i] = dest[:, i * DISPATCH_TILE:(i + 1) * DISPATCH_TILE]


def _plan(eidx, rank, pad_start):
    t = eidx.shape[1]
    tp = PLAN_TILE
    ne = pad_start.shape[0]
    tok = pl.BlockSpec((SUBLANES, tp), lambda i: (0, i))
    ntile = tp // DISPATCH_TILE
    return pl.pallas_call(
        _plan_kernel,
        grid=(t // tp,),
        in_specs=[tok, tok, pl.BlockSpec((ne, 1), lambda i: (0, 0))],
        out_specs=pl.BlockSpec((ntile, SUBLANES, DISPATCH_TILE), lambda i: (i, 0, 0)),
        out_shape=jax.ShapeDtypeStruct((t // DISPATCH_TILE, SUBLANES, DISPATCH_TILE), I32),
        compiler_params=_cparams(("parallel",)),
        name="plan",
    )(eidx, rank, pad_start.astype(F32).reshape(ne, 1))


def _dispatch_kernel(tt, nsteps, zs_ref, zc_ref, nu_ref, dest_ref, h_ref, xs_ref, idx_smem, zero_blk,
                     isem, sem, zsem, bsem):
    i = pl.program_id(0)
    n = tt * TOP_K
    ne = zs_ref.shape[0]
    nblk = xs_ref.shape[0] // EXPERT_BLOCK
    icp = pltpu.make_async_copy(dest_ref.at[pl.ds(i * tt * SUBLANES, n)], idx_smem, isem)
    icp.start()

    @pl.when(i == 0)
    def _():
        zero_blk[...] = jnp.zeros_like(zero_blk)

    icp.wait()

    def row_copy(k):
        return pltpu.make_async_copy(h_ref.at[k % tt], xs_ref.at[idx_smem[k]], sem)

    def drain(k, c):
        row_copy(k).wait()
        return c

    for k in range(n):
        row_copy(k).start(priority=k % 2)

    for q in range(pl.cdiv(ne, nsteps)):
        e = i + q * nsteps
        ec = jnp.minimum(e, ne - 1)
        start = zs_ref[ec]
        count = jnp.where(e < ne, zc_ref[ec], 0)

        def zero_copy(r):
            return pltpu.make_async_copy(zero_blk.at[0], xs_ref.at[start + r], zsem)

        def z_issue(r, c):
            zero_copy(r).start()
            return c

        def z_drain(r, c):
            zero_copy(r).wait()
            return c

        lax.fori_loop(0, count, z_issue, 0)
        lax.fori_loop(0, count, z_drain, 0)

    for q in range(pl.cdiv(nblk, nsteps)):
        blk = nu_ref[0] + i + q * nsteps

        @pl.when(blk < nblk)
        def _():
            bcp = pltpu.make_async_copy(
                zero_blk, xs_ref.at[pl.ds(pl.multiple_of(blk * EXPERT_BLOCK, EXPERT_BLOCK), EXPERT_BLOCK)], bsem)
            bcp.start()
            bcp.wait()

    lax.fori_loop(0, n, drain, 0, unroll=8)


def _dispatch(dest_flat, h2p, p_rows, zero_start, zero_count, n_used):
    t = h2p.shape[0]
    tt = DISPATCH_TILE
    anyspec = pl.BlockSpec(memory_space=pl.ANY)
    dma = pltpu.SemaphoreType.DMA(())
    grid_spec = pltpu.PrefetchScalarGridSpec(
        num_scalar_prefetch=3,
        grid=(t // tt,),
        in_specs=[anyspec, pl.BlockSpec((tt,) + h2p.shape[1:], lambda i, zs, zc, nu: (i, 0, 0))],
        out_specs=anyspec,
        scratch_shapes=[pltpu.SMEM((tt * TOP_K,), I32), pltpu.VMEM((EXPERT_BLOCK,) + h2p.shape[1:], h2p.dtype),
                        dma, dma, dma, dma],
    )
    return pl.pallas_call(
        functools.partial(_dispatch_kernel, tt, t // tt),
        grid_spec=grid_spec,
        out_shape=jax.ShapeDtypeStruct((p_rows,) + h2p.shape[1:], h2p.dtype),
        compiler_params=_cparams(("arbitrary",)),
        name="dispatch",
    )(zero_start, zero_count, n_used, dest_flat, h2p)


def _experts_kernel(be_ref, nu_ref, x_ref, wg_ref, wu_ref, wd_ref, y_ref, wg_s, wu_s, wd_s):
    b = pl.program_id(0)
    live = b < nu_ref[0]
    new_expert = jnp.logical_or(b == 0, be_ref[b] != be_ref[jnp.maximum(b - 1, 0)])

    @pl.when(jnp.logical_and(live, new_expert))
    def _():
        wg_s[...] = wg_ref[0].astype(BF16)
        wu_s[...] = wu_ref[0].astype(BF16)
        wd_s[...] = wd_ref[0].astype(BF16)

    @pl.when(live)
    def _():
        x = _unpack_rows(x_ref, wg_ref.shape[1] // PACK_W)
        gate = jnp.dot(x, wg_s[...], preferred_element_type=F32)
        up = jnp.dot(x, wu_s[...], preferred_element_type=F32)
        _pack_rows(_dot(_silu(gate) * up, wd_s[...]).astype(BF16), y_ref)

    @pl.when(b >= nu_ref[0])
    def _():
        y_ref[...] = jnp.zeros_like(y_ref)


def _experts(block_e, n_used, xs, wg, wu, wd):
    d, de = wg.shape[1], wg.shape[2]
    ns = d // PACK_W
    p_rows = xs.shape[0] // ns
    nblk = p_rows // EXPERT_BLOCK

    def live(b, nu):
        return jnp.minimum(b, jnp.maximum(nu[0] - 1, 0))

    grid_spec = pltpu.PrefetchScalarGridSpec(
        num_scalar_prefetch=2,
        grid=(nblk,),
        in_specs=[pl.BlockSpec((EXPERT_BLOCK * ns, LANES), lambda b, be, nu: (live(b, nu), 0)),
                  pl.BlockSpec((1, d, de), lambda b, be, nu: (be[live(b, nu)], 0, 0)),
                  pl.BlockSpec((1, d, de), lambda b, be, nu: (be[live(b, nu)], 0, 0)),
                  pl.BlockSpec((1, de, d), lambda b, be, nu: (be[live(b, nu)], 0, 0))],
        out_specs=pl.BlockSpec((EXPERT_BLOCK * ns, LANES), lambda b, be, nu: (b, 0)),
        scratch_shapes=[pltpu.VMEM((d, de), BF16), pltpu.VMEM((d, de), BF16), pltpu.VMEM((de, d), BF16)],
    )
    return pl.pallas_call(
        _experts_kernel,
        grid_spec=grid_spec,
        out_shape=jax.ShapeDtypeStruct((p_rows * ns, LANES), jnp.uint32),
        compiler_params=_cparams(("arbitrary",)),
        name="experts",
    )(block_e, n_used, xs, wg, wu, wd)


def _combine_kernel(tt, nsteps, dest_ref, y_ref, x1_ref, h2_ref, w_ref, mod_ref, npf_ref, sg_ref, su_ref, sd_ref,
                    o_ref, ybuf, idx_smem, isem, sems):
    i = pl.program_id(0)
    n = tt * TOP_K
    ns = sg_ref.shape[0] // PACK_W

    def row_copy(slot, k):
        dst = ybuf.at[pl.ds(pl.multiple_of((slot * n + k) * ns, ns), ns)]
        return pltpu.make_async_copy(y_ref.at[idx_smem[slot * n + k]], dst, sems.at[slot])

    def issue(step, slot):
        icp = pltpu.make_async_copy(dest_ref.at[pl.ds(step * tt * SUBLANES, n)],
                                    idx_smem.at[pl.ds(slot * n, n)], isem)
        icp.start()
        icp.wait()
        for k in range(n):
            row_copy(slot, k).start(priority=k % 2)

    def drain(slot):
        def body(k, c):
            row_copy(slot, k).wait()
            return c

        lax.fori_loop(0, n, body, 0, unroll=8)

    @pl.when(i == 0)
    def _():
        issue(0, 0)

    other = (i + 1) % 2
    issue(jnp.minimum(i + 1, nsteps - 1), other)

    h2 = _unpack_rows(h2_ref, ns)
    gate = jnp.dot(h2, sg_ref[...], preferred_element_type=F32)
    up = jnp.dot(h2, su_ref[...], preferred_element_type=F32)
    shared = _dot(_silu(gate) * up, sd_ref[...])
    w = w_ref[...]
    m6 = mod_ref[0]

    def finish(slot):
        drain(slot)
        acc = [None] * (2 * ns)
        for j in range(TOP_K):
            wj = w[:, j:j + 1]
            for s in range(ns):
                word = ybuf[pl.ds((slot * n + j * tt) * ns + s, tt, stride=ns), :]
                parts = (lax.bitcast_convert_type(word & jnp.uint32(0xFFFF0000), F32),
                         lax.bitcast_convert_type(word << 16, F32))
                for h, part in enumerate(parts):
                    term = wj * part
                    acc[2 * s + h] = term if acc[2 * s + h] is None else acc[2 * s + h] + term
        routed = jnp.concatenate(acc, axis=1)
        o_ref[...] = x1_ref[...] + m6[5:6] * _rms_rows(routed + shared, npf_ref[...])

    for slot in (0, 1):
        @pl.when(i % 2 == slot)
        def _():
            finish(slot)

    @pl.when(i == nsteps - 1)
    def _():
        drain(other)


def _combine(dest_flat, y3, x1, h2, wsel_t, mod, seq, npf, sg, su, sd):
    t, d = x1.shape
    tt = DISPATCH_TILE
    ns = d // PACK_W
    n = tt * TOP_K
    anyspec = pl.BlockSpec(memory_space=pl.ANY)

    def full(a):
        return pl.BlockSpec(a.shape, lambda i: (0,) * a.ndim)

    rows = pltpu.VMEM((2 * n * ns, LANES), jnp.uint32)
    return pl.pallas_call(
        functools.partial(_combine_kernel, tt, t // tt),
        grid=(t // tt,),
        in_specs=[anyspec, anyspec,
                  pl.BlockSpec((tt, d), lambda i: (i, 0)),
                  pl.BlockSpec((tt * ns, LANES), lambda i: (i, 0)),
                  pl.BlockSpec((tt, SUBLANES), lambda i: (i, 0)),
                  pl.BlockSpec((1, 6, d), lambda i: ((i * tt) // seq, 0, 0)),
                  full(npf), full(sg), full(su), full(sd)],
        out_specs=pl.BlockSpec((tt, d), lambda i: (i, 0)),
        out_shape=jax.ShapeDtypeStruct((t, d), F32),
        scratch_shapes=[rows, pltpu.SMEM((2 * n,), I32), pltpu.SemaphoreType.DMA(()),
                        pltpu.SemaphoreType.DMA((2,))],
        compiler_params=_cparams(("arbitrary",)),
        name="combine",
    )(dest_flat, y3, x1, h2, wsel_t, mod, npf, sg, su, sd)


def _moe_plan(eidx, rank, cnt, t):
    counts = cnt[:, 0]
    padded = (counts + EXPERT_BLOCK - 1) // EXPERT_BLOCK * EXPERT_BLOCK
    pad_end = jnp.cumsum(padded)
    pad_start = pad_end - padded
    n_blocks = (t * TOP_K + EXPERT_BLOCK - 1) // EXPERT_BLOCK + N_EXPERTS
    first_row = jnp.arange(n_blocks, dtype=I32) * EXPERT_BLOCK
    block_e = jnp.minimum(jnp.sum((pad_end[None, :] <= first_row[:, None]).astype(I32), axis=1),
                          N_EXPERTS - 1).astype(I32)
    n_used = (pad_end[-1:] // EXPERT_BLOCK).astype(I32)
    dest_flat = _plan(eidx, rank, pad_start).reshape(-1)
    zero_rows = ((pad_start + counts).astype(I32), (padded - counts).astype(I32))
    return block_e, n_used, dest_flat, n_blocks * EXPERT_BLOCK, zero_rows


def _trunk(x, mod, wc, oc, ec, norm_pre_mix):
    nb, seq, d = x.shape
    t = nb * seq
    x2 = x.reshape(t, d)
    mx = _mixer(x2, nb, seq, mod, norm_pre_mix, wc)
    x1, h2, logits_t = _outproj(x2, seq, mx, mx['z'], mod, wc, oc)
    eidx, wsel, rank, cnt = _route(logits_t, ec['e_bias'])
    block_e, n_used, dest_flat, p_rows, zero_rows = _moe_plan(eidx, rank, cnt, t)
    ns = d // PACK_W
    xs = _dispatch(dest_flat, h2.reshape(t, ns, LANES), p_rows, *zero_rows, n_used)
    y = _experts(block_e, n_used, xs.reshape(p_rows * ns, LANES), ec['wg'], ec['wu'], ec['wd'])
    out = _combine(dest_flat, y.reshape(p_rows, ns, LANES), x1, h2, wsel.T, mod, seq, oc['npo'],
                   ec['sg'], ec['su'], ec['sd'])
    return out.reshape(nb, seq, d)


def kernel(x_prompt, x_sample, c_prompt, c_sample, w_ada, b_ada, norm_pre_mix, norm_post_mix, norm_pre_ffn, norm_post_ffn, w_in, rw_mu, rw_w0, rw_w_up, rw_a0, rw_a_up, rw_g_up, rw_k_k, rw_k_a, rw_r_k, rw_ln_w, rw_ln_b, hg_lb_gamma, hg_norm_w, w_out, w_router, e_bias, w_exp_gate, w_exp_up, w_exp_down, w_sh_gate, w_sh_up, w_sh_down):
    d = x_prompt.shape[-1]
    wc = _layer_consts(w_in, rw_mu, rw_w0, rw_w_up, rw_a0, rw_a_up, rw_g_up, rw_k_k, rw_k_a, rw_r_k,
                       hg_lb_gamma)
    rw, hw = wc['rw'], wc['hw']
    e128, e128t = _indicator(hw, hw // HG_HEADS)
    wr_hi, wr_lo = _split2(w_router[0].T)
    oc = dict(ln_w=rw_ln_w[0].reshape(1, rw), ln_b=rw_ln_b[0].reshape(1, rw),
              hg_norm_w=hg_norm_w[0].reshape(1, hw), npm=norm_post_mix[0].reshape(1, d),
              npf=norm_pre_ffn[0].reshape(1, d), npo=norm_post_ffn[0].reshape(1, d),
              w_out=w_out[0].astype(BF16), wr_hi=wr_hi, wr_lo=wr_lo, e128=e128, e128t=e128t)
    ec = dict(e_bias=e_bias[0], wg=w_exp_gate[0], wu=w_exp_up[0], wd=w_exp_down[0], sg=w_sh_gate[0].astype(BF16), su=w_sh_up[0].astype(BF16),
              sd=w_sh_down[0].astype(BF16))
    nbp = c_prompt.shape[0]
    mod = _ada(jnp.concatenate([c_prompt, c_sample], axis=0), w_ada[0], b_ada[0]).reshape(-1, 6, d)
    y_prompt = _trunk(x_prompt, mod[:nbp], wc, oc, ec, norm_pre_mix[0])
    y_sample = _trunk(x_sample, mod[nbp:], wc, oc, ec, norm_pre_mix[0])
    return (y_prompt, y_sample)
```

```python
import functools
import math

import jax
import jax.numpy as jnp
from jax import lax
from jax.experimental import pallas as pl
from jax.experimental.pallas import tpu as pltpu

F32 = jnp.float32
BF16 = jnp.bfloat16
I32 = jnp.int32

RW_HEAD_DIM = 64
W_LORA = 64
A_LORA = 64
G_LORA = 128
RW_GN_EPS = 64e-5
HG_HEADS = 8
N_EXPERTS = 64
TOP_K = 6
N_GROUPS = 8
TOPK_GROUPS = 4
ROUTED_SCALE = 2.5
EXPERT_BLOCK = 256
NORM_EPS = 1e-6

LANES = 128
SUBLANES = 8
VMEM_LIMIT = 56 * 1024 * 1024

CHUNK = 64


def _cparams(sem, vmem=VMEM_LIMIT):
    return pltpu.CompilerParams(dimension_semantics=sem, vmem_limit_bytes=vmem)


def _sigmoid(x):
    return 1.0 / (1.0 + jnp.exp(-x))


def _silu(x):
    return x * _sigmoid(x)


def _dot(a, b):
    return jnp.dot(a.astype(BF16), b.astype(BF16), preferred_element_type=F32)


def _dot_nt(a, b):
    return lax.dot_general(a.astype(BF16), b.astype(BF16), (((1,), (1,)), ((), ())),
                           preferred_element_type=F32)


def _dot_tn(a, b):
    return lax.dot_general(a.astype(BF16), b.astype(BF16), (((0,), (0,)), ((), ())),
                           preferred_element_type=F32)


def _split2(x):
    hi = x.astype(BF16)
    lo = (x - hi.astype(F32)).astype(BF16)
    return hi, lo


def _split3(x):
    hi = x.astype(BF16)
    r1 = x - hi.astype(F32)
    mid = r1.astype(BF16)
    lo = (r1 - mid.astype(F32)).astype(BF16)
    return hi, mid, lo


def _seg_sum(x, e, et):
    hi, lo = _split2(x)
    s = jnp.dot(hi, e, preferred_element_type=F32) + jnp.dot(lo, e, preferred_element_type=F32)
    shi, slo = _split2(s)
    return jnp.dot(shi, et, preferred_element_type=F32) + jnp.dot(slo, et, preferred_element_type=F32)


PACK_W = 2 * LANES


def _pack_rows(x_bf, out_ref):
    n, d = x_bf.shape
    ns = d // PACK_W
    for s in range(ns):
        a = lax.bitcast_convert_type(x_bf[:, s * PACK_W:s * PACK_W + LANES].astype(F32), jnp.uint32)
        b = lax.bitcast_convert_type(x_bf[:, s * PACK_W + LANES:(s + 1) * PACK_W].astype(F32), jnp.uint32)
        out_ref[pl.ds(s, n, stride=ns), :] = a | (b >> 16)


def _unpack_rows(x_ref, ns):
    n = x_ref.shape[0] // ns
    parts = []
    for s in range(ns):
        w = x_ref[pl.ds(s, n, stride=ns), :]
        parts.append(lax.bitcast_convert_type(w & jnp.uint32(0xFFFF0000), F32).astype(BF16))
        parts.append(lax.bitcast_convert_type(w << 16, F32).astype(BF16))
    return jnp.concatenate(parts, axis=1)


def _rms_rows(x, g):
    return x * lax.rsqrt(jnp.mean(x * x, axis=-1, keepdims=True) + NORM_EPS) * g


def _ada_kernel(c_ref, w_ref, b_ref, o_ref):
    c = c_ref[...]
    o_ref[...] = _dot(_silu(c), w_ref[...]) + b_ref[...]


def _ada(c, w_ada, b_ada):
    nb, d = c.shape
    n = w_ada.shape[1]
    tn = 512
    return pl.pallas_call(
        _ada_kernel,
        grid=(n // tn,),
        in_specs=[pl.BlockSpec((nb, d), lambda j: (0, 0)),
                  pl.BlockSpec((d, tn), lambda j: (0, j)),
                  pl.BlockSpec((1, tn), lambda j: (0, j))],
        out_specs=pl.BlockSpec((nb, tn), lambda j: (0, j)),
        out_shape=jax.ShapeDtypeStruct((nb, n), F32),
        compiler_params=_cparams(("parallel",)),
        name="ada",
    )(c, w_ada, b_ada.reshape(1, n))


def _inproj_kernel(tn, q_cols, f_cols, x_ref, mod_ref, g_ref, w_ref, lb_ref, o_ref, h_scr):
    j = pl.program_id(1)

    @pl.when(j == 0)
    def _():
        m = mod_ref[0]
        h = _rms_rows(x_ref[...], g_ref[...]) * (1.0 + m[1:2]) + m[0:1]
        h_scr[...] = h.astype(BF16)

    acc = jnp.dot(h_scr[...], w_ref[...], preferred_element_type=F32)
    col = j * tn
    is_q = jnp.logical_and(col >= q_cols[0], col < q_cols[1])
    is_f = jnp.logical_and(col >= f_cols[0], col < f_cols[1])

    @pl.when(is_q)
    def _():
        o_ref[...] = _silu(acc)

    @pl.when(is_f)
    def _():
        lb = lb_ref[...]
        o_ref[...] = jnp.log(lb + (1.0 - lb) * _sigmoid(acc))

    @pl.when(jnp.logical_not(jnp.logical_or(is_q, is_f)))
    def _():
        o_ref[...] = acc


def _inproj(x2, mod, g, w_bf, seq, lb_cols, q_cols, f_cols):
    t, d = x2.shape
    n = w_bf.shape[1]
    tm, tn = 1024, 512
    return pl.pallas_call(
        functools.partial(_inproj_kernel, tn, q_cols, f_cols),
        grid=(t // tm, n // tn),
        in_specs=[pl.BlockSpec((tm, d), lambda i, j: (i, 0)),
                  pl.BlockSpec((1, 6, d), lambda i, j: ((i * tm) // seq, 0, 0)),
                  pl.BlockSpec((1, d), lambda i, j: (0, 0)),
                  pl.BlockSpec((d, tn), lambda i, j: (0, j)),
                  pl.BlockSpec((1, tn), lambda i, j: (0, j))],
        out_specs=pl.BlockSpec((tm, tn), lambda i, j: (i, j)),
        out_shape=jax.ShapeDtypeStruct((t, n), F32),
        scratch_shapes=[pltpu.VMEM((tm, d), BF16)],
        compiler_params=_cparams(("parallel", "arbitrary")),
        name="inproj",
    )(x2, mod, g.reshape(1, d), w_bf, lb_cols)


def _rwprep_kernel(seq, tm, rw,
                   z_ref, zp_ref, zn_ref, l_ref, lp_ref, ln_ref,
                   mu_ref, mul_ref, wup_ref, aup_ref, gup_ref, w0_ref, a0_ref,
                   kk_ref, ka_ref, rk_ref, e_ref, et_ref,
                   r_o, v_o, kk_o, g_o, bon_o, lw0_o, lw1_o, b0_o, b1_o, kd0_o, kd1_o):
    i = pl.program_id(0)
    first = (i * tm) % seq == 0
    last = ((i + 1) * tm) % seq == 0

    def shifted(cur, prev_blk, next_blk, mu):
        rows = lax.broadcasted_iota(I32, cur.shape, 0)
        prow = jnp.where(first, 0.0, prev_blk[SUBLANES - 1:SUBLANES, :])
        nrow = jnp.where(last, 0.0, next_blk[0:1, :])
        prev = jnp.where(rows == 0, prow, pltpu.roll(cur, 1, axis=0))
        nxt = jnp.where(rows == tm - 1, nrow, pltpu.roll(cur, tm - 1, axis=0))
        return cur + mu * (0.5 * (prev + nxt) - cur)

    lat = shifted(l_ref[...], lp_ref[...], ln_ref[...], mul_ref[...])
    w_lat = lat[:, 0:2 * W_LORA]
    a_lat = lat[:, 2 * W_LORA:2 * W_LORA + 2 * A_LORA]
    g_lat = lat[:, 2 * W_LORA + 2 * A_LORA:2 * W_LORA + 2 * A_LORA + G_LORA]
    w_raw = _dot(jnp.tanh(w_lat), wup_ref[...]) + w0_ref[...]
    a_all = _sigmoid(_dot(a_lat, aup_ref[...]) + a0_ref[...])
    g_o[...] = _dot(_sigmoid(g_lat), gup_ref[...])
    lw = (-math.exp(-0.5)) * _sigmoid(w_raw)
    lw0_o[...] = lw[:, :rw]
    lw1_o[...] = lw[:, rw:]

    r = shifted(z_ref[:, 0:rw], zp_ref[:, 0:rw], zn_ref[:, 0:rw], mu_ref[:, 0:rw])
    k = shifted(z_ref[:, rw:2 * rw], zp_ref[:, rw:2 * rw], zn_ref[:, rw:2 * rw], mu_ref[:, rw:2 * rw])
    v = shifted(z_ref[:, 2 * rw:3 * rw], zp_ref[:, 2 * rw:3 * rw], zn_ref[:, 2 * rw:3 * rw],
                mu_ref[:, 2 * rw:3 * rw])
    r_o[...] = r
    v_o[...] = v
    kk = k * kk_ref[...]
    ss = _seg_sum(kk * kk, e_ref[...], et_ref[...])
    kk = kk / jnp.maximum(jnp.sqrt(ss), 1e-12)
    kk_o[...] = kk
    a0 = a_all[:, :rw]
    a1 = a_all[:, rw:]
    b0_o[...] = kk * a0
    b1_o[...] = kk * a1
    kd0 = k * (1.0 + (a0 - 1.0) * ka_ref[...])
    kd1 = k * (1.0 + (a1 - 1.0) * ka_ref[...])
    kd0_o[...] = kd0
    kd1_o[...] = kd1
    kb = 0.5 * (kd0 + kd1)
    bon_o[...] = _seg_sum(r * kb * rk_ref[...], e_ref[...], et_ref[...]) * v


def _rwprep(z, seq, rw, lat_off, mu_rkv, mu_lat, wup, aup, gup, w0, a0, k_k, k_a, r_k, e64, e64t):
    t = z.shape[0]
    tm = 256
    nlat = 512
    nrkv = 3 * rw
    tb = tm // SUBLANES
    nblk8 = t // SUBLANES
    lat_blk = lat_off // nlat

    def cur(i):
        return (i, 0)

    def prv(i):
        return (jnp.maximum(i * tb - 1, 0), 0)

    def nxt(i):
        return (jnp.minimum((i + 1) * tb, nblk8 - 1), 0)

    def full(shape):
        return pl.BlockSpec(shape, lambda i: (0,) * len(shape))

    out = jax.ShapeDtypeStruct((t, rw), F32)
    ospec = pl.BlockSpec((tm, rw), lambda i: (i, 0))
    return pl.pallas_call(
        functools.partial(_rwprep_kernel, seq, tm, rw),
        grid=(t // tm,),
        in_specs=[pl.BlockSpec((tm, nrkv), cur),
                  pl.BlockSpec((SUBLANES, nrkv), prv),
                  pl.BlockSpec((SUBLANES, nrkv), nxt),
                  pl.BlockSpec((tm, nlat), lambda i: (i, lat_blk)),
                  pl.BlockSpec((SUBLANES, nlat), lambda i: (jnp.maximum(i * tb - 1, 0), lat_blk)),
                  pl.BlockSpec((SUBLANES, nlat), lambda i: (jnp.minimum((i + 1) * tb, nblk8 - 1), lat_blk)),
                  full((1, nrkv)), full((1, nlat)),
                  full(wup.shape), full(aup.shape), full(gup.shape),
                  full((1, 2 * rw)), full((1, 2 * rw)),
                  full((1, rw)), full((1, rw)), full((1, rw)),
                  full(e64.shape), full(e64t.shape)],
        out_specs=[ospec] * 11,
        out_shape=[out] * 11,
        compiler_params=_cparams(("parallel",)),
        name="rwprep",
    )(z, z, z, z, z, z, mu_rkv, mu_lat, wup, aup, gup, w0, a0, k_k, k_a, r_k, e64, e64t)


def _tri(n, rev):
    i = lax.broadcasted_iota(I32, (n, n), 0)
    j = lax.broadcasted_iota(I32, (n, n), 1)
    m = (j >= i) if rev else (j <= i)
    return jnp.where(m, 1.0, 0.0).astype(BF16)


def _rw_streams(streams):
    c = streams[0][0].shape[0]
    hd = RW_HEAD_DIM
    n2 = 2 * c
    ns = len(streams)
    revs = [s[7] for s in streams]
    lane = lax.broadcasted_iota(I32, (c, LANES), 1)
    head_a = lane < hd
    ri = lax.broadcasted_iota(I32, (n2, n2), 0)
    ci = lax.broadcasted_iota(I32, (n2, n2), 1)
    ti = ri % c
    si = ci % c
    same16 = (ri // 16) == (ci // 16)
    same32 = (ri // 32) == (ci // 32)
    mid32 = jnp.logical_and(same32, jnp.logical_not(same16))
    eye = jnp.where(ri == ci, 1.0, 0.0)
    strict = {False: si < ti, True: si > ti}
    incl = {False: si <= ti, True: si >= ti}
    tri = {rev: _tri(c, rev) for rev in set(revs)}
    ei = lax.broadcasted_iota(I32, (LANES, LANES), 0)
    ej = lax.broadcasted_iota(I32, (LANES, LANES), 1)
    eye_k = ei == ej

    def pair(x):
        return jnp.concatenate([jnp.where(head_a, x, 0.0), jnp.where(head_a, 0.0, x)], axis=0)

    cum = []
    for (r, v, kk, lw, b, kd, s_in, rev) in streams:
        hi, lo = _split2(lw)
        cs = jnp.dot(tri[rev], jnp.concatenate([hi, lo], axis=1), preferred_element_type=F32)
        cum.append(cs[:, :LANES] + cs[:, LANES:])
    ops = []
    for (r, v, kk, lw, b, kd, s_in, rev), cm in zip(streams, cum):
        tot = cm[0:1, :] if rev else cm[c - 1:c, :]
        g_inv = jnp.exp(-cm)
        g_tail = jnp.exp(tot - cm)
        ops.append(dict(p2=pair(-kk * jnp.exp(cm - lw)), r2=pair(r * jnp.exp(cm)),
                        bi2=pair(b * g_inv), ki2=pair(kd * g_inv), bt2=pair(b * g_tail),
                        kt2=pair(kd * g_tail), v2=pair(v), g_tot=jnp.exp(tot)))
    gm = [_dot_nt(jnp.concatenate([o['p2'], o['r2']], axis=0), jnp.concatenate([o['bi2'], o['ki2']], axis=0))
          for o in ops]
    a2 = [jnp.where(strict[rev], g[:n2, :n2], 0.0) for g, rev in zip(gm, revs)]
    b2 = [jnp.where(strict[rev], g[:n2, n2:], 0.0) for g, rev in zip(gm, revs)]
    ap2 = [jnp.where(incl[rev], g[n2:, :n2], 0.0) for g, rev in zip(gm, revs)]
    bp2 = [jnp.where(incl[rev], g[n2:, n2:], 0.0) for g, rev in zip(gm, revs)]
    vv = [_dot(jnp.concatenate([x, y], axis=0), o['v2']) for x, y, o in zip(b2, bp2, ops)]
    bv = [x[:n2] for x in vv]
    bpv = [x[n2:] for x in vv]

    x = [jnp.where(same16, a, 0.0) for a in a2]
    tinv = [eye + xi for xi in x]
    for _ in range(3):
        x = [_dot(xi, xi) for xi in x]
        tinv = [t + _dot(t, xi) for t, xi in zip(tinv, x)]
    for lvl in (mid32, jnp.logical_not(same32)):
        y = [_dot(t, jnp.where(lvl, a, 0.0)) for t, a in zip(tinv, a2)]
        tinv = [t + _dot(yi, t) for t, yi in zip(tinv, y)]

    wu = [_dot(t, jnp.concatenate([o['p2'], bvi], axis=1)) for t, o, bvi in zip(tinv, ops, bv)]
    qo = [_dot(a, w) for a, w in zip(ap2, wu)]
    m2 = [_dot_tn(w[:, :LANES], o['bt2']) + jnp.where(eye_k, jnp.broadcast_to(o['g_tot'], (LANES, LANES)), 0.0)
          for w, o in zip(wu, ops)]
    nn2 = [_dot_tn(w[:, LANES:], o['bt2']) + _dot_tn(o['v2'], o['kt2']) for w, o in zip(wu, ops)]

    nt = (((1,), (1,)), ((), ()))
    outs = []
    for i in range(ns):
        s_hi, s_lo = _split2(streams[i][6])
        q2b = (ops[i]['r2'] + qo[i][:, :LANES]).astype(BF16)
        oo = (lax.dot_general(q2b, s_hi, nt, preferred_element_type=F32)
              + lax.dot_general(q2b, s_lo, nt, preferred_element_type=F32)
              + qo[i][:, LANES:] + bpv[i])
        m2b = m2[i].astype(BF16)
        s_out = (jnp.dot(s_hi, m2b, preferred_element_type=F32)
                 + jnp.dot(s_lo, m2b, preferred_element_type=F32) + nn2[i])
        outs.append((oo[:c] + oo[c:], s_out))
    return outs


RW_PAIRS_PER_STEP = 8


def _rwscan_kernel(rf, vf, kkf, lwf, bf, kdf, rb, vb, kkb, lwb, bb, kdb, of_ref, ob_ref, sf, sb):
    @pl.when(pl.program_id(2) == 0)
    def _():
        sf[...] = jnp.zeros_like(sf)
        sb[...] = jnp.zeros_like(sb)

    streams = []
    for p in range(RW_PAIRS_PER_STEP):
        sl = slice(p * LANES, (p + 1) * LANES)
        streams.append((rf[:, sl], vf[:, sl], kkf[:, sl], lwf[:, sl], bf[:, sl], kdf[:, sl], sf[p], False))
        streams.append((rb[:, sl], vb[:, sl], kkb[:, sl], lwb[:, sl], bb[:, sl], kdb[:, sl], sb[p], True))
    outs = _rw_streams(streams)
    for p in range(RW_PAIRS_PER_STEP):
        sl = slice(p * LANES, (p + 1) * LANES)
        of_ref[:, sl], sf[p] = outs[2 * p]
        ob_ref[:, sl], sb[p] = outs[2 * p + 1]


def _rwscan(nb, seq, r, v, kk, lw0, lw1, b0, b1, kd0, kd1):
    t, rw = r.shape
    nc = seq // CHUNK
    wblk = RW_PAIRS_PER_STEP * LANES
    fw = pl.BlockSpec((CHUNK, wblk), lambda bi, hp, c: (bi * nc + c, hp))
    bw = pl.BlockSpec((CHUNK, wblk), lambda bi, hp, c: (bi * nc + nc - 1 - c, hp))
    out = jax.ShapeDtypeStruct((t, rw), F32)
    state = pltpu.VMEM((RW_PAIRS_PER_STEP, LANES, LANES), F32)
    return pl.pallas_call(
        _rwscan_kernel,
        grid=(nb, rw // wblk, nc),
        in_specs=[fw] * 6 + [bw] * 6,
        out_specs=[fw, bw],
        out_shape=[out, out],
        scratch_shapes=[state, state],
        compiler_params=_cparams(("parallel", "parallel", "arbitrary")),
        name="rwscan",
    )(r, v, kk, lw0, b0, kd0, r, v, kk, lw1, b1, kd1)


def _hg_streams(streams):
    c, dk = streams[0][0].shape
    revs = [s[5] for s in streams]
    tri = {rev: _tri(c, rev) for rev in set(revs)}
    row = lax.broadcasted_iota(I32, (c, dk), 0)
    ri = lax.broadcasted_iota(I32, (c, c), 0)
    ci = lax.broadcasted_iota(I32, (c, c), 1)

    cum = []
    for (q, k, v, lf, st, rev) in streams:
        hi, mid, lo = _split3(lf)
        cs = jnp.dot(tri[rev], jnp.concatenate([hi, mid, lo], axis=1), preferred_element_type=F32)
        cum.append(cs[:, :dk] + cs[:, dk:2 * dk] + cs[:, 2 * dk:])
    scores = [jnp.where(ri == ci, jnp.sum(s[0] * s[1], axis=1, keepdims=True), 0.0) for s in streams]
    sub = row % SUBLANES
    nt = (((1,), (1,)), ((), ()))

    def sub_bcast(x, idx):
        x3 = x.reshape(c // SUBLANES, SUBLANES, dk)
        return jnp.broadcast_to(x3[:, idx:idx + 1, :], x3.shape).reshape(c, dk)

    h = c // 2
    while h >= 1:
        blk = 2 * h
        upper = (row % blk) >= h
        same_blk = (ri // blk) == (ci // blk)
        r_up = (ri % blk) >= h
        c_up = (ci % blk) >= h
        q_rows = {False: upper, True: jnp.logical_not(upper)}
        sgn = {rev: jnp.where(q_rows[rev], 1.0, -1.0) for rev in (False, True)}
        pmask = {False: jnp.logical_and(same_blk, jnp.logical_and(r_up, jnp.logical_not(c_up))),
                 True: jnp.logical_and(same_blk, jnp.logical_and(jnp.logical_not(r_up), c_up))}
        sl = []
        for i, (q, k, v, lf, st, rev) in enumerate(streams):
            cm = cum[i]
            off = h if rev else h - 1
            if h >= SUBLANES:
                pieces = [jnp.broadcast_to(cm[m0 + off:m0 + off + 1, :], (blk, dk)) for m0 in range(0, c, blk)]
                ref = jnp.concatenate(pieces, axis=0) if len(pieces) > 1 else pieces[0]
            elif blk == SUBLANES:
                ref = sub_bcast(cm, off)
            elif 2 * blk == SUBLANES:
                ref = jnp.where(sub < blk, sub_bcast(cm, off), sub_bcast(cm, off + blk))
            else:
                ref = jnp.where(q_rows[rev], pltpu.roll(cm, c - 1 if rev else 1, axis=0), cm)
            e = jnp.minimum((cm - ref) * sgn[rev], 0.0)
            x = (jnp.where(q_rows[rev], q, k) * jnp.exp(e)).astype(BF16)
            sl.append(lax.dot_general(x, x, nt, preferred_element_type=F32))
        scores = [sc + jnp.where(pmask[rev], x, 0.0) for sc, x, rev in zip(scores, sl, revs)]
        h //= 2

    outs = []
    for (q, k, v, lf, st, rev), cm, sc in zip(streams, cum, scores):
        tot = cm[0:1, :] if rev else cm[c - 1:c, :]
        o = _dot(sc, v) + _dot_nt(q * jnp.exp(cm), st)
        st_new = st * jnp.exp(tot) + _dot_tn(v, k * jnp.exp(tot - cm))
        outs.append((o, st_new))
    return outs


HG_HEADS_PER_STEP = 4


def _hgscan_kernel(dk, qf, lff, i_f, qb, lfb, i_b, of_ref, ob_ref, sf, sb):
    @pl.when(pl.program_id(2) == 0)
    def _():
        sf[...] = jnp.zeros_like(sf)
        sb[...] = jnp.zeros_like(sb)

    streams = []
    for p in range(HG_HEADS_PER_STEP):
        sl = slice(p * dk, (p + 1) * dk)
        for (q_ref, lf_ref, i_ref, st_ref, rev) in ((qf, lff, i_f, sf, False), (qb, lfb, i_b, sb, True)):
            lf = lf_ref[:, sl]
            streams.append((q_ref[:, sl], 1.0 - jnp.exp(lf), i_ref[:, sl], lf, st_ref[p], rev))
    outs = _hg_streams(streams)
    for p in range(HG_HEADS_PER_STEP):
        sl = slice(p * dk, (p + 1) * dk)
        of_ref[:, sl], sf[p] = outs[2 * p]
        ob_ref[:, sl], sb[p] = outs[2 * p + 1]


def _hgscan(nb, seq, z, hg_off, hw):
    t = z.shape[0]
    nc = seq // CHUNK
    dk = hw // HG_HEADS
    wblk = HG_HEADS_PER_STEP * dk
    base = hg_off // wblk
    nh = hw // wblk

    def fw(comp):
        return pl.BlockSpec((CHUNK, wblk), lambda bi, h, c: (bi * nc + c, base + comp * nh + h))

    def bw(comp):
        return pl.BlockSpec((CHUNK, wblk), lambda bi, h, c: (bi * nc + nc - 1 - c, base + comp * nh + h))

    out = jax.ShapeDtypeStruct((t, hw), F32)
    state = pltpu.VMEM((HG_HEADS_PER_STEP, dk, dk), F32)
    return pl.pallas_call(
        functools.partial(_hgscan_kernel, dk),
        grid=(nb, nh, nc),
        in_specs=[fw(0), fw(1), fw(3), bw(0), bw(2), bw(3)],
        out_specs=[pl.BlockSpec((CHUNK, wblk), lambda bi, h, c: (bi * nc + c, h)),
                   pl.BlockSpec((CHUNK, wblk), lambda bi, h, c: (bi * nc + nc - 1 - c, h))],
        out_shape=[out, out],
        scratch_shapes=[state, state],
        compiler_params=_cparams(("parallel", "parallel", "arbitrary")),
        name="hgscan",
    )(z, z, z, z, z, z)


def _blockdiag2(w):
    _, r, n = w.shape
    z = jnp.zeros((r, n), w.dtype)
    return jnp.concatenate([jnp.concatenate([w[0], z], axis=1), jnp.concatenate([z, w[1]], axis=1)], axis=0)


def _indicator(width, seg):
    e = (jnp.arange(width)[:, None] // seg == jnp.arange(width // seg)[None, :]).astype(BF16)
    return e, e.T


def _layer_consts(w_in, rw_mu, rw_w0, rw_w_up, rw_a0, rw_a_up, rw_g_up, rw_k_k, rw_k_a, rw_r_k,
                  hg_lb_gamma):
    rw = rw_k_k.shape[-1]
    d = w_in.shape[1]
    nlat = 2 * W_LORA + 2 * A_LORA + G_LORA
    w = w_in[0]
    rkv = 3 * rw
    hg_cols = w.shape[1] - rkv - nlat
    pad = 512 - nlat
    w_perm = jnp.concatenate([w[:, :rkv], w[:, rkv + nlat:], w[:, rkv:rkv + nlat],
                              jnp.zeros((d, pad), w.dtype)], axis=1).astype(BF16)
    mu = rw_mu[0]
    lower = jnp.cumsum(jax.nn.softmax(hg_lb_gamma.astype(F32), axis=0), axis=0)[0]
    hw = lower.shape[-1]
    e64, e64t = _indicator(rw, RW_HEAD_DIM)
    return dict(
        rw=rw, hw=hw, hg_off=rkv, lat_off=rkv + hg_cols, w_in=w_perm,
        mu_rkv=mu[:rkv].reshape(1, rkv),
        mu_lat=jnp.pad(mu[rkv:rkv + nlat], (0, pad)).reshape(1, 512),
        wup=_blockdiag2(rw_w_up[0]).astype(BF16), aup=_blockdiag2(rw_a_up[0]).astype(BF16),
        gup=rw_g_up[0].astype(BF16),
        w0=rw_w0[0].reshape(1, 2 * rw), a0=rw_a0[0].reshape(1, 2 * rw),
        k_k=rw_k_k[0].reshape(1, rw), k_a=rw_k_a[0].reshape(1, rw), r_k=rw_r_k[0].reshape(1, rw),
        e64=e64, e64t=e64t,
        lb_cols=jnp.zeros((1, w_perm.shape[1]), F32).at[0, rkv + hw:rkv + 3 * hw].set(lower.reshape(-1)),
    )


def _mixer(x2, nb, seq, mod, norm_pre_mix, wc):
    hg, hw = wc['hg_off'], wc['hw']
    z = _inproj(x2, mod, norm_pre_mix, wc['w_in'], seq, wc['lb_cols'], (hg, hg + hw), (hg + hw, hg + 3 * hw))
    (r, v, kk, g, bonus, lw0, lw1, b0, b1, kd0, kd1) = _rwprep(
        z, seq, wc['rw'], wc['lat_off'], wc['mu_rkv'], wc['mu_lat'], wc['wup'], wc['aup'], wc['gup'],
        wc['w0'], wc['a0'], wc['k_k'], wc['k_a'], wc['r_k'], wc['e64'], wc['e64t'])
    rw_of, rw_ob = _rwscan(nb, seq, r, v, kk, lw0, lw1, b0, b1, kd0, kd1)
    hg_of, hg_ob = _hgscan(nb, seq, z, hg, hw)
    return dict(z=z, r=r, v=v, kk=kk, g=g, bonus=bonus, lw0=lw0, rw_of=rw_of, rw_ob=rw_ob,
                hg_of=hg_of, hg_ob=hg_ob)


def _outproj_kernel(rw, x_ref, rf_ref, rb_ref, bon_ref, g_ref, hf_ref, hb_ref, hgg_ref, mod_ref,
                    lnw_ref, lnb_ref, hnw_ref, npm_ref, npf_ref, wout_ref, wrh_ref, wrl_ref,
                    e64_ref, e64t_ref, e128_ref, e128t_ref,
                    x1_ref, h2_ref, lg_ref):
    m6 = mod_ref[0]
    o = rf_ref[...] + rb_ref[...]
    mean = _seg_sum(o, e64_ref[...], e64t_ref[...]) * (1.0 / RW_HEAD_DIM)
    dlt = o - mean
    var = _seg_sum(dlt * dlt, e64_ref[...], e64t_ref[...]) * (1.0 / RW_HEAD_DIM)
    o_rw = (dlt * lax.rsqrt(var + RW_GN_EPS) * lnw_ref[...] + lnb_ref[...] + bon_ref[...]) * g_ref[...]
    oh = hf_ref[...] + hb_ref[...]
    hd = oh.shape[1] // HG_HEADS
    ms = _seg_sum(oh * oh, e128_ref[...], e128t_ref[...]) * (1.0 / hd)
    o_hg = oh * lax.rsqrt(ms + NORM_EPS) * hnw_ref[...] * _silu(hgg_ref[...])
    m = _dot(o_rw, wout_ref[0:rw, :]) + _dot(o_hg, wout_ref[rw:, :])
    x1 = x_ref[...] + m6[2:3] * _rms_rows(m, npm_ref[...])
    x1_ref[...] = x1
    h2 = _rms_rows(x1, npf_ref[...]) * (1.0 + m6[4:5]) + m6[3:4]
    hi, lo = _split2(h2)
    _pack_rows(hi, h2_ref)
    nt = (((1,), (1,)), ((), ()))
    lg_ref[...] = (lax.dot_general(wrh_ref[...], hi, nt, preferred_element_type=F32)
                   + lax.dot_general(wrh_ref[...], lo, nt, preferred_element_type=F32)
                   + lax.dot_general(wrl_ref[...], hi, nt, preferred_element_type=F32))


def _outproj(x2, seq, mx, z, mod, wc, oc):
    t, d = x2.shape
    rw, hw = wc['rw'], wc['hw']
    tm = 256
    gblk = (wc['hg_off'] + 4 * hw) // hw

    def row(w):
        return pl.BlockSpec((tm, w), lambda i: (i, 0))

    def full(a):
        return pl.BlockSpec(a.shape, lambda i: (0,) * a.ndim)

    consts = [oc['ln_w'], oc['ln_b'], oc['hg_norm_w'], oc['npm'], oc['npf'], oc['w_out'], oc['wr_hi'],
              oc['wr_lo'], wc['e64'], wc['e64t'], oc['e128'], oc['e128t']]
    return pl.pallas_call(
        functools.partial(_outproj_kernel, rw),
        grid=(t // tm,),
        in_specs=[row(d), row(rw), row(rw), row(rw), row(rw), row(hw), row(hw),
                  pl.BlockSpec((tm, hw), lambda i: (i, gblk)),
                  pl.BlockSpec((1, 6, d), lambda i: ((i * tm) // seq, 0, 0))] + [full(a) for a in consts],
        out_specs=[row(d), pl.BlockSpec((tm * (d // PACK_W), LANES), lambda i: (i, 0)),
                   pl.BlockSpec((N_EXPERTS, tm), lambda i: (0, i))],
        out_shape=[jax.ShapeDtypeStruct((t, d), F32), jax.ShapeDtypeStruct((t * (d // PACK_W), LANES), jnp.uint32),
                   jax.ShapeDtypeStruct((N_EXPERTS, t), F32)],
        compiler_params=_cparams(("parallel",)),
        name="outproj",
    )(x2, mx['rw_of'], mx['rw_ob'], mx['bonus'], mx['g'], mx['hg_of'], mx['hg_ob'], z, mod, *consts)


ROUTE_TILE = 512


def _route_kernel(lg_ref, bias_ref, ut_ref, eidx_ref, wsel_ref, rank_ref, cnt_ref, carry):
    @pl.when(pl.program_id(0) == 0)
    def _():
        carry[...] = jnp.zeros_like(carry)

    ne, tt = lg_ref.shape
    gsz = ne // N_GROUPS
    neg = -jnp.inf
    s = _sigmoid(lg_ref[...])
    biased = s + bias_ref[...]
    io_g = lax.broadcasted_iota(I32, (gsz, tt), 0)
    gs_rows = []
    for gi in range(N_GROUPS):
        blk = biased[gi * gsz:(gi + 1) * gsz, :]
        m1 = jnp.max(blk, axis=0, keepdims=True)
        first = jnp.min(jnp.where(blk == m1, io_g, gsz), axis=0, keepdims=True)
        m2 = jnp.max(jnp.where(io_g == first, neg, blk), axis=0, keepdims=True)
        gs_rows.append(m1 + m2)
    gs = jnp.concatenate(gs_rows, axis=0)
    io_n = lax.broadcasted_iota(I32, (N_GROUPS, tt), 0)
    selg = jnp.zeros((N_GROUPS, tt), jnp.bool_)
    for _ in range(TOPK_GROUPS):
        m = jnp.max(gs, axis=0, keepdims=True)
        first = jnp.min(jnp.where(gs == m, io_n, N_GROUPS), axis=0, keepdims=True)
        pick = io_n == first
        selg = jnp.logical_or(selg, pick)
        gs = jnp.where(pick, neg, gs)
    emask = jnp.concatenate([jnp.broadcast_to(selg[gi:gi + 1, :], (gsz, tt)) for gi in range(N_GROUPS)],
                            axis=0)
    mb = jnp.where(emask, biased, neg)
    io_e = lax.broadcasted_iota(I32, (ne, tt), 0)
    sel = jnp.zeros((ne, tt), jnp.bool_)
    picks, idxs, ws = [], [], []
    for _ in range(TOP_K):
        m = jnp.max(mb, axis=0, keepdims=True)
        first = jnp.min(jnp.where(mb == m, io_e, ne), axis=0, keepdims=True)
        pick = io_e == first
        picks.append(pick)
        idxs.append(first)
        ws.append(jnp.sum(jnp.where(pick, s, 0.0), axis=0, keepdims=True))
        sel = jnp.logical_or(sel, pick)
        mb = jnp.where(pick, neg, mb)
    wsum = ws[0]
    for w in ws[1:]:
        wsum = wsum + w
    pos = jnp.dot(jnp.where(sel, 1.0, 0.0).astype(BF16), ut_ref[...], preferred_element_type=F32) + carry[...]
    ranks = [jnp.sum(jnp.where(p, pos, 0.0), axis=0, keepdims=True).astype(I32) for p in picks]
    carry[...] = carry[...] + jnp.sum(jnp.where(sel, 1.0, 0.0), axis=1, keepdims=True)
    zi = jnp.zeros((SUBLANES - TOP_K, tt), I32)
    eidx_ref[...] = jnp.concatenate(idxs + [zi], axis=0)
    rank_ref[...] = jnp.concatenate(ranks + [zi], axis=0)
    wsel_ref[...] = jnp.concatenate([w / wsum * ROUTED_SCALE for w in ws] + [zi.astype(F32)], axis=0)
    cnt_ref[...] = jnp.broadcast_to(carry[...], cnt_ref.shape).astype(I32)


def _route(logits_t, e_bias):
    ne, t = logits_t.shape
    tt = ROUTE_TILE
    ut = (jnp.arange(tt)[:, None] < jnp.arange(tt)[None, :]).astype(BF16)
    tok = pl.BlockSpec((SUBLANES, tt), lambda i: (0, i))
    return pl.pallas_call(
        _route_kernel,
        grid=(t // tt,),
        in_specs=[pl.BlockSpec((ne, tt), lambda i: (0, i)),
                  pl.BlockSpec((ne, 1), lambda i: (0, 0)),
                  pl.BlockSpec((tt, tt), lambda i: (0, 0))],
        out_specs=[tok, tok, tok, pl.BlockSpec((ne, LANES), lambda i: (0, 0))],
        out_shape=[jax.ShapeDtypeStruct((SUBLANES, t), I32), jax.ShapeDtypeStruct((SUBLANES, t), F32),
                   jax.ShapeDtypeStruct((SUBLANES, t), I32), jax.ShapeDtypeStruct((ne, LANES), I32)],
        scratch_shapes=[pltpu.VMEM((ne, 1), F32)],
        compiler_params=_cparams(("arbitrary",)),
        name="route",
    )(logits_t, e_bias.reshape(ne, 1), ut)


DISPATCH_TILE = 128
PLAN_TILE = 1024


def _plan_kernel(eidx_ref, rank_ref, ps_ref, dest_ref):
    ne = ps_ref.shape[0]
    tp = eidx_ref.shape[1]
    io_e = lax.broadcasted_iota(I32, (ne, tp), 0)
    ps = ps_ref[...]
    rows = []
    for j in range(TOP_K):
        start = jnp.sum(jnp.where(io_e == eidx_ref[j:j + 1, :], ps, 0.0), axis=0, keepdims=True)
        rows.append(start.astype(I32) + rank_ref[j:j + 1, :])
    dest = jnp.concatenate(rows + [jnp.zeros((SUBLANES - TOP_K, tp), I32)], axis=0)
    for i in range(tp // DISPATCH_TILE):
        dest_ref[i] = dest[:, i * DISPATCH_TILE:(i + 1) * DISPATCH_TILE]


def _plan(eidx, rank, pad_start):
    t = eidx.shape[1]
    tp = PLAN_TILE
    ne = pad_start.shape[0]
    tok = pl.BlockSpec((SUBLANES, tp), lambda i: (0, i))
    ntile = tp // DISPATCH_TILE
    return pl.pallas_call(
        _plan_kernel,
        grid=(t // tp,),
        in_specs=[tok, tok, pl.BlockSpec((ne, 1), lambda i: (0, 0))],
        out_specs=pl.BlockSpec((ntile, SUBLANES, DISPATCH_TILE), lambda i: (i, 0, 0)),
        out_shape=jax.ShapeDtypeStruct((t // DISPATCH_TILE, SUBLANES, DISPATCH_TILE), I32),
        compiler_params=_cparams(("parallel",)),
        name="plan",
    )(eidx, rank, pad_start.astype(F32).reshape(ne, 1))


def _dispatch_kernel(tt, nsteps, zs_ref, zc_ref, nu_ref, dest_ref, h_ref, xs_ref, idx_smem, zero_blk,
                     isem, sem, zsem, bsem):
    i = pl.program_id(0)
    n = tt * TOP_K
    ne = zs_ref.shape[0]
    nblk = xs_ref.shape[0] // EXPERT_BLOCK
    icp = pltpu.make_async_copy(dest_ref.at[pl.ds(i * tt * SUBLANES, n)], idx_smem, isem)
    icp.start()

    @pl.when(i == 0)
    def _():
        zero_blk[...] = jnp.zeros_like(zero_blk)

    icp.wait()

    def row_copy(k):
        return pltpu.make_async_copy(h_ref.at[k % tt], xs_ref.at[idx_smem[k]], sem)

    def drain(k, c):
        row_copy(k).wait()
        return c

    for k in range(n):
        row_copy(k).start(priority=k % 2)

    for q in range(pl.cdiv(ne, nsteps)):
        e = i + q * nsteps
        ec = jnp.minimum(e, ne - 1)
        start = zs_ref[ec]
        count = jnp.where(e < ne, zc_ref[ec], 0)

        def zero_copy(r):
            return pltpu.make_async_copy(zero_blk.at[0], xs_ref.at[start + r], zsem)

        def z_issue(r, c):
            zero_copy(r).start()
            return c

        def z_drain(r, c):
            zero_copy(r).wait()
            return c

        lax.fori_loop(0, count, z_issue, 0)
        lax.fori_loop(0, count, z_drain, 0)

    for q in range(pl.cdiv(nblk, nsteps)):
        blk = nu_ref[0] + i + q * nsteps

        @pl.when(blk < nblk)
        def _():
            bcp = pltpu.make_async_copy(
                zero_blk, xs_ref.at[pl.ds(pl.multiple_of(blk * EXPERT_BLOCK, EXPERT_BLOCK), EXPERT_BLOCK)], bsem)
            bcp.start()
            bcp.wait()

    lax.fori_loop(0, n, drain, 0, unroll=8)


def _dispatch(dest_flat, h2p, p_rows, zero_start, zero_count, n_used):
    t = h2p.shape[0]
    tt = DISPATCH_TILE
    anyspec = pl.BlockSpec(memory_space=pl.ANY)
    dma = pltpu.SemaphoreType.DMA(())
    grid_spec = pltpu.PrefetchScalarGridSpec(
        num_scalar_prefetch=3,
        grid=(t // tt,),
        in_specs=[anyspec, pl.BlockSpec((tt,) + h2p.shape[1:], lambda i, zs, zc, nu: (i, 0, 0))],
        out_specs=anyspec,
        scratch_shapes=[pltpu.SMEM((tt * TOP_K,), I32), pltpu.VMEM((EXPERT_BLOCK,) + h2p.shape[1:], h2p.dtype),
                        dma, dma, dma, dma],
    )
    return pl.pallas_call(
        functools.partial(_dispatch_kernel, tt, t // tt),
        grid_spec=grid_spec,
        out_shape=jax.ShapeDtypeStruct((p_rows,) + h2p.shape[1:], h2p.dtype),
        compiler_params=_cparams(("arbitrary",)),
        name="dispatch",
    )(zero_start, zero_count, n_used, dest_flat, h2p)


def _experts_kernel(be_ref, nu_ref, x_ref, wg_ref, wu_ref, wd_ref, y_ref, wg_s, wu_s, wd_s):
    b = pl.program_id(0)
    live = b < nu_ref[0]
    new_expert = jnp.logical_or(b == 0, be_ref[b] != be_ref[jnp.maximum(b - 1, 0)])

    @pl.when(jnp.logical_and(live, new_expert))
    def _():
        wg_s[...] = wg_ref[0].astype(BF16)
        wu_s[...] = wu_ref[0].astype(BF16)
        wd_s[...] = wd_ref[0].astype(BF16)

    @pl.when(live)
    def _():
        x = _unpack_rows(x_ref, wg_ref.shape[1] // PACK_W)
        gate = jnp.dot(x, wg_s[...], preferred_element_type=F32)
        up = jnp.dot(x, wu_s[...], preferred_element_type=F32)
        _pack_rows(_dot(_silu(gate) * up, wd_s[...]).astype(BF16), y_ref)

    @pl.when(b >= nu_ref[0])
    def _():
        y_ref[...] = jnp.zeros_like(y_ref)


def _experts(block_e, n_used, xs, wg, wu, wd):
    d, de = wg.shape[1], wg.shape[2]
    ns = d // PACK_W
    p_rows = xs.shape[0] // ns
    nblk = p_rows // EXPERT_BLOCK

    def live(b, nu):
        return jnp.minimum(b, jnp.maximum(nu[0] - 1, 0))

    grid_spec = pltpu.PrefetchScalarGridSpec(
        num_scalar_prefetch=2,
        grid=(nblk,),
        in_specs=[pl.BlockSpec((EXPERT_BLOCK * ns, LANES), lambda b, be, nu: (live(b, nu), 0)),
                  pl.BlockSpec((1, d, de), lambda b, be, nu: (be[live(b, nu)], 0, 0)),
                  pl.BlockSpec((1, d, de), lambda b, be, nu: (be[live(b, nu)], 0, 0)),
                  pl.BlockSpec((1, de, d), lambda b, be, nu: (be[live(b, nu)], 0, 0))],
        out_specs=pl.BlockSpec((EXPERT_BLOCK * ns, LANES), lambda b, be, nu: (b, 0)),
        scratch_shapes=[pltpu.VMEM((d, de), BF16), pltpu.VMEM((d, de), BF16), pltpu.VMEM((de, d), BF16)],
    )
    return pl.pallas_call(
        _experts_kernel,
        grid_spec=grid_spec,
        out_shape=jax.ShapeDtypeStruct((p_rows * ns, LANES), jnp.uint32),
        compiler_params=_cparams(("arbitrary",)),
        name="experts",
    )(block_e, n_used, xs, wg, wu, wd)


def _combine_kernel(tt, nsteps, dest_ref, y_ref, x1_ref, h2_ref, w_ref, mod_ref, npf_ref, sg_ref, su_ref, sd_ref,
                    o_ref, ybuf, idx_smem, isem, sems):
    i = pl.program_id(0)
    n = tt * TOP_K
    ns = sg_ref.shape[0] // PACK_W

    def row_copy(slot, k):
        dst = ybuf.at[pl.ds(pl.multiple_of((slot * n + k) * ns, ns), ns)]
        return pltpu.make_async_copy(y_ref.at[idx_smem[slot * n + k]], dst, sems.at[slot])

    def issue(step, slot):
        icp = pltpu.make_async_copy(dest_ref.at[pl.ds(step * tt * SUBLANES, n)],
                                    idx_smem.at[pl.ds(slot * n, n)], isem)
        icp.start()
        icp.wait()
        for k in range(n):
            row_copy(slot, k).start(priority=k % 2)

    def drain(slot):
        def body(k, c):
            row_copy(slot, k).wait()
            return c

        lax.fori_loop(0, n, body, 0, unroll=8)

    @pl.when(i == 0)
    def _():
        issue(0, 0)

    other = (i + 1) % 2
    issue(jnp.minimum(i + 1, nsteps - 1), other)

    h2 = _unpack_rows(h2_ref, ns)
    gate = jnp.dot(h2, sg_ref[...], preferred_element_type=F32)
    up = jnp.dot(h2, su_ref[...], preferred_element_type=F32)
    shared = _dot(_silu(gate) * up, sd_ref[...])
    w = w_ref[...]
    m6 = mod_ref[0]

    def finish(slot):
        drain(slot)
        acc = [None] * (2 * ns)
        for j in range(TOP_K):
            wj = w[:, j:j + 1]
            for s in range(ns):
                word = ybuf[pl.ds((slot * n + j * tt) * ns + s, tt, stride=ns), :]
                parts = (lax.bitcast_convert_type(word & jnp.uint32(0xFFFF0000), F32),
                         lax.bitcast_convert_type(word << 16, F32))
                for h, part in enumerate(parts):
                    term = wj * part
                    acc[2 * s + h] = term if acc[2 * s + h] is None else acc[2 * s + h] + term
        routed = jnp.concatenate(acc, axis=1)
        o_ref[...] = x1_ref[...] + m6[5:6] * _rms_rows(routed + shared, npf_ref[...])

    for slot in (0, 1):
        @pl.when(i % 2 == slot)
        def _():
            finish(slot)

    @pl.when(i == nsteps - 1)
    def _():
        drain(other)


def _combine(dest_flat, y3, x1, h2, wsel_t, mod, seq, npf, sg, su, sd):
    t, d = x1.shape
    tt = DISPATCH_TILE
    ns = d // PACK_W
    n = tt * TOP_K
    anyspec = pl.BlockSpec(memory_space=pl.ANY)

    def full(a):
        return pl.BlockSpec(a.shape, lambda i: (0,) * a.ndim)

    rows = pltpu.VMEM((2 * n * ns, LANES), jnp.uint32)
    return pl.pallas_call(
        functools.partial(_combine_kernel, tt, t // tt),
        grid=(t // tt,),
        in_specs=[anyspec, anyspec,
                  pl.BlockSpec((tt, d), lambda i: (i, 0)),
                  pl.BlockSpec((tt * ns, LANES), lambda i: (i, 0)),
                  pl.BlockSpec((tt, SUBLANES), lambda i: (i, 0)),
                  pl.BlockSpec((1, 6, d), lambda i: ((i * tt) // seq, 0, 0)),
                  full(npf), full(sg), full(su), full(sd)],
        out_specs=pl.BlockSpec((tt, d), lambda i: (i, 0)),
        out_shape=jax.ShapeDtypeStruct((t, d), F32),
        scratch_shapes=[rows, pltpu.SMEM((2 * n,), I32), pltpu.SemaphoreType.DMA(()),
                        pltpu.SemaphoreType.DMA((2,))],
        compiler_params=_cparams(("arbitrary",)),
        name="combine",
    )(dest_flat, y3, x1, h2, wsel_t, mod, npf, sg, su, sd)


def _moe_plan(eidx, rank, cnt, t):
    counts = cnt[:, 0]
    padded = (counts + EXPERT_BLOCK - 1) // EXPERT_BLOCK * EXPERT_BLOCK
    pad_end = jnp.cumsum(padded)
    pad_start = pad_end - padded
    n_blocks = (t * TOP_K + EXPERT_BLOCK - 1) // EXPERT_BLOCK + N_EXPERTS
    first_row = jnp.arange(n_blocks, dtype=I32) * EXPERT_BLOCK
    block_e = jnp.minimum(jnp.sum((pad_end[None, :] <= first_row[:, None]).astype(I32), axis=1),
                          N_EXPERTS - 1).astype(I32)
    n_used = (pad_end[-1:] // EXPERT_BLOCK).astype(I32)
    dest_flat = _plan(eidx, rank, pad_start).reshape(-1)
    zero_rows = ((pad_start + counts).astype(I32), (padded - counts).astype(I32))
    return block_e, n_used, dest_flat, n_blocks * EXPERT_BLOCK, zero_rows


def _trunk(x, mod, wc, oc, ec, norm_pre_mix):
    nb, seq, d = x.shape
    t = nb * seq
    x2 = x.reshape(t, d)
    mx = _mixer(x2, nb, seq, mod, norm_pre_mix, wc)
    x1, h2, logits_t = _outproj(x2, seq, mx, mx['z'], mod, wc, oc)
    eidx, wsel, rank, cnt = _route(logits_t, ec['e_bias'])
    block_e, n_used, dest_flat, p_rows, zero_rows = _moe_plan(eidx, rank, cnt, t)
    ns = d // PACK_W
    xs = _dispatch(dest_flat, h2.reshape(t, ns, LANES), p_rows, *zero_rows, n_used)
    y = _experts(block_e, n_used, xs.reshape(p_rows * ns, LANES), ec['wg'], ec['wu'], ec['wd'])
    out = _combine(dest_flat, y.reshape(p_rows, ns, LANES), x1, h2, wsel.T, mod, seq, oc['npo'],
                   ec['sg'], ec['su'], ec['sd'])
    return out.reshape(nb, seq, d)


def kernel(x_prompt, x_sample, c_prompt, c_sample, w_ada, b_ada, norm_pre_mix, norm_post_mix, norm_pre_ffn, norm_post_ffn, w_in, rw_mu, rw_w0, rw_w_up, rw_a0, rw_a_up, rw_g_up, rw_k_k, rw_k_a, rw_r_k, rw_ln_w, rw_ln_b, hg_lb_gamma, hg_norm_w, w_out, w_router, e_bias, w_exp_gate, w_exp_up, w_exp_down, w_sh_gate, w_sh_up, w_sh_down):
    d = x_prompt.shape[-1]
    wc = _layer_consts(w_in, rw_mu, rw_w0, rw_w_up, rw_a0, rw_a_up, rw_g_up, rw_k_k, rw_k_a, rw_r_k,
                       hg_lb_gamma)
    rw, hw = wc['rw'], wc['hw']
    e128, e128t = _indicator(hw, hw // HG_HEADS)
    wr_hi, wr_lo = _split2(w_router[0].T)
    oc = dict(ln_w=rw_ln_w[0].reshape(1, rw), ln_b=rw_ln_b[0].reshape(1, rw),
              hg_norm_w=hg_norm_w[0].reshape(1, hw), npm=norm_post_mix[0].reshape(1, d),
              npf=norm_pre_ffn[0].reshape(1, d), npo=norm_post_ffn[0].reshape(1, d),
              w_out=w_out[0].astype(BF16), wr_hi=wr_hi, wr_lo=wr_lo, e128=e128, e128t=e128t)
    ec = dict(e_bias=e_bias[0], wg=w_exp_gate[0], wu=w_exp_up[0], wd=w_exp_down[0], sg=w_sh_gate[0].astype(BF16), su=w_sh_up[0].astype(BF16),
              sd=w_sh_down[0].astype(BF16))
    nbp = c_prompt.shape[0]
    mod = _ada(jnp.concatenate([c_prompt, c_sample], axis=0), w_ada[0], b_ada[0]).reshape(-1, 6, d)
    y_prompt = _trunk(x_prompt, mod[:nbp], wc, oc, ec, norm_pre_mix[0])
    y_sample = _trunk(x_sample, mod[nbp:], wc, oc, ec, norm_pre_mix[0])
    return (y_prompt, y_sample)
```

```python
import functools
import math

import jax
import jax.numpy as jnp
from jax import lax
from jax.experimental import pallas as pl
from jax.experimental.pallas import tpu as pltpu

F32 = jnp.float32
BF16 = jnp.bfloat16
I32 = jnp.int32

RW_HEAD_DIM = 64
W_LORA = 64
A_LORA = 64
G_LORA = 128
RW_GN_EPS = 64e-5
HG_HEADS = 8
N_EXPERTS = 64
TOP_K = 6
N_GROUPS = 8
TOPK_GROUPS = 4
ROUTED_SCALE = 2.5
EXPERT_BLOCK = 256
NORM_EPS = 1e-6

LANES = 128
SUBLANES = 8
VMEM_LIMIT = 56 * 1024 * 1024

CHUNK = 64


def _cparams(sem, vmem=VMEM_LIMIT):
    return pltpu.CompilerParams(dimension_semantics=sem, vmem_limit_bytes=vmem)


def _sigmoid(x):
    return 1.0 / (1.0 + jnp.exp(-x))


def _silu(x):
    return x * _sigmoid(x)


def _dot(a, b):
    return jnp.dot(a.astype(BF16), b.astype(BF16), preferred_element_type=F32)


def _dot_nt(a, b):
    return lax.dot_general(a.astype(BF16), b.astype(BF16), (((1,), (1,)), ((), ())),
                           preferred_element_type=F32)


def _dot_tn(a, b):
    return lax.dot_general(a.astype(BF16), b.astype(BF16), (((0,), (0,)), ((), ())),
                           preferred_element_type=F32)


def _split2(x):
    hi = x.astype(BF16)
    lo = (x - hi.astype(F32)).astype(BF16)
    return hi, lo


def _split3(x):
    hi = x.astype(BF16)
    r1 = x - hi.astype(F32)
    mid = r1.astype(BF16)
    lo = (r1 - mid.astype(F32)).astype(BF16)
    return hi, mid, lo


def _seg_sum(x, e, et):
    hi, lo = _split2(x)
    s = jnp.dot(hi, e, preferred_element_type=F32) + jnp.dot(lo, e, preferred_element_type=F32)
    shi, slo = _split2(s)
    return jnp.dot(shi, et, preferred_element_type=F32) + jnp.dot(slo, et, preferred_element_type=F32)


PACK_W = 2 * LANES


def _pack_rows(x_bf, out_ref):
    n, d = x_bf.shape
    ns = d // PACK_W
    for s in range(ns):
        a = lax.bitcast_convert_type(x_bf[:, s * PACK_W:s * PACK_W + LANES].astype(F32), jnp.uint32)
        b = lax.bitcast_convert_type(x_bf[:, s * PACK_W + LANES:(s + 1) * PACK_W].astype(F32), jnp.uint32)
        out_ref[pl.ds(s, n, stride=ns), :] = a | (b >> 16)


def _unpack_rows(x_ref, ns):
    n = x_ref.shape[0] // ns
    parts = []
    for s in range(ns):
        w = x_ref[pl.ds(s, n, stride=ns), :]
        parts.append(lax.bitcast_convert_type(w & jnp.uint32(0xFFFF0000), F32).astype(BF16))
        parts.append(lax.bitcast_convert_type(w << 16, F32).astype(BF16))
    return jnp.concatenate(parts, axis=1)


def _rms_rows(x, g):
    return x * lax.rsqrt(jnp.mean(x * x, axis=-1, keepdims=True) + NORM_EPS) * g


def _ada_kernel(c_ref, w_ref, b_ref, o_ref):
    c = c_ref[...]
    o_ref[...] = _dot(_silu(c), w_ref[...]) + b_ref[...]


def _ada(c, w_ada, b_ada):
    nb, d = c.shape
    n = w_ada.shape[1]
    tn = 512
    return pl.pallas_call(
        _ada_kernel,
        grid=(n // tn,),
        in_specs=[pl.BlockSpec((nb, d), lambda j: (0, 0)),
                  pl.BlockSpec((d, tn), lambda j: (0, j)),
                  pl.BlockSpec((1, tn), lambda j: (0, j))],
        out_specs=pl.BlockSpec((nb, tn), lambda j: (0, j)),
        out_shape=jax.ShapeDtypeStruct((nb, n), F32),
        compiler_params=_cparams(("parallel",)),
        name="ada",
    )(c, w_ada, b_ada.reshape(1, n))


def _inproj_kernel(x_ref, mod_ref, g_ref, w_ref, o_ref, h_scr):
    @pl.when(pl.program_id(1) == 0)
    def _():
        m = mod_ref[0]
        h = _rms_rows(x_ref[...], g_ref[...]) * (1.0 + m[1:2]) + m[0:1]
        h_scr[...] = h.astype(BF16)

    o_ref[...] = jnp.dot(h_scr[...], w_ref[...], preferred_element_type=F32)


def _inproj(x2, mod, g, w_bf, seq):
    t, d = x2.shape
    n = w_bf.shape[1]
    tm, tn = 1024, 512
    return pl.pallas_call(
        _inproj_kernel,
        grid=(t // tm, n // tn),
        in_specs=[pl.BlockSpec((tm, d), lambda i, j: (i, 0)),
                  pl.BlockSpec((1, 6, d), lambda i, j: ((i * tm) // seq, 0, 0)),
                  pl.BlockSpec((1, d), lambda i, j: (0, 0)),
                  pl.BlockSpec((d, tn), lambda i, j: (0, j))],
        out_specs=pl.BlockSpec((tm, tn), lambda i, j: (i, j)),
        out_shape=jax.ShapeDtypeStruct((t, n), F32),
        scratch_shapes=[pltpu.VMEM((tm, d), BF16)],
        compiler_params=_cparams(("parallel", "arbitrary")),
        name="inproj",
    )(x2, mod, g.reshape(1, d), w_bf)


def _rwprep_kernel(seq, tm, rw,
                   z_ref, zp_ref, zn_ref, l_ref, lp_ref, ln_ref,
                   mu_ref, mul_ref, wup_ref, aup_ref, gup_ref, w0_ref, a0_ref,
                   kk_ref, ka_ref, rk_ref, e_ref, et_ref,
                   r_o, v_o, kk_o, g_o, bon_o, lw0_o, lw1_o, b0_o, b1_o, kd0_o, kd1_o):
    i = pl.program_id(0)
    first = (i * tm) % seq == 0
    last = ((i + 1) * tm) % seq == 0

    def shifted(cur, prev_blk, next_blk, mu):
        rows = lax.broadcasted_iota(I32, cur.shape, 0)
        prow = jnp.where(first, 0.0, prev_blk[SUBLANES - 1:SUBLANES, :])
        nrow = jnp.where(last, 0.0, next_blk[0:1, :])
        prev = jnp.where(rows == 0, prow, pltpu.roll(cur, 1, axis=0))
        nxt = jnp.where(rows == tm - 1, nrow, pltpu.roll(cur, tm - 1, axis=0))
        return cur + mu * (0.5 * (prev + nxt) - cur)

    lat = shifted(l_ref[...], lp_ref[...], ln_ref[...], mul_ref[...])
    w_lat = lat[:, 0:2 * W_LORA]
    a_lat = lat[:, 2 * W_LORA:2 * W_LORA + 2 * A_LORA]
    g_lat = lat[:, 2 * W_LORA + 2 * A_LORA:2 * W_LORA + 2 * A_LORA + G_LORA]
    w_raw = _dot(jnp.tanh(w_lat), wup_ref[...]) + w0_ref[...]
    a_all = _sigmoid(_dot(a_lat, aup_ref[...]) + a0_ref[...])
    g_o[...] = _dot(_sigmoid(g_lat), gup_ref[...])
    lw = (-math.exp(-0.5)) * _sigmoid(w_raw)
    lw0_o[...] = lw[:, :rw]
    lw1_o[...] = lw[:, rw:]

    r = shifted(z_ref[:, 0:rw], zp_ref[:, 0:rw], zn_ref[:, 0:rw], mu_ref[:, 0:rw])
    k = shifted(z_ref[:, rw:2 * rw], zp_ref[:, rw:2 * rw], zn_ref[:, rw:2 * rw], mu_ref[:, rw:2 * rw])
    v = shifted(z_ref[:, 2 * rw:3 * rw], zp_ref[:, 2 * rw:3 * rw], zn_ref[:, 2 * rw:3 * rw],
                mu_ref[:, 2 * rw:3 * rw])
    r_o[...] = r
    v_o[...] = v
    kk = k * kk_ref[...]
    ss = _seg_sum(kk * kk, e_ref[...], et_ref[...])
    kk = kk / jnp.maximum(jnp.sqrt(ss), 1e-12)
    kk_o[...] = kk
    a0 = a_all[:, :rw]
    a1 = a_all[:, rw:]
    b0_o[...] = kk * a0
    b1_o[...] = kk * a1
    kd0 = k * (1.0 + (a0 - 1.0) * ka_ref[...])
    kd1 = k * (1.0 + (a1 - 1.0) * ka_ref[...])
    kd0_o[...] = kd0
    kd1_o[...] = kd1
    kb = 0.5 * (kd0 + kd1)
    bon_o[...] = _seg_sum(r * kb * rk_ref[...], e_ref[...], et_ref[...]) * v


def _rwprep(z, seq, rw, lat_off, mu_rkv, mu_lat, wup, aup, gup, w0, a0, k_k, k_a, r_k, e64, e64t):
    t = z.shape[0]
    tm = 256
    nlat = 512
    nrkv = 3 * rw
    tb = tm // SUBLANES
    nblk8 = t // SUBLANES
    lat_blk = lat_off // nlat

    def cur(i):
        return (i, 0)

    def prv(i):
        return (jnp.maximum(i * tb - 1, 0), 0)

    def nxt(i):
        return (jnp.minimum((i + 1) * tb, nblk8 - 1), 0)

    def full(shape):
        return pl.BlockSpec(shape, lambda i: (0,) * len(shape))

    out = jax.ShapeDtypeStruct((t, rw), F32)
    ospec = pl.BlockSpec((tm, rw), lambda i: (i, 0))
    return pl.pallas_call(
        functools.partial(_rwprep_kernel, seq, tm, rw),
        grid=(t // tm,),
        in_specs=[pl.BlockSpec((tm, nrkv), cur),
                  pl.BlockSpec((SUBLANES, nrkv), prv),
                  pl.BlockSpec((SUBLANES, nrkv), nxt),
                  pl.BlockSpec((tm, nlat), lambda i: (i, lat_blk)),
                  pl.BlockSpec((SUBLANES, nlat), lambda i: (jnp.maximum(i * tb - 1, 0), lat_blk)),
                  pl.BlockSpec((SUBLANES, nlat), lambda i: (jnp.minimum((i + 1) * tb, nblk8 - 1), lat_blk)),
                  full((1, nrkv)), full((1, nlat)),
                  full(wup.shape), full(aup.shape), full(gup.shape),
                  full((1, 2 * rw)), full((1, 2 * rw)),
                  full((1, rw)), full((1, rw)), full((1, rw)),
                  full(e64.shape), full(e64t.shape)],
        out_specs=[ospec] * 11,
        out_shape=[out] * 11,
        compiler_params=_cparams(("parallel",)),
        name="rwprep",
    )(z, z, z, z, z, z, mu_rkv, mu_lat, wup, aup, gup, w0, a0, k_k, k_a, r_k, e64, e64t)


def _tri(n, rev):
    i = lax.broadcasted_iota(I32, (n, n), 0)
    j = lax.broadcasted_iota(I32, (n, n), 1)
    m = (j >= i) if rev else (j <= i)
    return jnp.where(m, 1.0, 0.0).astype(BF16)


def _rw_streams(streams):
    c = streams[0][0].shape[0]
    hd = RW_HEAD_DIM
    n2 = 2 * c
    ns = len(streams)
    revs = [s[7] for s in streams]
    lane = lax.broadcasted_iota(I32, (c, LANES), 1)
    head_a = lane < hd
    ri = lax.broadcasted_iota(I32, (n2, n2), 0)
    ci = lax.broadcasted_iota(I32, (n2, n2), 1)
    ti = ri % c
    si = ci % c
    same16 = (ri // 16) == (ci // 16)
    same32 = (ri // 32) == (ci // 32)
    mid32 = jnp.logical_and(same32, jnp.logical_not(same16))
    eye = jnp.where(ri == ci, 1.0, 0.0)
    strict = {False: si < ti, True: si > ti}
    incl = {False: si <= ti, True: si >= ti}
    tri = {rev: _tri(c, rev) for rev in set(revs)}

    def pair(x):
        return jnp.concatenate([jnp.where(head_a, x, 0.0), jnp.where(head_a, 0.0, x)], axis=0)

    cum = []
    for (r, v, kk, lw, b, kd, s_in, rev) in streams:
        hi, lo = _split2(lw)
        cs = jnp.dot(tri[rev], jnp.concatenate([hi, lo], axis=1), preferred_element_type=F32)
        cum.append(cs[:, :LANES] + cs[:, LANES:])
    ops = []
    for (r, v, kk, lw, b, kd, s_in, rev), cm in zip(streams, cum):
        tot = cm[0:1, :] if rev else cm[c - 1:c, :]
        g_inv = jnp.exp(-cm)
        g_tail = jnp.exp(tot - cm)
        ops.append(dict(p2=pair(-kk * jnp.exp(cm - lw)), r2=pair(r * jnp.exp(cm)),
                        bi2=pair(b * g_inv), ki2=pair(kd * g_inv), bt2=pair(b * g_tail),
                        kt2=pair(kd * g_tail), v2=pair(v), g_tot=jnp.exp(tot)))
    gm = [_dot_nt(jnp.concatenate([o['p2'], o['r2']], axis=0), jnp.concatenate([o['bi2'], o['ki2']], axis=0))
          for o in ops]
    a2 = [jnp.where(strict[rev], g[:n2, :n2], 0.0) for g, rev in zip(gm, revs)]
    b2 = [jnp.where(strict[rev], g[:n2, n2:], 0.0) for g, rev in zip(gm, revs)]
    ap2 = [jnp.where(incl[rev], g[n2:, :n2], 0.0) for g, rev in zip(gm, revs)]
    bp2 = [jnp.where(incl[rev], g[n2:, n2:], 0.0) for g, rev in zip(gm, revs)]
    vv = [_dot(jnp.concatenate([x, y], axis=0), o['v2']) for x, y, o in zip(b2, bp2, ops)]
    bv = [x[:n2] for x in vv]
    bpv = [x[n2:] for x in vv]

    x = [jnp.where(same16, a, 0.0) for a in a2]
    tinv = [eye + xi for xi in x]
    for _ in range(3):
        x = [_dot(xi, xi) for xi in x]
        tinv = [t + _dot(t, xi) for t, xi in zip(tinv, x)]
    for lvl in (mid32, jnp.logical_not(same32)):
        y = [_dot(t, jnp.where(lvl, a, 0.0)) for t, a in zip(tinv, a2)]
        tinv = [t + _dot(yi, t) for t, yi in zip(tinv, y)]

    wu = [_dot(t, jnp.concatenate([o['p2'], bvi], axis=1)) for t, o, bvi in zip(tinv, ops, bv)]
    qo = [_dot(a, w) for a, w in zip(ap2, wu)]
    m2 = [_dot_tn(w[:, :LANES], o['bt2']) for w, o in zip(wu, ops)]
    nn2 = [_dot_tn(w[:, LANES:], o['bt2']) + _dot_tn(o['v2'], o['kt2']) for w, o in zip(wu, ops)]

    outs = []
    for i in range(ns):
        s_in = streams[i][6]
        oo = _dot_nt(ops[i]['r2'] + qo[i][:, :LANES], s_in) + qo[i][:, LANES:] + bpv[i]
        s_out = s_in * ops[i]['g_tot'] + _dot(s_in, m2[i]) + nn2[i]
        outs.append((oo[:c] + oo[c:], s_out))
    return outs


RW_PAIRS_PER_STEP = 8


def _rwscan_kernel(rf, vf, kkf, lwf, bf, kdf, rb, vb, kkb, lwb, bb, kdb, of_ref, ob_ref, sf, sb):
    @pl.when(pl.program_id(2) == 0)
    def _():
        sf[...] = jnp.zeros_like(sf)
        sb[...] = jnp.zeros_like(sb)

    streams = []
    for p in range(RW_PAIRS_PER_STEP):
        sl = slice(p * LANES, (p + 1) * LANES)
        streams.append((rf[:, sl], vf[:, sl], kkf[:, sl], lwf[:, sl], bf[:, sl], kdf[:, sl], sf[p], False))
        streams.append((rb[:, sl], vb[:, sl], kkb[:, sl], lwb[:, sl], bb[:, sl], kdb[:, sl], sb[p], True))
    outs = _rw_streams(streams)
    for p in range(RW_PAIRS_PER_STEP):
        sl = slice(p * LANES, (p + 1) * LANES)
        of_ref[:, sl], sf[p] = outs[2 * p]
        ob_ref[:, sl], sb[p] = outs[2 * p + 1]


def _rwscan(nb, seq, r, v, kk, lw0, lw1, b0, b1, kd0, kd1):
    t, rw = r.shape
    nc = seq // CHUNK
    wblk = RW_PAIRS_PER_STEP * LANES
    fw = pl.BlockSpec((CHUNK, wblk), lambda bi, hp, c: (bi * nc + c, hp))
    bw = pl.BlockSpec((CHUNK, wblk), lambda bi, hp, c: (bi * nc + nc - 1 - c, hp))
    out = jax.ShapeDtypeStruct((t, rw), F32)
    state = pltpu.VMEM((RW_PAIRS_PER_STEP, LANES, LANES), F32)
    return pl.pallas_call(
        _rwscan_kernel,
        grid=(nb, rw // wblk, nc),
        in_specs=[fw] * 6 + [bw] * 6,
        out_specs=[fw, bw],
        out_shape=[out, out],
        scratch_shapes=[state, state],
        compiler_params=_cparams(("parallel", "parallel", "arbitrary")),
        name="rwscan",
    )(r, v, kk, lw0, b0, kd0, r, v, kk, lw1, b1, kd1)


def _hg_streams(streams):
    c, dk = streams[0][0].shape
    revs = [s[5] for s in streams]
    tri = {rev: _tri(c, rev) for rev in set(revs)}
    row = lax.broadcasted_iota(I32, (c, dk), 0)
    ri = lax.broadcasted_iota(I32, (c, c), 0)
    ci = lax.broadcasted_iota(I32, (c, c), 1)

    cum = []
    for (q, k, v, lf, st, rev) in streams:
        hi, mid, lo = _split3(lf)
        cs = jnp.dot(tri[rev], jnp.concatenate([hi, mid, lo], axis=1), preferred_element_type=F32)
        cum.append(cs[:, :dk] + cs[:, dk:2 * dk] + cs[:, 2 * dk:])
    scores = [jnp.where(ri == ci, jnp.sum(s[0] * s[1], axis=1, keepdims=True), 0.0) for s in streams]
    sub = row % SUBLANES
    nt = (((1,), (1,)), ((), ()))

    def sub_bcast(x, idx):
        x3 = x.reshape(c // SUBLANES, SUBLANES, dk)
        return jnp.broadcast_to(x3[:, idx:idx + 1, :], x3.shape).reshape(c, dk)

    h = c // 2
    while h >= 1:
        blk = 2 * h
        upper = (row % blk) >= h
        same_blk = (ri // blk) == (ci // blk)
        r_up = (ri % blk) >= h
        c_up = (ci % blk) >= h
        q_rows = {False: upper, True: jnp.logical_not(upper)}
        sgn = {rev: jnp.where(q_rows[rev], 1.0, -1.0) for rev in (False, True)}
        pmask = {False: jnp.logical_and(same_blk, jnp.logical_and(r_up, jnp.logical_not(c_up))),
                 True: jnp.logical_and(same_blk, jnp.logical_and(jnp.logical_not(r_up), c_up))}
        sl = []
        for i, (q, k, v, lf, st, rev) in enumerate(streams):
            cm = cum[i]
            off = h if rev else h - 1
            if h >= SUBLANES:
                pieces = [jnp.broadcast_to(cm[m0 + off:m0 + off + 1, :], (blk, dk)) for m0 in range(0, c, blk)]
                ref = jnp.concatenate(pieces, axis=0) if len(pieces) > 1 else pieces[0]
            elif blk == SUBLANES:
                ref = sub_bcast(cm, off)
            elif 2 * blk == SUBLANES:
                ref = jnp.where(sub < blk, sub_bcast(cm, off), sub_bcast(cm, off + blk))
            else:
                ref = jnp.where(q_rows[rev], pltpu.roll(cm, c - 1 if rev else 1, axis=0), cm)
            e = jnp.minimum((cm - ref) * sgn[rev], 0.0)
            x = (jnp.where(q_rows[rev], q, k) * jnp.exp(e)).astype(BF16)
            sl.append(lax.dot_general(x, x, nt, preferred_element_type=F32))
        scores = [sc + jnp.where(pmask[rev], x, 0.0) for sc, x, rev in zip(scores, sl, revs)]
        h //= 2

    outs = []
    for (q, k, v, lf, st, rev), cm, sc in zip(streams, cum, scores):
        tot = cm[0:1, :] if rev else cm[c - 1:c, :]
        o = _dot(sc, v) + _dot_nt(q * jnp.exp(cm), st)
        st_new = st * jnp.exp(tot) + _dot_tn(v, k * jnp.exp(tot - cm))
        outs.append((o, st_new))
    return outs


HG_HEADS_PER_STEP = 4


def _hgscan_kernel(dk, qf, fff, i_f, qb, ffb, i_b, lb_ref, of_ref, ob_ref, sf, sb):
    @pl.when(pl.program_id(2) == 0)
    def _():
        sf[...] = jnp.zeros_like(sf)
        sb[...] = jnp.zeros_like(sb)

    streams = []
    for p in range(HG_HEADS_PER_STEP):
        sl = slice(p * dk, (p + 1) * dk)
        for (q_ref, ff_ref, i_ref, st_ref, d) in ((qf, fff, i_f, sf, 0), (qb, ffb, i_b, sb, 1)):
            lbv = lb_ref[d:d + 1, sl]
            f = lbv + (1.0 - lbv) * _sigmoid(ff_ref[:, sl])
            streams.append((_silu(q_ref[:, sl]), 1.0 - f, i_ref[:, sl], jnp.log(f), st_ref[p], d == 1))
    outs = _hg_streams(streams)
    for p in range(HG_HEADS_PER_STEP):
        sl = slice(p * dk, (p + 1) * dk)
        of_ref[:, sl], sf[p] = outs[2 * p]
        ob_ref[:, sl], sb[p] = outs[2 * p + 1]


def _hgscan(nb, seq, z, lb, hg_off, hw):
    t = z.shape[0]
    nc = seq // CHUNK
    dk = hw // HG_HEADS
    wblk = HG_HEADS_PER_STEP * dk
    base = hg_off // wblk
    nh = hw // wblk

    def fw(comp):
        return pl.BlockSpec((CHUNK, wblk), lambda bi, h, c: (bi * nc + c, base + comp * nh + h))

    def bw(comp):
        return pl.BlockSpec((CHUNK, wblk), lambda bi, h, c: (bi * nc + nc - 1 - c, base + comp * nh + h))

    out = jax.ShapeDtypeStruct((t, hw), F32)
    state = pltpu.VMEM((HG_HEADS_PER_STEP, dk, dk), F32)
    return pl.pallas_call(
        functools.partial(_hgscan_kernel, dk),
        grid=(nb, nh, nc),
        in_specs=[fw(0), fw(1), fw(3), bw(0), bw(2), bw(3),
                  pl.BlockSpec((2, wblk), lambda bi, h, c: (0, h))],
        out_specs=[pl.BlockSpec((CHUNK, wblk), lambda bi, h, c: (bi * nc + c, h)),
                   pl.BlockSpec((CHUNK, wblk), lambda bi, h, c: (bi * nc + nc - 1 - c, h))],
        out_shape=[out, out],
        scratch_shapes=[state, state],
        compiler_params=_cparams(("parallel", "parallel", "arbitrary")),
        name="hgscan",
    )(z, z, z, z, z, z, lb)


def _blockdiag2(w):
    _, r, n = w.shape
    z = jnp.zeros((r, n), w.dtype)
    return jnp.concatenate([jnp.concatenate([w[0], z], axis=1), jnp.concatenate([z, w[1]], axis=1)], axis=0)


def _indicator(width, seg):
    e = (jnp.arange(width)[:, None] // seg == jnp.arange(width // seg)[None, :]).astype(BF16)
    return e, e.T


def _layer_consts(w_in, rw_mu, rw_w0, rw_w_up, rw_a0, rw_a_up, rw_g_up, rw_k_k, rw_k_a, rw_r_k,
                  hg_lb_gamma):
    rw = rw_k_k.shape[-1]
    d = w_in.shape[1]
    nlat = 2 * W_LORA + 2 * A_LORA + G_LORA
    w = w_in[0]
    rkv = 3 * rw
    hg_cols = w.shape[1] - rkv - nlat
    pad = 512 - nlat
    w_perm = jnp.concatenate([w[:, :rkv], w[:, rkv + nlat:], w[:, rkv:rkv + nlat],
                              jnp.zeros((d, pad), w.dtype)], axis=1).astype(BF16)
    mu = rw_mu[0]
    lower = jnp.cumsum(jax.nn.softmax(hg_lb_gamma.astype(F32), axis=0), axis=0)[0]
    hw = lower.shape[-1]
    e64, e64t = _indicator(rw, RW_HEAD_DIM)
    return dict(
        rw=rw, hw=hw, hg_off=rkv, lat_off=rkv + hg_cols, w_in=w_perm,
        mu_rkv=mu[:rkv].reshape(1, rkv),
        mu_lat=jnp.pad(mu[rkv:rkv + nlat], (0, pad)).reshape(1, 512),
        wup=_blockdiag2(rw_w_up[0]).astype(BF16), aup=_blockdiag2(rw_a_up[0]).astype(BF16),
        gup=rw_g_up[0].astype(BF16),
        w0=rw_w0[0].reshape(1, 2 * rw), a0=rw_a0[0].reshape(1, 2 * rw),
        k_k=rw_k_k[0].reshape(1, rw), k_a=rw_k_a[0].reshape(1, rw), r_k=rw_r_k[0].reshape(1, rw),
        e64=e64, e64t=e64t,
        lb=lower,
    )


def _mixer(x2, nb, seq, mod, norm_pre_mix, wc):
    z = _inproj(x2, mod, norm_pre_mix, wc['w_in'], seq)
    (r, v, kk, g, bonus, lw0, lw1, b0, b1, kd0, kd1) = _rwprep(
        z, seq, wc['rw'], wc['lat_off'], wc['mu_rkv'], wc['mu_lat'], wc['wup'], wc['aup'], wc['gup'],
        wc['w0'], wc['a0'], wc['k_k'], wc['k_a'], wc['r_k'], wc['e64'], wc['e64t'])
    rw_of, rw_ob = _rwscan(nb, seq, r, v, kk, lw0, lw1, b0, b1, kd0, kd1)
    hg_of, hg_ob = _hgscan(nb, seq, z, wc['lb'], wc['hg_off'], wc['hw'])
    return dict(z=z, r=r, v=v, kk=kk, g=g, bonus=bonus, lw0=lw0, rw_of=rw_of, rw_ob=rw_ob,
                hg_of=hg_of, hg_ob=hg_ob)


def _outproj_kernel(rw, x_ref, rf_ref, rb_ref, bon_ref, g_ref, hf_ref, hb_ref, hgg_ref, mod_ref,
                    lnw_ref, lnb_ref, hnw_ref, npm_ref, npf_ref, wout_ref, wrh_ref, wrl_ref,
                    e64_ref, e64t_ref, e128_ref, e128t_ref,
                    x1_ref, h2_ref, lg_ref):
    m6 = mod_ref[0]
    o = rf_ref[...] + rb_ref[...]
    mean = _seg_sum(o, e64_ref[...], e64t_ref[...]) * (1.0 / RW_HEAD_DIM)
    dlt = o - mean
    var = _seg_sum(dlt * dlt, e64_ref[...], e64t_ref[...]) * (1.0 / RW_HEAD_DIM)
    o_rw = (dlt * lax.rsqrt(var + RW_GN_EPS) * lnw_ref[...] + lnb_ref[...] + bon_ref[...]) * g_ref[...]
    oh = hf_ref[...] + hb_ref[...]
    hd = oh.shape[1] // HG_HEADS
    ms = _seg_sum(oh * oh, e128_ref[...], e128t_ref[...]) * (1.0 / hd)
    o_hg = oh * lax.rsqrt(ms + NORM_EPS) * hnw_ref[...] * _silu(hgg_ref[...])
    m = _dot(o_rw, wout_ref[0:rw, :]) + _dot(o_hg, wout_ref[rw:, :])
    x1 = x_ref[...] + m6[2:3] * _rms_rows(m, npm_ref[...])
    x1_ref[...] = x1
    h2 = _rms_rows(x1, npf_ref[...]) * (1.0 + m6[4:5]) + m6[3:4]
    hi, lo = _split2(h2)
    _pack_rows(hi, h2_ref)
    nt = (((1,), (1,)), ((), ()))
    lg_ref[...] = (lax.dot_general(wrh_ref[...], hi, nt, preferred_element_type=F32)
                   + lax.dot_general(wrh_ref[...], lo, nt, preferred_element_type=F32)
                   + lax.dot_general(wrl_ref[...], hi, nt, preferred_element_type=F32))


def _outproj(x2, seq, mx, z, mod, wc, oc):
    t, d = x2.shape
    rw, hw = wc['rw'], wc['hw']
    tm = 256
    gblk = (wc['hg_off'] + 4 * hw) // hw

    def row(w):
        return pl.BlockSpec((tm, w), lambda i: (i, 0))

    def full(a):
        return pl.BlockSpec(a.shape, lambda i: (0,) * a.ndim)

    consts = [oc['ln_w'], oc['ln_b'], oc['hg_norm_w'], oc['npm'], oc['npf'], oc['w_out'], oc['wr_hi'],
              oc['wr_lo'], wc['e64'], wc['e64t'], oc['e128'], oc['e128t']]
    return pl.pallas_call(
        functools.partial(_outproj_kernel, rw),
        grid=(t // tm,),
        in_specs=[row(d), row(rw), row(rw), row(rw), row(rw), row(hw), row(hw),
                  pl.BlockSpec((tm, hw), lambda i: (i, gblk)),
                  pl.BlockSpec((1, 6, d), lambda i: ((i * tm) // seq, 0, 0))] + [full(a) for a in consts],
        out_specs=[row(d), pl.BlockSpec((tm * (d // PACK_W), LANES), lambda i: (i, 0)),
                   pl.BlockSpec((N_EXPERTS, tm), lambda i: (0, i))],
        out_shape=[jax.ShapeDtypeStruct((t, d), F32), jax.ShapeDtypeStruct((t * (d // PACK_W), LANES), jnp.uint32),
                   jax.ShapeDtypeStruct((N_EXPERTS, t), F32)],
        compiler_params=_cparams(("parallel",)),
        name="outproj",
    )(x2, mx['rw_of'], mx['rw_ob'], mx['bonus'], mx['g'], mx['hg_of'], mx['hg_ob'], z, mod, *consts)


ROUTE_TILE = 512


def _route_kernel(lg_ref, bias_ref, ut_ref, eidx_ref, wsel_ref, rank_ref, cnt_ref, carry):
    @pl.when(pl.program_id(0) == 0)
    def _():
        carry[...] = jnp.zeros_like(carry)

    ne, tt = lg_ref.shape
    gsz = ne // N_GROUPS
    neg = -jnp.inf
    s = _sigmoid(lg_ref[...])
    biased = s + bias_ref[...]
    io_g = lax.broadcasted_iota(I32, (gsz, tt), 0)
    gs_rows = []
    for gi in range(N_GROUPS):
        blk = biased[gi * gsz:(gi + 1) * gsz, :]
        m1 = jnp.max(blk, axis=0, keepdims=True)
        first = jnp.min(jnp.where(blk == m1, io_g, gsz), axis=0, keepdims=True)
        m2 = jnp.max(jnp.where(io_g == first, neg, blk), axis=0, keepdims=True)
        gs_rows.append(m1 + m2)
    gs = jnp.concatenate(gs_rows, axis=0)
    io_n = lax.broadcasted_iota(I32, (N_GROUPS, tt), 0)
    selg = jnp.zeros((N_GROUPS, tt), jnp.bool_)
    for _ in range(TOPK_GROUPS):
        m = jnp.max(gs, axis=0, keepdims=True)
        first = jnp.min(jnp.where(gs == m, io_n, N_GROUPS), axis=0, keepdims=True)
        pick = io_n == first
        selg = jnp.logical_or(selg, pick)
        gs = jnp.where(pick, neg, gs)
    emask = jnp.concatenate([jnp.broadcast_to(selg[gi:gi + 1, :], (gsz, tt)) for gi in range(N_GROUPS)],
                            axis=0)
    mb = jnp.where(emask, biased, neg)
    io_e = lax.broadcasted_iota(I32, (ne, tt), 0)
    sel = jnp.zeros((ne, tt), jnp.bool_)
    picks, idxs, ws = [], [], []
    for _ in range(TOP_K):
        m = jnp.max(mb, axis=0, keepdims=True)
        first = jnp.min(jnp.where(mb == m, io_e, ne), axis=0, keepdims=True)
        pick = io_e == first
        picks.append(pick)
        idxs.append(first)
        ws.append(jnp.sum(jnp.where(pick, s, 0.0), axis=0, keepdims=True))
        sel = jnp.logical_or(sel, pick)
        mb = jnp.where(pick, neg, mb)
    wsum = ws[0]
    for w in ws[1:]:
        wsum = wsum + w
    pos = jnp.dot(jnp.where(sel, 1.0, 0.0).astype(BF16), ut_ref[...], preferred_element_type=F32) + carry[...]
    ranks = [jnp.sum(jnp.where(p, pos, 0.0), axis=0, keepdims=True).astype(I32) for p in picks]
    carry[...] = carry[...] + jnp.sum(jnp.where(sel, 1.0, 0.0), axis=1, keepdims=True)
    zi = jnp.zeros((SUBLANES - TOP_K, tt), I32)
    eidx_ref[...] = jnp.concatenate(idxs + [zi], axis=0)
    rank_ref[...] = jnp.concatenate(ranks + [zi], axis=0)
    wsel_ref[...] = jnp.concatenate([w / wsum * ROUTED_SCALE for w in ws] + [zi.astype(F32)], axis=0)
    cnt_ref[...] = jnp.broadcast_to(carry[...], cnt_ref.shape).astype(I32)


def _route(logits_t, e_bias):
    ne, t = logits_t.shape
    tt = ROUTE_TILE
    ut = (jnp.arange(tt)[:, None] < jnp.arange(tt)[None, :]).astype(BF16)
    tok = pl.BlockSpec((SUBLANES, tt), lambda i: (0, i))
    return pl.pallas_call(
        _route_kernel,
        grid=(t // tt,),
        in_specs=[pl.BlockSpec((ne, tt), lambda i: (0, i)),
                  pl.BlockSpec((ne, 1), lambda i: (0, 0)),
                  pl.BlockSpec((tt, tt), lambda i: (0, 0))],
        out_specs=[tok, tok, tok, pl.BlockSpec((ne, LANES), lambda i: (0, 0))],
        out_shape=[jax.ShapeDtypeStruct((SUBLANES, t), I32), jax.ShapeDtypeStruct((SUBLANES, t), F32),
                   jax.ShapeDtypeStruct((SUBLANES, t), I32), jax.ShapeDtypeStruct((ne, LANES), I32)],
        scratch_shapes=[pltpu.VMEM((ne, 1), F32)],
        compiler_params=_cparams(("arbitrary",)),
        name="route",
    )(logits_t, e_bias.reshape(ne, 1), ut)


DISPATCH_TILE = 128
PLAN_TILE = 1024


def _plan_kernel(eidx_ref, rank_ref, ps_ref, dest_ref):
    ne = ps_ref.shape[0]
    tp = eidx_ref.shape[1]
    io_e = lax.broadcasted_iota(I32, (ne, tp), 0)
    ps = ps_ref[...]
    rows = []
    for j in range(TOP_K):
        start = jnp.sum(jnp.where(io_e == eidx_ref[j:j + 1, :], ps, 0.0), axis=0, keepdims=True)
        rows.append(start.astype(I32) + rank_ref[j:j + 1, :])
    dest = jnp.concatenate(rows + [jnp.zeros((SUBLANES - TOP_K, tp), I32)], axis=0)
    for i in range(tp // DISPATCH_TILE):
        dest_ref[i] = dest[:, i * DISPATCH_TILE:(i + 1) * DISPATCH_TILE]


def _plan(eidx, rank, pad_start):
    t = eidx.shape[1]
    tp = PLAN_TILE
    ne = pad_start.shape[0]
    tok = pl.BlockSpec((SUBLANES, tp), lambda i: (0, i))
    ntile = tp // DISPATCH_TILE
    return pl.pallas_call(
        _plan_kernel,
        grid=(t // tp,),
        in_specs=[tok, tok, pl.BlockSpec((ne, 1), lambda i: (0, 0))],
        out_specs=pl.BlockSpec((ntile, SUBLANES, DISPATCH_TILE), lambda i: (i, 0, 0)),
        out_shape=jax.ShapeDtypeStruct((t // DISPATCH_TILE, SUBLANES, DISPATCH_TILE), I32),
        compiler_params=_cparams(("parallel",)),
        name="plan",
    )(eidx, rank, pad_start.astype(F32).reshape(ne, 1))


def _dispatch_kernel(tt, nsteps, zs_ref, zc_ref, nu_ref, dest_ref, h_ref, xs_ref, idx_smem, zero_blk,
                     isem, sem, zsem, bsem):
    i = pl.program_id(0)
    n = tt * TOP_K
    ne = zs_ref.shape[0]
    nblk = xs_ref.shape[0] // EXPERT_BLOCK
    icp = pltpu.make_async_copy(dest_ref.at[pl.ds(i * tt * SUBLANES, n)], idx_smem, isem)
    icp.start()

    @pl.when(i == 0)
    def _():
        zero_blk[...] = jnp.zeros_like(zero_blk)

    icp.wait()

    def row_copy(k):
        return pltpu.make_async_copy(h_ref.at[k % tt], xs_ref.at[idx_smem[k]], sem)

    def drain(k, c):
        row_copy(k).wait()
        return c

    for k in range(n):
        row_copy(k).start(priority=k % 2)

    for q in range(pl.cdiv(ne, nsteps)):
        e = i + q * nsteps
        ec = jnp.minimum(e, ne - 1)
        start = zs_ref[ec]
        count = jnp.where(e < ne, zc_ref[ec], 0)

        def zero_copy(r):
            return pltpu.make_async_copy(zero_blk.at[0], xs_ref.at[start + r], zsem)

        def z_issue(r, c):
            zero_copy(r).start()
            return c

        def z_drain(r, c):
            zero_copy(r).wait()
            return c

        lax.fori_loop(0, count, z_issue, 0)
        lax.fori_loop(0, count, z_drain, 0)

    for q in range(pl.cdiv(nblk, nsteps)):
        blk = nu_ref[0] + i + q * nsteps

        @pl.when(blk < nblk)
        def _():
            bcp = pltpu.make_async_copy(
                zero_blk, xs_ref.at[pl.ds(pl.multiple_of(blk * EXPERT_BLOCK, EXPERT_BLOCK), EXPERT_BLOCK)], bsem)
            bcp.start()
            bcp.wait()

    lax.fori_loop(0, n, drain, 0, unroll=8)


def _dispatch(dest_flat, h2p, p_rows, zero_start, zero_count, n_used):
    t = h2p.shape[0]
    tt = DISPATCH_TILE
    anyspec = pl.BlockSpec(memory_space=pl.ANY)
    dma = pltpu.SemaphoreType.DMA(())
    grid_spec = pltpu.PrefetchScalarGridSpec(
        num_scalar_prefetch=3,
        grid=(t // tt,),
        in_specs=[anyspec, pl.BlockSpec((tt,) + h2p.shape[1:], lambda i, zs, zc, nu: (i, 0, 0))],
        out_specs=anyspec,
        scratch_shapes=[pltpu.SMEM((tt * TOP_K,), I32), pltpu.VMEM((EXPERT_BLOCK,) + h2p.shape[1:], h2p.dtype),
                        dma, dma, dma, dma],
    )
    return pl.pallas_call(
        functools.partial(_dispatch_kernel, tt, t // tt),
        grid_spec=grid_spec,
        out_shape=jax.ShapeDtypeStruct((p_rows,) + h2p.shape[1:], h2p.dtype),
        compiler_params=_cparams(("arbitrary",)),
        name="dispatch",
    )(zero_start, zero_count, n_used, dest_flat, h2p)


def _experts_kernel(be_ref, nu_ref, x_ref, wg_ref, wu_ref, wd_ref, y_ref, wg_s, wu_s, wd_s):
    b = pl.program_id(0)
    live = b < nu_ref[0]
    new_expert = jnp.logical_or(b == 0, be_ref[b] != be_ref[jnp.maximum(b - 1, 0)])

    @pl.when(jnp.logical_and(live, new_expert))
    def _():
        wg_s[...] = wg_ref[0].astype(BF16)
        wu_s[...] = wu_ref[0].astype(BF16)
        wd_s[...] = wd_ref[0].astype(BF16)

    @pl.when(live)
    def _():
        x = _unpack_rows(x_ref, wg_ref.shape[1] // PACK_W)
        gate = jnp.dot(x, wg_s[...], preferred_element_type=F32)
        up = jnp.dot(x, wu_s[...], preferred_element_type=F32)
        _pack_rows(_dot(_silu(gate) * up, wd_s[...]).astype(BF16), y_ref)

    @pl.when(b >= nu_ref[0])
    def _():
        y_ref[...] = jnp.zeros_like(y_ref)


def _experts(block_e, n_used, xs, wg, wu, wd):
    d, de = wg.shape[1], wg.shape[2]
    ns = d // PACK_W
    p_rows = xs.shape[0] // ns
    nblk = p_rows // EXPERT_BLOCK

    def live(b, nu):
        return jnp.minimum(b, jnp.maximum(nu[0] - 1, 0))

    grid_spec = pltpu.PrefetchScalarGridSpec(
        num_scalar_prefetch=2,
        grid=(nblk,),
        in_specs=[pl.BlockSpec((EXPERT_BLOCK * ns, LANES), lambda b, be, nu: (live(b, nu), 0)),
                  pl.BlockSpec((1, d, de), lambda b, be, nu: (be[live(b, nu)], 0, 0)),
                  pl.BlockSpec((1, d, de), lambda b, be, nu: (be[live(b, nu)], 0, 0)),
                  pl.BlockSpec((1, de, d), lambda b, be, nu: (be[live(b, nu)], 0, 0))],
        out_specs=pl.BlockSpec((EXPERT_BLOCK * ns, LANES), lambda b, be, nu: (b, 0)),
        scratch_shapes=[pltpu.VMEM((d, de), BF16), pltpu.VMEM((d, de), BF16), pltpu.VMEM((de, d), BF16)],
    )
    return pl.pallas_call(
        _experts_kernel,
        grid_spec=grid_spec,
        out_shape=jax.ShapeDtypeStruct((p_rows * ns, LANES), jnp.uint32),
        compiler_params=_cparams(("arbitrary",)),
        name="experts",
    )(block_e, n_used, xs, wg, wu, wd)


def _combine_kernel(tt, nsteps, dest_ref, y_ref, x1_ref, h2_ref, w_ref, mod_ref, npf_ref, sg_ref, su_ref, sd_ref,
                    o_ref, ybuf, idx_smem, isem, sems):
    i = pl.program_id(0)
    n = tt * TOP_K
    ns = sg_ref.shape[0] // PACK_W

    def row_copy(slot, k):
        dst = ybuf.at[pl.ds(pl.multiple_of((slot * n + k) * ns, ns), ns)]
        return pltpu.make_async_copy(y_ref.at[idx_smem[slot * n + k]], dst, sems.at[slot])

    def issue(step, slot):
        icp = pltpu.make_async_copy(dest_ref.at[pl.ds(step * tt * SUBLANES, n)],
                                    idx_smem.at[pl.ds(slot * n, n)], isem)
        icp.start()
        icp.wait()
        for k in range(n):
            row_copy(slot, k).start(priority=k % 2)

    def drain(slot):
        def body(k, c):
            row_copy(slot, k).wait()
            return c

        lax.fori_loop(0, n, body, 0, unroll=8)

    @pl.when(i == 0)
    def _():
        issue(0, 0)

    other = (i + 1) % 2
    issue(jnp.minimum(i + 1, nsteps - 1), other)

    h2 = _unpack_rows(h2_ref, ns)
    gate = jnp.dot(h2, sg_ref[...], preferred_element_type=F32)
    up = jnp.dot(h2, su_ref[...], preferred_element_type=F32)
    shared = _dot(_silu(gate) * up, sd_ref[...])
    w = w_ref[...]
    m6 = mod_ref[0]

    def finish(slot):
        drain(slot)
        acc = [None] * (2 * ns)
        for j in range(TOP_K):
            wj = w[:, j:j + 1]
            for s in range(ns):
                word = ybuf[pl.ds((slot * n + j * tt) * ns + s, tt, stride=ns), :]
                parts = (lax.bitcast_convert_type(word & jnp.uint32(0xFFFF0000), F32),
                         lax.bitcast_convert_type(word << 16, F32))
                for h, part in enumerate(parts):
                    term = wj * part
                    acc[2 * s + h] = term if acc[2 * s + h] is None else acc[2 * s + h] + term
        routed = jnp.concatenate(acc, axis=1)
        o_ref[...] = x1_ref[...] + m6[5:6] * _rms_rows(routed + shared, npf_ref[...])

    for slot in (0, 1):
        @pl.when(i % 2 == slot)
        def _():
            finish(slot)

    @pl.when(i == nsteps - 1)
    def _():
        drain(other)


def _combine(dest_flat, y3, x1, h2, wsel_t, mod, seq, npf, sg, su, sd):
    t, d = x1.shape
    tt = DISPATCH_TILE
    ns = d // PACK_W
    n = tt * TOP_K
    anyspec = pl.BlockSpec(memory_space=pl.ANY)

    def full(a):
        return pl.BlockSpec(a.shape, lambda i: (0,) * a.ndim)

    rows = pltpu.VMEM((2 * n * ns, LANES), jnp.uint32)
    return pl.pallas_call(
        functools.partial(_combine_kernel, tt, t // tt),
        grid=(t // tt,),
        in_specs=[anyspec, anyspec,
                  pl.BlockSpec((tt, d), lambda i: (i, 0)),
                  pl.BlockSpec((tt * ns, LANES), lambda i: (i, 0)),
                  pl.BlockSpec((tt, SUBLANES), lambda i: (i, 0)),
                  pl.BlockSpec((1, 6, d), lambda i: ((i * tt) // seq, 0, 0)),
                  full(npf), full(sg), full(su), full(sd)],
        out_specs=pl.BlockSpec((tt, d), lambda i: (i, 0)),
        out_shape=jax.ShapeDtypeStruct((t, d), F32),
        scratch_shapes=[rows, pltpu.SMEM((2 * n,), I32), pltpu.SemaphoreType.DMA(()),
                        pltpu.SemaphoreType.DMA((2,))],
        compiler_params=_cparams(("arbitrary",)),
        name="combine",
    )(dest_flat, y3, x1, h2, wsel_t, mod, npf, sg, su, sd)


def _moe_plan(eidx, rank, cnt, t):
    counts = cnt[:, 0]
    padded = (counts + EXPERT_BLOCK - 1) // EXPERT_BLOCK * EXPERT_BLOCK
    pad_end = jnp.cumsum(padded)
    pad_start = pad_end - padded
    n_blocks = (t * TOP_K + EXPERT_BLOCK - 1) // EXPERT_BLOCK + N_EXPERTS
    first_row = jnp.arange(n_blocks, dtype=I32) * EXPERT_BLOCK
    block_e = jnp.minimum(jnp.sum((pad_end[None, :] <= first_row[:, None]).astype(I32), axis=1),
                          N_EXPERTS - 1).astype(I32)
    n_used = (pad_end[-1:] // EXPERT_BLOCK).astype(I32)
    dest_flat = _plan(eidx, rank, pad_start).reshape(-1)
    zero_rows = ((pad_start + counts).astype(I32), (padded - counts).astype(I32))
    return block_e, n_used, dest_flat, n_blocks * EXPERT_BLOCK, zero_rows


def _trunk(x, mod, wc, oc, ec, norm_pre_mix):
    nb, seq, d = x.shape
    t = nb * seq
    x2 = x.reshape(t, d)
    mx = _mixer(x2, nb, seq, mod, norm_pre_mix, wc)
    x1, h2, logits_t = _outproj(x2, seq, mx, mx['z'], mod, wc, oc)
    eidx, wsel, rank, cnt = _route(logits_t, ec['e_bias'])
    block_e, n_used, dest_flat, p_rows, zero_rows = _moe_plan(eidx, rank, cnt, t)
    ns = d // PACK_W
    xs = _dispatch(dest_flat, h2.reshape(t, ns, LANES), p_rows, *zero_rows, n_used)
    y = _experts(block_e, n_used, xs.reshape(p_rows * ns, LANES), ec['wg'], ec['wu'], ec['wd'])
    out = _combine(dest_flat, y.reshape(p_rows, ns, LANES), x1, h2, wsel.T, mod, seq, oc['npo'],
                   ec['sg'], ec['su'], ec['sd'])
    return out.reshape(nb, seq, d)


def kernel(x_prompt, x_sample, c_prompt, c_sample, w_ada, b_ada, norm_pre_mix, norm_post_mix, norm_pre_ffn, norm_post_ffn, w_in, rw_mu, rw_w0, rw_w_up, rw_a0, rw_a_up, rw_g_up, rw_k_k, rw_k_a, rw_r_k, rw_ln_w, rw_ln_b, hg_lb_gamma, hg_norm_w, w_out, w_router, e_bias, w_exp_gate, w_exp_up, w_exp_down, w_sh_gate, w_sh_up, w_sh_down):
    d = x_prompt.shape[-1]
    wc = _layer_consts(w_in, rw_mu, rw_w0, rw_w_up, rw_a0, rw_a_up, rw_g_up, rw_k_k, rw_k_a, rw_r_k,
                       hg_lb_gamma)
    rw, hw = wc['rw'], wc['hw']
    e128, e128t = _indicator(hw, hw // HG_HEADS)
    wr_hi, wr_lo = _split2(w_router[0].T)
    oc = dict(ln_w=rw_ln_w[0].reshape(1, rw), ln_b=rw_ln_b[0].reshape(1, rw),
              hg_norm_w=hg_norm_w[0].reshape(1, hw), npm=norm_post_mix[0].reshape(1, d),
              npf=norm_pre_ffn[0].reshape(1, d), npo=norm_post_ffn[0].reshape(1, d),
              w_out=w_out[0].astype(BF16), wr_hi=wr_hi, wr_lo=wr_lo, e128=e128, e128t=e128t)
    ec = dict(e_bias=e_bias[0], wg=w_exp_gate[0], wu=w_exp_up[0], wd=w_exp_down[0], sg=w_sh_gate[0].astype(BF16), su=w_sh_up[0].astype(BF16),
              sd=w_sh_down[0].astype(BF16))
    nbp = c_prompt.shape[0]
    mod = _ada(jnp.concatenate([c_prompt, c_sample], axis=0), w_ada[0], b_ada[0]).reshape(-1, 6, d)
    y_prompt = _trunk(x_prompt, mod[:nbp], wc, oc, ec, norm_pre_mix[0])
    y_sample = _trunk(x_sample, mod[nbp:], wc, oc, ec, norm_pre_mix[0])
    return (y_prompt, y_sample)
```

```python
import functools
import math

import jax
import jax.numpy as jnp
from jax import lax
from jax.experimental import pallas as pl
from jax.experimental.pallas import tpu as pltpu

F32 = jnp.float32
BF16 = jnp.bfloat16
I32 = jnp.int32

RW_HEAD_DIM = 64
W_LORA = 64
A_LORA = 64
G_LORA = 128
RW_GN_EPS = 64e-5
HG_HEADS = 8
N_EXPERTS = 64
TOP_K = 6
N_GROUPS = 8
TOPK_GROUPS = 4
ROUTED_SCALE = 2.5
EXPERT_BLOCK = 256
NORM_EPS = 1e-6

LANES = 128
SUBLANES = 8
VMEM_LIMIT = 56 * 1024 * 1024

CHUNK = 64


def _cparams(sem, vmem=VMEM_LIMIT):
    return pltpu.CompilerParams(dimension_semantics=sem, vmem_limit_bytes=vmem)


def _sigmoid(x):
    return 1.0 / (1.0 + jnp.exp(-x))


def _silu(x):
    return x * _sigmoid(x)


def _dot(a, b):
    return jnp.dot(a.astype(BF16), b.astype(BF16), preferred_element_type=F32)


def _dot_nt(a, b):
    return lax.dot_general(a.astype(BF16), b.astype(BF16), (((1,), (1,)), ((), ())),
                           preferred_element_type=F32)


def _dot_tn(a, b):
    return lax.dot_general(a.astype(BF16), b.astype(BF16), (((0,), (0,)), ((), ())),
                           preferred_element_type=F32)


def _split2(x):
    hi = x.astype(BF16)
    lo = (x - hi.astype(F32)).astype(BF16)
    return hi, lo


def _split3(x):
    hi = x.astype(BF16)
    r1 = x - hi.astype(F32)
    mid = r1.astype(BF16)
    lo = (r1 - mid.astype(F32)).astype(BF16)
    return hi, mid, lo


def _seg_sum(x, e, et):
    hi, lo = _split2(x)
    s = jnp.dot(hi, e, preferred_element_type=F32) + jnp.dot(lo, e, preferred_element_type=F32)
    shi, slo = _split2(s)
    return jnp.dot(shi, et, preferred_element_type=F32) + jnp.dot(slo, et, preferred_element_type=F32)


PACK_W = 2 * LANES


def _pack_rows(x_bf, out_ref):
    n, d = x_bf.shape
    ns = d // PACK_W
    for s in range(ns):
        a = lax.bitcast_convert_type(x_bf[:, s * PACK_W:s * PACK_W + LANES].astype(F32), jnp.uint32)
        b = lax.bitcast_convert_type(x_bf[:, s * PACK_W + LANES:(s + 1) * PACK_W].astype(F32), jnp.uint32)
        out_ref[pl.ds(s, n, stride=ns), :] = a | (b >> 16)


def _unpack_rows(x_ref, ns):
    n = x_ref.shape[0] // ns
    parts = []
    for s in range(ns):
        w = x_ref[pl.ds(s, n, stride=ns), :]
        parts.append(lax.bitcast_convert_type(w & jnp.uint32(0xFFFF0000), F32).astype(BF16))
        parts.append(lax.bitcast_convert_type(w << 16, F32).astype(BF16))
    return jnp.concatenate(parts, axis=1)


def _rms_rows(x, g):
    return x * lax.rsqrt(jnp.mean(x * x, axis=-1, keepdims=True) + NORM_EPS) * g


def _ada_kernel(c_ref, w_ref, b_ref, o_ref):
    c = c_ref[...]
    o_ref[...] = _dot(_silu(c), w_ref[...]) + b_ref[...]


def _ada(c, w_ada, b_ada):
    nb, d = c.shape
    n = w_ada.shape[1]
    tn = 512
    return pl.pallas_call(
        _ada_kernel,
        grid=(n // tn,),
        in_specs=[pl.BlockSpec((nb, d), lambda j: (0, 0)),
                  pl.BlockSpec((d, tn), lambda j: (0, j)),
                  pl.BlockSpec((1, tn), lambda j: (0, j))],
        out_specs=pl.BlockSpec((nb, tn), lambda j: (0, j)),
        out_shape=jax.ShapeDtypeStruct((nb, n), F32),
        compiler_params=_cparams(("parallel",)),
        name="ada",
    )(c, w_ada, b_ada.reshape(1, n))


def _inproj_kernel(x_ref, mod_ref, g_ref, w_ref, o_ref, h_scr):
    @pl.when(pl.program_id(1) == 0)
    def _():
        m = mod_ref[0]
        h = _rms_rows(x_ref[...], g_ref[...]) * (1.0 + m[1:2]) + m[0:1]
        h_scr[...] = h.astype(BF16)

    o_ref[...] = jnp.dot(h_scr[...], w_ref[...], preferred_element_type=F32)


def _inproj(x2, mod, g, w_bf, seq):
    t, d = x2.shape
    n = w_bf.shape[1]
    tm, tn = 1024, 512
    return pl.pallas_call(
        _inproj_kernel,
        grid=(t // tm, n // tn),
        in_specs=[pl.BlockSpec((tm, d), lambda i, j: (i, 0)),
                  pl.BlockSpec((1, 6, d), lambda i, j: ((i * tm) // seq, 0, 0)),
                  pl.BlockSpec((1, d), lambda i, j: (0, 0)),
                  pl.BlockSpec((d, tn), lambda i, j: (0, j))],
        out_specs=pl.BlockSpec((tm, tn), lambda i, j: (i, j)),
        out_shape=jax.ShapeDtypeStruct((t, n), F32),
        scratch_shapes=[pltpu.VMEM((tm, d), BF16)],
        compiler_params=_cparams(("parallel", "arbitrary")),
        name="inproj",
    )(x2, mod, g.reshape(1, d), w_bf)


def _rwprep_kernel(seq, tm, rw,
                   z_ref, zp_ref, zn_ref, l_ref, lp_ref, ln_ref,
                   mu_ref, mul_ref, wup_ref, aup_ref, gup_ref, w0_ref, a0_ref,
                   kk_ref, ka_ref, rk_ref, e_ref, et_ref,
                   r_o, v_o, kk_o, g_o, bon_o, lw0_o, lw1_o, b0_o, b1_o, kd0_o, kd1_o):
    i = pl.program_id(0)
    first = (i * tm) % seq == 0
    last = ((i + 1) * tm) % seq == 0

    def shifted(cur, prev_blk, next_blk, mu):
        rows = lax.broadcasted_iota(I32, cur.shape, 0)
        prow = jnp.where(first, 0.0, prev_blk[SUBLANES - 1:SUBLANES, :])
        nrow = jnp.where(last, 0.0, next_blk[0:1, :])
        prev = jnp.where(rows == 0, prow, pltpu.roll(cur, 1, axis=0))
        nxt = jnp.where(rows == tm - 1, nrow, pltpu.roll(cur, tm - 1, axis=0))
        return cur + mu * (0.5 * (prev + nxt) - cur)

    lat = shifted(l_ref[...], lp_ref[...], ln_ref[...], mul_ref[...])
    w_lat = lat[:, 0:2 * W_LORA]
    a_lat = lat[:, 2 * W_LORA:2 * W_LORA + 2 * A_LORA]
    g_lat = lat[:, 2 * W_LORA + 2 * A_LORA:2 * W_LORA + 2 * A_LORA + G_LORA]
    w_raw = _dot(jnp.tanh(w_lat), wup_ref[...]) + w0_ref[...]
    a_all = _sigmoid(_dot(a_lat, aup_ref[...]) + a0_ref[...])
    g_o[...] = _dot(_sigmoid(g_lat), gup_ref[...])
    lw = (-math.exp(-0.5)) * _sigmoid(w_raw)
    lw0_o[...] = lw[:, :rw]
    lw1_o[...] = lw[:, rw:]

    r = shifted(z_ref[:, 0:rw], zp_ref[:, 0:rw], zn_ref[:, 0:rw], mu_ref[:, 0:rw])
    k = shifted(z_ref[:, rw:2 * rw], zp_ref[:, rw:2 * rw], zn_ref[:, rw:2 * rw], mu_ref[:, rw:2 * rw])
    v = shifted(z_ref[:, 2 * rw:3 * rw], zp_ref[:, 2 * rw:3 * rw], zn_ref[:, 2 * rw:3 * rw],
                mu_ref[:, 2 * rw:3 * rw])
    r_o[...] = r
    v_o[...] = v
    kk = k * kk_ref[...]
    ss = _seg_sum(kk * kk, e_ref[...], et_ref[...])
    kk = kk / jnp.maximum(jnp.sqrt(ss), 1e-12)
    kk_o[...] = kk
    a0 = a_all[:, :rw]
    a1 = a_all[:, rw:]
    b0_o[...] = kk * a0
    b1_o[...] = kk * a1
    kd0 = k * (1.0 + (a0 - 1.0) * ka_ref[...])
    kd1 = k * (1.0 + (a1 - 1.0) * ka_ref[...])
    kd0_o[...] = kd0
    kd1_o[...] = kd1
    kb = 0.5 * (kd0 + kd1)
    bon_o[...] = _seg_sum(r * kb * rk_ref[...], e_ref[...], et_ref[...]) * v


def _rwprep(z, seq, rw, lat_off, mu_rkv, mu_lat, wup, aup, gup, w0, a0, k_k, k_a, r_k, e64, e64t):
    t = z.shape[0]
    tm = 256
    nlat = 512
    nrkv = 3 * rw
    tb = tm // SUBLANES
    nblk8 = t // SUBLANES
    lat_blk = lat_off // nlat

    def cur(i):
        return (i, 0)

    def prv(i):
        return (jnp.maximum(i * tb - 1, 0), 0)

    def nxt(i):
        return (jnp.minimum((i + 1) * tb, nblk8 - 1), 0)

    def full(shape):
        return pl.BlockSpec(shape, lambda i: (0,) * len(shape))

    out = jax.ShapeDtypeStruct((t, rw), F32)
    ospec = pl.BlockSpec((tm, rw), lambda i: (i, 0))
    return pl.pallas_call(
        functools.partial(_rwprep_kernel, seq, tm, rw),
        grid=(t // tm,),
        in_specs=[pl.BlockSpec((tm, nrkv), cur),
                  pl.BlockSpec((SUBLANES, nrkv), prv),
                  pl.BlockSpec((SUBLANES, nrkv), nxt),
                  pl.BlockSpec((tm, nlat), lambda i: (i, lat_blk)),
                  pl.BlockSpec((SUBLANES, nlat), lambda i: (jnp.maximum(i * tb - 1, 0), lat_blk)),
                  pl.BlockSpec((SUBLANES, nlat), lambda i: (jnp.minimum((i + 1) * tb, nblk8 - 1), lat_blk)),
                  full((1, nrkv)), full((1, nlat)),
                  full(wup.shape), full(aup.shape), full(gup.shape),
                  full((1, 2 * rw)), full((1, 2 * rw)),
                  full((1, rw)), full((1, rw)), full((1, rw)),
                  full(e64.shape), full(e64t.shape)],
        out_specs=[ospec] * 11,
        out_shape=[out] * 11,
        compiler_params=_cparams(("parallel",)),
        name="rwprep",
    )(z, z, z, z, z, z, mu_rkv, mu_lat, wup, aup, gup, w0, a0, k_k, k_a, r_k, e64, e64t)


def _tri(n, rev):
    i = lax.broadcasted_iota(I32, (n, n), 0)
    j = lax.broadcasted_iota(I32, (n, n), 1)
    m = (j >= i) if rev else (j <= i)
    return jnp.where(m, 1.0, 0.0).astype(BF16)


def _rw_streams(streams):
    c = streams[0][0].shape[0]
    hd = RW_HEAD_DIM
    n2 = 2 * c
    ns = len(streams)
    revs = [s[7] for s in streams]
    lane = lax.broadcasted_iota(I32, (c, LANES), 1)
    head_a = lane < hd
    ri = lax.broadcasted_iota(I32, (n2, n2), 0)
    ci = lax.broadcasted_iota(I32, (n2, n2), 1)
    ti = ri % c
    si = ci % c
    same16 = (ri // 16) == (ci // 16)
    same32 = (ri // 32) == (ci // 32)
    mid32 = jnp.logical_and(same32, jnp.logical_not(same16))
    eye = jnp.where(ri == ci, 1.0, 0.0)
    strict = {False: si < ti, True: si > ti}
    incl = {False: si <= ti, True: si >= ti}
    tri = {rev: _tri(c, rev) for rev in set(revs)}

    def pair(x):
        return jnp.concatenate([jnp.where(head_a, x, 0.0), jnp.where(head_a, 0.0, x)], axis=0)

    cum = []
    for (r, v, kk, lw, b, kd, s_in, rev) in streams:
        hi, lo = _split2(lw)
        cs = jnp.dot(tri[rev], jnp.concatenate([hi, lo], axis=1), preferred_element_type=F32)
        cum.append(cs[:, :LANES] + cs[:, LANES:])
    ops = []
    for (r, v, kk, lw, b, kd, s_in, rev), cm in zip(streams, cum):
        tot = cm[0:1, :] if rev else cm[c - 1:c, :]
        g_inv = jnp.exp(-cm)
        g_tail = jnp.exp(tot - cm)
        ops.append(dict(p2=pair(-kk * jnp.exp(cm - lw)), r2=pair(r * jnp.exp(cm)),
                        bi2=pair(b * g_inv), ki2=pair(kd * g_inv), bt2=pair(b * g_tail),
                        kt2=pair(kd * g_tail), v2=pair(v), g_tot=jnp.exp(tot)))
    gm = [_dot_nt(jnp.concatenate([o['p2'], o['r2']], axis=0), jnp.concatenate([o['bi2'], o['ki2']], axis=0))
          for o in ops]
    a2 = [jnp.where(strict[rev], g[:n2, :n2], 0.0) for g, rev in zip(gm, revs)]
    b2 = [jnp.where(strict[rev], g[:n2, n2:], 0.0) for g, rev in zip(gm, revs)]
    ap2 = [jnp.where(incl[rev], g[n2:, :n2], 0.0) for g, rev in zip(gm, revs)]
    bp2 = [jnp.where(incl[rev], g[n2:, n2:], 0.0) for g, rev in zip(gm, revs)]
    vv = [_dot(jnp.concatenate([x, y], axis=0), o['v2']) for x, y, o in zip(b2, bp2, ops)]
    bv = [x[:n2] for x in vv]
    bpv = [x[n2:] for x in vv]

    x = [jnp.where(same16, a, 0.0) for a in a2]
    tinv = [eye + xi for xi in x]
    for _ in range(3):
        x = [_dot(xi, xi) for xi in x]
        tinv = [t + _dot(t, xi) for t, xi in zip(tinv, x)]
    for lvl in (mid32, jnp.logical_not(same32)):
        y = [_dot(t, jnp.where(lvl, a, 0.0)) for t, a in zip(tinv, a2)]
        tinv = [t + _dot(yi, t) for t, yi in zip(tinv, y)]

    wu = [_dot(t, jnp.concatenate([o['p2'], bvi], axis=1)) for t, o, bvi in zip(tinv, ops, bv)]
    qo = [_dot(a, w) for a, w in zip(ap2, wu)]
    m2 = [_dot_tn(w[:, :LANES], o['bt2']) for w, o in zip(wu, ops)]
    nn2 = [_dot_tn(w[:, LANES:], o['bt2']) + _dot_tn(o['v2'], o['kt2']) for w, o in zip(wu, ops)]

    outs = []
    for i in range(ns):
        s_in = streams[i][6]
        oo = _dot_nt(ops[i]['r2'] + qo[i][:, :LANES], s_in) + qo[i][:, LANES:] + bpv[i]
        s_out = s_in * ops[i]['g_tot'] + _dot(s_in, m2[i]) + nn2[i]
        outs.append((oo[:c] + oo[c:], s_out))
    return outs


RW_PAIRS_PER_STEP = 8


def _rwscan_kernel(rf, vf, kkf, lwf, bf, kdf, rb, vb, kkb, lwb, bb, kdb, of_ref, ob_ref, sf, sb):
    @pl.when(pl.program_id(2) == 0)
    def _():
        sf[...] = jnp.zeros_like(sf)
        sb[...] = jnp.zeros_like(sb)

    streams = []
    for p in range(RW_PAIRS_PER_STEP):
        sl = slice(p * LANES, (p + 1) * LANES)
        streams.append((rf[:, sl], vf[:, sl], kkf[:, sl], lwf[:, sl], bf[:, sl], kdf[:, sl], sf[p], False))
        streams.append((rb[:, sl], vb[:, sl], kkb[:, sl], lwb[:, sl], bb[:, sl], kdb[:, sl], sb[p], True))
    outs = _rw_streams(streams)
    for p in range(RW_PAIRS_PER_STEP):
        sl = slice(p * LANES, (p + 1) * LANES)
        of_ref[:, sl], sf[p] = outs[2 * p]
        ob_ref[:, sl], sb[p] = outs[2 * p + 1]


def _rwscan(nb, seq, r, v, kk, lw0, lw1, b0, b1, kd0, kd1):
    t, rw = r.shape
    nc = seq // CHUNK
    wblk = RW_PAIRS_PER_STEP * LANES
    fw = pl.BlockSpec((CHUNK, wblk), lambda bi, hp, c: (bi * nc + c, hp))
    bw = pl.BlockSpec((CHUNK, wblk), lambda bi, hp, c: (bi * nc + nc - 1 - c, hp))
    out = jax.ShapeDtypeStruct((t, rw), F32)
    state = pltpu.VMEM((RW_PAIRS_PER_STEP, LANES, LANES), F32)
    return pl.pallas_call(
        _rwscan_kernel,
        grid=(nb, rw // wblk, nc),
        in_specs=[fw] * 6 + [bw] * 6,
        out_specs=[fw, bw],
        out_shape=[out, out],
        scratch_shapes=[state, state],
        compiler_params=_cparams(("parallel", "parallel", "arbitrary")),
        name="rwscan",
    )(r, v, kk, lw0, b0, kd0, r, v, kk, lw1, b1, kd1)


def _hg_streams(streams):
    c, dk = streams[0][0].shape
    revs = [s[5] for s in streams]
    tri = {rev: _tri(c, rev) for rev in set(revs)}
    row = lax.broadcasted_iota(I32, (c, dk), 0)
    ri = lax.broadcasted_iota(I32, (c, c), 0)
    ci = lax.broadcasted_iota(I32, (c, c), 1)

    cum = []
    for (q, k, v, lf, st, rev) in streams:
        hi, mid, lo = _split3(lf)
        cs = jnp.dot(tri[rev], jnp.concatenate([hi, mid, lo], axis=1), preferred_element_type=F32)
        cum.append((cs[:, :dk] + cs[:, dk:2 * dk] + cs[:, 2 * dk:]) * math.log2(math.e))
    scores = [jnp.where(ri == ci, jnp.sum(s[0] * s[1], axis=1, keepdims=True), 0.0) for s in streams]
    sub = row % SUBLANES
    nt = (((1,), (1,)), ((), ()))

    def sub_bcast(x, idx):
        x3 = x.reshape(c // SUBLANES, SUBLANES, dk)
        return jnp.broadcast_to(x3[:, idx:idx + 1, :], x3.shape).reshape(c, dk)

    h = c // 2
    while h >= 1:
        blk = 2 * h
        upper = (row % blk) >= h
        same_blk = (ri // blk) == (ci // blk)
        r_up = (ri % blk) >= h
        c_up = (ci % blk) >= h
        q_rows = {False: upper, True: jnp.logical_not(upper)}
        sgn = {rev: jnp.where(q_rows[rev], 1.0, -1.0) for rev in (False, True)}
        pmask = {False: jnp.logical_and(same_blk, jnp.logical_and(r_up, jnp.logical_not(c_up))),
                 True: jnp.logical_and(same_blk, jnp.logical_and(jnp.logical_not(r_up), c_up))}
        sl = []
        for i, (q, k, v, lf, st, rev) in enumerate(streams):
            cm = cum[i]
            off = h if rev else h - 1
            if h >= SUBLANES:
                pieces = [jnp.broadcast_to(cm[m0 + off:m0 + off + 1, :], (blk, dk)) for m0 in range(0, c, blk)]
                ref = jnp.concatenate(pieces, axis=0) if len(pieces) > 1 else pieces[0]
            elif blk == SUBLANES:
                ref = sub_bcast(cm, off)
            elif 2 * blk == SUBLANES:
                ref = jnp.where(sub < blk, sub_bcast(cm, off), sub_bcast(cm, off + blk))
            else:
                ref = jnp.where(q_rows[rev], pltpu.roll(cm, c - 1 if rev else 1, axis=0), cm)
            e = jnp.minimum((cm - ref) * sgn[rev], 0.0)
            x = (jnp.where(q_rows[rev], q, k) * jnp.exp2(e)).astype(BF16)
            sl.append(lax.dot_general(x, x, nt, preferred_element_type=F32))
        scores = [sc + jnp.where(pmask[rev], x, 0.0) for sc, x, rev in zip(scores, sl, revs)]
        h //= 2

    outs = []
    for (q, k, v, lf, st, rev), cm, sc in zip(streams, cum, scores):
        tot = cm[0:1, :] if rev else cm[c - 1:c, :]
        o = _dot(sc, v) + _dot_nt(q * jnp.exp2(cm), st)
        st_new = st * jnp.exp2(tot) + _dot_tn(v, k * jnp.exp2(tot - cm))
        outs.append((o, st_new))
    return outs


HG_HEADS_PER_STEP = 8


def _hgscan_kernel(dk, qf, fff, i_f, qb, ffb, i_b, lb_ref, of_ref, ob_ref, sf, sb):
    @pl.when(pl.program_id(2) == 0)
    def _():
        sf[...] = jnp.zeros_like(sf)
        sb[...] = jnp.zeros_like(sb)

    streams = []
    for p in range(HG_HEADS_PER_STEP):
        sl = slice(p * dk, (p + 1) * dk)
        for (q_ref, ff_ref, i_ref, st_ref, d) in ((qf, fff, i_f, sf, 0), (qb, ffb, i_b, sb, 1)):
            lbv = lb_ref[d:d + 1, sl]
            f = lbv + (1.0 - lbv) * _sigmoid(ff_ref[:, sl])
            streams.append((_silu(q_ref[:, sl]), 1.0 - f, i_ref[:, sl], jnp.log(f), st_ref[p], d == 1))
    outs = _hg_streams(streams)
    for p in range(HG_HEADS_PER_STEP):
        sl = slice(p * dk, (p + 1) * dk)
        of_ref[:, sl], sf[p] = outs[2 * p]
        ob_ref[:, sl], sb[p] = outs[2 * p + 1]


def _hgscan(nb, seq, z, lb, hg_off, hw):
    t = z.shape[0]
    nc = seq // CHUNK
    dk = hw // HG_HEADS
    wblk = HG_HEADS_PER_STEP * dk
    base = hg_off // wblk
    nh = hw // wblk

    def fw(comp):
        return pl.BlockSpec((CHUNK, wblk), lambda bi, h, c: (bi * nc + c, base + comp * nh + h))

    def bw(comp):
        return pl.BlockSpec((CHUNK, wblk), lambda bi, h, c: (bi * nc + nc - 1 - c, base + comp * nh + h))

    out = jax.ShapeDtypeStruct((t, hw), F32)
    state = pltpu.VMEM((HG_HEADS_PER_STEP, dk, dk), F32)
    return pl.pallas_call(
        functools.partial(_hgscan_kernel, dk),
        grid=(nb, nh, nc),
        in_specs=[fw(0), fw(1), fw(3), bw(0), bw(2), bw(3),
                  pl.BlockSpec((2, wblk), lambda bi, h, c: (0, h))],
        out_specs=[pl.BlockSpec((CHUNK, wblk), lambda bi, h, c: (bi * nc + c, h)),
                   pl.BlockSpec((CHUNK, wblk), lambda bi, h, c: (bi * nc + nc - 1 - c, h))],
        out_shape=[out, out],
        scratch_shapes=[state, state],
        compiler_params=_cparams(("parallel", "parallel", "arbitrary")),
        name="hgscan",
    )(z, z, z, z, z, z, lb)


def _blockdiag2(w):
    _, r, n = w.shape
    z = jnp.zeros((r, n), w.dtype)
    return jnp.concatenate([jnp.concatenate([w[0], z], axis=1), jnp.concatenate([z, w[1]], axis=1)], axis=0)


def _indicator(width, seg):
    e = (jnp.arange(width)[:, None] // seg == jnp.arange(width // seg)[None, :]).astype(BF16)
    return e, e.T


def _layer_consts(w_in, rw_mu, rw_w0, rw_w_up, rw_a0, rw_a_up, rw_g_up, rw_k_k, rw_k_a, rw_r_k,
                  hg_lb_gamma):
    rw = rw_k_k.shape[-1]
    d = w_in.shape[1]
    nlat = 2 * W_LORA + 2 * A_LORA + G_LORA
    w = w_in[0]
    rkv = 3 * rw
    hg_cols = w.shape[1] - rkv - nlat
    pad = 512 - nlat
    w_perm = jnp.concatenate([w[:, :rkv], w[:, rkv + nlat:], w[:, rkv:rkv + nlat],
                              jnp.zeros((d, pad), w.dtype)], axis=1).astype(BF16)
    mu = rw_mu[0]
    lower = jnp.cumsum(jax.nn.softmax(hg_lb_gamma.astype(F32), axis=0), axis=0)[0]
    hw = lower.shape[-1]
    e64, e64t = _indicator(rw, RW_HEAD_DIM)
    return dict(
        rw=rw, hw=hw, hg_off=rkv, lat_off=rkv + hg_cols, w_in=w_perm,
        mu_rkv=mu[:rkv].reshape(1, rkv),
        mu_lat=jnp.pad(mu[rkv:rkv + nlat], (0, pad)).reshape(1, 512),
        wup=_blockdiag2(rw_w_up[0]).astype(BF16), aup=_blockdiag2(rw_a_up[0]).astype(BF16),
        gup=rw_g_up[0].astype(BF16),
        w0=rw_w0[0].reshape(1, 2 * rw), a0=rw_a0[0].reshape(1, 2 * rw),
        k_k=rw_k_k[0].reshape(1, rw), k_a=rw_k_a[0].reshape(1, rw), r_k=rw_r_k[0].reshape(1, rw),
        e64=e64, e64t=e64t,
        lb=lower,
    )


def _mixer(x2, nb, seq, mod, norm_pre_mix, wc):
    z = _inproj(x2, mod, norm_pre_mix, wc['w_in'], seq)
    (r, v, kk, g, bonus, lw0, lw1, b0, b1, kd0, kd1) = _rwprep(
        z, seq, wc['rw'], wc['lat_off'], wc['mu_rkv'], wc['mu_lat'], wc['wup'], wc['aup'], wc['gup'],
        wc['w0'], wc['a0'], wc['k_k'], wc['k_a'], wc['r_k'], wc['e64'], wc['e64t'])
    rw_of, rw_ob = _rwscan(nb, seq, r, v, kk, lw0, lw1, b0, b1, kd0, kd1)
    hg_of, hg_ob = _hgscan(nb, seq, z, wc['lb'], wc['hg_off'], wc['hw'])
    return dict(z=z, r=r, v=v, kk=kk, g=g, bonus=bonus, lw0=lw0, rw_of=rw_of, rw_ob=rw_ob,
                hg_of=hg_of, hg_ob=hg_ob)


def _outproj_kernel(rw, x_ref, rf_ref, rb_ref, bon_ref, g_ref, hf_ref, hb_ref, hgg_ref, mod_ref,
                    lnw_ref, lnb_ref, hnw_ref, npm_ref, npf_ref, wout_ref, wrh_ref, wrl_ref,
                    e64_ref, e64t_ref, e128_ref, e128t_ref,
                    x1_ref, h2_ref, lg_ref):
    m6 = mod_ref[0]
    o = rf_ref[...] + rb_ref[...]
    mean = _seg_sum(o, e64_ref[...], e64t_ref[...]) * (1.0 / RW_HEAD_DIM)
    dlt = o - mean
    var = _seg_sum(dlt * dlt, e64_ref[...], e64t_ref[...]) * (1.0 / RW_HEAD_DIM)
    o_rw = (dlt * lax.rsqrt(var + RW_GN_EPS) * lnw_ref[...] + lnb_ref[...] + bon_ref[...]) * g_ref[...]
    oh = hf_ref[...] + hb_ref[...]
    hd = oh.shape[1] // HG_HEADS
    ms = _seg_sum(oh * oh, e128_ref[...], e128t_ref[...]) * (1.0 / hd)
    o_hg = oh * lax.rsqrt(ms + NORM_EPS) * hnw_ref[...] * _silu(hgg_ref[...])
    m = _dot(o_rw, wout_ref[0:rw, :]) + _dot(o_hg, wout_ref[rw:, :])
    x1 = x_ref[...] + m6[2:3] * _rms_rows(m, npm_ref[...])
    x1_ref[...] = x1
    h2 = _rms_rows(x1, npf_ref[...]) * (1.0 + m6[4:5]) + m6[3:4]
    hi, lo = _split2(h2)
    _pack_rows(hi, h2_ref)
    nt = (((1,), (1,)), ((), ()))
    lg_ref[...] = (lax.dot_general(wrh_ref[...], hi, nt, preferred_element_type=F32)
                   + lax.dot_general(wrh_ref[...], lo, nt, preferred_element_type=F32)
                   + lax.dot_general(wrl_ref[...], hi, nt, preferred_element_type=F32))


def _outproj(x2, seq, mx, z, mod, wc, oc):
    t, d = x2.shape
    rw, hw = wc['rw'], wc['hw']
    tm = 256
    gblk = (wc['hg_off'] + 4 * hw) // hw

    def row(w):
        return pl.BlockSpec((tm, w), lambda i: (i, 0))

    def full(a):
        return pl.BlockSpec(a.shape, lambda i: (0,) * a.ndim)

    consts = [oc['ln_w'], oc['ln_b'], oc['hg_norm_w'], oc['npm'], oc['npf'], oc['w_out'], oc['wr_hi'],
              oc['wr_lo'], wc['e64'], wc['e64t'], oc['e128'], oc['e128t']]
    return pl.pallas_call(
        functools.partial(_outproj_kernel, rw),
        grid=(t // tm,),
        in_specs=[row(d), row(rw), row(rw), row(rw), row(rw), row(hw), row(hw),
                  pl.BlockSpec((tm, hw), lambda i: (i, gblk)),
                  pl.BlockSpec((1, 6, d), lambda i: ((i * tm) // seq, 0, 0))] + [full(a) for a in consts],
        out_specs=[row(d), pl.BlockSpec((tm * (d // PACK_W), LANES), lambda i: (i, 0)),
                   pl.BlockSpec((N_EXPERTS, tm), lambda i: (0, i))],
        out_shape=[jax.ShapeDtypeStruct((t, d), F32), jax.ShapeDtypeStruct((t * (d // PACK_W), LANES), jnp.uint32),
                   jax.ShapeDtypeStruct((N_EXPERTS, t), F32)],
        compiler_params=_cparams(("parallel",)),
        name="outproj",
    )(x2, mx['rw_of'], mx['rw_ob'], mx['bonus'], mx['g'], mx['hg_of'], mx['hg_ob'], z, mod, *consts)


ROUTE_TILE = 512


def _route_kernel(lg_ref, bias_ref, ut_ref, eidx_ref, wsel_ref, rank_ref, cnt_ref, carry):
    @pl.when(pl.program_id(0) == 0)
    def _():
        carry[...] = jnp.zeros_like(carry)

    ne, tt = lg_ref.shape
    gsz = ne // N_GROUPS
    neg = -jnp.inf
    s = _sigmoid(lg_ref[...])
    biased = s + bias_ref[...]
    io_g = lax.broadcasted_iota(I32, (gsz, tt), 0)
    gs_rows = []
    for gi in range(N_GROUPS):
        blk = biased[gi * gsz:(gi + 1) * gsz, :]
        m1 = jnp.max(blk, axis=0, keepdims=True)
        first = jnp.min(jnp.where(blk == m1, io_g, gsz), axis=0, keepdims=True)
        m2 = jnp.max(jnp.where(io_g == first, neg, blk), axis=0, keepdims=True)
        gs_rows.append(m1 + m2)
    gs = jnp.concatenate(gs_rows, axis=0)
    io_n = lax.broadcasted_iota(I32, (N_GROUPS, tt), 0)
    selg = jnp.zeros((N_GROUPS, tt), jnp.bool_)
    for _ in range(TOPK_GROUPS):
        m = jnp.max(gs, axis=0, keepdims=True)
        first = jnp.min(jnp.where(gs == m, io_n, N_GROUPS), axis=0, keepdims=True)
        pick = io_n == first
        selg = jnp.logical_or(selg, pick)
        gs = jnp.where(pick, neg, gs)
    emask = jnp.concatenate([jnp.broadcast_to(selg[gi:gi + 1, :], (gsz, tt)) for gi in range(N_GROUPS)],
                            axis=0)
    mb = jnp.where(emask, biased, neg)
    io_e = lax.broadcasted_iota(I32, (ne, tt), 0)
    sel = jnp.zeros((ne, tt), jnp.bool_)
    picks, idxs, ws = [], [], []
    for _ in range(TOP_K):
        m = jnp.max(mb, axis=0, keepdims=True)
        first = jnp.min(jnp.where(mb == m, io_e, ne), axis=0, keepdims=True)
        pick = io_e == first
        picks.append(pick)
        idxs.append(first)
        ws.append(jnp.sum(jnp.where(pick, s, 0.0), axis=0, keepdims=True))
        sel = jnp.logical_or(sel, pick)
        mb = jnp.where(pick, neg, mb)
    wsum = ws[0]
    for w in ws[1:]:
        wsum = wsum + w
    pos = jnp.dot(jnp.where(sel, 1.0, 0.0).astype(BF16), ut_ref[...], preferred_element_type=F32) + carry[...]
    ranks = [jnp.sum(jnp.where(p, pos, 0.0), axis=0, keepdims=True).astype(I32) for p in picks]
    carry[...] = carry[...] + jnp.sum(jnp.where(sel, 1.0, 0.0), axis=1, keepdims=True)
    zi = jnp.zeros((SUBLANES - TOP_K, tt), I32)
    eidx_ref[...] = jnp.concatenate(idxs + [zi], axis=0)
    rank_ref[...] = jnp.concatenate(ranks + [zi], axis=0)
    wsel_ref[...] = jnp.concatenate([w / wsum * ROUTED_SCALE for w in ws] + [zi.astype(F32)], axis=0)
    cnt_ref[...] = jnp.broadcast_to(carry[...], cnt_ref.shape).astype(I32)


def _route(logits_t, e_bias):
    ne, t = logits_t.shape
    tt = ROUTE_TILE
    ut = (jnp.arange(tt)[:, None] < jnp.arange(tt)[None, :]).astype(BF16)
    tok = pl.BlockSpec((SUBLANES, tt), lambda i: (0, i))
    return pl.pallas_call(
        _route_kernel,
        grid=(t // tt,),
        in_specs=[pl.BlockSpec((ne, tt), lambda i: (0, i)),
                  pl.BlockSpec((ne, 1), lambda i: (0, 0)),
                  pl.BlockSpec((tt, tt), lambda i: (0, 0))],
        out_specs=[tok, tok, tok, pl.BlockSpec((ne, LANES), lambda i: (0, 0))],
        out_shape=[jax.ShapeDtypeStruct((SUBLANES, t), I32), jax.ShapeDtypeStruct((SUBLANES, t), F32),
                   jax.ShapeDtypeStruct((SUBLANES, t), I32), jax.ShapeDtypeStruct((ne, LANES), I32)],
        scratch_shapes=[pltpu.VMEM((ne, 1), F32)],
        compiler_params=_cparams(("arbitrary",)),
        name="route",
    )(logits_t, e_bias.reshape(ne, 1), ut)


DISPATCH_TILE = 128
PLAN_TILE = 1024


def _plan_kernel(eidx_ref, rank_ref, ps_ref, dest_ref):
    ne = ps_ref.shape[0]
    tp = eidx_ref.shape[1]
    io_e = lax.broadcasted_iota(I32, (ne, tp), 0)
    ps = ps_ref[...]
    rows = []
    for j in range(TOP_K):
        start = jnp.sum(jnp.where(io_e == eidx_ref[j:j + 1, :], ps, 0.0), axis=0, keepdims=True)
        rows.append(start.astype(I32) + rank_ref[j:j + 1, :])
    dest = jnp.concatenate(rows + [jnp.zeros((SUBLANES - TOP_K, tp), I32)], axis=0)
    for i in range(tp // DISPATCH_TILE):
        dest_ref[i] = dest[:, i * DISPATCH_TILE:(i + 1) * DISPATCH_TILE]


def _plan(eidx, rank, pad_start):
    t = eidx.shape[1]
    tp = PLAN_TILE
    ne = pad_start.shape[0]
    tok = pl.BlockSpec((SUBLANES, tp), lambda i: (0, i))
    ntile = tp // DISPATCH_TILE
    return pl.pallas_call(
        _plan_kernel,
        grid=(t // tp,),
        in_specs=[tok, tok, pl.BlockSpec((ne, 1), lambda i: (0, 0))],
        out_specs=pl.BlockSpec((ntile, SUBLANES, DISPATCH_TILE), lambda i: (i, 0, 0)),
        out_shape=jax.ShapeDtypeStruct((t // DISPATCH_TILE, SUBLANES, DISPATCH_TILE), I32),
        compiler_params=_cparams(("parallel",)),
        name="plan",
    )(eidx, rank, pad_start.astype(F32).reshape(ne, 1))


def _dispatch_kernel(tt, nsteps, zs_ref, zc_ref, nu_ref, dest_ref, h_ref, xs_ref, idx_smem, zero_blk,
                     isem, sem, zsem, bsem):
    i = pl.program_id(0)
    n = tt * TOP_K
    ne = zs_ref.shape[0]
    nblk = xs_ref.shape[0] // EXPERT_BLOCK
    icp = pltpu.make_async_copy(dest_ref.at[pl.ds(i * tt * SUBLANES, n)], idx_smem, isem)
    icp.start()

    @pl.when(i == 0)
    def _():
        zero_blk[...] = jnp.zeros_like(zero_blk)

    icp.wait()

    def row_copy(k):
        return pltpu.make_async_copy(h_ref.at[k % tt], xs_ref.at[idx_smem[k]], sem)

    def drain(k, c):
        row_copy(k).wait()
        return c

    for k in range(n):
        row_copy(k).start(priority=k % 2)

    for q in range(pl.cdiv(ne, nsteps)):
        e = i + q * nsteps
        ec = jnp.minimum(e, ne - 1)
        start = zs_ref[ec]
        count = jnp.where(e < ne, zc_ref[ec], 0)

        def zero_copy(r):
            return pltpu.make_async_copy(zero_blk.at[0], xs_ref.at[start + r], zsem)

        def z_issue(r, c):
            zero_copy(r).start()
            return c

        def z_drain(r, c):
            zero_copy(r).wait()
            return c

        lax.fori_loop(0, count, z_issue, 0)
        lax.fori_loop(0, count, z_drain, 0)

    for q in range(pl.cdiv(nblk, nsteps)):
        blk = nu_ref[0] + i + q * nsteps

        @pl.when(blk < nblk)
        def _():
            bcp = pltpu.make_async_copy(
                zero_blk, xs_ref.at[pl.ds(pl.multiple_of(blk * EXPERT_BLOCK, EXPERT_BLOCK), EXPERT_BLOCK)], bsem)
            bcp.start()
            bcp.wait()

    lax.fori_loop(0, n, drain, 0, unroll=8)


def _dispatch(dest_flat, h2p, p_rows, zero_start, zero_count, n_used):
    t = h2p.shape[0]
    tt = DISPATCH_TILE
    anyspec = pl.BlockSpec(memory_space=pl.ANY)
    dma = pltpu.SemaphoreType.DMA(())
    grid_spec = pltpu.PrefetchScalarGridSpec(
        num_scalar_prefetch=3,
        grid=(t // tt,),
        in_specs=[anyspec, pl.BlockSpec((tt,) + h2p.shape[1:], lambda i, zs, zc, nu: (i, 0, 0))],
        out_specs=anyspec,
        scratch_shapes=[pltpu.SMEM((tt * TOP_K,), I32), pltpu.VMEM((EXPERT_BLOCK,) + h2p.shape[1:], h2p.dtype),
                        dma, dma, dma, dma],
    )
    return pl.pallas_call(
        functools.partial(_dispatch_kernel, tt, t // tt),
        grid_spec=grid_spec,
        out_shape=jax.ShapeDtypeStruct((p_rows,) + h2p.shape[1:], h2p.dtype),
        compiler_params=_cparams(("arbitrary",)),
        name="dispatch",
    )(zero_start, zero_count, n_used, dest_flat, h2p)


def _experts_kernel(be_ref, nu_ref, x_ref, wg_ref, wu_ref, wd_ref, y_ref, wg_s, wu_s, wd_s):
    b = pl.program_id(0)
    live = b < nu_ref[0]
    new_expert = jnp.logical_or(b == 0, be_ref[b] != be_ref[jnp.maximum(b - 1, 0)])

    @pl.when(jnp.logical_and(live, new_expert))
    def _():
        wg_s[...] = wg_ref[0].astype(BF16)
        wu_s[...] = wu_ref[0].astype(BF16)
        wd_s[...] = wd_ref[0].astype(BF16)

    @pl.when(live)
    def _():
        x = _unpack_rows(x_ref, wg_ref.shape[1] // PACK_W)
        gate = jnp.dot(x, wg_s[...], preferred_element_type=F32)
        up = jnp.dot(x, wu_s[...], preferred_element_type=F32)
        _pack_rows(_dot(_silu(gate) * up, wd_s[...]).astype(BF16), y_ref)

    @pl.when(b >= nu_ref[0])
    def _():
        y_ref[...] = jnp.zeros_like(y_ref)


def _experts(block_e, n_used, xs, wg, wu, wd):
    d, de = wg.shape[1], wg.shape[2]
    ns = d // PACK_W
    p_rows = xs.shape[0] // ns
    nblk = p_rows // EXPERT_BLOCK

    def live(b, nu):
        return jnp.minimum(b, jnp.maximum(nu[0] - 1, 0))

    grid_spec = pltpu.PrefetchScalarGridSpec(
        num_scalar_prefetch=2,
        grid=(nblk,),
        in_specs=[pl.BlockSpec((EXPERT_BLOCK * ns, LANES), lambda b, be, nu: (live(b, nu), 0)),
                  pl.BlockSpec((1, d, de), lambda b, be, nu: (be[live(b, nu)], 0, 0)),
                  pl.BlockSpec((1, d, de), lambda b, be, nu: (be[live(b, nu)], 0, 0)),
                  pl.BlockSpec((1, de, d), lambda b, be, nu: (be[live(b, nu)], 0, 0))],
        out_specs=pl.BlockSpec((EXPERT_BLOCK * ns, LANES), lambda b, be, nu: (b, 0)),
        scratch_shapes=[pltpu.VMEM((d, de), BF16), pltpu.VMEM((d, de), BF16), pltpu.VMEM((de, d), BF16)],
    )
    return pl.pallas_call(
        _experts_kernel,
        grid_spec=grid_spec,
        out_shape=jax.ShapeDtypeStruct((p_rows * ns, LANES), jnp.uint32),
        compiler_params=_cparams(("arbitrary",)),
        name="experts",
    )(block_e, n_used, xs, wg, wu, wd)


def _combine_kernel(tt, nsteps, dest_ref, y_ref, x1_ref, h2_ref, w_ref, mod_ref, npf_ref, sg_ref, su_ref, sd_ref,
                    o_ref, ybuf, idx_smem, isem, sems):
    i = pl.program_id(0)
    n = tt * TOP_K
    ns = sg_ref.shape[0] // PACK_W

    def row_copy(slot, k):
        dst = ybuf.at[pl.ds(pl.multiple_of((slot * n + k) * ns, ns), ns)]
        return pltpu.make_async_copy(y_ref.at[idx_smem[slot * n + k]], dst, sems.at[slot])

    def issue(step, slot):
        icp = pltpu.make_async_copy(dest_ref.at[pl.ds(step * tt * SUBLANES, n)],
                                    idx_smem.at[pl.ds(slot * n, n)], isem)
        icp.start()
        icp.wait()
        for k in range(n):
            row_copy(slot, k).start(priority=k % 2)

    def drain(slot):
        def body(k, c):
            row_copy(slot, k).wait()
            return c

        lax.fori_loop(0, n, body, 0, unroll=8)

    @pl.when(i == 0)
    def _():
        issue(0, 0)

    other = (i + 1) % 2
    issue(jnp.minimum(i + 1, nsteps - 1), other)

    h2 = _unpack_rows(h2_ref, ns)
    gate = jnp.dot(h2, sg_ref[...], preferred_element_type=F32)
    up = jnp.dot(h2, su_ref[...], preferred_element_type=F32)
    shared = _dot(_silu(gate) * up, sd_ref[...])
    w = w_ref[...]
    m6 = mod_ref[0]

    def finish(slot):
        drain(slot)
        acc = [None] * (2 * ns)
        for j in range(TOP_K):
            wj = w[:, j:j + 1]
            for s in range(ns):
                word = ybuf[pl.ds((slot * n + j * tt) * ns + s, tt, stride=ns), :]
                parts = (lax.bitcast_convert_type(word & jnp.uint32(0xFFFF0000), F32),
                         lax.bitcast_convert_type(word << 16, F32))
                for h, part in enumerate(parts):
                    term = wj * part
                    acc[2 * s + h] = term if acc[2 * s + h] is None else acc[2 * s + h] + term
        routed = jnp.concatenate(acc, axis=1)
        o_ref[...] = x1_ref[...] + m6[5:6] * _rms_rows(routed + shared, npf_ref[...])

    for slot in (0, 1):
        @pl.when(i % 2 == slot)
        def _():
            finish(slot)

    @pl.when(i == nsteps - 1)
    def _():
        drain(other)


def _combine(dest_flat, y3, x1, h2, wsel_t, mod, seq, npf, sg, su, sd):
    t, d = x1.shape
    tt = DISPATCH_TILE
    ns = d // PACK_W
    n = tt * TOP_K
    anyspec = pl.BlockSpec(memory_space=pl.ANY)

    def full(a):
        return pl.BlockSpec(a.shape, lambda i: (0,) * a.ndim)

    rows = pltpu.VMEM((2 * n * ns, LANES), jnp.uint32)
    return pl.pallas_call(
        functools.partial(_combine_kernel, tt, t // tt),
        grid=(t // tt,),
        in_specs=[anyspec, anyspec,
                  pl.BlockSpec((tt, d), lambda i: (i, 0)),
                  pl.BlockSpec((tt * ns, LANES), lambda i: (i, 0)),
                  pl.BlockSpec((tt, SUBLANES), lambda i: (i, 0)),
                  pl.BlockSpec((1, 6, d), lambda i: ((i * tt) // seq, 0, 0)),
                  full(npf), full(sg), full(su), full(sd)],
        out_specs=pl.BlockSpec((tt, d), lambda i: (i, 0)),
        out_shape=jax.ShapeDtypeStruct((t, d), F32),
        scratch_shapes=[rows, pltpu.SMEM((2 * n,), I32), pltpu.SemaphoreType.DMA(()),
                        pltpu.SemaphoreType.DMA((2,))],
        compiler_params=_cparams(("arbitrary",)),
        name="combine",
    )(dest_flat, y3, x1, h2, wsel_t, mod, npf, sg, su, sd)


def _moe_plan(eidx, rank, cnt, t):
    counts = cnt[:, 0]
    padded = (counts + EXPERT_BLOCK - 1) // EXPERT_BLOCK * EXPERT_BLOCK
    pad_end = jnp.cumsum(padded)
    pad_start = pad_end - padded
    n_blocks = (t * TOP_K + EXPERT_BLOCK - 1) // EXPERT_BLOCK + N_EXPERTS
    first_row = jnp.arange(n_blocks, dtype=I32) * EXPERT_BLOCK
    block_e = jnp.minimum(jnp.sum((pad_end[None, :] <= first_row[:, None]).astype(I32), axis=1),
                          N_EXPERTS - 1).astype(I32)
    n_used = (pad_end[-1:] // EXPERT_BLOCK).astype(I32)
    dest_flat = _plan(eidx, rank, pad_start).reshape(-1)
    zero_rows = ((pad_start + counts).astype(I32), (padded - counts).astype(I32))
    return block_e, n_used, dest_flat, n_blocks * EXPERT_BLOCK, zero_rows


def _trunk(x, mod, wc, oc, ec, norm_pre_mix):
    nb, seq, d = x.shape
    t = nb * seq
    x2 = x.reshape(t, d)
    mx = _mixer(x2, nb, seq, mod, norm_pre_mix, wc)
    x1, h2, logits_t = _outproj(x2, seq, mx, mx['z'], mod, wc, oc)
    eidx, wsel, rank, cnt = _route(logits_t, ec['e_bias'])
    block_e, n_used, dest_flat, p_rows, zero_rows = _moe_plan(eidx, rank, cnt, t)
    ns = d // PACK_W
    xs = _dispatch(dest_flat, h2.reshape(t, ns, LANES), p_rows, *zero_rows, n_used)
    y = _experts(block_e, n_used, xs.reshape(p_rows * ns, LANES), ec['wg'], ec['wu'], ec['wd'])
    out = _combine(dest_flat, y.reshape(p_rows, ns, LANES), x1, h2, wsel.T, mod, seq, oc['npo'],
                   ec['sg'], ec['su'], ec['sd'])
    return out.reshape(nb, seq, d)


def kernel(x_prompt, x_sample, c_prompt, c_sample, w_ada, b_ada, norm_pre_mix, norm_post_mix, norm_pre_ffn, norm_post_ffn, w_in, rw_mu, rw_w0, rw_w_up, rw_a0, rw_a_up, rw_g_up, rw_k_k, rw_k_a, rw_r_k, rw_ln_w, rw_ln_b, hg_lb_gamma, hg_norm_w, w_out, w_router, e_bias, w_exp_gate, w_exp_up, w_exp_down, w_sh_gate, w_sh_up, w_sh_down):
    d = x_prompt.shape[-1]
    wc = _layer_consts(w_in, rw_mu, rw_w0, rw_w_up, rw_a0, rw_a_up, rw_g_up, rw_k_k, rw_k_a, rw_r_k,
                       hg_lb_gamma)
    rw, hw = wc['rw'], wc['hw']
    e128, e128t = _indicator(hw, hw // HG_HEADS)
    wr_hi, wr_lo = _split2(w_router[0].T)
    oc = dict(ln_w=rw_ln_w[0].reshape(1, rw), ln_b=rw_ln_b[0].reshape(1, rw),
              hg_norm_w=hg_norm_w[0].reshape(1, hw), npm=norm_post_mix[0].reshape(1, d),
              npf=norm_pre_ffn[0].reshape(1, d), npo=norm_post_ffn[0].reshape(1, d),
              w_out=w_out[0].astype(BF16), wr_hi=wr_hi, wr_lo=wr_lo, e128=e128, e128t=e128t)
    ec = dict(e_bias=e_bias[0], wg=w_exp_gate[0], wu=w_exp_up[0], wd=w_exp_down[0], sg=w_sh_gate[0].astype(BF16), su=w_sh_up[0].astype(BF16),
              sd=w_sh_down[0].astype(BF16))
    nbp = c_prompt.shape[0]
    mod = _ada(jnp.concatenate([c_prompt, c_sample], axis=0), w_ada[0], b_ada[0]).reshape(-1, 6, d)
    y_prompt = _trunk(x_prompt, mod[:nbp], wc, oc, ec, norm_pre_mix[0])
    y_sample = _trunk(x_sample, mod[nbp:], wc, oc, ec, norm_pre_mix[0])
    return (y_prompt, y_sample)
```

```python
import functools
import math

import jax
import jax.numpy as jnp
from jax import lax
from jax.experimental import pallas as pl
from jax.experimental.pallas import tpu as pltpu

F32 = jnp.float32
BF16 = jnp.bfloat16
I32 = jnp.int32

RW_HEAD_DIM = 64
W_LORA = 64
A_LORA = 64
G_LORA = 128
RW_GN_EPS = 64e-5
HG_HEADS = 8
N_EXPERTS = 64
TOP_K = 6
N_GROUPS = 8
TOPK_GROUPS = 4
ROUTED_SCALE = 2.5
EXPERT_BLOCK = 256
NORM_EPS = 1e-6

LANES = 128
SUBLANES = 8
VMEM_LIMIT = 56 * 1024 * 1024

CHUNK = 64


def _cparams(sem, vmem=VMEM_LIMIT):
    return pltpu.CompilerParams(dimension_semantics=sem, vmem_limit_bytes=vmem)


def _sigmoid(x):
    return 1.0 / (1.0 + jnp.exp(-x))


def _silu(x):
    return x * _sigmoid(x)


def _dot(a, b):
    return jnp.dot(a.astype(BF16), b.astype(BF16), preferred_element_type=F32)


def _dot_nt(a, b):
    return lax.dot_general(a.astype(BF16), b.astype(BF16), (((1,), (1,)), ((), ())),
                           preferred_element_type=F32)


def _dot_tn(a, b):
    return lax.dot_general(a.astype(BF16), b.astype(BF16), (((0,), (0,)), ((), ())),
                           preferred_element_type=F32)


def _split2(x):
    hi = x.astype(BF16)
    lo = (x - hi.astype(F32)).astype(BF16)
    return hi, lo


def _split3(x):
    hi = x.astype(BF16)
    r1 = x - hi.astype(F32)
    mid = r1.astype(BF16)
    lo = (r1 - mid.astype(F32)).astype(BF16)
    return hi, mid, lo


def _seg_sum(x, e, et):
    hi, lo = _split2(x)
    s = jnp.dot(hi, e, preferred_element_type=F32) + jnp.dot(lo, e, preferred_element_type=F32)
    shi, slo = _split2(s)
    return jnp.dot(shi, et, preferred_element_type=F32) + jnp.dot(slo, et, preferred_element_type=F32)


PACK_W = 2 * LANES


def _pack_rows(x_bf, out_ref):
    n, d = x_bf.shape
    ns = d // PACK_W
    for s in range(ns):
        a = lax.bitcast_convert_type(x_bf[:, s * PACK_W:s * PACK_W + LANES].astype(F32), jnp.uint32)
        b = lax.bitcast_convert_type(x_bf[:, s * PACK_W + LANES:(s + 1) * PACK_W].astype(F32), jnp.uint32)
        out_ref[pl.ds(s, n, stride=ns), :] = a | (b >> 16)


def _unpack_rows(x_ref, ns):
    n = x_ref.shape[0] // ns
    parts = []
    for s in range(ns):
        w = x_ref[pl.ds(s, n, stride=ns), :]
        parts.append(lax.bitcast_convert_type(w & jnp.uint32(0xFFFF0000), F32).astype(BF16))
        parts.append(lax.bitcast_convert_type(w << 16, F32).astype(BF16))
    return jnp.concatenate(parts, axis=1)


def _rms_rows(x, g):
    return x * lax.rsqrt(jnp.mean(x * x, axis=-1, keepdims=True) + NORM_EPS) * g


def _ada_kernel(c_ref, w_ref, b_ref, o_ref):
    c = c_ref[...]
    o_ref[...] = _dot(_silu(c), w_ref[...]) + b_ref[...]


def _ada(c, w_ada, b_ada):
    nb, d = c.shape
    n = w_ada.shape[1]
    tn = 512
    return pl.pallas_call(
        _ada_kernel,
        grid=(n // tn,),
        in_specs=[pl.BlockSpec((nb, d), lambda j: (0, 0)),
                  pl.BlockSpec((d, tn), lambda j: (0, j)),
                  pl.BlockSpec((1, tn), lambda j: (0, j))],
        out_specs=pl.BlockSpec((nb, tn), lambda j: (0, j)),
        out_shape=jax.ShapeDtypeStruct((nb, n), F32),
        compiler_params=_cparams(("parallel",)),
        name="ada",
    )(c, w_ada, b_ada.reshape(1, n))


def _inproj_kernel(x_ref, mod_ref, g_ref, w_ref, o_ref, h_scr):
    @pl.when(pl.program_id(1) == 0)
    def _():
        m = mod_ref[0]
        h = _rms_rows(x_ref[...], g_ref[...]) * (1.0 + m[1:2]) + m[0:1]
        h_scr[...] = h.astype(BF16)

    o_ref[...] = jnp.dot(h_scr[...], w_ref[...], preferred_element_type=F32)


def _inproj(x2, mod, g, w_bf, seq):
    t, d = x2.shape
    n = w_bf.shape[1]
    tm, tn = 1024, 512
    return pl.pallas_call(
        _inproj_kernel,
        grid=(t // tm, n // tn),
        in_specs=[pl.BlockSpec((tm, d), lambda i, j: (i, 0)),
                  pl.BlockSpec((1, 6, d), lambda i, j: ((i * tm) // seq, 0, 0)),
                  pl.BlockSpec((1, d), lambda i, j: (0, 0)),
                  pl.BlockSpec((d, tn), lambda i, j: (0, j))],
        out_specs=pl.BlockSpec((tm, tn), lambda i, j: (i, j)),
        out_shape=jax.ShapeDtypeStruct((t, n), F32),
        scratch_shapes=[pltpu.VMEM((tm, d), BF16)],
        compiler_params=_cparams(("parallel", "arbitrary")),
        name="inproj",
    )(x2, mod, g.reshape(1, d), w_bf)


def _rwprep_kernel(seq, tm, rw,
                   z_ref, zp_ref, zn_ref, l_ref, lp_ref, ln_ref,
                   mu_ref, mul_ref, wup_ref, aup_ref, gup_ref, w0_ref, a0_ref,
                   kk_ref, ka_ref, rk_ref, e_ref, et_ref,
                   r_o, v_o, kk_o, g_o, bon_o, lw0_o, lw1_o, b0_o, b1_o, kd0_o, kd1_o):
    i = pl.program_id(0)
    first = (i * tm) % seq == 0
    last = ((i + 1) * tm) % seq == 0

    def shifted(cur, prev_blk, next_blk, mu):
        rows = lax.broadcasted_iota(I32, cur.shape, 0)
        prow = jnp.where(first, 0.0, prev_blk[SUBLANES - 1:SUBLANES, :])
        nrow = jnp.where(last, 0.0, next_blk[0:1, :])
        prev = jnp.where(rows == 0, prow, pltpu.roll(cur, 1, axis=0))
        nxt = jnp.where(rows == tm - 1, nrow, pltpu.roll(cur, tm - 1, axis=0))
        return cur + mu * (0.5 * (prev + nxt) - cur)

    lat = shifted(l_ref[...], lp_ref[...], ln_ref[...], mul_ref[...])
    w_lat = lat[:, 0:2 * W_LORA]
    a_lat = lat[:, 2 * W_LORA:2 * W_LORA + 2 * A_LORA]
    g_lat = lat[:, 2 * W_LORA + 2 * A_LORA:2 * W_LORA + 2 * A_LORA + G_LORA]
    w_raw = _dot(jnp.tanh(w_lat), wup_ref[...]) + w0_ref[...]
    a_all = _sigmoid(_dot(a_lat, aup_ref[...]) + a0_ref[...])
    g_o[...] = _dot(_sigmoid(g_lat), gup_ref[...])
    lw = (-math.exp(-0.5)) * _sigmoid(w_raw)
    lw0_o[...] = lw[:, :rw]
    lw1_o[...] = lw[:, rw:]

    r = shifted(z_ref[:, 0:rw], zp_ref[:, 0:rw], zn_ref[:, 0:rw], mu_ref[:, 0:rw])
    k = shifted(z_ref[:, rw:2 * rw], zp_ref[:, rw:2 * rw], zn_ref[:, rw:2 * rw], mu_ref[:, rw:2 * rw])
    v = shifted(z_ref[:, 2 * rw:3 * rw], zp_ref[:, 2 * rw:3 * rw], zn_ref[:, 2 * rw:3 * rw],
                mu_ref[:, 2 * rw:3 * rw])
    r_o[...] = r
    v_o[...] = v
    kk = k * kk_ref[...]
    ss = _seg_sum(kk * kk, e_ref[...], et_ref[...])
    kk = kk / jnp.maximum(jnp.sqrt(ss), 1e-12)
    kk_o[...] = kk
    a0 = a_all[:, :rw]
    a1 = a_all[:, rw:]
    b0_o[...] = kk * a0
    b1_o[...] = kk * a1
    kd0 = k * (1.0 + (a0 - 1.0) * ka_ref[...])
    kd1 = k * (1.0 + (a1 - 1.0) * ka_ref[...])
    kd0_o[...] = kd0
    kd1_o[...] = kd1
    kb = 0.5 * (kd0 + kd1)
    bon_o[...] = _seg_sum(r * kb * rk_ref[...], e_ref[...], et_ref[...]) * v


def _rwprep(z, seq, rw, lat_off, mu_rkv, mu_lat, wup, aup, gup, w0, a0, k_k, k_a, r_k, e64, e64t):
    t = z.shape[0]
    tm = 256
    nlat = 512
    nrkv = 3 * rw
    tb = tm // SUBLANES
    nblk8 = t // SUBLANES
    lat_blk = lat_off // nlat

    def cur(i):
        return (i, 0)

    def prv(i):
        return (jnp.maximum(i * tb - 1, 0), 0)

    def nxt(i):
        return (jnp.minimum((i + 1) * tb, nblk8 - 1), 0)

    def full(shape):
        return pl.BlockSpec(shape, lambda i: (0,) * len(shape))

    out = jax.ShapeDtypeStruct((t, rw), F32)
    ospec = pl.BlockSpec((tm, rw), lambda i: (i, 0))
    return pl.pallas_call(
        functools.partial(_rwprep_kernel, seq, tm, rw),
        grid=(t // tm,),
        in_specs=[pl.BlockSpec((tm, nrkv), cur),
                  pl.BlockSpec((SUBLANES, nrkv), prv),
                  pl.BlockSpec((SUBLANES, nrkv), nxt),
                  pl.BlockSpec((tm, nlat), lambda i: (i, lat_blk)),
                  pl.BlockSpec((SUBLANES, nlat), lambda i: (jnp.maximum(i * tb - 1, 0), lat_blk)),
                  pl.BlockSpec((SUBLANES, nlat), lambda i: (jnp.minimum((i + 1) * tb, nblk8 - 1), lat_blk)),
                  full((1, nrkv)), full((1, nlat)),
                  full(wup.shape), full(aup.shape), full(gup.shape),
                  full((1, 2 * rw)), full((1, 2 * rw)),
                  full((1, rw)), full((1, rw)), full((1, rw)),
                  full(e64.shape), full(e64t.shape)],
        out_specs=[ospec] * 11,
        out_shape=[out] * 11,
        compiler_params=_cparams(("parallel",)),
        name="rwprep",
    )(z, z, z, z, z, z, mu_rkv, mu_lat, wup, aup, gup, w0, a0, k_k, k_a, r_k, e64, e64t)


def _tri(n, rev):
    i = lax.broadcasted_iota(I32, (n, n), 0)
    j = lax.broadcasted_iota(I32, (n, n), 1)
    m = (j >= i) if rev else (j <= i)
    return jnp.where(m, 1.0, 0.0).astype(BF16)


def _rw_streams(streams):
    c = streams[0][0].shape[0]
    hd = RW_HEAD_DIM
    n2 = 2 * c
    ns = len(streams)
    revs = [s[7] for s in streams]
    lane = lax.broadcasted_iota(I32, (c, LANES), 1)
    head_a = lane < hd
    ri = lax.broadcasted_iota(I32, (n2, n2), 0)
    ci = lax.broadcasted_iota(I32, (n2, n2), 1)
    ti = ri % c
    si = ci % c
    same16 = (ri // 16) == (ci // 16)
    same32 = (ri // 32) == (ci // 32)
    mid32 = jnp.logical_and(same32, jnp.logical_not(same16))
    eye = jnp.where(ri == ci, 1.0, 0.0)
    strict = {False: si < ti, True: si > ti}
    incl = {False: si <= ti, True: si >= ti}
    tri = {rev: _tri(c, rev) for rev in set(revs)}

    def pair(x):
        return jnp.concatenate([jnp.where(head_a, x, 0.0), jnp.where(head_a, 0.0, x)], axis=0)

    cum = []
    for (r, v, kk, lw, b, kd, s_in, rev) in streams:
        hi, lo = _split2(lw)
        cs = jnp.dot(tri[rev], jnp.concatenate([hi, lo], axis=1), preferred_element_type=F32)
        cum.append(cs[:, :LANES] + cs[:, LANES:])
    ops = []
    for (r, v, kk, lw, b, kd, s_in, rev), cm in zip(streams, cum):
        tot = cm[0:1, :] if rev else cm[c - 1:c, :]
        g_inv = jnp.exp(-cm)
        g_tail = jnp.exp(tot - cm)
        ops.append(dict(p2=pair(-kk * jnp.exp(cm - lw)), r2=pair(r * jnp.exp(cm)),
                        bi2=pair(b * g_inv), ki2=pair(kd * g_inv), bt2=pair(b * g_tail),
                        kt2=pair(kd * g_tail), v2=pair(v), g_tot=jnp.exp(tot)))
    gm = [_dot_nt(jnp.concatenate([o['p2'], o['r2']], axis=0), jnp.concatenate([o['bi2'], o['ki2']], axis=0))
          for o in ops]
    a2 = [jnp.where(strict[rev], g[:n2, :n2], 0.0) for g, rev in zip(gm, revs)]
    b2 = [jnp.where(strict[rev], g[:n2, n2:], 0.0) for g, rev in zip(gm, revs)]
    ap2 = [jnp.where(incl[rev], g[n2:, :n2], 0.0) for g, rev in zip(gm, revs)]
    bp2 = [jnp.where(incl[rev], g[n2:, n2:], 0.0) for g, rev in zip(gm, revs)]
    vv = [_dot(jnp.concatenate([x, y], axis=0), o['v2']) for x, y, o in zip(b2, bp2, ops)]
    bv = [x[:n2] for x in vv]
    bpv = [x[n2:] for x in vv]

    x = [jnp.where(same16, a, 0.0) for a in a2]
    tinv = [eye + xi for xi in x]
    for _ in range(3):
        x = [_dot(xi, xi) for xi in x]
        tinv = [t + _dot(t, xi) for t, xi in zip(tinv, x)]
    for lvl in (mid32, jnp.logical_not(same32)):
        y = [_dot(t, jnp.where(lvl, a, 0.0)) for t, a in zip(tinv, a2)]
        tinv = [t + _dot(yi, t) for t, yi in zip(tinv, y)]

    wu = [_dot(t, jnp.concatenate([o['p2'], bvi], axis=1)) for t, o, bvi in zip(tinv, ops, bv)]
    qo = [_dot(a, w) for a, w in zip(ap2, wu)]
    m2 = [_dot_tn(w[:, :LANES], o['bt2']) for w, o in zip(wu, ops)]
    nn2 = [_dot_tn(w[:, LANES:], o['bt2']) + _dot_tn(o['v2'], o['kt2']) for w, o in zip(wu, ops)]

    outs = []
    for i in range(ns):
        s_in = streams[i][6]
        oo = _dot_nt(ops[i]['r2'] + qo[i][:, :LANES], s_in) + qo[i][:, LANES:] + bpv[i]
        s_out = s_in * ops[i]['g_tot'] + _dot(s_in, m2[i]) + nn2[i]
        outs.append((oo[:c] + oo[c:], s_out))
    return outs


RW_PAIRS_PER_STEP = 8


def _rwscan_kernel(rf, vf, kkf, lwf, bf, kdf, rb, vb, kkb, lwb, bb, kdb, of_ref, ob_ref, sf, sb):
    @pl.when(pl.program_id(2) == 0)
    def _():
        sf[...] = jnp.zeros_like(sf)
        sb[...] = jnp.zeros_like(sb)

    streams = []
    for p in range(RW_PAIRS_PER_STEP):
        sl = slice(p * LANES, (p + 1) * LANES)
        streams.append((rf[:, sl], vf[:, sl], kkf[:, sl], lwf[:, sl], bf[:, sl], kdf[:, sl], sf[p], False))
        streams.append((rb[:, sl], vb[:, sl], kkb[:, sl], lwb[:, sl], bb[:, sl], kdb[:, sl], sb[p], True))
    outs = _rw_streams(streams)
    for p in range(RW_PAIRS_PER_STEP):
        sl = slice(p * LANES, (p + 1) * LANES)
        of_ref[:, sl], sf[p] = outs[2 * p]
        ob_ref[:, sl], sb[p] = outs[2 * p + 1]


def _rwscan(nb, seq, r, v, kk, lw0, lw1, b0, b1, kd0, kd1):
    t, rw = r.shape
    nc = seq // CHUNK
    wblk = RW_PAIRS_PER_STEP * LANES
    fw = pl.BlockSpec((CHUNK, wblk), lambda bi, hp, c: (bi * nc + c, hp))
    bw = pl.BlockSpec((CHUNK, wblk), lambda bi, hp, c: (bi * nc + nc - 1 - c, hp))
    out = jax.ShapeDtypeStruct((t, rw), F32)
    state = pltpu.VMEM((RW_PAIRS_PER_STEP, LANES, LANES), F32)
    return pl.pallas_call(
        _rwscan_kernel,
        grid=(nb, rw // wblk, nc),
        in_specs=[fw] * 6 + [bw] * 6,
        out_specs=[fw, bw],
        out_shape=[out, out],
        scratch_shapes=[state, state],
        compiler_params=_cparams(("parallel", "parallel", "arbitrary")),
        name="rwscan",
    )(r, v, kk, lw0, b0, kd0, r, v, kk, lw1, b1, kd1)


def _hg_streams(streams):
    c, dk = streams[0][0].shape
    revs = [s[5] for s in streams]
    tri = {rev: _tri(c, rev) for rev in set(revs)}
    row = lax.broadcasted_iota(I32, (c, dk), 0)
    ri = lax.broadcasted_iota(I32, (c, c), 0)
    ci = lax.broadcasted_iota(I32, (c, c), 1)

    cum = []
    for (q, k, v, lf, st, rev) in streams:
        hi, mid, lo = _split3(lf)
        cs = jnp.dot(tri[rev], jnp.concatenate([hi, mid, lo], axis=1), preferred_element_type=F32)
        cum.append((cs[:, :dk] + cs[:, dk:2 * dk] + cs[:, 2 * dk:]) * math.log2(math.e))
    scores = [jnp.where(ri == ci, jnp.sum(s[0] * s[1], axis=1, keepdims=True), 0.0) for s in streams]
    sub = row % SUBLANES
    nt = (((1,), (1,)), ((), ()))

    def sub_bcast(x, idx):
        x3 = x.reshape(c // SUBLANES, SUBLANES, dk)
        return jnp.broadcast_to(x3[:, idx:idx + 1, :], x3.shape).reshape(c, dk)

    h = c // 2
    while h >= 1:
        blk = 2 * h
        upper = (row % blk) >= h
        same_blk = (ri // blk) == (ci // blk)
        r_up = (ri % blk) >= h
        c_up = (ci % blk) >= h
        q_rows = {False: upper, True: jnp.logical_not(upper)}
        sgn = {rev: jnp.where(q_rows[rev], 1.0, -1.0) for rev in (False, True)}
        pmask = {False: jnp.logical_and(same_blk, jnp.logical_and(r_up, jnp.logical_not(c_up))),
                 True: jnp.logical_and(same_blk, jnp.logical_and(jnp.logical_not(r_up), c_up))}
        sl = []
        for i, (q, k, v, lf, st, rev) in enumerate(streams):
            cm = cum[i]
            off = h if rev else h - 1
            if h >= SUBLANES:
                pieces = [jnp.broadcast_to(cm[m0 + off:m0 + off + 1, :], (blk, dk)) for m0 in range(0, c, blk)]
                ref = jnp.concatenate(pieces, axis=0) if len(pieces) > 1 else pieces[0]
            elif blk == SUBLANES:
                ref = sub_bcast(cm, off)
            elif 2 * blk == SUBLANES:
                ref = jnp.where(sub < blk, sub_bcast(cm, off), sub_bcast(cm, off + blk))
            else:
                ref = jnp.where(q_rows[rev], pltpu.roll(cm, c - 1 if rev else 1, axis=0), cm)
            e = jnp.minimum((cm - ref) * sgn[rev], 0.0)
            x = (jnp.where(q_rows[rev], q, k) * jnp.exp2(e)).astype(BF16)
            sl.append(lax.dot_general(x, x, nt, preferred_element_type=F32))
        scores = [sc + jnp.where(pmask[rev], x, 0.0) for sc, x, rev in zip(scores, sl, revs)]
        h //= 2

    outs = []
    for (q, k, v, lf, st, rev), cm, sc in zip(streams, cum, scores):
        tot = cm[0:1, :] if rev else cm[c - 1:c, :]
        o = _dot(sc, v) + _dot_nt(q * jnp.exp2(cm), st)
        st_new = st * jnp.exp2(tot) + _dot_tn(v, k * jnp.exp2(tot - cm))
        outs.append((o, st_new))
    return outs


HG_HEADS_PER_STEP = 8
HG_CHUNKS_PER_STEP = 2


def _hgscan_kernel(dk, qf, fff, i_f, qb, ffb, i_b, lb_ref, of_ref, ob_ref, sf, sb):
    @pl.when(pl.program_id(2) == 0)
    def _():
        sf[...] = jnp.zeros_like(sf)
        sb[...] = jnp.zeros_like(sb)

    st_f = [sf[p] for p in range(HG_HEADS_PER_STEP)]
    st_b = [sb[p] for p in range(HG_HEADS_PER_STEP)]
    for sub in range(HG_CHUNKS_PER_STEP):
        rows_f = slice(sub * CHUNK, (sub + 1) * CHUNK)
        rows_b = slice((HG_CHUNKS_PER_STEP - 1 - sub) * CHUNK, (HG_CHUNKS_PER_STEP - sub) * CHUNK)
        streams = []
        for p in range(HG_HEADS_PER_STEP):
            sl = slice(p * dk, (p + 1) * dk)
            for (q_ref, ff_ref, i_ref, st, d, rows) in ((qf, fff, i_f, st_f[p], 0, rows_f),
                                                        (qb, ffb, i_b, st_b[p], 1, rows_b)):
                lbv = lb_ref[d:d + 1, sl]
                f = lbv + (1.0 - lbv) * _sigmoid(ff_ref[rows, sl])
                streams.append((_silu(q_ref[rows, sl]), 1.0 - f, i_ref[rows, sl], jnp.log(f), st, d == 1))
        outs = _hg_streams(streams)
        for p in range(HG_HEADS_PER_STEP):
            sl = slice(p * dk, (p + 1) * dk)
            of_ref[rows_f, sl], st_f[p] = outs[2 * p]
            ob_ref[rows_b, sl], st_b[p] = outs[2 * p + 1]
    for p in range(HG_HEADS_PER_STEP):
        sf[p] = st_f[p]
        sb[p] = st_b[p]


def _hgscan(nb, seq, z, lb, hg_off, hw):
    t = z.shape[0]
    rows = HG_CHUNKS_PER_STEP * CHUNK
    nc = seq // rows
    dk = hw // HG_HEADS
    wblk = HG_HEADS_PER_STEP * dk
    base = hg_off // wblk
    nh = hw // wblk

    def fw(comp):
        return pl.BlockSpec((rows, wblk), lambda bi, h, c: (bi * nc + c, base + comp * nh + h))

    def bw(comp):
        return pl.BlockSpec((rows, wblk), lambda bi, h, c: (bi * nc + nc - 1 - c, base + comp * nh + h))

    out = jax.ShapeDtypeStruct((t, hw), F32)
    state = pltpu.VMEM((HG_HEADS_PER_STEP, dk, dk), F32)
    return pl.pallas_call(
        functools.partial(_hgscan_kernel, dk),
        grid=(nb, nh, nc),
        in_specs=[fw(0), fw(1), fw(3), bw(0), bw(2), bw(3),
                  pl.BlockSpec((2, wblk), lambda bi, h, c: (0, h))],
        out_specs=[pl.BlockSpec((rows, wblk), lambda bi, h, c: (bi * nc + c, h)),
                   pl.BlockSpec((rows, wblk), lambda bi, h, c: (bi * nc + nc - 1 - c, h))],
        out_shape=[out, out],
        scratch_shapes=[state, state],
        compiler_params=_cparams(("parallel", "parallel", "arbitrary")),
        name="hgscan",
    )(z, z, z, z, z, z, lb)


def _blockdiag2(w):
    _, r, n = w.shape
    z = jnp.zeros((r, n), w.dtype)
    return jnp.concatenate([jnp.concatenate([w[0], z], axis=1), jnp.concatenate([z, w[1]], axis=1)], axis=0)


def _indicator(width, seg):
    e = (jnp.arange(width)[:, None] // seg == jnp.arange(width // seg)[None, :]).astype(BF16)
    return e, e.T


def _layer_consts(w_in, rw_mu, rw_w0, rw_w_up, rw_a0, rw_a_up, rw_g_up, rw_k_k, rw_k_a, rw_r_k,
                  hg_lb_gamma):
    rw = rw_k_k.shape[-1]
    d = w_in.shape[1]
    nlat = 2 * W_LORA + 2 * A_LORA + G_LORA
    w = w_in[0]
    rkv = 3 * rw
    hg_cols = w.shape[1] - rkv - nlat
    pad = 512 - nlat
    w_perm = jnp.concatenate([w[:, :rkv], w[:, rkv + nlat:], w[:, rkv:rkv + nlat],
                              jnp.zeros((d, pad), w.dtype)], axis=1).astype(BF16)
    mu = rw_mu[0]
    lower = jnp.cumsum(jax.nn.softmax(hg_lb_gamma.astype(F32), axis=0), axis=0)[0]
    hw = lower.shape[-1]
    e64, e64t = _indicator(rw, RW_HEAD_DIM)
    return dict(
        rw=rw, hw=hw, hg_off=rkv, lat_off=rkv + hg_cols, w_in=w_perm,
        mu_rkv=mu[:rkv].reshape(1, rkv),
        mu_lat=jnp.pad(mu[rkv:rkv + nlat], (0, pad)).reshape(1, 512),
        wup=_blockdiag2(rw_w_up[0]).astype(BF16), aup=_blockdiag2(rw_a_up[0]).astype(BF16),
        gup=rw_g_up[0].astype(BF16),
        w0=rw_w0[0].reshape(1, 2 * rw), a0=rw_a0[0].reshape(1, 2 * rw),
        k_k=rw_k_k[0].reshape(1, rw), k_a=rw_k_a[0].reshape(1, rw), r_k=rw_r_k[0].reshape(1, rw),
        e64=e64, e64t=e64t,
        lb=lower,
    )


def _mixer(x2, nb, seq, mod, norm_pre_mix, wc):
    z = _inproj(x2, mod, norm_pre_mix, wc['w_in'], seq)
    (r, v, kk, g, bonus, lw0, lw1, b0, b1, kd0, kd1) = _rwprep(
        z, seq, wc['rw'], wc['lat_off'], wc['mu_rkv'], wc['mu_lat'], wc['wup'], wc['aup'], wc['gup'],
        wc['w0'], wc['a0'], wc['k_k'], wc['k_a'], wc['r_k'], wc['e64'], wc['e64t'])
    rw_of, rw_ob = _rwscan(nb, seq, r, v, kk, lw0, lw1, b0, b1, kd0, kd1)
    hg_of, hg_ob = _hgscan(nb, seq, z, wc['lb'], wc['hg_off'], wc['hw'])
    return dict(z=z, r=r, v=v, kk=kk, g=g, bonus=bonus, lw0=lw0, rw_of=rw_of, rw_ob=rw_ob,
                hg_of=hg_of, hg_ob=hg_ob)


def _outproj_kernel(rw, x_ref, rf_ref, rb_ref, bon_ref, g_ref, hf_ref, hb_ref, hgg_ref, mod_ref,
                    lnw_ref, lnb_ref, hnw_ref, npm_ref, npf_ref, wout_ref, wrh_ref, wrl_ref,
                    e64_ref, e64t_ref, e128_ref, e128t_ref,
                    x1_ref, h2_ref, lg_ref):
    m6 = mod_ref[0]
    o = rf_ref[...] + rb_ref[...]
    mean = _seg_sum(o, e64_ref[...], e64t_ref[...]) * (1.0 / RW_HEAD_DIM)
    dlt = o - mean
    var = _seg_sum(dlt * dlt, e64_ref[...], e64t_ref[...]) * (1.0 / RW_HEAD_DIM)
    o_rw = (dlt * lax.rsqrt(var + RW_GN_EPS) * lnw_ref[...] + lnb_ref[...] + bon_ref[...]) * g_ref[...]
    oh = hf_ref[...] + hb_ref[...]
    hd = oh.shape[1] // HG_HEADS
    ms = _seg_sum(oh * oh, e128_ref[...], e128t_ref[...]) * (1.0 / hd)
    o_hg = oh * lax.rsqrt(ms + NORM_EPS) * hnw_ref[...] * _silu(hgg_ref[...])
    m = _dot(o_rw, wout_ref[0:rw, :]) + _dot(o_hg, wout_ref[rw:, :])
    x1 = x_ref[...] + m6[2:3] * _rms_rows(m, npm_ref[...])
    x1_ref[...] = x1
    h2 = _rms_rows(x1, npf_ref[...]) * (1.0 + m6[4:5]) + m6[3:4]
    hi, lo = _split2(h2)
    _pack_rows(hi, h2_ref)
    nt = (((1,), (1,)), ((), ()))
    lg_ref[...] = (lax.dot_general(wrh_ref[...], hi, nt, preferred_element_type=F32)
                   + lax.dot_general(wrh_ref[...], lo, nt, preferred_element_type=F32)
                   + lax.dot_general(wrl_ref[...], hi, nt, preferred_element_type=F32))


def _outproj(x2, seq, mx, z, mod, wc, oc):
    t, d = x2.shape
    rw, hw = wc['rw'], wc['hw']
    tm = 256
    gblk = (wc['hg_off'] + 4 * hw) // hw

    def row(w):
        return pl.BlockSpec((tm, w), lambda i: (i, 0))

    def full(a):
        return pl.BlockSpec(a.shape, lambda i: (0,) * a.ndim)

    consts = [oc['ln_w'], oc['ln_b'], oc['hg_norm_w'], oc['npm'], oc['npf'], oc['w_out'], oc['wr_hi'],
              oc['wr_lo'], wc['e64'], wc['e64t'], oc['e128'], oc['e128t']]
    return pl.pallas_call(
        functools.partial(_outproj_kernel, rw),
        grid=(t // tm,),
        in_specs=[row(d), row(rw), row(rw), row(rw), row(rw), row(hw), row(hw),
                  pl.BlockSpec((tm, hw), lambda i: (i, gblk)),
                  pl.BlockSpec((1, 6, d), lambda i: ((i * tm) // seq, 0, 0))] + [full(a) for a in consts],
        out_specs=[row(d), pl.BlockSpec((tm * (d // PACK_W), LANES), lambda i: (i, 0)),
                   pl.BlockSpec((N_EXPERTS, tm), lambda i: (0, i))],
        out_shape=[jax.ShapeDtypeStruct((t, d), F32), jax.ShapeDtypeStruct((t * (d // PACK_W), LANES), jnp.uint32),
                   jax.ShapeDtypeStruct((N_EXPERTS, t), F32)],
        compiler_params=_cparams(("parallel",)),
        name="outproj",
    )(x2, mx['rw_of'], mx['rw_ob'], mx['bonus'], mx['g'], mx['hg_of'], mx['hg_ob'], z, mod, *consts)


ROUTE_TILE = 512


def _route_kernel(lg_ref, bias_ref, ut_ref, eidx_ref, wsel_ref, rank_ref, cnt_ref, carry):
    @pl.when(pl.program_id(0) == 0)
    def _():
        carry[...] = jnp.zeros_like(carry)

    ne, tt = lg_ref.shape
    gsz = ne // N_GROUPS
    neg = -jnp.inf
    s = _sigmoid(lg_ref[...])
    biased = s + bias_ref[...]
    io_g = lax.broadcasted_iota(I32, (gsz, tt), 0)
    gs_rows = []
    for gi in range(N_GROUPS):
        blk = biased[gi * gsz:(gi + 1) * gsz, :]
        m1 = jnp.max(blk, axis=0, keepdims=True)
        first = jnp.min(jnp.where(blk == m1, io_g, gsz), axis=0, keepdims=True)
        m2 = jnp.max(jnp.where(io_g == first, neg, blk), axis=0, keepdims=True)
        gs_rows.append(m1 + m2)
    gs = jnp.concatenate(gs_rows, axis=0)
    io_n = lax.broadcasted_iota(I32, (N_GROUPS, tt), 0)
    selg = jnp.zeros((N_GROUPS, tt), jnp.bool_)
    for _ in range(TOPK_GROUPS):
        m = jnp.max(gs, axis=0, keepdims=True)
        first = jnp.min(jnp.where(gs == m, io_n, N_GROUPS), axis=0, keepdims=True)
        pick = io_n == first
        selg = jnp.logical_or(selg, pick)
        gs = jnp.where(pick, neg, gs)
    emask = jnp.concatenate([jnp.broadcast_to(selg[gi:gi + 1, :], (gsz, tt)) for gi in range(N_GROUPS)],
                            axis=0)
    mb = jnp.where(emask, biased, neg)
    io_e = lax.broadcasted_iota(I32, (ne, tt), 0)
    sel = jnp.zeros((ne, tt), jnp.bool_)
    picks, idxs, ws = [], [], []
    for _ in range(TOP_K):
        m = jnp.max(mb, axis=0, keepdims=True)
        first = jnp.min(jnp.where(mb == m, io_e, ne), axis=0, keepdims=True)
        pick = io_e == first
        picks.append(pick)
        idxs.append(first)
        ws.append(jnp.sum(jnp.where(pick, s, 0.0), axis=0, keepdims=True))
        sel = jnp.logical_or(sel, pick)
        mb = jnp.where(pick, neg, mb)
    wsum = ws[0]
    for w in ws[1:]:
        wsum = wsum + w
    pos = jnp.dot(jnp.where(sel, 1.0, 0.0).astype(BF16), ut_ref[...], preferred_element_type=F32) + carry[...]
    ranks = [jnp.sum(jnp.where(p, pos, 0.0), axis=0, keepdims=True).astype(I32) for p in picks]
    carry[...] = carry[...] + jnp.sum(jnp.where(sel, 1.0, 0.0), axis=1, keepdims=True)
    zi = jnp.zeros((SUBLANES - TOP_K, tt), I32)
    eidx_ref[...] = jnp.concatenate(idxs + [zi], axis=0)
    rank_ref[...] = jnp.concatenate(ranks + [zi], axis=0)
    wsel_ref[...] = jnp.concatenate([w / wsum * ROUTED_SCALE for w in ws] + [zi.astype(F32)], axis=0)
    cnt_ref[...] = jnp.broadcast_to(carry[...], cnt_ref.shape).astype(I32)


def _route(logits_t, e_bias):
    ne, t = logits_t.shape
    tt = ROUTE_TILE
    ut = (jnp.arange(tt)[:, None] < jnp.arange(tt)[None, :]).astype(BF16)
    tok = pl.BlockSpec((SUBLANES, tt), lambda i: (0, i))
    return pl.pallas_call(
        _route_kernel,
        grid=(t // tt,),
        in_specs=[pl.BlockSpec((ne, tt), lambda i: (0, i)),
                  pl.BlockSpec((ne, 1), lambda i: (0, 0)),
                  pl.BlockSpec((tt, tt), lambda i: (0, 0))],
        out_specs=[tok, tok, tok, pl.BlockSpec((ne, LANES), lambda i: (0, 0))],
        out_shape=[jax.ShapeDtypeStruct((SUBLANES, t), I32), jax.ShapeDtypeStruct((SUBLANES, t), F32),
                   jax.ShapeDtypeStruct((SUBLANES, t), I32), jax.ShapeDtypeStruct((ne, LANES), I32)],
        scratch_shapes=[pltpu.VMEM((ne, 1), F32)],
        compiler_params=_cparams(("arbitrary",)),
        name="route",
    )(logits_t, e_bias.reshape(ne, 1), ut)


DISPATCH_TILE = 128
PLAN_TILE = 1024


def _plan_kernel(eidx_ref, rank_ref, ps_ref, dest_ref):
    ne = ps_ref.shape[0]
    tp = eidx_ref.shape[1]
    io_e = lax.broadcasted_iota(I32, (ne, tp), 0)
    ps = ps_ref[...]
    rows = []
    for j in range(TOP_K):
        start = jnp.sum(jnp.where(io_e == eidx_ref[j:j + 1, :], ps, 0.0), axis=0, keepdims=True)
        rows.append(start.astype(I32) + rank_ref[j:j + 1, :])
    dest = jnp.concatenate(rows + [jnp.zeros((SUBLANES - TOP_K, tp), I32)], axis=0)
    for i in range(tp // DISPATCH_TILE):
        dest_ref[i] = dest[:, i * DISPATCH_TILE:(i + 1) * DISPATCH_TILE]


def _plan(eidx, rank, pad_start):
    t = eidx.shape[1]
    tp = PLAN_TILE
    ne = pad_start.shape[0]
    tok = pl.BlockSpec((SUBLANES, tp), lambda i: (0, i))
    ntile = tp // DISPATCH_TILE
    return pl.pallas_call(
        _plan_kernel,
        grid=(t // tp,),
        in_specs=[tok, tok, pl.BlockSpec((ne, 1), lambda i: (0, 0))],
        out_specs=pl.BlockSpec((ntile, SUBLANES, DISPATCH_TILE), lambda i: (i, 0, 0)),
        out_shape=jax.ShapeDtypeStruct((t // DISPATCH_TILE, SUBLANES, DISPATCH_TILE), I32),
        compiler_params=_cparams(("parallel",)),
        name="plan",
    )(eidx, rank, pad_start.astype(F32).reshape(ne, 1))


def _dispatch_kernel(tt, nsteps, zs_ref, zc_ref, nu_ref, dest_ref, h_ref, xs_ref, idx_smem, zero_blk,
                     isem, sem, zsem, bsem):
    i = pl.program_id(0)
    n = tt * TOP_K
    ne = zs_ref.shape[0]
    nblk = xs_ref.shape[0] // EXPERT_BLOCK
    icp = pltpu.make_async_copy(dest_ref.at[pl.ds(i * tt * SUBLANES, n)], idx_smem, isem)
    icp.start()

    @pl.when(i == 0)
    def _():
        zero_blk[...] = jnp.zeros_like(zero_blk)

    icp.wait()

    def row_copy(k):
        return pltpu.make_async_copy(h_ref.at[k % tt], xs_ref.at[idx_smem[k]], sem)

    def drain(k, c):
        row_copy(k).wait()
        return c

    for k in range(n):
        row_copy(k).start(priority=k % 2)

    for q in range(pl.cdiv(ne, nsteps)):
        e = i + q * nsteps
        ec = jnp.minimum(e, ne - 1)
        start = zs_ref[ec]
        count = jnp.where(e < ne, zc_ref[ec], 0)

        def zero_copy(r):
            return pltpu.make_async_copy(zero_blk.at[0], xs_ref.at[start + r], zsem)

        def z_issue(r, c):
            zero_copy(r).start()
            return c

        def z_drain(r, c):
            zero_copy(r).wait()
            return c

        lax.fori_loop(0, count, z_issue, 0)
        lax.fori_loop(0, count, z_drain, 0)

    for q in range(pl.cdiv(nblk, nsteps)):
        blk = nu_ref[0] + i + q * nsteps

        @pl.when(blk < nblk)
        def _():
            bcp = pltpu.make_async_copy(
                zero_blk, xs_ref.at[pl.ds(pl.multiple_of(blk * EXPERT_BLOCK, EXPERT_BLOCK), EXPERT_BLOCK)], bsem)
            bcp.start()
            bcp.wait()

    lax.fori_loop(0, n, drain, 0, unroll=8)


def _dispatch(dest_flat, h2p, p_rows, zero_start, zero_count, n_used):
    t = h2p.shape[0]
    tt = DISPATCH_TILE
    anyspec = pl.BlockSpec(memory_space=pl.ANY)
    dma = pltpu.SemaphoreType.DMA(())
    grid_spec = pltpu.PrefetchScalarGridSpec(
        num_scalar_prefetch=3,
        grid=(t // tt,),
        in_specs=[anyspec, pl.BlockSpec((tt,) + h2p.shape[1:], lambda i, zs, zc, nu: (i, 0, 0))],
        out_specs=anyspec,
        scratch_shapes=[pltpu.SMEM((tt * TOP_K,), I32), pltpu.VMEM((EXPERT_BLOCK,) + h2p.shape[1:], h2p.dtype),
                        dma, dma, dma, dma],
    )
    return pl.pallas_call(
        functools.partial(_dispatch_kernel, tt, t // tt),
        grid_spec=grid_spec,
        out_shape=jax.ShapeDtypeStruct((p_rows,) + h2p.shape[1:], h2p.dtype),
        compiler_params=_cparams(("arbitrary",)),
        name="dispatch",
    )(zero_start, zero_count, n_used, dest_flat, h2p)


def _experts_kernel(be_ref, nu_ref, x_ref, wg_ref, wu_ref, wd_ref, y_ref, wg_s, wu_s, wd_s):
    b = pl.program_id(0)
    live = b < nu_ref[0]
    new_expert = jnp.logical_or(b == 0, be_ref[b] != be_ref[jnp.maximum(b - 1, 0)])

    @pl.when(jnp.logical_and(live, new_expert))
    def _():
        wg_s[...] = wg_ref[0].astype(BF16)
        wu_s[...] = wu_ref[0].astype(BF16)
        wd_s[...] = wd_ref[0].astype(BF16)

    @pl.when(live)
    def _():
        x = _unpack_rows(x_ref, wg_ref.shape[1] // PACK_W)
        gate = jnp.dot(x, wg_s[...], preferred_element_type=F32)
        up = jnp.dot(x, wu_s[...], preferred_element_type=F32)
        _pack_rows(_dot(_silu(gate) * up, wd_s[...]).astype(BF16), y_ref)

    @pl.when(b >= nu_ref[0])
    def _():
        y_ref[...] = jnp.zeros_like(y_ref)


def _experts(block_e, n_used, xs, wg, wu, wd):
    d, de = wg.shape[1], wg.shape[2]
    ns = d // PACK_W
    p_rows = xs.shape[0] // ns
    nblk = p_rows // EXPERT_BLOCK

    def live(b, nu):
        return jnp.minimum(b, jnp.maximum(nu[0] - 1, 0))

    grid_spec = pltpu.PrefetchScalarGridSpec(
        num_scalar_prefetch=2,
        grid=(nblk,),
        in_specs=[pl.BlockSpec((EXPERT_BLOCK * ns, LANES), lambda b, be, nu: (live(b, nu), 0)),
                  pl.BlockSpec((1, d, de), lambda b, be, nu: (be[live(b, nu)], 0, 0)),
                  pl.BlockSpec((1, d, de), lambda b, be, nu: (be[live(b, nu)], 0, 0)),
                  pl.BlockSpec((1, de, d), lambda b, be, nu: (be[live(b, nu)], 0, 0))],
        out_specs=pl.BlockSpec((EXPERT_BLOCK * ns, LANES), lambda b, be, nu: (b, 0)),
        scratch_shapes=[pltpu.VMEM((d, de), BF16), pltpu.VMEM((d, de), BF16), pltpu.VMEM((de, d), BF16)],
    )
    return pl.pallas_call(
        _experts_kernel,
        grid_spec=grid_spec,
        out_shape=jax.ShapeDtypeStruct((p_rows * ns, LANES), jnp.uint32),
        compiler_params=_cparams(("arbitrary",)),
        name="experts",
    )(block_e, n_used, xs, wg, wu, wd)


def _combine_kernel(tt, nsteps, dest_ref, y_ref, x1_ref, h2_ref, w_ref, mod_ref, npf_ref, sg_ref, su_ref, sd_ref,
                    o_ref, ybuf, idx_smem, isem, sems):
    i = pl.program_id(0)
    n = tt * TOP_K
    ns = sg_ref.shape[0] // PACK_W

    def row_copy(slot, k):
        dst = ybuf.at[pl.ds(pl.multiple_of((slot * n + k) * ns, ns), ns)]
        return pltpu.make_async_copy(y_ref.at[idx_smem[slot * n + k]], dst, sems.at[slot])

    def issue(step, slot):
        icp = pltpu.make_async_copy(dest_ref.at[pl.ds(step * tt * SUBLANES, n)],
                                    idx_smem.at[pl.ds(slot * n, n)], isem)
        icp.start()
        icp.wait()
        for k in range(n):
            row_copy(slot, k).start(priority=k % 2)

    def drain(slot):
        def body(k, c):
            row_copy(slot, k).wait()
            return c

        lax.fori_loop(0, n, body, 0, unroll=8)

    @pl.when(i == 0)
    def _():
        issue(0, 0)

    other = (i + 1) % 2
    issue(jnp.minimum(i + 1, nsteps - 1), other)

    h2 = _unpack_rows(h2_ref, ns)
    gate = jnp.dot(h2, sg_ref[...], preferred_element_type=F32)
    up = jnp.dot(h2, su_ref[...], preferred_element_type=F32)
    shared = _dot(_silu(gate) * up, sd_ref[...])
    w = w_ref[...]
    m6 = mod_ref[0]

    def finish(slot):
        drain(slot)
        acc = [None] * (2 * ns)
        for j in range(TOP_K):
            wj = w[:, j:j + 1]
            for s in range(ns):
                word = ybuf[pl.ds((slot * n + j * tt) * ns + s, tt, stride=ns), :]
                parts = (lax.bitcast_convert_type(word & jnp.uint32(0xFFFF0000), F32),
                         lax.bitcast_convert_type(word << 16, F32))
                for h, part in enumerate(parts):
                    term = wj * part
                    acc[2 * s + h] = term if acc[2 * s + h] is None else acc[2 * s + h] + term
        routed = jnp.concatenate(acc, axis=1)
        o_ref[...] = x1_ref[...] + m6[5:6] * _rms_rows(routed + shared, npf_ref[...])

    for slot in (0, 1):
        @pl.when(i % 2 == slot)
        def _():
            finish(slot)

    @pl.when(i == nsteps - 1)
    def _():
        drain(other)


def _combine(dest_flat, y3, x1, h2, wsel_t, mod, seq, npf, sg, su, sd):
    t, d = x1.shape
    tt = DISPATCH_TILE
    ns = d // PACK_W
    n = tt * TOP_K
    anyspec = pl.BlockSpec(memory_space=pl.ANY)

    def full(a):
        return pl.BlockSpec(a.shape, lambda i: (0,) * a.ndim)

    rows = pltpu.VMEM((2 * n * ns, LANES), jnp.uint32)
    return pl.pallas_call(
        functools.partial(_combine_kernel, tt, t // tt),
        grid=(t // tt,),
        in_specs=[anyspec, anyspec,
                  pl.BlockSpec((tt, d), lambda i: (i, 0)),
                  pl.BlockSpec((tt * ns, LANES), lambda i: (i, 0)),
                  pl.BlockSpec((tt, SUBLANES), lambda i: (i, 0)),
                  pl.BlockSpec((1, 6, d), lambda i: ((i * tt) // seq, 0, 0)),
                  full(npf), full(sg), full(su), full(sd)],
        out_specs=pl.BlockSpec((tt, d), lambda i: (i, 0)),
        out_shape=jax.ShapeDtypeStruct((t, d), F32),
        scratch_shapes=[rows, pltpu.SMEM((2 * n,), I32), pltpu.SemaphoreType.DMA(()),
                        pltpu.SemaphoreType.DMA((2,))],
        compiler_params=_cparams(("arbitrary",)),
        name="combine",
    )(dest_flat, y3, x1, h2, wsel_t, mod, npf, sg, su, sd)


def _moe_plan(eidx, rank, cnt, t):
    counts = cnt[:, 0]
    padded = (counts + EXPERT_BLOCK - 1) // EXPERT_BLOCK * EXPERT_BLOCK
    pad_end = jnp.cumsum(padded)
    pad_start = pad_end - padded
    n_blocks = (t * TOP_K + EXPERT_BLOCK - 1) // EXPERT_BLOCK + N_EXPERTS
    first_row = jnp.arange(n_blocks, dtype=I32) * EXPERT_BLOCK
    block_e = jnp.minimum(jnp.sum((pad_end[None, :] <= first_row[:, None]).astype(I32), axis=1),
                          N_EXPERTS - 1).astype(I32)
    n_used = (pad_end[-1:] // EXPERT_BLOCK).astype(I32)
    dest_flat = _plan(eidx, rank, pad_start).reshape(-1)
    zero_rows = ((pad_start + counts).astype(I32), (padded - counts).astype(I32))
    return block_e, n_used, dest_flat, n_blocks * EXPERT_BLOCK, zero_rows


def _trunk(x, mod, wc, oc, ec, norm_pre_mix):
    nb, seq, d = x.shape
    t = nb * seq
    x2 = x.reshape(t, d)
    mx = _mixer(x2, nb, seq, mod, norm_pre_mix, wc)
    x1, h2, logits_t = _outproj(x2, seq, mx, mx['z'], mod, wc, oc)
    eidx, wsel, rank, cnt = _route(logits_t, ec['e_bias'])
    block_e, n_used, dest_flat, p_rows, zero_rows = _moe_plan(eidx, rank, cnt, t)
    ns = d // PACK_W
    xs = _dispatch(dest_flat, h2.reshape(t, ns, LANES), p_rows, *zero_rows, n_used)
    y = _experts(block_e, n_used, xs.reshape(p_rows * ns, LANES), ec['wg'], ec['wu'], ec['wd'])
    out = _combine(dest_flat, y.reshape(p_rows, ns, LANES), x1, h2, wsel.T, mod, seq, oc['npo'],
                   ec['sg'], ec['su'], ec['sd'])
    return out.reshape(nb, seq, d)


def kernel(x_prompt, x_sample, c_prompt, c_sample, w_ada, b_ada, norm_pre_mix, norm_post_mix, norm_pre_ffn, norm_post_ffn, w_in, rw_mu, rw_w0, rw_w_up, rw_a0, rw_a_up, rw_g_up, rw_k_k, rw_k_a, rw_r_k, rw_ln_w, rw_ln_b, hg_lb_gamma, hg_norm_w, w_out, w_router, e_bias, w_exp_gate, w_exp_up, w_exp_down, w_sh_gate, w_sh_up, w_sh_down):
    d = x_prompt.shape[-1]
    wc = _layer_consts(w_in, rw_mu, rw_w0, rw_w_up, rw_a0, rw_a_up, rw_g_up, rw_k_k, rw_k_a, rw_r_k,
                       hg_lb_gamma)
    rw, hw = wc['rw'], wc['hw']
    e128, e128t = _indicator(hw, hw // HG_HEADS)
    wr_hi, wr_lo = _split2(w_router[0].T)
    oc = dict(ln_w=rw_ln_w[0].reshape(1, rw), ln_b=rw_ln_b[0].reshape(1, rw),
              hg_norm_w=hg_norm_w[0].reshape(1, hw), npm=norm_post_mix[0].reshape(1, d),
              npf=norm_pre_ffn[0].reshape(1, d), npo=norm_post_ffn[0].reshape(1, d),
              w_out=w_out[0].astype(BF16), wr_hi=wr_hi, wr_lo=wr_lo, e128=e128, e128t=e128t)
    ec = dict(e_bias=e_bias[0], wg=w_exp_gate[0], wu=w_exp_up[0], wd=w_exp_down[0], sg=w_sh_gate[0].astype(BF16), su=w_sh_up[0].astype(BF16),
              sd=w_sh_down[0].astype(BF16))
    nbp = c_prompt.shape[0]
    mod = _ada(jnp.concatenate([c_prompt, c_sample], axis=0), w_ada[0], b_ada[0]).reshape(-1, 6, d)
    y_prompt = _trunk(x_prompt, mod[:nbp], wc, oc, ec, norm_pre_mix[0])
    y_sample = _trunk(x_sample, mod[nbp:], wc, oc, ec, norm_pre_mix[0])
    return (y_prompt, y_sample)
```

```python
import functools
import math

import jax
import jax.numpy as jnp
from jax import lax
from jax.experimental import pallas as pl
from jax.experimental.pallas import tpu as pltpu

F32 = jnp.float32
BF16 = jnp.bfloat16
I32 = jnp.int32

RW_HEAD_DIM = 64
W_LORA = 64
A_LORA = 64
G_LORA = 128
RW_GN_EPS = 64e-5
HG_HEADS = 8
N_EXPERTS = 64
TOP_K = 6
N_GROUPS = 8
TOPK_GROUPS = 4
ROUTED_SCALE = 2.5
EXPERT_BLOCK = 256
NORM_EPS = 1e-6

LANES = 128
SUBLANES = 8
VMEM_LIMIT = 56 * 1024 * 1024

CHUNK = 64


def _cparams(sem, vmem=VMEM_LIMIT):
    return pltpu.CompilerParams(dimension_semantics=sem, vmem_limit_bytes=vmem)


def _sigmoid(x):
    return 1.0 / (1.0 + jnp.exp(-x))


def _silu(x):
    return x * _sigmoid(x)


def _dot(a, b):
    return jnp.dot(a.astype(BF16), b.astype(BF16), preferred_element_type=F32)


def _dot_nt(a, b):
    return lax.dot_general(a.astype(BF16), b.astype(BF16), (((1,), (1,)), ((), ())),
                           preferred_element_type=F32)


def _dot_tn(a, b):
    return lax.dot_general(a.astype(BF16), b.astype(BF16), (((0,), (0,)), ((), ())),
                           preferred_element_type=F32)


def _split2(x):
    hi = x.astype(BF16)
    lo = (x - hi.astype(F32)).astype(BF16)
    return hi, lo


def _split3(x):
    hi = x.astype(BF16)
    r1 = x - hi.astype(F32)
    mid = r1.astype(BF16)
    lo = (r1 - mid.astype(F32)).astype(BF16)
    return hi, mid, lo


def _seg_sum(x, e, et):
    hi, lo = _split2(x)
    s = jnp.dot(hi, e, preferred_element_type=F32) + jnp.dot(lo, e, preferred_element_type=F32)
    shi, slo = _split2(s)
    return jnp.dot(shi, et, preferred_element_type=F32) + jnp.dot(slo, et, preferred_element_type=F32)


PACK_W = 2 * LANES


def _pack_rows(x_bf, out_ref):
    n, d = x_bf.shape
    ns = d // PACK_W
    for s in range(ns):
        a = lax.bitcast_convert_type(x_bf[:, s * PACK_W:s * PACK_W + LANES].astype(F32), jnp.uint32)
        b = lax.bitcast_convert_type(x_bf[:, s * PACK_W + LANES:(s + 1) * PACK_W].astype(F32), jnp.uint32)
        out_ref[pl.ds(s, n, stride=ns), :] = a | (b >> 16)


def _unpack_rows(x_ref, ns):
    n = x_ref.shape[0] // ns
    parts = []
    for s in range(ns):
        w = x_ref[pl.ds(s, n, stride=ns), :]
        parts.append(lax.bitcast_convert_type(w & jnp.uint32(0xFFFF0000), F32).astype(BF16))
        parts.append(lax.bitcast_convert_type(w << 16, F32).astype(BF16))
    return jnp.concatenate(parts, axis=1)


def _rms_rows(x, g):
    return x * lax.rsqrt(jnp.mean(x * x, axis=-1, keepdims=True) + NORM_EPS) * g


def _ada_kernel(c_ref, w_ref, b_ref, o_ref):
    c = c_ref[...]
    o_ref[...] = _dot(_silu(c), w_ref[...]) + b_ref[...]


def _ada(c, w_ada, b_ada):
    nb, d = c.shape
    n = w_ada.shape[1]
    tn = 512
    return pl.pallas_call(
        _ada_kernel,
        grid=(n // tn,),
        in_specs=[pl.BlockSpec((nb, d), lambda j: (0, 0)),
                  pl.BlockSpec((d, tn), lambda j: (0, j)),
                  pl.BlockSpec((1, tn), lambda j: (0, j))],
        out_specs=pl.BlockSpec((nb, tn), lambda j: (0, j)),
        out_shape=jax.ShapeDtypeStruct((nb, n), F32),
        compiler_params=_cparams(("parallel",)),
        name="ada",
    )(c, w_ada, b_ada.reshape(1, n))


def _inproj_kernel(x_ref, mod_ref, g_ref, w_ref, o_ref, h_scr):
    @pl.when(pl.program_id(1) == 0)
    def _():
        m = mod_ref[0]
        h = _rms_rows(x_ref[...], g_ref[...]) * (1.0 + m[1:2]) + m[0:1]
        h_scr[...] = h.astype(BF16)

    o_ref[...] = jnp.dot(h_scr[...], w_ref[...], preferred_element_type=F32)


def _inproj(x2, mod, g, w_bf, seq):
    t, d = x2.shape
    n = w_bf.shape[1]
    tm, tn = 1024, 512
    return pl.pallas_call(
        _inproj_kernel,
        grid=(t // tm, n // tn),
        in_specs=[pl.BlockSpec((tm, d), lambda i, j: (i, 0)),
                  pl.BlockSpec((1, 6, d), lambda i, j: ((i * tm) // seq, 0, 0)),
                  pl.BlockSpec((1, d), lambda i, j: (0, 0)),
                  pl.BlockSpec((d, tn), lambda i, j: (0, j))],
        out_specs=pl.BlockSpec((tm, tn), lambda i, j: (i, j)),
        out_shape=jax.ShapeDtypeStruct((t, n), F32),
        scratch_shapes=[pltpu.VMEM((tm, d), BF16)],
        compiler_params=_cparams(("parallel", "arbitrary")),
        name="inproj",
    )(x2, mod, g.reshape(1, d), w_bf)


def _rwprep_kernel(seq, tm, rw,
                   z_ref, zp_ref, zn_ref, l_ref, lp_ref, ln_ref,
                   mu_ref, mul_ref, wup_ref, aup_ref, gup_ref, w0_ref, a0_ref,
                   kk_ref, ka_ref, rk_ref, e_ref, et_ref,
                   r_o, v_o, kk_o, g_o, bon_o, lw0_o, lw1_o, b0_o, b1_o, kd0_o, kd1_o):
    i = pl.program_id(0)
    first = (i * tm) % seq == 0
    last = ((i + 1) * tm) % seq == 0

    def shifted(cur, prev_blk, next_blk, mu):
        rows = lax.broadcasted_iota(I32, cur.shape, 0)
        prow = jnp.where(first, 0.0, prev_blk[SUBLANES - 1:SUBLANES, :])
        nrow = jnp.where(last, 0.0, next_blk[0:1, :])
        prev = jnp.where(rows == 0, prow, pltpu.roll(cur, 1, axis=0))
        nxt = jnp.where(rows == tm - 1, nrow, pltpu.roll(cur, tm - 1, axis=0))
        return cur + mu * (0.5 * (prev + nxt) - cur)

    lat = shifted(l_ref[...], lp_ref[...], ln_ref[...], mul_ref[...])
    w_lat = lat[:, 0:2 * W_LORA]
    a_lat = lat[:, 2 * W_LORA:2 * W_LORA + 2 * A_LORA]
    g_lat = lat[:, 2 * W_LORA + 2 * A_LORA:2 * W_LORA + 2 * A_LORA + G_LORA]
    w_raw = _dot(jnp.tanh(w_lat), wup_ref[...]) + w0_ref[...]
    a_all = _sigmoid(_dot(a_lat, aup_ref[...]) + a0_ref[...])
    g_o[...] = _dot(_sigmoid(g_lat), gup_ref[...])
    lw = (-math.exp(-0.5)) * _sigmoid(w_raw)
    lw0_o[...] = lw[:, :rw]
    lw1_o[...] = lw[:, rw:]

    r = shifted(z_ref[:, 0:rw], zp_ref[:, 0:rw], zn_ref[:, 0:rw], mu_ref[:, 0:rw])
    k = shifted(z_ref[:, rw:2 * rw], zp_ref[:, rw:2 * rw], zn_ref[:, rw:2 * rw], mu_ref[:, rw:2 * rw])
    v = shifted(z_ref[:, 2 * rw:3 * rw], zp_ref[:, 2 * rw:3 * rw], zn_ref[:, 2 * rw:3 * rw],
                mu_ref[:, 2 * rw:3 * rw])
    r_o[...] = r
    v_o[...] = v
    kk = k * kk_ref[...]
    ss = _seg_sum(kk * kk, e_ref[...], et_ref[...])
    kk = kk / jnp.maximum(jnp.sqrt(ss), 1e-12)
    kk_o[...] = kk
    a0 = a_all[:, :rw]
    a1 = a_all[:, rw:]
    b0_o[...] = kk * a0
    b1_o[...] = kk * a1
    kd0 = k * (1.0 + (a0 - 1.0) * ka_ref[...])
    kd1 = k * (1.0 + (a1 - 1.0) * ka_ref[...])
    kd0_o[...] = kd0
    kd1_o[...] = kd1
    kb = 0.5 * (kd0 + kd1)
    bon_o[...] = _seg_sum(r * kb * rk_ref[...], e_ref[...], et_ref[...]) * v


def _rwprep(z, seq, rw, lat_off, mu_rkv, mu_lat, wup, aup, gup, w0, a0, k_k, k_a, r_k, e64, e64t):
    t = z.shape[0]
    tm = 256
    nlat = 512
    nrkv = 3 * rw
    tb = tm // SUBLANES
    nblk8 = t // SUBLANES
    lat_blk = lat_off // nlat

    def cur(i):
        return (i, 0)

    def prv(i):
        return (jnp.maximum(i * tb - 1, 0), 0)

    def nxt(i):
        return (jnp.minimum((i + 1) * tb, nblk8 - 1), 0)

    def full(shape):
        return pl.BlockSpec(shape, lambda i: (0,) * len(shape))

    out = jax.ShapeDtypeStruct((t, rw), F32)
    ospec = pl.BlockSpec((tm, rw), lambda i: (i, 0))
    return pl.pallas_call(
        functools.partial(_rwprep_kernel, seq, tm, rw),
        grid=(t // tm,),
        in_specs=[pl.BlockSpec((tm, nrkv), cur),
                  pl.BlockSpec((SUBLANES, nrkv), prv),
                  pl.BlockSpec((SUBLANES, nrkv), nxt),
                  pl.BlockSpec((tm, nlat), lambda i: (i, lat_blk)),
                  pl.BlockSpec((SUBLANES, nlat), lambda i: (jnp.maximum(i * tb - 1, 0), lat_blk)),
                  pl.BlockSpec((SUBLANES, nlat), lambda i: (jnp.minimum((i + 1) * tb, nblk8 - 1), lat_blk)),
                  full((1, nrkv)), full((1, nlat)),
                  full(wup.shape), full(aup.shape), full(gup.shape),
                  full((1, 2 * rw)), full((1, 2 * rw)),
                  full((1, rw)), full((1, rw)), full((1, rw)),
                  full(e64.shape), full(e64t.shape)],
        out_specs=[ospec] * 11,
        out_shape=[out] * 11,
        compiler_params=_cparams(("parallel",)),
        name="rwprep",
    )(z, z, z, z, z, z, mu_rkv, mu_lat, wup, aup, gup, w0, a0, k_k, k_a, r_k, e64, e64t)


def _tri(n, rev):
    i = lax.broadcasted_iota(I32, (n, n), 0)
    j = lax.broadcasted_iota(I32, (n, n), 1)
    m = (j >= i) if rev else (j <= i)
    return jnp.where(m, 1.0, 0.0).astype(BF16)


def _rw_streams(streams):
    c = streams[0][0].shape[0]
    hd = RW_HEAD_DIM
    n2 = 2 * c
    ns = len(streams)
    revs = [s[7] for s in streams]
    lane = lax.broadcasted_iota(I32, (c, LANES), 1)
    head_a = lane < hd
    ri = lax.broadcasted_iota(I32, (n2, n2), 0)
    ci = lax.broadcasted_iota(I32, (n2, n2), 1)
    ti = ri % c
    si = ci % c
    same16 = (ri // 16) == (ci // 16)
    same32 = (ri // 32) == (ci // 32)
    mid32 = jnp.logical_and(same32, jnp.logical_not(same16))
    eye = jnp.where(ri == ci, 1.0, 0.0)
    strict = {False: si < ti, True: si > ti}
    incl = {False: si <= ti, True: si >= ti}
    tri = {rev: _tri(c, rev) for rev in set(revs)}

    def pair(x):
        return jnp.concatenate([jnp.where(head_a, x, 0.0), jnp.where(head_a, 0.0, x)], axis=0)

    cum = []
    for (r, v, kk, lw, b, kd, s_in, rev) in streams:
        hi, lo = _split2(lw)
        cs = jnp.dot(tri[rev], jnp.concatenate([hi, lo], axis=1), preferred_element_type=F32)
        cum.append(cs[:, :LANES] + cs[:, LANES:])
    ops = []
    for (r, v, kk, lw, b, kd, s_in, rev), cm in zip(streams, cum):
        tot = cm[0:1, :] if rev else cm[c - 1:c, :]
        g_inv = jnp.exp(-cm)
        g_tail = jnp.exp(tot - cm)
        ops.append(dict(p2=pair(-kk * jnp.exp(cm - lw)), r2=pair(r * jnp.exp(cm)),
                        bi2=pair(b * g_inv), ki2=pair(kd * g_inv), bt2=pair(b * g_tail),
                        kt2=pair(kd * g_tail), v2=pair(v), g_tot=jnp.exp(tot)))
    gm = [_dot_nt(jnp.concatenate([o['p2'], o['r2']], axis=0), jnp.concatenate([o['bi2'], o['ki2']], axis=0))
          for o in ops]
    a2 = [jnp.where(strict[rev], g[:n2, :n2], 0.0) for g, rev in zip(gm, revs)]
    b2 = [jnp.where(strict[rev], g[:n2, n2:], 0.0) for g, rev in zip(gm, revs)]
    ap2 = [jnp.where(incl[rev], g[n2:, :n2], 0.0) for g, rev in zip(gm, revs)]
    bp2 = [jnp.where(incl[rev], g[n2:, n2:], 0.0) for g, rev in zip(gm, revs)]
    vv = [_dot(jnp.concatenate([x, y], axis=0), o['v2']) for x, y, o in zip(b2, bp2, ops)]
    bv = [x[:n2] for x in vv]
    bpv = [x[n2:] for x in vv]

    x = [jnp.where(same16, a, 0.0) for a in a2]
    tinv = [eye + xi for xi in x]
    for _ in range(3):
        x = [_dot(xi, xi) for xi in x]
        tinv = [t + _dot(t, xi) for t, xi in zip(tinv, x)]
    for lvl in (mid32, jnp.logical_not(same32)):
        y = [_dot(t, jnp.where(lvl, a, 0.0)) for t, a in zip(tinv, a2)]
        tinv = [t + _dot(yi, t) for t, yi in zip(tinv, y)]

    wu = [_dot(t, jnp.concatenate([o['p2'], bvi], axis=1)) for t, o, bvi in zip(tinv, ops, bv)]
    qo = [_dot(a, w) for a, w in zip(ap2, wu)]
    m2 = [_dot_tn(w[:, :LANES], o['bt2']) for w, o in zip(wu, ops)]
    nn2 = [_dot_tn(w[:, LANES:], o['bt2']) + _dot_tn(o['v2'], o['kt2']) for w, o in zip(wu, ops)]

    outs = []
    for i in range(ns):
        s_in = streams[i][6]
        oo = _dot_nt(ops[i]['r2'] + qo[i][:, :LANES], s_in) + qo[i][:, LANES:] + bpv[i]
        s_out = s_in * ops[i]['g_tot'] + _dot(s_in, m2[i]) + nn2[i]
        outs.append((oo[:c] + oo[c:], s_out))
    return outs


RW_PAIRS_PER_STEP = 8


def _rwscan_kernel(rf, vf, kkf, lwf, bf, kdf, rb, vb, kkb, lwb, bb, kdb, of_ref, ob_ref, sf, sb):
    @pl.when(pl.program_id(2) == 0)
    def _():
        sf[...] = jnp.zeros_like(sf)
        sb[...] = jnp.zeros_like(sb)

    streams = []
    for p in range(RW_PAIRS_PER_STEP):
        sl = slice(p * LANES, (p + 1) * LANES)
        streams.append((rf[:, sl], vf[:, sl], kkf[:, sl], lwf[:, sl], bf[:, sl], kdf[:, sl], sf[p], False))
        streams.append((rb[:, sl], vb[:, sl], kkb[:, sl], lwb[:, sl], bb[:, sl], kdb[:, sl], sb[p], True))
    outs = _rw_streams(streams)
    for p in range(RW_PAIRS_PER_STEP):
        sl = slice(p * LANES, (p + 1) * LANES)
        of_ref[:, sl], sf[p] = outs[2 * p]
        ob_ref[:, sl], sb[p] = outs[2 * p + 1]


def _rwscan(nb, seq, r, v, kk, lw0, lw1, b0, b1, kd0, kd1):
    t, rw = r.shape
    nc = seq // CHUNK
    wblk = RW_PAIRS_PER_STEP * LANES
    fw = pl.BlockSpec((CHUNK, wblk), lambda bi, hp, c: (bi * nc + c, hp))
    bw = pl.BlockSpec((CHUNK, wblk), lambda bi, hp, c: (bi * nc + nc - 1 - c, hp))
    out = jax.ShapeDtypeStruct((t, rw), F32)
    state = pltpu.VMEM((RW_PAIRS_PER_STEP, LANES, LANES), F32)
    return pl.pallas_call(
        _rwscan_kernel,
        grid=(nb, rw // wblk, nc),
        in_specs=[fw] * 6 + [bw] * 6,
        out_specs=[fw, bw],
        out_shape=[out, out],
        scratch_shapes=[state, state],
        compiler_params=_cparams(("parallel", "parallel", "arbitrary")),
        name="rwscan",
    )(r, v, kk, lw0, b0, kd0, r, v, kk, lw1, b1, kd1)


def _hg_streams(streams):
    c, dk = streams[0][0].shape
    revs = [s[5] for s in streams]
    tri = {rev: _tri(c, rev) for rev in set(revs)}
    row = lax.broadcasted_iota(I32, (c, dk), 0)
    ri = lax.broadcasted_iota(I32, (c, c), 0)
    ci = lax.broadcasted_iota(I32, (c, c), 1)

    cum = []
    for (q, k, v, lf, st, rev) in streams:
        hi, mid, lo = _split3(lf)
        cs = jnp.dot(tri[rev], jnp.concatenate([hi, mid, lo], axis=1), preferred_element_type=F32)
        cum.append((cs[:, :dk] + cs[:, dk:2 * dk] + cs[:, 2 * dk:]) * math.log2(math.e))
    scores = [jnp.where(ri == ci, jnp.sum(s[0] * s[1], axis=1, keepdims=True), 0.0) for s in streams]
    sub = row % SUBLANES
    nt = (((1,), (1,)), ((), ()))

    def sub_bcast(x, idx):
        x3 = x.reshape(c // SUBLANES, SUBLANES, dk)
        return jnp.broadcast_to(x3[:, idx:idx + 1, :], x3.shape).reshape(c, dk)

    h = c // 2
    while h >= 1:
        blk = 2 * h
        upper = (row % blk) >= h
        same_blk = (ri // blk) == (ci // blk)
        r_up = (ri % blk) >= h
        c_up = (ci % blk) >= h
        q_rows = {False: upper, True: jnp.logical_not(upper)}
        sgn = {rev: jnp.where(q_rows[rev], 1.0, -1.0) for rev in (False, True)}
        pmask = {False: jnp.logical_and(same_blk, jnp.logical_and(r_up, jnp.logical_not(c_up))),
                 True: jnp.logical_and(same_blk, jnp.logical_and(jnp.logical_not(r_up), c_up))}
        sl = []
        for i, (q, k, v, lf, st, rev) in enumerate(streams):
            cm = cum[i]
            off = h if rev else h - 1
            if h >= SUBLANES:
                pieces = [jnp.broadcast_to(cm[m0 + off:m0 + off + 1, :], (blk, dk)) for m0 in range(0, c, blk)]
                ref = jnp.concatenate(pieces, axis=0) if len(pieces) > 1 else pieces[0]
            elif blk == SUBLANES:
                ref = sub_bcast(cm, off)
            elif 2 * blk == SUBLANES:
                ref = jnp.where(sub < blk, sub_bcast(cm, off), sub_bcast(cm, off + blk))
            else:
                ref = jnp.where(q_rows[rev], pltpu.roll(cm, c - 1 if rev else 1, axis=0), cm)
            e = jnp.minimum((cm - ref) * sgn[rev], 0.0)
            x = (jnp.where(q_rows[rev], q, k) * jnp.exp2(e)).astype(BF16)
            sl.append(lax.dot_general(x, x, nt, preferred_element_type=F32))
        scores = [sc + jnp.where(pmask[rev], x, 0.0) for sc, x, rev in zip(scores, sl, revs)]
        h //= 2

    outs = []
    for (q, k, v, lf, st, rev), cm, sc in zip(streams, cum, scores):
        tot = cm[0:1, :] if rev else cm[c - 1:c, :]
        o = _dot(sc, v) + _dot_nt(q * jnp.exp2(cm), st)
        st_new = st * jnp.exp2(tot) + _dot_tn(v, k * jnp.exp2(tot - cm))
        outs.append((o, st_new))
    return outs


HG_HEADS_PER_STEP = 8
HG_CHUNKS_PER_STEP = 4


def _hgscan_kernel(dk, qf, fff, i_f, qb, ffb, i_b, lb_ref, of_ref, ob_ref, sf, sb):
    @pl.when(pl.program_id(2) == 0)
    def _():
        sf[...] = jnp.zeros_like(sf)
        sb[...] = jnp.zeros_like(sb)

    st_f = [sf[p] for p in range(HG_HEADS_PER_STEP)]
    st_b = [sb[p] for p in range(HG_HEADS_PER_STEP)]
    for sub in range(HG_CHUNKS_PER_STEP):
        rows_f = slice(sub * CHUNK, (sub + 1) * CHUNK)
        rows_b = slice((HG_CHUNKS_PER_STEP - 1 - sub) * CHUNK, (HG_CHUNKS_PER_STEP - sub) * CHUNK)
        streams = []
        for p in range(HG_HEADS_PER_STEP):
            sl = slice(p * dk, (p + 1) * dk)
            for (q_ref, ff_ref, i_ref, st, d, rows) in ((qf, fff, i_f, st_f[p], 0, rows_f),
                                                        (qb, ffb, i_b, st_b[p], 1, rows_b)):
                lbv = lb_ref[d:d + 1, sl]
                f = lbv + (1.0 - lbv) * _sigmoid(ff_ref[rows, sl])
                streams.append((_silu(q_ref[rows, sl]), 1.0 - f, i_ref[rows, sl], jnp.log(f), st, d == 1))
        outs = _hg_streams(streams)
        for p in range(HG_HEADS_PER_STEP):
            sl = slice(p * dk, (p + 1) * dk)
            of_ref[rows_f, sl], st_f[p] = outs[2 * p]
            ob_ref[rows_b, sl], st_b[p] = outs[2 * p + 1]
    for p in range(HG_HEADS_PER_STEP):
        sf[p] = st_f[p]
        sb[p] = st_b[p]


def _hgscan(nb, seq, z, lb, hg_off, hw):
    t = z.shape[0]
    rows = HG_CHUNKS_PER_STEP * CHUNK
    nc = seq // rows
    dk = hw // HG_HEADS
    wblk = HG_HEADS_PER_STEP * dk
    base = hg_off // wblk
    nh = hw // wblk

    def fw(comp):
        return pl.BlockSpec((rows, wblk), lambda bi, h, c: (bi * nc + c, base + comp * nh + h))

    def bw(comp):
        return pl.BlockSpec((rows, wblk), lambda bi, h, c: (bi * nc + nc - 1 - c, base + comp * nh + h))

    out = jax.ShapeDtypeStruct((t, hw), F32)
    state = pltpu.VMEM((HG_HEADS_PER_STEP, dk, dk), F32)
    return pl.pallas_call(
        functools.partial(_hgscan_kernel, dk),
        grid=(nb, nh, nc),
        in_specs=[fw(0), fw(1), fw(3), bw(0), bw(2), bw(3),
                  pl.BlockSpec((2, wblk), lambda bi, h, c: (0, h))],
        out_specs=[pl.BlockSpec((rows, wblk), lambda bi, h, c: (bi * nc + c, h)),
                   pl.BlockSpec((rows, wblk), lambda bi, h, c: (bi * nc + nc - 1 - c, h))],
        out_shape=[out, out],
        scratch_shapes=[state, state],
        compiler_params=_cparams(("parallel", "parallel", "arbitrary")),
        name="hgscan",
    )(z, z, z, z, z, z, lb)


def _blockdiag2(w):
    _, r, n = w.shape
    z = jnp.zeros((r, n), w.dtype)
    return jnp.concatenate([jnp.concatenate([w[0], z], axis=1), jnp.concatenate([z, w[1]], axis=1)], axis=0)


def _indicator(width, seg):
    e = (jnp.arange(width)[:, None] // seg == jnp.arange(width // seg)[None, :]).astype(BF16)
    return e, e.T


def _layer_consts(w_in, rw_mu, rw_w0, rw_w_up, rw_a0, rw_a_up, rw_g_up, rw_k_k, rw_k_a, rw_r_k,
                  hg_lb_gamma):
    rw = rw_k_k.shape[-1]
    d = w_in.shape[1]
    nlat = 2 * W_LORA + 2 * A_LORA + G_LORA
    w = w_in[0]
    rkv = 3 * rw
    hg_cols = w.shape[1] - rkv - nlat
    pad = 512 - nlat
    w_perm = jnp.concatenate([w[:, :rkv], w[:, rkv + nlat:], w[:, rkv:rkv + nlat],
                              jnp.zeros((d, pad), w.dtype)], axis=1).astype(BF16)
    mu = rw_mu[0]
    lower = jnp.cumsum(jax.nn.softmax(hg_lb_gamma.astype(F32), axis=0), axis=0)[0]
    hw = lower.shape[-1]
    e64, e64t = _indicator(rw, RW_HEAD_DIM)
    return dict(
        rw=rw, hw=hw, hg_off=rkv, lat_off=rkv + hg_cols, w_in=w_perm,
        mu_rkv=mu[:rkv].reshape(1, rkv),
        mu_lat=jnp.pad(mu[rkv:rkv + nlat], (0, pad)).reshape(1, 512),
        wup=_blockdiag2(rw_w_up[0]).astype(BF16), aup=_blockdiag2(rw_a_up[0]).astype(BF16),
        gup=rw_g_up[0].astype(BF16),
        w0=rw_w0[0].reshape(1, 2 * rw), a0=rw_a0[0].reshape(1, 2 * rw),
        k_k=rw_k_k[0].reshape(1, rw), k_a=rw_k_a[0].reshape(1, rw), r_k=rw_r_k[0].reshape(1, rw),
        e64=e64, e64t=e64t,
        lb=lower,
    )


def _mixer(x2, nb, seq, mod, norm_pre_mix, wc):
    z = _inproj(x2, mod, norm_pre_mix, wc['w_in'], seq)
    (r, v, kk, g, bonus, lw0, lw1, b0, b1, kd0, kd1) = _rwprep(
        z, seq, wc['rw'], wc['lat_off'], wc['mu_rkv'], wc['mu_lat'], wc['wup'], wc['aup'], wc['gup'],
        wc['w0'], wc['a0'], wc['k_k'], wc['k_a'], wc['r_k'], wc['e64'], wc['e64t'])
    rw_of, rw_ob = _rwscan(nb, seq, r, v, kk, lw0, lw1, b0, b1, kd0, kd1)
    hg_of, hg_ob = _hgscan(nb, seq, z, wc['lb'], wc['hg_off'], wc['hw'])
    return dict(z=z, r=r, v=v, kk=kk, g=g, bonus=bonus, lw0=lw0, rw_of=rw_of, rw_ob=rw_ob,
                hg_of=hg_of, hg_ob=hg_ob)


def _outproj_kernel(rw, x_ref, rf_ref, rb_ref, bon_ref, g_ref, hf_ref, hb_ref, hgg_ref, mod_ref,
                    lnw_ref, lnb_ref, hnw_ref, npm_ref, npf_ref, wout_ref, wrh_ref, wrl_ref,
                    e64_ref, e64t_ref, e128_ref, e128t_ref,
                    x1_ref, h2_ref, lg_ref):
    m6 = mod_ref[0]
    o = rf_ref[...] + rb_ref[...]
    mean = _seg_sum(o, e64_ref[...], e64t_ref[...]) * (1.0 / RW_HEAD_DIM)
    dlt = o - mean
    var = _seg_sum(dlt * dlt, e64_ref[...], e64t_ref[...]) * (1.0 / RW_HEAD_DIM)
    o_rw = (dlt * lax.rsqrt(var + RW_GN_EPS) * lnw_ref[...] + lnb_ref[...] + bon_ref[...]) * g_ref[...]
    oh = hf_ref[...] + hb_ref[...]
    hd = oh.shape[1] // HG_HEADS
    ms = _seg_sum(oh * oh, e128_ref[...], e128t_ref[...]) * (1.0 / hd)
    o_hg = oh * lax.rsqrt(ms + NORM_EPS) * hnw_ref[...] * _silu(hgg_ref[...])
    m = _dot(o_rw, wout_ref[0:rw, :]) + _dot(o_hg, wout_ref[rw:, :])
    x1 = x_ref[...] + m6[2:3] * _rms_rows(m, npm_ref[...])
    x1_ref[...] = x1
    h2 = _rms_rows(x1, npf_ref[...]) * (1.0 + m6[4:5]) + m6[3:4]
    hi, lo = _split2(h2)
    _pack_rows(hi, h2_ref)
    nt = (((1,), (1,)), ((), ()))
    lg_ref[...] = (lax.dot_general(wrh_ref[...], hi, nt, preferred_element_type=F32)
                   + lax.dot_general(wrh_ref[...], lo, nt, preferred_element_type=F32)
                   + lax.dot_general(wrl_ref[...], hi, nt, preferred_element_type=F32))


def _outproj(x2, seq, mx, z, mod, wc, oc):
    t, d = x2.shape
    rw, hw = wc['rw'], wc['hw']
    tm = 256
    gblk = (wc['hg_off'] + 4 * hw) // hw

    def row(w):
        return pl.BlockSpec((tm, w), lambda i: (i, 0))

    def full(a):
        return pl.BlockSpec(a.shape, lambda i: (0,) * a.ndim)

    consts = [oc['ln_w'], oc['ln_b'], oc['hg_norm_w'], oc['npm'], oc['npf'], oc['w_out'], oc['wr_hi'],
              oc['wr_lo'], wc['e64'], wc['e64t'], oc['e128'], oc['e128t']]
    return pl.pallas_call(
        functools.partial(_outproj_kernel, rw),
        grid=(t // tm,),
        in_specs=[row(d), row(rw), row(rw), row(rw), row(rw), row(hw), row(hw),
                  pl.BlockSpec((tm, hw), lambda i: (i, gblk)),
                  pl.BlockSpec((1, 6, d), lambda i: ((i * tm) // seq, 0, 0))] + [full(a) for a in consts],
        out_specs=[row(d), pl.BlockSpec((tm * (d // PACK_W), LANES), lambda i: (i, 0)),
                   pl.BlockSpec((N_EXPERTS, tm), lambda i: (0, i))],
        out_shape=[jax.ShapeDtypeStruct((t, d), F32), jax.ShapeDtypeStruct((t * (d // PACK_W), LANES), jnp.uint32),
                   jax.ShapeDtypeStruct((N_EXPERTS, t), F32)],
        compiler_params=_cparams(("parallel",)),
        name="outproj",
    )(x2, mx['rw_of'], mx['rw_ob'], mx['bonus'], mx['g'], mx['hg_of'], mx['hg_ob'], z, mod, *consts)


ROUTE_TILE = 512


def _route_kernel(lg_ref, bias_ref, ut_ref, eidx_ref, wsel_ref, rank_ref, cnt_ref, carry):
    @pl.when(pl.program_id(0) == 0)
    def _():
        carry[...] = jnp.zeros_like(carry)

    ne, tt = lg_ref.shape
    gsz = ne // N_GROUPS
    neg = -jnp.inf
    s = _sigmoid(lg_ref[...])
    biased = s + bias_ref[...]
    io_g = lax.broadcasted_iota(I32, (gsz, tt), 0)
    gs_rows = []
    for gi in range(N_GROUPS):
        blk = biased[gi * gsz:(gi + 1) * gsz, :]
        m1 = jnp.max(blk, axis=0, keepdims=True)
        first = jnp.min(jnp.where(blk == m1, io_g, gsz), axis=0, keepdims=True)
        m2 = jnp.max(jnp.where(io_g == first, neg, blk), axis=0, keepdims=True)
        gs_rows.append(m1 + m2)
    gs = jnp.concatenate(gs_rows, axis=0)
    io_n = lax.broadcasted_iota(I32, (N_GROUPS, tt), 0)
    selg = jnp.zeros((N_GROUPS, tt), jnp.bool_)
    for _ in range(TOPK_GROUPS):
        m = jnp.max(gs, axis=0, keepdims=True)
        first = jnp.min(jnp.where(gs == m, io_n, N_GROUPS), axis=0, keepdims=True)
        pick = io_n == first
        selg = jnp.logical_or(selg, pick)
        gs = jnp.where(pick, neg, gs)
    emask = jnp.concatenate([jnp.broadcast_to(selg[gi:gi + 1, :], (gsz, tt)) for gi in range(N_GROUPS)],
                            axis=0)
    mb = jnp.where(emask, biased, neg)
    io_e = lax.broadcasted_iota(I32, (ne, tt), 0)
    sel = jnp.zeros((ne, tt), jnp.bool_)
    picks, idxs, ws = [], [], []
    for _ in range(TOP_K):
        m = jnp.max(mb, axis=0, keepdims=True)
        first = jnp.min(jnp.where(mb == m, io_e, ne), axis=0, keepdims=True)
        pick = io_e == first
        picks.append(pick)
        idxs.append(first)
        ws.append(jnp.sum(jnp.where(pick, s, 0.0), axis=0, keepdims=True))
        sel = jnp.logical_or(sel, pick)
        mb = jnp.where(pick, neg, mb)
    wsum = ws[0]
    for w in ws[1:]:
        wsum = wsum + w
    pos = jnp.dot(jnp.where(sel, 1.0, 0.0).astype(BF16), ut_ref[...], preferred_element_type=F32) + carry[...]
    ranks = [jnp.sum(jnp.where(p, pos, 0.0), axis=0, keepdims=True).astype(I32) for p in picks]
    carry[...] = carry[...] + jnp.sum(jnp.where(sel, 1.0, 0.0), axis=1, keepdims=True)
    zi = jnp.zeros((SUBLANES - TOP_K, tt), I32)
    eidx_ref[...] = jnp.concatenate(idxs + [zi], axis=0)
    rank_ref[...] = jnp.concatenate(ranks + [zi], axis=0)
    wsel_ref[...] = jnp.concatenate([w / wsum * ROUTED_SCALE for w in ws] + [zi.astype(F32)], axis=0)
    cnt_ref[...] = jnp.broadcast_to(carry[...], cnt_ref.shape).astype(I32)


def _route(logits_t, e_bias):
    ne, t = logits_t.shape
    tt = ROUTE_TILE
    ut = (jnp.arange(tt)[:, None] < jnp.arange(tt)[None, :]).astype(BF16)
    tok = pl.BlockSpec((SUBLANES, tt), lambda i: (0, i))
    return pl.pallas_call(
        _route_kernel,
        grid=(t // tt,),
        in_specs=[pl.BlockSpec((ne, tt), lambda i: (0, i)),
                  pl.BlockSpec((ne, 1), lambda i: (0, 0)),
                  pl.BlockSpec((tt, tt), lambda i: (0, 0))],
        out_specs=[tok, tok, tok, pl.BlockSpec((ne, LANES), lambda i: (0, 0))],
        out_shape=[jax.ShapeDtypeStruct((SUBLANES, t), I32), jax.ShapeDtypeStruct((SUBLANES, t), F32),
                   jax.ShapeDtypeStruct((SUBLANES, t), I32), jax.ShapeDtypeStruct((ne, LANES), I32)],
        scratch_shapes=[pltpu.VMEM((ne, 1), F32)],
        compiler_params=_cparams(("arbitrary",)),
        name="route",
    )(logits_t, e_bias.reshape(ne, 1), ut)


DISPATCH_TILE = 128
PLAN_TILE = 1024


def _plan_kernel(eidx_ref, rank_ref, ps_ref, dest_ref):
    ne = ps_ref.shape[0]
    tp = eidx_ref.shape[1]
    io_e = lax.broadcasted_iota(I32, (ne, tp), 0)
    ps = ps_ref[...]
    rows = []
    for j in range(TOP_K):
        start = jnp.sum(jnp.where(io_e == eidx_ref[j:j + 1, :], ps, 0.0), axis=0, keepdims=True)
        rows.append(start.astype(I32) + rank_ref[j:j + 1, :])
    dest = jnp.concatenate(rows + [jnp.zeros((SUBLANES - TOP_K, tp), I32)], axis=0)
    for i in range(tp // DISPATCH_TILE):
        dest_ref[i] = dest[:, i * DISPATCH_TILE:(i + 1) * DISPATCH_TILE]


def _plan(eidx, rank, pad_start):
    t = eidx.shape[1]
    tp = PLAN_TILE
    ne = pad_start.shape[0]
    tok = pl.BlockSpec((SUBLANES, tp), lambda i: (0, i))
    ntile = tp // DISPATCH_TILE
    return pl.pallas_call(
        _plan_kernel,
        grid=(t // tp,),
        in_specs=[tok, tok, pl.BlockSpec((ne, 1), lambda i: (0, 0))],
        out_specs=pl.BlockSpec((ntile, SUBLANES, DISPATCH_TILE), lambda i: (i, 0, 0)),
        out_shape=jax.ShapeDtypeStruct((t // DISPATCH_TILE, SUBLANES, DISPATCH_TILE), I32),
        compiler_params=_cparams(("parallel",)),
        name="plan",
    )(eidx, rank, pad_start.astype(F32).reshape(ne, 1))


def _dispatch_kernel(tt, nsteps, zs_ref, zc_ref, nu_ref, dest_ref, h_ref, xs_ref, idx_smem, zero_blk,
                     isem, sem, zsem, bsem):
    i = pl.program_id(0)
    n = tt * TOP_K
    ne = zs_ref.shape[0]
    nblk = xs_ref.shape[0] // EXPERT_BLOCK
    icp = pltpu.make_async_copy(dest_ref.at[pl.ds(i * tt * SUBLANES, n)], idx_smem, isem)
    icp.start()

    @pl.when(i == 0)
    def _():
        zero_blk[...] = jnp.zeros_like(zero_blk)

    icp.wait()

    def row_copy(k):
        return pltpu.make_async_copy(h_ref.at[k % tt], xs_ref.at[idx_smem[k]], sem)

    def drain(k, c):
        row_copy(k).wait()
        return c

    for k in range(n):
        row_copy(k).start(priority=k % 2)

    for q in range(pl.cdiv(ne, nsteps)):
        e = i + q * nsteps
        ec = jnp.minimum(e, ne - 1)
        start = zs_ref[ec]
        count = jnp.where(e < ne, zc_ref[ec], 0)

        def zero_copy(r):
            return pltpu.make_async_copy(zero_blk.at[0], xs_ref.at[start + r], zsem)

        def z_issue(r, c):
            zero_copy(r).start()
            return c

        def z_drain(r, c):
            zero_copy(r).wait()
            return c

        lax.fori_loop(0, count, z_issue, 0)
        lax.fori_loop(0, count, z_drain, 0)

    for q in range(pl.cdiv(nblk, nsteps)):
        blk = nu_ref[0] + i + q * nsteps

        @pl.when(blk < nblk)
        def _():
            bcp = pltpu.make_async_copy(
                zero_blk, xs_ref.at[pl.ds(pl.multiple_of(blk * EXPERT_BLOCK, EXPERT_BLOCK), EXPERT_BLOCK)], bsem)
            bcp.start()
            bcp.wait()

    lax.fori_loop(0, n, drain, 0, unroll=8)


def _dispatch(dest_flat, h2p, p_rows, zero_start, zero_count, n_used):
    t = h2p.shape[0]
    tt = DISPATCH_TILE
    anyspec = pl.BlockSpec(memory_space=pl.ANY)
    dma = pltpu.SemaphoreType.DMA(())
    grid_spec = pltpu.PrefetchScalarGridSpec(
        num_scalar_prefetch=3,
        grid=(t // tt,),
        in_specs=[anyspec, pl.BlockSpec((tt,) + h2p.shape[1:], lambda i, zs, zc, nu: (i, 0, 0))],
        out_specs=anyspec,
        scratch_shapes=[pltpu.SMEM((tt * TOP_K,), I32), pltpu.VMEM((EXPERT_BLOCK,) + h2p.shape[1:], h2p.dtype),
                        dma, dma, dma, dma],
    )
    return pl.pallas_call(
        functools.partial(_dispatch_kernel, tt, t // tt),
        grid_spec=grid_spec,
        out_shape=jax.ShapeDtypeStruct((p_rows,) + h2p.shape[1:], h2p.dtype),
        compiler_params=_cparams(("arbitrary",)),
        name="dispatch",
    )(zero_start, zero_count, n_used, dest_flat, h2p)


def _experts_kernel(be_ref, nu_ref, x_ref, wg_ref, wu_ref, wd_ref, y_ref, wg_s, wu_s, wd_s):
    b = pl.program_id(0)
    live = b < nu_ref[0]
    new_expert = jnp.logical_or(b == 0, be_ref[b] != be_ref[jnp.maximum(b - 1, 0)])

    @pl.when(jnp.logical_and(live, new_expert))
    def _():
        wg_s[...] = wg_ref[0].astype(BF16)
        wu_s[...] = wu_ref[0].astype(BF16)
        wd_s[...] = wd_ref[0].astype(BF16)

    @pl.when(live)
    def _():
        x = _unpack_rows(x_ref, wg_ref.shape[1] // PACK_W)
        gate = jnp.dot(x, wg_s[...], preferred_element_type=F32)
        up = jnp.dot(x, wu_s[...], preferred_element_type=F32)
        _pack_rows(_dot(_silu(gate) * up, wd_s[...]).astype(BF16), y_ref)

    @pl.when(b >= nu_ref[0])
    def _():
        y_ref[...] = jnp.zeros_like(y_ref)


def _experts(block_e, n_used, xs, wg, wu, wd):
    d, de = wg.shape[1], wg.shape[2]
    ns = d // PACK_W
    p_rows = xs.shape[0] // ns
    nblk = p_rows // EXPERT_BLOCK

    def live(b, nu):
        return jnp.minimum(b, jnp.maximum(nu[0] - 1, 0))

    grid_spec = pltpu.PrefetchScalarGridSpec(
        num_scalar_prefetch=2,
        grid=(nblk,),
        in_specs=[pl.BlockSpec((EXPERT_BLOCK * ns, LANES), lambda b, be, nu: (live(b, nu), 0)),
                  pl.BlockSpec((1, d, de), lambda b, be, nu: (be[live(b, nu)], 0, 0)),
                  pl.BlockSpec((1, d, de), lambda b, be, nu: (be[live(b, nu)], 0, 0)),
                  pl.BlockSpec((1, de, d), lambda b, be, nu: (be[live(b, nu)], 0, 0))],
        out_specs=pl.BlockSpec((EXPERT_BLOCK * ns, LANES), lambda b, be, nu: (b, 0)),
        scratch_shapes=[pltpu.VMEM((d, de), BF16), pltpu.VMEM((d, de), BF16), pltpu.VMEM((de, d), BF16)],
    )
    return pl.pallas_call(
        _experts_kernel,
        grid_spec=grid_spec,
        out_shape=jax.ShapeDtypeStruct((p_rows * ns, LANES), jnp.uint32),
        compiler_params=_cparams(("arbitrary",)),
        name="experts",
    )(block_e, n_used, xs, wg, wu, wd)


def _combine_kernel(tt, nsteps, dest_ref, y_ref, x1_ref, h2_ref, w_ref, mod_ref, npf_ref, sg_ref, su_ref, sd_ref,
                    o_ref, ybuf, idx_smem, isem, sems):
    i = pl.program_id(0)
    n = tt * TOP_K
    ns = sg_ref.shape[0] // PACK_W

    def row_copy(slot, k):
        dst = ybuf.at[pl.ds(pl.multiple_of((slot * n + k) * ns, ns), ns)]
        return pltpu.make_async_copy(y_ref.at[idx_smem[slot * n + k]], dst, sems.at[slot])

    def issue(step, slot):
        icp = pltpu.make_async_copy(dest_ref.at[pl.ds(step * tt * SUBLANES, n)],
                                    idx_smem.at[pl.ds(slot * n, n)], isem)
        icp.start()
        icp.wait()
        for k in range(n):
            row_copy(slot, k).start(priority=k % 2)

    def drain(slot):
        def body(k, c):
            row_copy(slot, k).wait()
            return c

        lax.fori_loop(0, n, body, 0, unroll=8)

    @pl.when(i == 0)
    def _():
        issue(0, 0)

    other = (i + 1) % 2
    issue(jnp.minimum(i + 1, nsteps - 1), other)

    h2 = _unpack_rows(h2_ref, ns)
    gate = jnp.dot(h2, sg_ref[...], preferred_element_type=F32)
    up = jnp.dot(h2, su_ref[...], preferred_element_type=F32)
    shared = _dot(_silu(gate) * up, sd_ref[...])
    w = w_ref[...]
    m6 = mod_ref[0]

    def finish(slot):
        drain(slot)
        acc = [None] * (2 * ns)
        for j in range(TOP_K):
            wj = w[:, j:j + 1]
            for s in range(ns):
                word = ybuf[pl.ds((slot * n + j * tt) * ns + s, tt, stride=ns), :]
                parts = (lax.bitcast_convert_type(word & jnp.uint32(0xFFFF0000), F32),
                         lax.bitcast_convert_type(word << 16, F32))
                for h, part in enumerate(parts):
                    term = wj * part
                    acc[2 * s + h] = term if acc[2 * s + h] is None else acc[2 * s + h] + term
        routed = jnp.concatenate(acc, axis=1)
        o_ref[...] = x1_ref[...] + m6[5:6] * _rms_rows(routed + shared, npf_ref[...])

    for slot in (0, 1):
        @pl.when(i % 2 == slot)
        def _():
            finish(slot)

    @pl.when(i == nsteps - 1)
    def _():
        drain(other)


def _combine(dest_flat, y3, x1, h2, wsel_t, mod, seq, npf, sg, su, sd):
    t, d = x1.shape
    tt = DISPATCH_TILE
    ns = d // PACK_W
    n = tt * TOP_K
    anyspec = pl.BlockSpec(memory_space=pl.ANY)

    def full(a):
        return pl.BlockSpec(a.shape, lambda i: (0,) * a.ndim)

    rows = pltpu.VMEM((2 * n * ns, LANES), jnp.uint32)
    return pl.pallas_call(
        functools.partial(_combine_kernel, tt, t // tt),
        grid=(t // tt,),
        in_specs=[anyspec, anyspec,
                  pl.BlockSpec((tt, d), lambda i: (i, 0)),
                  pl.BlockSpec((tt * ns, LANES), lambda i: (i, 0)),
                  pl.BlockSpec((tt, SUBLANES), lambda i: (i, 0)),
                  pl.BlockSpec((1, 6, d), lambda i: ((i * tt) // seq, 0, 0)),
                  full(npf), full(sg), full(su), full(sd)],
        out_specs=pl.BlockSpec((tt, d), lambda i: (i, 0)),
        out_shape=jax.ShapeDtypeStruct((t, d), F32),
        scratch_shapes=[rows, pltpu.SMEM((2 * n,), I32), pltpu.SemaphoreType.DMA(()),
                        pltpu.SemaphoreType.DMA((2,))],
        compiler_params=_cparams(("arbitrary",)),
        name="combine",
    )(dest_flat, y3, x1, h2, wsel_t, mod, npf, sg, su, sd)


def _moe_plan(eidx, rank, cnt, t):
    counts = cnt[:, 0]
    padded = (counts + EXPERT_BLOCK - 1) // EXPERT_BLOCK * EXPERT_BLOCK
    pad_end = jnp.cumsum(padded)
    pad_start = pad_end - padded
    n_blocks = (t * TOP_K + EXPERT_BLOCK - 1) // EXPERT_BLOCK + N_EXPERTS
    first_row = jnp.arange(n_blocks, dtype=I32) * EXPERT_BLOCK
    block_e = jnp.minimum(jnp.sum((pad_end[None, :] <= first_row[:, None]).astype(I32), axis=1),
                          N_EXPERTS - 1).astype(I32)
    n_used = (pad_end[-1:] // EXPERT_BLOCK).astype(I32)
    dest_flat = _plan(eidx, rank, pad_start).reshape(-1)
    zero_rows = ((pad_start + counts).astype(I32), (padded - counts).astype(I32))
    return block_e, n_used, dest_flat, n_blocks * EXPERT_BLOCK, zero_rows


def _trunk(x, mod, wc, oc, ec, norm_pre_mix):
    nb, seq, d = x.shape
    t = nb * seq
    x2 = x.reshape(t, d)
    mx = _mixer(x2, nb, seq, mod, norm_pre_mix, wc)
    x1, h2, logits_t = _outproj(x2, seq, mx, mx['z'], mod, wc, oc)
    eidx, wsel, rank, cnt = _route(logits_t, ec['e_bias'])
    block_e, n_used, dest_flat, p_rows, zero_rows = _moe_plan(eidx, rank, cnt, t)
    ns = d // PACK_W
    xs = _dispatch(dest_flat, h2.reshape(t, ns, LANES), p_rows, *zero_rows, n_used)
    y = _experts(block_e, n_used, xs.reshape(p_rows * ns, LANES), ec['wg'], ec['wu'], ec['wd'])
    out = _combine(dest_flat, y.reshape(p_rows, ns, LANES), x1, h2, wsel.T, mod, seq, oc['npo'],
                   ec['sg'], ec['su'], ec['sd'])
    return out.reshape(nb, seq, d)


def kernel(x_prompt, x_sample, c_prompt, c_sample, w_ada, b_ada, norm_pre_mix, norm_post_mix, norm_pre_ffn, norm_post_ffn, w_in, rw_mu, rw_w0, rw_w_up, rw_a0, rw_a_up, rw_g_up, rw_k_k, rw_k_a, rw_r_k, rw_ln_w, rw_ln_b, hg_lb_gamma, hg_norm_w, w_out, w_router, e_bias, w_exp_gate, w_exp_up, w_exp_down, w_sh_gate, w_sh_up, w_sh_down):
    d = x_prompt.shape[-1]
    wc = _layer_consts(w_in, rw_mu, rw_w0, rw_w_up, rw_a0, rw_a_up, rw_g_up, rw_k_k, rw_k_a, rw_r_k,
                       hg_lb_gamma)
    rw, hw = wc['rw'], wc['hw']
    e128, e128t = _indicator(hw, hw // HG_HEADS)
    wr_hi, wr_lo = _split2(w_router[0].T)
    oc = dict(ln_w=rw_ln_w[0].reshape(1, rw), ln_b=rw_ln_b[0].reshape(1, rw),
              hg_norm_w=hg_norm_w[0].reshape(1, hw), npm=norm_post_mix[0].reshape(1, d),
              npf=norm_pre_ffn[0].reshape(1, d), npo=norm_post_ffn[0].reshape(1, d),
              w_out=w_out[0].astype(BF16), wr_hi=wr_hi, wr_lo=wr_lo, e128=e128, e128t=e128t)
    ec = dict(e_bias=e_bias[0], wg=w_exp_gate[0], wu=w_exp_up[0], wd=w_exp_down[0], sg=w_sh_gate[0].astype(BF16), su=w_sh_up[0].astype(BF16),
              sd=w_sh_down[0].astype(BF16))
    nbp = c_prompt.shape[0]
    mod = _ada(jnp.concatenate([c_prompt, c_sample], axis=0), w_ada[0], b_ada[0]).reshape(-1, 6, d)
    y_prompt = _trunk(x_prompt, mod[:nbp], wc, oc, ec, norm_pre_mix[0])
    y_sample = _trunk(x_sample, mod[nbp:], wc, oc, ec, norm_pre_mix[0])
    return (y_prompt, y_sample)
```

```python
import functools
import math

import jax
import jax.numpy as jnp
from jax import lax
from jax.experimental import pallas as pl
from jax.experimental.pallas import tpu as pltpu

F32 = jnp.float32
BF16 = jnp.bfloat16
I32 = jnp.int32

RW_HEAD_DIM = 64
W_LORA = 64
A_LORA = 64
G_LORA = 128
RW_GN_EPS = 64e-5
HG_HEADS = 8
N_EXPERTS = 64
TOP_K = 6
N_GROUPS = 8
TOPK_GROUPS = 4
ROUTED_SCALE = 2.5
EXPERT_BLOCK = 256
NORM_EPS = 1e-6

LANES = 128
SUBLANES = 8
VMEM_LIMIT = 56 * 1024 * 1024

CHUNK = 64


def _cparams(sem, vmem=VMEM_LIMIT):
    return pltpu.CompilerParams(dimension_semantics=sem, vmem_limit_bytes=vmem)


def _sigmoid(x):
    return 1.0 / (1.0 + jnp.exp(-x))


def _silu(x):
    return x * _sigmoid(x)


def _dot(a, b):
    return jnp.dot(a.astype(BF16), b.astype(BF16), preferred_element_type=F32)


def _dot_nt(a, b):
    return lax.dot_general(a.astype(BF16), b.astype(BF16), (((1,), (1,)), ((), ())),
                           preferred_element_type=F32)


def _dot_tn(a, b):
    return lax.dot_general(a.astype(BF16), b.astype(BF16), (((0,), (0,)), ((), ())),
                           preferred_element_type=F32)


def _split2(x):
    hi = x.astype(BF16)
    lo = (x - hi.astype(F32)).astype(BF16)
    return hi, lo


def _split3(x):
    hi = x.astype(BF16)
    r1 = x - hi.astype(F32)
    mid = r1.astype(BF16)
    lo = (r1 - mid.astype(F32)).astype(BF16)
    return hi, mid, lo


def _seg_sum(x, e, et):
    hi, lo = _split2(x)
    s = jnp.dot(hi, e, preferred_element_type=F32) + jnp.dot(lo, e, preferred_element_type=F32)
    shi, slo = _split2(s)
    return jnp.dot(shi, et, preferred_element_type=F32) + jnp.dot(slo, et, preferred_element_type=F32)


PACK_W = 2 * LANES


def _pack_rows(x_bf, out_ref):
    n, d = x_bf.shape
    ns = d // PACK_W
    for s in range(ns):
        a = lax.bitcast_convert_type(x_bf[:, s * PACK_W:s * PACK_W + LANES].astype(F32), jnp.uint32)
        b = lax.bitcast_convert_type(x_bf[:, s * PACK_W + LANES:(s + 1) * PACK_W].astype(F32), jnp.uint32)
        out_ref[pl.ds(s, n, stride=ns), :] = a | (b >> 16)


def _unpack_rows(x_ref, ns):
    n = x_ref.shape[0] // ns
    parts = []
    for s in range(ns):
        w = x_ref[pl.ds(s, n, stride=ns), :]
        parts.append(lax.bitcast_convert_type(w & jnp.uint32(0xFFFF0000), F32).astype(BF16))
        parts.append(lax.bitcast_convert_type(w << 16, F32).astype(BF16))
    return jnp.concatenate(parts, axis=1)


def _rms_rows(x, g):
    return x * lax.rsqrt(jnp.mean(x * x, axis=-1, keepdims=True) + NORM_EPS) * g


def _ada_kernel(c_ref, w_ref, b_ref, o_ref):
    c = c_ref[...]
    o_ref[...] = _dot(_silu(c), w_ref[...]) + b_ref[...]


def _ada(c, w_ada, b_ada):
    nb, d = c.shape
    n = w_ada.shape[1]
    tn = 512
    return pl.pallas_call(
        _ada_kernel,
        grid=(n // tn,),
        in_specs=[pl.BlockSpec((nb, d), lambda j: (0, 0)),
                  pl.BlockSpec((d, tn), lambda j: (0, j)),
                  pl.BlockSpec((1, tn), lambda j: (0, j))],
        out_specs=pl.BlockSpec((nb, tn), lambda j: (0, j)),
        out_shape=jax.ShapeDtypeStruct((nb, n), F32),
        compiler_params=_cparams(("parallel",)),
        name="ada",
    )(c, w_ada, b_ada.reshape(1, n))


def _inproj_kernel(x_ref, mod_ref, g_ref, w_ref, o_ref, h_scr):
    @pl.when(pl.program_id(1) == 0)
    def _():
        m = mod_ref[0]
        h = _rms_rows(x_ref[...], g_ref[...]) * (1.0 + m[1:2]) + m[0:1]
        h_scr[...] = h.astype(BF16)

    o_ref[...] = jnp.dot(h_scr[...], w_ref[...], preferred_element_type=F32)


def _inproj(x2, mod, g, w_bf, seq):
    t, d = x2.shape
    n = w_bf.shape[1]
    tm, tn = 1024, 512
    return pl.pallas_call(
        _inproj_kernel,
        grid=(t // tm, n // tn),
        in_specs=[pl.BlockSpec((tm, d), lambda i, j: (i, 0)),
                  pl.BlockSpec((1, 6, d), lambda i, j: ((i * tm) // seq, 0, 0)),
                  pl.BlockSpec((1, d), lambda i, j: (0, 0)),
                  pl.BlockSpec((d, tn), lambda i, j: (0, j))],
        out_specs=pl.BlockSpec((tm, tn), lambda i, j: (i, j)),
        out_shape=jax.ShapeDtypeStruct((t, n), F32),
        scratch_shapes=[pltpu.VMEM((tm, d), BF16)],
        compiler_params=_cparams(("parallel", "arbitrary")),
        name="inproj",
    )(x2, mod, g.reshape(1, d), w_bf)


def _rwprep_kernel(seq, tm, rw,
                   z_ref, zp_ref, zn_ref, l_ref, lp_ref, ln_ref,
                   mu_ref, mul_ref, wup_ref, aup_ref, gup_ref, w0_ref, a0_ref,
                   kk_ref, ka_ref, rk_ref, e_ref, et_ref,
                   r_o, v_o, kk_o, g_o, bon_o, lw0_o, lw1_o, b0_o, b1_o, kd0_o, kd1_o):
    i = pl.program_id(0)
    first = (i * tm) % seq == 0
    last = ((i + 1) * tm) % seq == 0

    def shifted(cur, prev_blk, next_blk, mu):
        rows = lax.broadcasted_iota(I32, cur.shape, 0)
        prow = jnp.where(first, 0.0, prev_blk[SUBLANES - 1:SUBLANES, :])
        nrow = jnp.where(last, 0.0, next_blk[0:1, :])
        prev = jnp.where(rows == 0, prow, pltpu.roll(cur, 1, axis=0))
        nxt = jnp.where(rows == tm - 1, nrow, pltpu.roll(cur, tm - 1, axis=0))
        return cur + mu * (0.5 * (prev + nxt) - cur)

    lat = shifted(l_ref[...], lp_ref[...], ln_ref[...], mul_ref[...])
    w_lat = lat[:, 0:2 * W_LORA]
    a_lat = lat[:, 2 * W_LORA:2 * W_LORA + 2 * A_LORA]
    g_lat = lat[:, 2 * W_LORA + 2 * A_LORA:2 * W_LORA + 2 * A_LORA + G_LORA]
    w_raw = _dot(jnp.tanh(w_lat), wup_ref[...]) + w0_ref[...]
    a_all = _sigmoid(_dot(a_lat, aup_ref[...]) + a0_ref[...])
    g_o[...] = _dot(_sigmoid(g_lat), gup_ref[...])
    lw = (-math.exp(-0.5)) * _sigmoid(w_raw)
    lw0_o[...] = lw[:, :rw]
    lw1_o[...] = lw[:, rw:]

    r = shifted(z_ref[:, 0:rw], zp_ref[:, 0:rw], zn_ref[:, 0:rw], mu_ref[:, 0:rw])
    k = shifted(z_ref[:, rw:2 * rw], zp_ref[:, rw:2 * rw], zn_ref[:, rw:2 * rw], mu_ref[:, rw:2 * rw])
    v = shifted(z_ref[:, 2 * rw:3 * rw], zp_ref[:, 2 * rw:3 * rw], zn_ref[:, 2 * rw:3 * rw],
                mu_ref[:, 2 * rw:3 * rw])
    r_o[...] = r
    v_o[...] = v
    kk = k * kk_ref[...]
    ss = _seg_sum(kk * kk, e_ref[...], et_ref[...])
    kk = kk / jnp.maximum(jnp.sqrt(ss), 1e-12)
    kk_o[...] = kk
    a0 = a_all[:, :rw]
    a1 = a_all[:, rw:]
    b0_o[...] = kk * a0
    b1_o[...] = kk * a1
    kd0 = k * (1.0 + (a0 - 1.0) * ka_ref[...])
    kd1 = k * (1.0 + (a1 - 1.0) * ka_ref[...])
    kd0_o[...] = kd0
    kd1_o[...] = kd1
    kb = 0.5 * (kd0 + kd1)
    bon_o[...] = _seg_sum(r * kb * rk_ref[...], e_ref[...], et_ref[...]) * v


def _rwprep(z, seq, rw, lat_off, mu_rkv, mu_lat, wup, aup, gup, w0, a0, k_k, k_a, r_k, e64, e64t):
    t = z.shape[0]
    tm = 256
    nlat = 512
    nrkv = 3 * rw
    tb = tm // SUBLANES
    nblk8 = t // SUBLANES
    lat_blk = lat_off // nlat

    def cur(i):
        return (i, 0)

    def prv(i):
        return (jnp.maximum(i * tb - 1, 0), 0)

    def nxt(i):
        return (jnp.minimum((i + 1) * tb, nblk8 - 1), 0)

    def full(shape):
        return pl.BlockSpec(shape, lambda i: (0,) * len(shape))

    out = jax.ShapeDtypeStruct((t, rw), F32)
    ospec = pl.BlockSpec((tm, rw), lambda i: (i, 0))
    return pl.pallas_call(
        functools.partial(_rwprep_kernel, seq, tm, rw),
        grid=(t // tm,),
        in_specs=[pl.BlockSpec((tm, nrkv), cur),
                  pl.BlockSpec((SUBLANES, nrkv), prv),
                  pl.BlockSpec((SUBLANES, nrkv), nxt),
                  pl.BlockSpec((tm, nlat), lambda i: (i, lat_blk)),
                  pl.BlockSpec((SUBLANES, nlat), lambda i: (jnp.maximum(i * tb - 1, 0), lat_blk)),
                  pl.BlockSpec((SUBLANES, nlat), lambda i: (jnp.minimum((i + 1) * tb, nblk8 - 1), lat_blk)),
                  full((1, nrkv)), full((1, nlat)),
                  full(wup.shape), full(aup.shape), full(gup.shape),
                  full((1, 2 * rw)), full((1, 2 * rw)),
                  full((1, rw)), full((1, rw)), full((1, rw)),
                  full(e64.shape), full(e64t.shape)],
        out_specs=[ospec] * 11,
        out_shape=[out] * 11,
        compiler_params=_cparams(("parallel",)),
        name="rwprep",
    )(z, z, z, z, z, z, mu_rkv, mu_lat, wup, aup, gup, w0, a0, k_k, k_a, r_k, e64, e64t)


def _tri(n, rev):
    i = lax.broadcasted_iota(I32, (n, n), 0)
    j = lax.broadcasted_iota(I32, (n, n), 1)
    m = (j >= i) if rev else (j <= i)
    return jnp.where(m, 1.0, 0.0).astype(BF16)


def _rw_streams(streams):
    c = streams[0][0].shape[0]
    hd = RW_HEAD_DIM
    n2 = 2 * c
    ns = len(streams)
    revs = [s[7] for s in streams]
    lane = lax.broadcasted_iota(I32, (c, LANES), 1)
    head_a = lane < hd
    ri = lax.broadcasted_iota(I32, (n2, n2), 0)
    ci = lax.broadcasted_iota(I32, (n2, n2), 1)
    ti = ri % c
    si = ci % c
    same16 = (ri // 16) == (ci // 16)
    same32 = (ri // 32) == (ci // 32)
    mid32 = jnp.logical_and(same32, jnp.logical_not(same16))
    eye = jnp.where(ri == ci, 1.0, 0.0)
    strict = {False: si < ti, True: si > ti}
    incl = {False: si <= ti, True: si >= ti}
    tri = {rev: _tri(c, rev) for rev in set(revs)}

    def pair(x):
        return jnp.concatenate([jnp.where(head_a, x, 0.0), jnp.where(head_a, 0.0, x)], axis=0)

    cum = []
    for (r, v, kk, lw, b, kd, s_in, rev) in streams:
        hi, lo = _split2(lw)
        cs = jnp.dot(tri[rev], jnp.concatenate([hi, lo], axis=1), preferred_element_type=F32)
        cum.append(cs[:, :LANES] + cs[:, LANES:])
    ops = []
    for (r, v, kk, lw, b, kd, s_in, rev), cm in zip(streams, cum):
        tot = cm[0:1, :] if rev else cm[c - 1:c, :]
        g_inv = jnp.exp(-cm)
        g_tail = jnp.exp(tot - cm)
        ops.append(dict(p2=pair(-kk * jnp.exp(cm - lw)), r2=pair(r * jnp.exp(cm)),
                        bi2=pair(b * g_inv), ki2=pair(kd * g_inv), bt2=pair(b * g_tail),
                        kt2=pair(kd * g_tail), v2=pair(v), g_tot=jnp.exp(tot)))
    gm = [_dot_nt(jnp.concatenate([o['p2'], o['r2']], axis=0), jnp.concatenate([o['bi2'], o['ki2']], axis=0))
          for o in ops]
    a2 = [jnp.where(strict[rev], g[:n2, :n2], 0.0) for g, rev in zip(gm, revs)]
    b2 = [jnp.where(strict[rev], g[:n2, n2:], 0.0) for g, rev in zip(gm, revs)]
    ap2 = [jnp.where(incl[rev], g[n2:, :n2], 0.0) for g, rev in zip(gm, revs)]
    bp2 = [jnp.where(incl[rev], g[n2:, n2:], 0.0) for g, rev in zip(gm, revs)]
    vv = [_dot(jnp.concatenate([x, y], axis=0), o['v2']) for x, y, o in zip(b2, bp2, ops)]
    bv = [x[:n2] for x in vv]
    bpv = [x[n2:] for x in vv]

    x = [jnp.where(same16, a, 0.0) for a in a2]
    tinv = [eye + xi for xi in x]
    for _ in range(3):
        x = [_dot(xi, xi) for xi in x]
        tinv = [t + _dot(t, xi) for t, xi in zip(tinv, x)]
    for lvl in (mid32, jnp.logical_not(same32)):
        y = [_dot(t, jnp.where(lvl, a, 0.0)) for t, a in zip(tinv, a2)]
        tinv = [t + _dot(yi, t) for t, yi in zip(tinv, y)]

    wu = [_dot(t, jnp.concatenate([o['p2'], bvi], axis=1)) for t, o, bvi in zip(tinv, ops, bv)]
    qo = [_dot(a, w) for a, w in zip(ap2, wu)]
    m2 = [_dot_tn(w[:, :LANES], o['bt2']) for w, o in zip(wu, ops)]
    nn2 = [_dot_tn(w[:, LANES:], o['bt2']) + _dot_tn(o['v2'], o['kt2']) for w, o in zip(wu, ops)]

    outs = []
    for i in range(ns):
        s_in = streams[i][6]
        oo = _dot_nt(ops[i]['r2'] + qo[i][:, :LANES], s_in) + qo[i][:, LANES:] + bpv[i]
        s_out = s_in * ops[i]['g_tot'] + _dot(s_in, m2[i]) + nn2[i]
        outs.append((oo[:c] + oo[c:], s_out))
    return outs


RW_PAIRS_PER_STEP = 8
RW_CHUNKS_PER_STEP = 2


def _rwscan_kernel(rf, vf, kkf, lwf, bf, kdf, rb, vb, kkb, lwb, bb, kdb, of_ref, ob_ref, sf, sb):
    @pl.when(pl.program_id(2) == 0)
    def _():
        sf[...] = jnp.zeros_like(sf)
        sb[...] = jnp.zeros_like(sb)

    st_f = [sf[p] for p in range(RW_PAIRS_PER_STEP)]
    st_b = [sb[p] for p in range(RW_PAIRS_PER_STEP)]
    for sub in range(RW_CHUNKS_PER_STEP):
        fr = slice(sub * CHUNK, (sub + 1) * CHUNK)
        br = slice((RW_CHUNKS_PER_STEP - 1 - sub) * CHUNK, (RW_CHUNKS_PER_STEP - sub) * CHUNK)
        streams = []
        for p in range(RW_PAIRS_PER_STEP):
            sl = slice(p * LANES, (p + 1) * LANES)
            streams.append((rf[fr, sl], vf[fr, sl], kkf[fr, sl], lwf[fr, sl], bf[fr, sl], kdf[fr, sl], st_f[p], False))
            streams.append((rb[br, sl], vb[br, sl], kkb[br, sl], lwb[br, sl], bb[br, sl], kdb[br, sl], st_b[p], True))
        outs = _rw_streams(streams)
        for p in range(RW_PAIRS_PER_STEP):
            sl = slice(p * LANES, (p + 1) * LANES)
            of_ref[fr, sl], st_f[p] = outs[2 * p]
            ob_ref[br, sl], st_b[p] = outs[2 * p + 1]
    for p in range(RW_PAIRS_PER_STEP):
        sf[p] = st_f[p]
        sb[p] = st_b[p]


def _rwscan(nb, seq, r, v, kk, lw0, lw1, b0, b1, kd0, kd1):
    t, rw = r.shape
    rows = RW_CHUNKS_PER_STEP * CHUNK
    nc = seq // rows
    wblk = RW_PAIRS_PER_STEP * LANES
    fw = pl.BlockSpec((rows, wblk), lambda bi, hp, c: (bi * nc + c, hp))
    bw = pl.BlockSpec((rows, wblk), lambda bi, hp, c: (bi * nc + nc - 1 - c, hp))
    out = jax.ShapeDtypeStruct((t, rw), F32)
    state = pltpu.VMEM((RW_PAIRS_PER_STEP, LANES, LANES), F32)
    return pl.pallas_call(
        _rwscan_kernel,
        grid=(nb, rw // wblk, nc),
        in_specs=[fw] * 6 + [bw] * 6,
        out_specs=[fw, bw],
        out_shape=[out, out],
        scratch_shapes=[state, state],
        compiler_params=_cparams(("parallel", "parallel", "arbitrary")),
        name="rwscan",
    )(r, v, kk, lw0, b0, kd0, r, v, kk, lw1, b1, kd1)


def _hg_streams(streams):
    c, dk = streams[0][0].shape
    revs = [s[5] for s in streams]
    tri = {rev: _tri(c, rev) for rev in set(revs)}
    row = lax.broadcasted_iota(I32, (c, dk), 0)
    ri = lax.broadcasted_iota(I32, (c, c), 0)
    ci = lax.broadcasted_iota(I32, (c, c), 1)

    cum = []
    for (q, k, v, lf, st, rev) in streams:
        hi, mid, lo = _split3(lf)
        cs = jnp.dot(tri[rev], jnp.concatenate([hi, mid, lo], axis=1), preferred_element_type=F32)
        cum.append((cs[:, :dk] + cs[:, dk:2 * dk] + cs[:, 2 * dk:]) * math.log2(math.e))
    scores = [jnp.where(ri == ci, jnp.sum(s[0] * s[1], axis=1, keepdims=True), 0.0) for s in streams]
    sub = row % SUBLANES
    nt = (((1,), (1,)), ((), ()))

    def sub_bcast(x, idx):
        x3 = x.reshape(c // SUBLANES, SUBLANES, dk)
        return jnp.broadcast_to(x3[:, idx:idx + 1, :], x3.shape).reshape(c, dk)

    h = c // 2
    while h >= 1:
        blk = 2 * h
        upper = (row % blk) >= h
        same_blk = (ri // blk) == (ci // blk)
        r_up = (ri % blk) >= h
        c_up = (ci % blk) >= h
        q_rows = {False: upper, True: jnp.logical_not(upper)}
        sgn = {rev: jnp.where(q_rows[rev], 1.0, -1.0) for rev in (False, True)}
        pmask = {False: jnp.logical_and(same_blk, jnp.logical_and(r_up, jnp.logical_not(c_up))),
                 True: jnp.logical_and(same_blk, jnp.logical_and(jnp.logical_not(r_up), c_up))}
        sl = []
        for i, (q, k, v, lf, st, rev) in enumerate(streams):
            cm = cum[i]
            off = h if rev else h - 1
            if h >= SUBLANES:
                pieces = [jnp.broadcast_to(cm[m0 + off:m0 + off + 1, :], (blk, dk)) for m0 in range(0, c, blk)]
                ref = jnp.concatenate(pieces, axis=0) if len(pieces) > 1 else pieces[0]
            elif blk == SUBLANES:
                ref = sub_bcast(cm, off)
            elif 2 * blk == SUBLANES:
                ref = jnp.where(sub < blk, sub_bcast(cm, off), sub_bcast(cm, off + blk))
            else:
                ref = jnp.where(q_rows[rev], pltpu.roll(cm, c - 1 if rev else 1, axis=0), cm)
            e = jnp.minimum((cm - ref) * sgn[rev], 0.0)
            x = (jnp.where(q_rows[rev], q, k) * jnp.exp2(e)).astype(BF16)
            sl.append(lax.dot_general(x, x, nt, preferred_element_type=F32))
        scores = [sc + jnp.where(pmask[rev], x, 0.0) for sc, x, rev in zip(scores, sl, revs)]
        h //= 2

    outs = []
    for (q, k, v, lf, st, rev), cm, sc in zip(streams, cum, scores):
        tot = cm[0:1, :] if rev else cm[c - 1:c, :]
        o = _dot(sc, v) + _dot_nt(q * jnp.exp2(cm), st)
        st_new = st * jnp.exp2(tot) + _dot_tn(v, k * jnp.exp2(tot - cm))
        outs.append((o, st_new))
    return outs


HG_HEADS_PER_STEP = 8
HG_CHUNKS_PER_STEP = 2


def _hgscan_kernel(dk, qf, fff, i_f, qb, ffb, i_b, lb_ref, of_ref, ob_ref, sf, sb):
    @pl.when(pl.program_id(2) == 0)
    def _():
        sf[...] = jnp.zeros_like(sf)
        sb[...] = jnp.zeros_like(sb)

    st_f = [sf[p] for p in range(HG_HEADS_PER_STEP)]
    st_b = [sb[p] for p in range(HG_HEADS_PER_STEP)]
    for sub in range(HG_CHUNKS_PER_STEP):
        rows_f = slice(sub * CHUNK, (sub + 1) * CHUNK)
        rows_b = slice((HG_CHUNKS_PER_STEP - 1 - sub) * CHUNK, (HG_CHUNKS_PER_STEP - sub) * CHUNK)
        streams = []
        for p in range(HG_HEADS_PER_STEP):
            sl = slice(p * dk, (p + 1) * dk)
            for (q_ref, ff_ref, i_ref, st, d, rows) in ((qf, fff, i_f, st_f[p], 0, rows_f),
                                                        (qb, ffb, i_b, st_b[p], 1, rows_b)):
                lbv = lb_ref[d:d + 1, sl]
                f = lbv + (1.0 - lbv) * _sigmoid(ff_ref[rows, sl])
                streams.append((_silu(q_ref[rows, sl]), 1.0 - f, i_ref[rows, sl], jnp.log(f), st, d == 1))
        outs = _hg_streams(streams)
        for p in range(HG_HEADS_PER_STEP):
            sl = slice(p * dk, (p + 1) * dk)
            of_ref[rows_f, sl], st_f[p] = outs[2 * p]
            ob_ref[rows_b, sl], st_b[p] = outs[2 * p + 1]
    for p in range(HG_HEADS_PER_STEP):
        sf[p] = st_f[p]
        sb[p] = st_b[p]


def _hgscan(nb, seq, z, lb, hg_off, hw):
    t = z.shape[0]
    rows = HG_CHUNKS_PER_STEP * CHUNK
    nc = seq // rows
    dk = hw // HG_HEADS
    wblk = HG_HEADS_PER_STEP * dk
    base = hg_off // wblk
    nh = hw // wblk

    def fw(comp):
        return pl.BlockSpec((rows, wblk), lambda bi, h, c: (bi * nc + c, base + comp * nh + h))

    def bw(comp):
        return pl.BlockSpec((rows, wblk), lambda bi, h, c: (bi * nc + nc - 1 - c, base + comp * nh + h))

    out = jax.ShapeDtypeStruct((t, hw), F32)
    state = pltpu.VMEM((HG_HEADS_PER_STEP, dk, dk), F32)
    return pl.pallas_call(
        functools.partial(_hgscan_kernel, dk),
        grid=(nb, nh, nc),
        in_specs=[fw(0), fw(1), fw(3), bw(0), bw(2), bw(3),
                  pl.BlockSpec((2, wblk), lambda bi, h, c: (0, h))],
        out_specs=[pl.BlockSpec((rows, wblk), lambda bi, h, c: (bi * nc + c, h)),
                   pl.BlockSpec((rows, wblk), lambda bi, h, c: (bi * nc + nc - 1 - c, h))],
        out_shape=[out, out],
        scratch_shapes=[state, state],
        compiler_params=_cparams(("parallel", "parallel", "arbitrary")),
        name="hgscan",
    )(z, z, z, z, z, z, lb)


def _blockdiag2(w):
    _, r, n = w.shape
    z = jnp.zeros((r, n), w.dtype)
    return jnp.concatenate([jnp.concatenate([w[0], z], axis=1), jnp.concatenate([z, w[1]], axis=1)], axis=0)


def _indicator(width, seg):
    e = (jnp.arange(width)[:, None] // seg == jnp.arange(width // seg)[None, :]).astype(BF16)
    return e, e.T


def _layer_consts(w_in, rw_mu, rw_w0, rw_w_up, rw_a0, rw_a_up, rw_g_up, rw_k_k, rw_k_a, rw_r_k,
                  hg_lb_gamma):
    rw = rw_k_k.shape[-1]
    d = w_in.shape[1]
    nlat = 2 * W_LORA + 2 * A_LORA + G_LORA
    w = w_in[0]
    rkv = 3 * rw
    hg_cols = w.shape[1] - rkv - nlat
    pad = 512 - nlat
    w_perm = jnp.concatenate([w[:, :rkv], w[:, rkv + nlat:], w[:, rkv:rkv + nlat],
                              jnp.zeros((d, pad), w.dtype)], axis=1).astype(BF16)
    mu = rw_mu[0]
    lower = jnp.cumsum(jax.nn.softmax(hg_lb_gamma.astype(F32), axis=0), axis=0)[0]
    hw = lower.shape[-1]
    e64, e64t = _indicator(rw, RW_HEAD_DIM)
    return dict(
        rw=rw, hw=hw, hg_off=rkv, lat_off=rkv + hg_cols, w_in=w_perm,
        mu_rkv=mu[:rkv].reshape(1, rkv),
        mu_lat=jnp.pad(mu[rkv:rkv + nlat], (0, pad)).reshape(1, 512),
        wup=_blockdiag2(rw_w_up[0]).astype(BF16), aup=_blockdiag2(rw_a_up[0]).astype(BF16),
        gup=rw_g_up[0].astype(BF16),
        w0=rw_w0[0].reshape(1, 2 * rw), a0=rw_a0[0].reshape(1, 2 * rw),
        k_k=rw_k_k[0].reshape(1, rw), k_a=rw_k_a[0].reshape(1, rw), r_k=rw_r_k[0].reshape(1, rw),
        e64=e64, e64t=e64t,
        lb=lower,
    )


def _mixer(x2, nb, seq, mod, norm_pre_mix, wc):
    z = _inproj(x2, mod, norm_pre_mix, wc['w_in'], seq)
    (r, v, kk, g, bonus, lw0, lw1, b0, b1, kd0, kd1) = _rwprep(
        z, seq, wc['rw'], wc['lat_off'], wc['mu_rkv'], wc['mu_lat'], wc['wup'], wc['aup'], wc['gup'],
        wc['w0'], wc['a0'], wc['k_k'], wc['k_a'], wc['r_k'], wc['e64'], wc['e64t'])
    rw_of, rw_ob = _rwscan(nb, seq, r, v, kk, lw0, lw1, b0, b1, kd0, kd1)
    hg_of, hg_ob = _hgscan(nb, seq, z, wc['lb'], wc['hg_off'], wc['hw'])
    return dict(z=z, r=r, v=v, kk=kk, g=g, bonus=bonus, lw0=lw0, rw_of=rw_of, rw_ob=rw_ob,
                hg_of=hg_of, hg_ob=hg_ob)


def _outproj_kernel(rw, x_ref, rf_ref, rb_ref, bon_ref, g_ref, hf_ref, hb_ref, hgg_ref, mod_ref,
                    lnw_ref, lnb_ref, hnw_ref, npm_ref, npf_ref, wout_ref, wrh_ref, wrl_ref,
                    e64_ref, e64t_ref, e128_ref, e128t_ref,
                    x1_ref, h2_ref, lg_ref):
    m6 = mod_ref[0]
    o = rf_ref[...] + rb_ref[...]
    mean = _seg_sum(o, e64_ref[...], e64t_ref[...]) * (1.0 / RW_HEAD_DIM)
    dlt = o - mean
    var = _seg_sum(dlt * dlt, e64_ref[...], e64t_ref[...]) * (1.0 / RW_HEAD_DIM)
    o_rw = (dlt * lax.rsqrt(var + RW_GN_EPS) * lnw_ref[...] + lnb_ref[...] + bon_ref[...]) * g_ref[...]
    oh = hf_ref[...] + hb_ref[...]
    hd = oh.shape[1] // HG_HEADS
    ms = _seg_sum(oh * oh, e128_ref[...], e128t_ref[...]) * (1.0 / hd)
    o_hg = oh * lax.rsqrt(ms + NORM_EPS) * hnw_ref[...] * _silu(hgg_ref[...])
    m = _dot(o_rw, wout_ref[0:rw, :]) + _dot(o_hg, wout_ref[rw:, :])
    x1 = x_ref[...] + m6[2:3] * _rms_rows(m, npm_ref[...])
    x1_ref[...] = x1
    h2 = _rms_rows(x1, npf_ref[...]) * (1.0 + m6[4:5]) + m6[3:4]
    hi, lo = _split2(h2)
    _pack_rows(hi, h2_ref)
    nt = (((1,), (1,)), ((), ()))
    lg_ref[...] = (lax.dot_general(wrh_ref[...], hi, nt, preferred_element_type=F32)
                   + lax.dot_general(wrh_ref[...], lo, nt, preferred_element_type=F32)
                   + lax.dot_general(wrl_ref[...], hi, nt, preferred_element_type=F32))


def _outproj(x2, seq, mx, z, mod, wc, oc):
    t, d = x2.shape
    rw, hw = wc['rw'], wc['hw']
    tm = 256
    gblk = (wc['hg_off'] + 4 * hw) // hw

    def row(w):
        return pl.BlockSpec((tm, w), lambda i: (i, 0))

    def full(a):
        return pl.BlockSpec(a.shape, lambda i: (0,) * a.ndim)

    consts = [oc['ln_w'], oc['ln_b'], oc['hg_norm_w'], oc['npm'], oc['npf'], oc['w_out'], oc['wr_hi'],
              oc['wr_lo'], wc['e64'], wc['e64t'], oc['e128'], oc['e128t']]
    return pl.pallas_call(
        functools.partial(_outproj_kernel, rw),
        grid=(t // tm,),
        in_specs=[row(d), row(rw), row(rw), row(rw), row(rw), row(hw), row(hw),
                  pl.BlockSpec((tm, hw), lambda i: (i, gblk)),
                  pl.BlockSpec((1, 6, d), lambda i: ((i * tm) // seq, 0, 0))] + [full(a) for a in consts],
        out_specs=[row(d), pl.BlockSpec((tm * (d // PACK_W), LANES), lambda i: (i, 0)),
                   pl.BlockSpec((N_EXPERTS, tm), lambda i: (0, i))],
        out_shape=[jax.ShapeDtypeStruct((t, d), F32), jax.ShapeDtypeStruct((t * (d // PACK_W), LANES), jnp.uint32),
                   jax.ShapeDtypeStruct((N_EXPERTS, t), F32)],
        compiler_params=_cparams(("parallel",)),
        name="outproj",
    )(x2, mx['rw_of'], mx['rw_ob'], mx['bonus'], mx['g'], mx['hg_of'], mx['hg_ob'], z, mod, *consts)


ROUTE_TILE = 512


def _route_kernel(lg_ref, bias_ref, ut_ref, eidx_ref, wsel_ref, rank_ref, cnt_ref, carry):
    @pl.when(pl.program_id(0) == 0)
    def _():
        carry[...] = jnp.zeros_like(carry)

    ne, tt = lg_ref.shape
    gsz = ne // N_GROUPS
    neg = -jnp.inf
    s = _sigmoid(lg_ref[...])
    biased = s + bias_ref[...]
    io_g = lax.broadcasted_iota(I32, (gsz, tt), 0)
    gs_rows = []
    for gi in range(N_GROUPS):
        blk = biased[gi * gsz:(gi + 1) * gsz, :]
        m1 = jnp.max(blk, axis=0, keepdims=True)
        first = jnp.min(jnp.where(blk == m1, io_g, gsz), axis=0, keepdims=True)
        m2 = jnp.max(jnp.where(io_g == first, neg, blk), axis=0, keepdims=True)
        gs_rows.append(m1 + m2)
    gs = jnp.concatenate(gs_rows, axis=0)
    io_n = lax.broadcasted_iota(I32, (N_GROUPS, tt), 0)
    selg = jnp.zeros((N_GROUPS, tt), jnp.bool_)
    for _ in range(TOPK_GROUPS):
        m = jnp.max(gs, axis=0, keepdims=True)
        first = jnp.min(jnp.where(gs == m, io_n, N_GROUPS), axis=0, keepdims=True)
        pick = io_n == first
        selg = jnp.logical_or(selg, pick)
        gs = jnp.where(pick, neg, gs)
    emask = jnp.concatenate([jnp.broadcast_to(selg[gi:gi + 1, :], (gsz, tt)) for gi in range(N_GROUPS)],
                            axis=0)
    mb = jnp.where(emask, biased, neg)
    io_e = lax.broadcasted_iota(I32, (ne, tt), 0)
    sel = jnp.zeros((ne, tt), jnp.bool_)
    picks, idxs, ws = [], [], []
    for _ in range(TOP_K):
        m = jnp.max(mb, axis=0, keepdims=True)
        first = jnp.min(jnp.where(mb == m, io_e, ne), axis=0, keepdims=True)
        pick = io_e == first
        picks.append(pick)
        idxs.append(first)
        ws.append(jnp.sum(jnp.where(pick, s, 0.0), axis=0, keepdims=True))
        sel = jnp.logical_or(sel, pick)
        mb = jnp.where(pick, neg, mb)
    wsum = ws[0]
    for w in ws[1:]:
        wsum = wsum + w
    pos = jnp.dot(jnp.where(sel, 1.0, 0.0).astype(BF16), ut_ref[...], preferred_element_type=F32) + carry[...]
    ranks = [jnp.sum(jnp.where(p, pos, 0.0), axis=0, keepdims=True).astype(I32) for p in picks]
    carry[...] = carry[...] + jnp.sum(jnp.where(sel, 1.0, 0.0), axis=1, keepdims=True)
    zi = jnp.zeros((SUBLANES - TOP_K, tt), I32)
    eidx_ref[...] = jnp.concatenate(idxs + [zi], axis=0)
    rank_ref[...] = jnp.concatenate(ranks + [zi], axis=0)
    wsel_ref[...] = jnp.concatenate([w / wsum * ROUTED_SCALE for w in ws] + [zi.astype(F32)], axis=0)
    cnt_ref[...] = jnp.broadcast_to(carry[...], cnt_ref.shape).astype(I32)


def _route(logits_t, e_bias):
    ne, t = logits_t.shape
    tt = ROUTE_TILE
    ut = (jnp.arange(tt)[:, None] < jnp.arange(tt)[None, :]).astype(BF16)
    tok = pl.BlockSpec((SUBLANES, tt), lambda i: (0, i))
    return pl.pallas_call(
        _route_kernel,
        grid=(t // tt,),
        in_specs=[pl.BlockSpec((ne, tt), lambda i: (0, i)),
                  pl.BlockSpec((ne, 1), lambda i: (0, 0)),
                  pl.BlockSpec((tt, tt), lambda i: (0, 0))],
        out_specs=[tok, tok, tok, pl.BlockSpec((ne, LANES), lambda i: (0, 0))],
        out_shape=[jax.ShapeDtypeStruct((SUBLANES, t), I32), jax.ShapeDtypeStruct((SUBLANES, t), F32),
                   jax.ShapeDtypeStruct((SUBLANES, t), I32), jax.ShapeDtypeStruct((ne, LANES), I32)],
        scratch_shapes=[pltpu.VMEM((ne, 1), F32)],
        compiler_params=_cparams(("arbitrary",)),
        name="route",
    )(logits_t, e_bias.reshape(ne, 1), ut)


DISPATCH_TILE = 128
PLAN_TILE = 1024


def _plan_kernel(eidx_ref, rank_ref, ps_ref, dest_ref):
    ne = ps_ref.shape[0]
    tp = eidx_ref.shape[1]
    io_e = lax.broadcasted_iota(I32, (ne, tp), 0)
    ps = ps_ref[...]
    rows = []
    for j in range(TOP_K):
        start = jnp.sum(jnp.where(io_e == eidx_ref[j:j + 1, :], ps, 0.0), axis=0, keepdims=True)
        rows.append(start.astype(I32) + rank_ref[j:j + 1, :])
    dest = jnp.concatenate(rows + [jnp.zeros((SUBLANES - TOP_K, tp), I32)], axis=0)
    for i in range(tp // DISPATCH_TILE):
        dest_ref[i] = dest[:, i * DISPATCH_TILE:(i + 1) * DISPATCH_TILE]


def _plan(eidx, rank, pad_start):
    t = eidx.shape[1]
    tp = PLAN_TILE
    ne = pad_start.shape[0]
    tok = pl.BlockSpec((SUBLANES, tp), lambda i: (0, i))
    ntile = tp // DISPATCH_TILE
    return pl.pallas_call(
        _plan_kernel,
        grid=(t // tp,),
        in_specs=[tok, tok, pl.BlockSpec((ne, 1), lambda i: (0, 0))],
        out_specs=pl.BlockSpec((ntile, SUBLANES, DISPATCH_TILE), lambda i: (i, 0, 0)),
        out_shape=jax.ShapeDtypeStruct((t // DISPATCH_TILE, SUBLANES, DISPATCH_TILE), I32),
        compiler_params=_cparams(("parallel",)),
        name="plan",
    )(eidx, rank, pad_start.astype(F32).reshape(ne, 1))


def _dispatch_kernel(tt, nsteps, zs_ref, zc_ref, nu_ref, dest_ref, h_ref, xs_ref, idx_smem, zero_blk,
                     isem, sem, zsem, bsem):
    i = pl.program_id(0)
    n = tt * TOP_K
    ne = zs_ref.shape[0]
    nblk = xs_ref.shape[0] // EXPERT_BLOCK
    icp = pltpu.make_async_copy(dest_ref.at[pl.ds(i * tt * SUBLANES, n)], idx_smem, isem)
    icp.start()

    @pl.when(i == 0)
    def _():
        zero_blk[...] = jnp.zeros_like(zero_blk)

    icp.wait()

    def row_copy(k):
        return pltpu.make_async_copy(h_ref.at[k % tt], xs_ref.at[idx_smem[k]], sem)

    def drain(k, c):
        row_copy(k).wait()
        return c

    for k in range(n):
        row_copy(k).start(priority=k % 2)

    for q in range(pl.cdiv(ne, nsteps)):
        e = i + q * nsteps
        ec = jnp.minimum(e, ne - 1)
        start = zs_ref[ec]
        count = jnp.where(e < ne, zc_ref[ec], 0)

        def zero_copy(r):
            return pltpu.make_async_copy(zero_blk.at[0], xs_ref.at[start + r], zsem)

        def z_issue(r, c):
            zero_copy(r).start()
            return c

        def z_drain(r, c):
            zero_copy(r).wait()
            return c

        lax.fori_loop(0, count, z_issue, 0)
        lax.fori_loop(0, count, z_drain, 0)

    for q in range(pl.cdiv(nblk, nsteps)):
        blk = nu_ref[0] + i + q * nsteps

        @pl.when(blk < nblk)
        def _():
            bcp = pltpu.make_async_copy(
                zero_blk, xs_ref.at[pl.ds(pl.multiple_of(blk * EXPERT_BLOCK, EXPERT_BLOCK), EXPERT_BLOCK)], bsem)
            bcp.start()
            bcp.wait()

    lax.fori_loop(0, n, drain, 0, unroll=8)


def _dispatch(dest_flat, h2p, p_rows, zero_start, zero_count, n_used):
    t = h2p.shape[0]
    tt = DISPATCH_TILE
    anyspec = pl.BlockSpec(memory_space=pl.ANY)
    dma = pltpu.SemaphoreType.DMA(())
    grid_spec = pltpu.PrefetchScalarGridSpec(
        num_scalar_prefetch=3,
        grid=(t // tt,),
        in_specs=[anyspec, pl.BlockSpec((tt,) + h2p.shape[1:], lambda i, zs, zc, nu: (i, 0, 0))],
        out_specs=anyspec,
        scratch_shapes=[pltpu.SMEM((tt * TOP_K,), I32), pltpu.VMEM((EXPERT_BLOCK,) + h2p.shape[1:], h2p.dtype),
                        dma, dma, dma, dma],
    )
    return pl.pallas_call(
        functools.partial(_dispatch_kernel, tt, t // tt),
        grid_spec=grid_spec,
        out_shape=jax.ShapeDtypeStruct((p_rows,) + h2p.shape[1:], h2p.dtype),
        compiler_params=_cparams(("arbitrary",)),
        name="dispatch",
    )(zero_start, zero_count, n_used, dest_flat, h2p)


def _experts_kernel(be_ref, nu_ref, x_ref, wg_ref, wu_ref, wd_ref, y_ref, wg_s, wu_s, wd_s):
    b = pl.program_id(0)
    live = b < nu_ref[0]
    new_expert = jnp.logical_or(b == 0, be_ref[b] != be_ref[jnp.maximum(b - 1, 0)])

    @pl.when(jnp.logical_and(live, new_expert))
    def _():
        wg_s[...] = wg_ref[0].astype(BF16)
        wu_s[...] = wu_ref[0].astype(BF16)
        wd_s[...] = wd_ref[0].astype(BF16)

    @pl.when(live)
    def _():
        x = _unpack_rows(x_ref, wg_ref.shape[1] // PACK_W)
        gate = jnp.dot(x, wg_s[...], preferred_element_type=F32)
        up = jnp.dot(x, wu_s[...], preferred_element_type=F32)
        _pack_rows(_dot(_silu(gate) * up, wd_s[...]).astype(BF16), y_ref)

    @pl.when(b >= nu_ref[0])
    def _():
        y_ref[...] = jnp.zeros_like(y_ref)


def _experts(block_e, n_used, xs, wg, wu, wd):
    d, de = wg.shape[1], wg.shape[2]
    ns = d // PACK_W
    p_rows = xs.shape[0] // ns
    nblk = p_rows // EXPERT_BLOCK

    def live(b, nu):
        return jnp.minimum(b, jnp.maximum(nu[0] - 1, 0))

    grid_spec = pltpu.PrefetchScalarGridSpec(
        num_scalar_prefetch=2,
        grid=(nblk,),
        in_specs=[pl.BlockSpec((EXPERT_BLOCK * ns, LANES), lambda b, be, nu: (live(b, nu), 0)),
                  pl.BlockSpec((1, d, de), lambda b, be, nu: (be[live(b, nu)], 0, 0)),
                  pl.BlockSpec((1, d, de), lambda b, be, nu: (be[live(b, nu)], 0, 0)),
                  pl.BlockSpec((1, de, d), lambda b, be, nu: (be[live(b, nu)], 0, 0))],
        out_specs=pl.BlockSpec((EXPERT_BLOCK * ns, LANES), lambda b, be, nu: (b, 0)),
        scratch_shapes=[pltpu.VMEM((d, de), BF16), pltpu.VMEM((d, de), BF16), pltpu.VMEM((de, d), BF16)],
    )
    return pl.pallas_call(
        _experts_kernel,
        grid_spec=grid_spec,
        out_shape=jax.ShapeDtypeStruct((p_rows * ns, LANES), jnp.uint32),
        compiler_params=_cparams(("arbitrary",)),
        name="experts",
    )(block_e, n_used, xs, wg, wu, wd)


def _combine_kernel(tt, nsteps, dest_ref, y_ref, x1_ref, h2_ref, w_ref, mod_ref, npf_ref, sg_ref, su_ref, sd_ref,
                    o_ref, ybuf, idx_smem, isem, sems):
    i = pl.program_id(0)
    n = tt * TOP_K
    ns = sg_ref.shape[0] // PACK_W

    def row_copy(slot, k):
        dst = ybuf.at[pl.ds(pl.multiple_of((slot * n + k) * ns, ns), ns)]
        return pltpu.make_async_copy(y_ref.at[idx_smem[slot * n + k]], dst, sems.at[slot])

    def issue(step, slot):
        icp = pltpu.make_async_copy(dest_ref.at[pl.ds(step * tt * SUBLANES, n)],
                                    idx_smem.at[pl.ds(slot * n, n)], isem)
        icp.start()
        icp.wait()
        for k in range(n):
            row_copy(slot, k).start(priority=k % 2)

    def drain(slot):
        def body(k, c):
            row_copy(slot, k).wait()
            return c

        lax.fori_loop(0, n, body, 0, unroll=8)

    @pl.when(i == 0)
    def _():
        issue(0, 0)

    other = (i + 1) % 2
    issue(jnp.minimum(i + 1, nsteps - 1), other)

    h2 = _unpack_rows(h2_ref, ns)
    gate = jnp.dot(h2, sg_ref[...], preferred_element_type=F32)
    up = jnp.dot(h2, su_ref[...], preferred_element_type=F32)
    shared = _dot(_silu(gate) * up, sd_ref[...])
    w = w_ref[...]
    m6 = mod_ref[0]

    def finish(slot):
        drain(slot)
        acc = [None] * (2 * ns)
        for j in range(TOP_K):
            wj = w[:, j:j + 1]
            for s in range(ns):
                word = ybuf[pl.ds((slot * n + j * tt) * ns + s, tt, stride=ns), :]
                parts = (lax.bitcast_convert_type(word & jnp.uint32(0xFFFF0000), F32),
                         lax.bitcast_convert_type(word << 16, F32))
                for h, part in enumerate(parts):
                    term = wj * part
                    acc[2 * s + h] = term if acc[2 * s + h] is None else acc[2 * s + h] + term
        routed = jnp.concatenate(acc, axis=1)
        o_ref[...] = x1_ref[...] + m6[5:6] * _rms_rows(routed + shared, npf_ref[...])

    for slot in (0, 1):
        @pl.when(i % 2 == slot)
        def _():
            finish(slot)

    @pl.when(i == nsteps - 1)
    def _():
        drain(other)


def _combine(dest_flat, y3, x1, h2, wsel_t, mod, seq, npf, sg, su, sd):
    t, d = x1.shape
    tt = DISPATCH_TILE
    ns = d // PACK_W
    n = tt * TOP_K
    anyspec = pl.BlockSpec(memory_space=pl.ANY)

    def full(a):
        return pl.BlockSpec(a.shape, lambda i: (0,) * a.ndim)

    rows = pltpu.VMEM((2 * n * ns, LANES), jnp.uint32)
    return pl.pallas_call(
        functools.partial(_combine_kernel, tt, t // tt),
        grid=(t // tt,),
        in_specs=[anyspec, anyspec,
                  pl.BlockSpec((tt, d), lambda i: (i, 0)),
                  pl.BlockSpec((tt * ns, LANES), lambda i: (i, 0)),
                  pl.BlockSpec((tt, SUBLANES), lambda i: (i, 0)),
                  pl.BlockSpec((1, 6, d), lambda i: ((i * tt) // seq, 0, 0)),
                  full(npf), full(sg), full(su), full(sd)],
        out_specs=pl.BlockSpec((tt, d), lambda i: (i, 0)),
        out_shape=jax.ShapeDtypeStruct((t, d), F32),
        scratch_shapes=[rows, pltpu.SMEM((2 * n,), I32), pltpu.SemaphoreType.DMA(()),
                        pltpu.SemaphoreType.DMA((2,))],
        compiler_params=_cparams(("arbitrary",)),
        name="combine",
    )(dest_flat, y3, x1, h2, wsel_t, mod, npf, sg, su, sd)


def _moe_plan(eidx, rank, cnt, t):
    counts = cnt[:, 0]
    padded = (counts + EXPERT_BLOCK - 1) // EXPERT_BLOCK * EXPERT_BLOCK
    pad_end = jnp.cumsum(padded)
    pad_start = pad_end - padded
    n_blocks = (t * TOP_K + EXPERT_BLOCK - 1) // EXPERT_BLOCK + N_EXPERTS
    first_row = jnp.arange(n_blocks, dtype=I32) * EXPERT_BLOCK
    block_e = jnp.minimum(jnp.sum((pad_end[None, :] <= first_row[:, None]).astype(I32), axis=1),
                          N_EXPERTS - 1).astype(I32)
    n_used = (pad_end[-1:] // EXPERT_BLOCK).astype(I32)
    dest_flat = _plan(eidx, rank, pad_start).reshape(-1)
    zero_rows = ((pad_start + counts).astype(I32), (padded - counts).astype(I32))
    return block_e, n_used, dest_flat, n_blocks * EXPERT_BLOCK, zero_rows


def _trunk(x, mod, wc, oc, ec, norm_pre_mix):
    nb, seq, d = x.shape
    t = nb * seq
    x2 = x.reshape(t, d)
    mx = _mixer(x2, nb, seq, mod, norm_pre_mix, wc)
    x1, h2, logits_t = _outproj(x2, seq, mx, mx['z'], mod, wc, oc)
    eidx, wsel, rank, cnt = _route(logits_t, ec['e_bias'])
    block_e, n_used, dest_flat, p_rows, zero_rows = _moe_plan(eidx, rank, cnt, t)
    ns = d // PACK_W
    xs = _dispatch(dest_flat, h2.reshape(t, ns, LANES), p_rows, *zero_rows, n_used)
    y = _experts(block_e, n_used, xs.reshape(p_rows * ns, LANES), ec['wg'], ec['wu'], ec['wd'])
    out = _combine(dest_flat, y.reshape(p_rows, ns, LANES), x1, h2, wsel.T, mod, seq, oc['npo'],
                   ec['sg'], ec['su'], ec['sd'])
    return out.reshape(nb, seq, d)


def kernel(x_prompt, x_sample, c_prompt, c_sample, w_ada, b_ada, norm_pre_mix, norm_post_mix, norm_pre_ffn, norm_post_ffn, w_in, rw_mu, rw_w0, rw_w_up, rw_a0, rw_a_up, rw_g_up, rw_k_k, rw_k_a, rw_r_k, rw_ln_w, rw_ln_b, hg_lb_gamma, hg_norm_w, w_out, w_router, e_bias, w_exp_gate, w_exp_up, w_exp_down, w_sh_gate, w_sh_up, w_sh_down):
    d = x_prompt.shape[-1]
    wc = _layer_consts(w_in, rw_mu, rw_w0, rw_w_up, rw_a0, rw_a_up, rw_g_up, rw_k_k, rw_k_a, rw_r_k,
                       hg_lb_gamma)
    rw, hw = wc['rw'], wc['hw']
    e128, e128t = _indicator(hw, hw // HG_HEADS)
    wr_hi, wr_lo = _split2(w_router[0].T)
    oc = dict(ln_w=rw_ln_w[0].reshape(1, rw), ln_b=rw_ln_b[0].reshape(1, rw),
              hg_norm_w=hg_norm_w[0].reshape(1, hw), npm=norm_post_mix[0].reshape(1, d),
              npf=norm_pre_ffn[0].reshape(1, d), npo=norm_post_ffn[0].reshape(1, d),
              w_out=w_out[0].astype(BF16), wr_hi=wr_hi, wr_lo=wr_lo, e128=e128, e128t=e128t)
    ec = dict(e_bias=e_bias[0], wg=w_exp_gate[0], wu=w_exp_up[0], wd=w_exp_down[0], sg=w_sh_gate[0].astype(BF16), su=w_sh_up[0].astype(BF16),
              sd=w_sh_down[0].astype(BF16))
    nbp = c_prompt.shape[0]
    mod = _ada(jnp.concatenate([c_prompt, c_sample], axis=0), w_ada[0], b_ada[0]).reshape(-1, 6, d)
    y_prompt = _trunk(x_prompt, mod[:nbp], wc, oc, ec, norm_pre_mix[0])
    y_sample = _trunk(x_sample, mod[nbp:], wc, oc, ec, norm_pre_mix[0])
    return (y_prompt, y_sample)
```
